```python
import math
import jax, jax.numpy as jnp
from jax import lax
import numpy as np

D_MODEL = 1024
BATCH = 2
SEQ = 8192
DEPTH = 1

D_MIX = D_MODEL
ATTN_WIDTH = D_MIX // 2
CONV_WIDTH = D_MIX - ATTN_WIDTH
N_DIFF_HEADS = 4
DIFF_HEAD_DIM = ATTN_WIDTH // N_DIFF_HEADS // 2
V_HEAD_DIM = 2 * DIFF_HEAD_DIM
ROT_DIM = DIFF_HEAD_DIM // 4
ROPE_THETA = 500000.0
CONV_K = 31
IN_COLS = 3 * ATTN_WIDTH + 2 * CONV_WIDTH
N_GROUPS = 4
EXPERTS_PER_GROUP = 8
N_EXPERTS = N_GROUPS * EXPERTS_PER_GROUP
EXPERT_FF = 256
TOP_K_INNER = 2
Q_BLOCK = 128
EPS = 1e-6

kernel_name = "hymba_diffattn_conformer_hmoe_adaln"


def rms_norm(x, g):
    xf = x.astype(jnp.float32)
    y = xf * lax.rsqrt(jnp.mean(xf * xf, axis=-1, keepdims=True) + EPS)
    return (y * g.astype(jnp.float32)).astype(x.dtype)


def layer_norm(x, g, b):
    xf = x.astype(jnp.float32)
    mu = jnp.mean(xf, axis=-1, keepdims=True)
    var = jnp.mean(jnp.square(xf - mu), axis=-1, keepdims=True)
    y = (xf - mu) * lax.rsqrt(var + EPS) * g.astype(jnp.float32) + b.astype(jnp.float32)
    return y.astype(x.dtype)


def rope_tables(positions):
    half = ROT_DIM // 2
    inv_freq = ROPE_THETA ** (-jnp.arange(0, ROT_DIM, 2, dtype=jnp.float32) / ROT_DIM)
    ang = positions.astype(jnp.float32)[..., None] * inv_freq
    return jnp.cos(ang)[:, :, None, None, :], jnp.sin(ang)[:, :, None, None, :]


def partial_rope(t, cos, sin):
    half = ROT_DIM // 2
    tf = t[..., :ROT_DIM].astype(jnp.float32)
    t1, t2 = tf[..., :half], tf[..., half:]
    rot = jnp.concatenate([t1 * cos - t2 * sin, t2 * cos + t1 * sin], axis=-1)
    return jnp.concatenate([rot.astype(t.dtype), t[..., ROT_DIM:]], axis=-1)


def diff_attention(q, k, v, lam, lambda_init, subln_g):
    B, S, H = q.shape[0], q.shape[1], q.shape[2]
    nb = S // Q_BLOCK
    scale = DIFF_HEAD_DIM ** -0.5
    kf = k.astype(jnp.float32)
    vf = v.astype(jnp.float32)
    qb = q.reshape(B, nb, Q_BLOCK, H, 2, DIFF_HEAD_DIM).transpose(1, 0, 2, 3, 4, 5)
    k_pos = jnp.arange(S)

    def block(args):
        q_blk, i = args
        s = jnp.einsum('bqhcd,bkhcd->bhcqk', q_blk.astype(jnp.float32), kf) * scale
        q_pos = i * Q_BLOCK + jnp.arange(Q_BLOCK)
        mask = k_pos[None, :] <= q_pos[:, None]
        p = jax.nn.softmax(jnp.where(mask, s, -jnp.inf), axis=-1)
        a = p[:, :, 0] - lam * p[:, :, 1]
        return jnp.einsum('bhqk,bkhe->bqhe', a, vf)

    o = lax.map(block, (qb, jnp.arange(nb)))
    o = o.transpose(1, 0, 2, 3, 4).reshape(B, S, H, V_HEAD_DIM).astype(q.dtype)
    o = rms_norm(o, subln_g) * (1.0 - lambda_init)
    return o.reshape(B, S, H * V_HEAD_DIM)


def conformer_conv(glu_in, w_dw, b_dw, ln_g, ln_b):
    a, gate = glu_in[..., :CONV_WIDTH], glu_in[..., CONV_WIDTH:]
    u = a * jax.nn.sigmoid(gate)
    y = lax.conv_general_dilated(u, w_dw[:, None, :], window_strides=(1,), padding=[(CONV_K - 1, 0)],
                                 dimension_numbers=('NWC', 'WIO', 'NWC'),
                                 feature_group_count=CONV_WIDTH) + b_dw
    return jax.nn.silu(layer_norm(y, ln_g, ln_b))


def hier_moe(h, w_group, b_group, w_router, b_router, w_gate, w_up, w_down):
    B, S, D = h.shape
    t = h.reshape(-1, D)
    n = t.shape[0]
    g_prob = jax.nn.softmax((t @ w_group).astype(jnp.float32) + b_group.astype(jnp.float32), axis=-1)
    g_p, g_idx = lax.top_k(g_prob, 1)
    e_logits = ((t @ w_router).astype(jnp.float32) + b_router.astype(jnp.float32)).reshape(n, N_GROUPS, EXPERTS_PER_GROUP)
    e_in = e_logits[jnp.arange(n), g_idx[:, 0]]
    e_p, e_idx = lax.top_k(jax.nn.softmax(e_in, axis=-1), TOP_K_INNER)
    e_p = e_p / jnp.sum(e_p, axis=-1, keepdims=True)
    wts = g_p * e_p
    expert_id = g_idx * EXPERTS_PER_GROUP + e_idx
    combine = jnp.sum(jax.nn.one_hot(expert_id, N_EXPERTS, dtype=jnp.float32) * wts[..., None], axis=1)
    gt = jnp.einsum('nd,edf->nef', t, w_gate)
    up = jnp.einsum('nd,edf->nef', t, w_up)
    hid = jax.nn.silu(gt) * up * combine.astype(t.dtype)[:, :, None]
    out = jnp.einsum('nef,efd->nd', hid, w_down)
    return out.reshape(B, S, D)


def _normal(k, shape, std):
    return jax.random.normal(k, shape, jnp.float32) * std


def setup_inputs(seed: int = 0) -> dict:
    key = jax.random.key(seed)
    ks = jax.random.split(key, 32)
    L, D = DEPTH, D_MODEL
    offset = jax.random.randint(ks[2], (BATCH, 1), 0, 1024, dtype=jnp.int32)
    positions = offset + jnp.arange(SEQ, dtype=jnp.int32)[None, :]
    return {
        "x": _normal(ks[0], (BATCH, SEQ, D), 1.0),
        "c": _normal(ks[1], (BATCH, D), 1.0),
        "positions": positions,
        "w_ada": _normal(ks[3], (L, D, 6 * D), D ** -0.5),
        "b_ada": _normal(ks[4], (L, 6 * D), 0.02),
        "g_mix": 1.0 + _normal(ks[5], (L, D), 0.02),
        "w_in": _normal(ks[6], (L, D, IN_COLS), D ** -0.5),
        "q_norm_g": 1.0 + _normal(ks[7], (L, DIFF_HEAD_DIM), 0.02),
        "k_norm_g": 1.0 + _normal(ks[8], (L, DIFF_HEAD_DIM), 0.02),
        "lambda_q1": _normal(ks[9], (L, DIFF_HEAD_DIM), 0.1),
        "lambda_k1": _normal(ks[10], (L, DIFF_HEAD_DIM), 0.1),
        "lambda_q2": _normal(ks[11], (L, DIFF_HEAD_DIM), 0.1),
        "lambda_k2": _normal(ks[12], (L, DIFF_HEAD_DIM), 0.1),
        "subln_g": 1.0 + _normal(ks[13], (L, V_HEAD_DIM), 0.02),
        "b_glu": _normal(ks[14], (L, 2 * CONV_WIDTH), 0.02),
        "w_dw": _normal(ks[15], (L, CONV_K, CONV_WIDTH), CONV_K ** -0.5),
        "b_dw": _normal(ks[16], (L, CONV_WIDTH), 0.02),
        "conv_ln_g": 1.0 + _normal(ks[17], (L, CONV_WIDTH), 0.02),
        "conv_ln_b": _normal(ks[18], (L, CONV_WIDTH), 0.02),
        "w_out": _normal(ks[19], (L, D_MIX, D), D_MIX ** -0.5),
        "g_ffn": 1.0 + _normal(ks[20], (L, D), 0.02),
        "w_group": _normal(ks[21], (L, D, N_GROUPS), D ** -0.5),
        "b_group": _normal(ks[22], (L, N_GROUPS), 0.01),
        "w_router": _normal(ks[23], (L, D, N_EXPERTS), D ** -0.5),
        "b_router": _normal(ks[24], (L, N_EXPERTS), 0.01),
        "w_gate": _normal(ks[25], (L, N_EXPERTS, D, EXPERT_FF), D ** -0.5),
        "w_up": _normal(ks[26], (L, N_EXPERTS, D, EXPERT_FF), D ** -0.5),
        "w_down": _normal(ks[27], (L, N_EXPERTS, EXPERT_FF, D), EXPERT_FF ** -0.5),
    }


def reference(x, c, positions, w_ada, b_ada, g_mix, w_in, q_norm_g, k_norm_g, lambda_q1, lambda_k1,
              lambda_q2, lambda_k2, subln_g, b_glu, w_dw, b_dw, conv_ln_g, conv_ln_b, w_out, g_ffn,
              w_group, b_group, w_router, b_router, w_gate, w_up, w_down):
    B, S, D = x.shape
    cos, sin = rope_tables(positions)
    c_act = jax.nn.silu(c)
    for l in range(DEPTH):
        lambda_init = 0.8 - 0.6 * math.exp(-0.3 * l)
        mod = c_act @ w_ada[l] + b_ada[l]
        sh_a, sc_a, gt_a, sh_f, sc_f, gt_f = [m[:, None, :] for m in jnp.split(mod, 6, axis=-1)]

        h = rms_norm(x, g_mix[l]) * (1.0 + sc_a) + sh_a
        p = h @ w_in[l]
        q = p[..., :ATTN_WIDTH].reshape(B, S, N_DIFF_HEADS, 2, DIFF_HEAD_DIM)
        k = p[..., ATTN_WIDTH:2 * ATTN_WIDTH].reshape(B, S, N_DIFF_HEADS, 2, DIFF_HEAD_DIM)
        v = p[..., 2 * ATTN_WIDTH:3 * ATTN_WIDTH].reshape(B, S, N_DIFF_HEADS, V_HEAD_DIM)
        glu_in = p[..., 3 * ATTN_WIDTH:] + b_glu[l]
        q = partial_rope(rms_norm(q, q_norm_g[l]), cos, sin)
        k = partial_rope(rms_norm(k, k_norm_g[l]), cos, sin)
        lam = (jnp.exp(jnp.sum(lambda_q1[l].astype(jnp.float32) * lambda_k1[l].astype(jnp.float32)))
               - jnp.exp(jnp.sum(lambda_q2[l].astype(jnp.float32) * lambda_k2[l].astype(jnp.float32)))
               + lambda_init)
        attn_out = diff_attention(q, k, v, lam, lambda_init, subln_g[l])
        conv_out = conformer_conv(glu_in, w_dw[l], b_dw[l], conv_ln_g[l], conv_ln_b[l])
        y = jnp.concatenate([attn_out, conv_out], axis=-1) @ w_out[l]
        x = x + gt_a * y

        h = rms_norm(x, g_ffn[l]) * (1.0 + sc_f) + sh_f
        x = x + gt_f * hier_moe(h, w_group[l], b_group[l], w_router[l], b_router[l],
                                w_gate[l], w_up[l], w_down[l])
    return x
```

```python
import functools
import math

import numpy as np
import jax
import jax.numpy as jnp
from jax import lax
from jax.experimental import pallas as pl
from jax.experimental.pallas import tpu as pltpu

F32 = jnp.float32
BF16 = jnp.bfloat16

EPS = 1e-6
ROPE_THETA = 500000.0
N_DIFF_HEADS = 4
TOP_K_INNER = 2

LANES = 128
CONV_HALO = 32
VMEM_LIMIT = 48 * 1024 * 1024


def _cparams(sem):
    return pltpu.CompilerParams(dimension_semantics=sem, vmem_limit_bytes=VMEM_LIMIT)


def _silu(x):
    return x * jax.nn.sigmoid(x)


def _mod_kernel(c_ref, w_ref, b_ref, o_ref):
    c = c_ref[...]
    o_ref[...] = jnp.dot(_silu(c), w_ref[...], preferred_element_type=F32,
                         precision=lax.Precision.HIGHEST) + b_ref[...]


def _modulation(c, w_ada, b_ada):
    B, D = c.shape
    n_out = w_ada.shape[1]
    rows = 8
    c_pad = jnp.pad(c, ((0, rows - B), (0, 0)))
    bn = 1024
    out = pl.pallas_call(
        _mod_kernel,
        grid=(n_out // bn,),
        in_specs=[pl.BlockSpec((rows, D), lambda j: (0, 0)),
                  pl.BlockSpec((D, bn), lambda j: (0, j)),
                  pl.BlockSpec((1, bn), lambda j: (0, j))],
        out_specs=pl.BlockSpec((rows, bn), lambda j: (0, j)),
        out_shape=jax.ShapeDtypeStruct((rows, n_out), F32),
        compiler_params=_cparams(("parallel",)),
        name="adaln_mod",
    )(c_pad, w_ada, b_ada.reshape(1, n_out))
    return out[:B].reshape(B, 6, D)


def _inproj_kernel(x_ref, mod_ref, g_ref, pos_ref, wv_ref, wqk_ref, bglu_ref, gqk_ref, invf_ref,
                   q_ref, kt_ref, v_ref, u_ref, *, aw, cw, dh, rot):
    T = x_ref.shape[0]
    half = rot // 2
    x = x_ref[...]
    ms = jnp.mean(x * x, axis=-1, keepdims=True)
    sh = mod_ref[0:1, :]
    sc = mod_ref[1:2, :]
    h = x * lax.rsqrt(ms + EPS) * g_ref[...] * (1.0 + sc) + sh
    hb = h.astype(BF16)

    pv = jnp.dot(hb, wv_ref[...], preferred_element_type=F32)
    v_ref[...] = pv[:, :aw].astype(BF16)
    a = pv[:, aw:aw + cw] + bglu_ref[:, :cw]
    gate = pv[:, aw + cw:] + bglu_ref[:, cw:]
    u_ref[...] = (a * jax.nn.sigmoid(gate)).astype(BF16)

    qkt = lax.dot_general(wqk_ref[...], hb, (((1,), (1,)), ((), ())), preferred_element_type=F32)
    nch = 2 * aw // dh
    for c in range(T // LANES):
        sl = slice(c * LANES, (c + 1) * LANES)
        s3 = qkt[:, sl].reshape(nch, dh, LANES)
        ssq = jnp.mean(s3 * s3, axis=1, keepdims=True)
        y = s3 * lax.rsqrt(ssq + EPS) * gqk_ref[...].reshape(nch, dh, LANES)
        ang = invf_ref[...] * pos_ref[:, sl]
        cs = jnp.cos(ang)
        sn = jnp.sin(ang)
        t1 = y[:, 0:half, :]
        t2 = y[:, half:rot, :]
        y = jnp.concatenate([t1 * cs - t2 * sn, t2 * cs + t1 * sn, y[:, rot:, :]], axis=1)
        y2 = y.reshape(2 * aw, LANES)
        kt_ref[:, sl] = y2[aw:].astype(BF16)
        q_ref[sl, :] = y2[:aw].T.astype(BF16)


def _inproj(x2, mod, g_mix, pos_row, w_vglu, w_qk_t, b_glu, gqk_tab, invf_tab, *, B, S, aw, cw, dh, rot, T):
    N, D = x2.shape
    nS = S // T
    kern = functools.partial(_inproj_kernel, aw=aw, cw=cw, dh=dh, rot=rot)
    return pl.pallas_call(
        kern,
        grid=(N // T,),
        in_specs=[pl.BlockSpec((T, D), lambda i: (i, 0)),
                  pl.BlockSpec((None, 6, D), lambda i: (i // nS, 0, 0)),
                  pl.BlockSpec((1, D), lambda i: (0, 0)),
                  pl.BlockSpec((1, T), lambda i: (0, i)),
                  pl.BlockSpec(w_vglu.shape, lambda i: (0, 0)),
                  pl.BlockSpec(w_qk_t.shape, lambda i: (0, 0)),
                  pl.BlockSpec((1, 2 * cw), lambda i: (0, 0)),
                  pl.BlockSpec(gqk_tab.shape, lambda i: (0, 0)),
                  pl.BlockSpec(invf_tab.shape, lambda i: (0, 0))],
        out_specs=[pl.BlockSpec((T, aw), lambda i: (i, 0)),
                   pl.BlockSpec((None, aw, T), lambda i: (i // nS, 0, i % nS)),
                   pl.BlockSpec((T, aw), lambda i: (i, 0)),
                   pl.BlockSpec((T, cw), lambda i: (i, 0))],
        out_shape=[jax.ShapeDtypeStruct((N, aw), BF16),
                   jax.ShapeDtypeStruct((B, aw, S), BF16),
                   jax.ShapeDtypeStruct((N, aw), BF16),
                   jax.ShapeDtypeStruct((N, cw), BF16)],
        compiler_params=_cparams(("parallel",)),
        name="inproj_qknorm_rope_glu",
    )(x2, mod, g_mix, pos_row, w_vglu, w_qk_t, b_glu, gqk_tab, invf_tab)


def _attn_kernel(q_ref, kt_ref, v_ref, lq1_ref, lk1_ref, lq2_ref, lk2_ref, sg_ref, o_ref, acc_ref,
                 *, tq, dh, lambda_init):
    i = pl.program_id(2)
    q = q_ref[...]
    lane = lax.broadcasted_iota(jnp.int32, q.shape, 1)
    zero = jnp.zeros_like(q)
    q2 = jnp.concatenate([jnp.where(lane < dh, q, zero), jnp.where(lane >= dh, q, zero)], axis=0)

    acc_ref[...] = jnp.zeros_like(acc_ref)

    def step(j, carry, masked):
        m, l = carry
        start = pl.multiple_of(j * tq, tq)
        kt = kt_ref[:, pl.ds(start, tq)]
        vt = v_ref[pl.ds(start, tq), :]
        s = jnp.dot(q2, kt, preferred_element_type=F32)
        if masked:
            row = lax.broadcasted_iota(jnp.int32, s.shape, 0)
            col = lax.broadcasted_iota(jnp.int32, s.shape, 1)
            qpos = jnp.where(row >= tq, row - tq, row)
            s = jnp.where(col <= qpos, s, -jnp.inf)
        m_new = jnp.maximum(m, jnp.max(s, axis=1, keepdims=True))
        alpha = jnp.exp(m - m_new)
        p = jnp.exp(s - m_new)
        l_new = alpha * l + jnp.sum(p, axis=1, keepdims=True)
        acc_ref[...] = alpha * acc_ref[...] + jnp.dot(p.astype(BF16), vt, preferred_element_type=F32)
        return m_new, l_new

    m0 = jnp.full((2 * tq, 1), -jnp.inf, F32)
    l0 = jnp.zeros((2 * tq, 1), F32)
    m, l = lax.fori_loop(0, i, functools.partial(step, masked=False), (m0, l0))
    m, l = step(i, (m, l), True)

    lam = (jnp.exp(jnp.sum(lq1_ref[...] * lk1_ref[...], axis=1, keepdims=True))
           - jnp.exp(jnp.sum(lq2_ref[...] * lk2_ref[...], axis=1, keepdims=True)) + lambda_init)
    o = acc_ref[0:tq, :] / l[0:tq] - lam * (acc_ref[tq:, :] / l[tq:])
    ms = jnp.mean(o * o, axis=1, keepdims=True)
    o = o * lax.rsqrt(ms + EPS) * sg_ref[...] * (1.0 - lambda_init)
    o_ref[...] = o.astype(o_ref.dtype)


def _diff_attention(q, kt, v, lq1, lk1, lq2, lk2, subln_g, *, B, S, H, dh, lambda_init, tq):
    aw = H * 2 * dh
    q3 = q.reshape(B, S, aw)
    v3 = v.reshape(B, S, aw)
    hd = 2 * dh
    kern = functools.partial(_attn_kernel, tq=tq, dh=dh, lambda_init=lambda_init)
    vec = pl.BlockSpec((1, dh), lambda b, h, i: (0, 0))
    out = pl.pallas_call(
        kern,
        grid=(B, H, S // tq),
        in_specs=[pl.BlockSpec((None, tq, hd), lambda b, h, i: (b, i, h)),
                  pl.BlockSpec((None, hd, S), lambda b, h, i: (b, h, 0)),
                  pl.BlockSpec((None, S, hd), lambda b, h, i: (b, 0, h)),
                  vec, vec, vec, vec,
                  pl.BlockSpec((1, hd), lambda b, h, i: (0, 0))],
        out_specs=pl.BlockSpec((None, tq, hd), lambda b, h, i: (b, i, h)),
        out_shape=jax.ShapeDtypeStruct((B, S, aw), BF16),
        scratch_shapes=[pltpu.VMEM((2 * tq, hd), F32)],
        compiler_params=_cparams(("parallel", "parallel", "parallel")),
        name="diff_flash_attention",
    )(q3, kt, v3, lq1, lk1, lq2, lk2, subln_g)
    return out.reshape(B * S, aw)


def _mixout_kernel(attn_ref, ucur_ref, uhalo_ref, x_ref, mod_ref, wdw_ref, bdw_ref, lng_ref, lnb_ref,
                   wo1_ref, wo2_ref, gffn_ref, wrh_ref, wrl_ref, br_ref,
                   x1_ref, h2_ref, comb_ref, ubuf_ref, conv_ref,
                   *, nS, conv_k, n_exp, n_grp):
    T = x_ref.shape[0]
    i = pl.program_id(0)
    first = (i % nS) == 0
    halo = uhalo_ref[...].astype(F32)
    ubuf_ref[0:CONV_HALO, :] = jnp.where(first, jnp.zeros_like(halo), halo)
    ubuf_ref[CONV_HALO:, :] = ucur_ref[...].astype(F32)

    off = CONV_HALO - (conv_k - 1)
    rows = 32
    for r0 in range(0, T, rows):
        acc = jnp.zeros((rows, ucur_ref.shape[1]), F32)
        for j in range(conv_k):
            acc = acc + wdw_ref[j:j + 1, :] * ubuf_ref[r0 + j + off:r0 + j + off + rows, :]
        y = acc + bdw_ref[...]
        mu = jnp.mean(y, axis=1, keepdims=True)
        d = y - mu
        var = jnp.mean(d * d, axis=1, keepdims=True)
        z = d * lax.rsqrt(var + EPS) * lng_ref[...] + lnb_ref[...]
        conv_ref[r0:r0 + rows, :] = _silu(z).astype(BF16)

    yo = (jnp.dot(attn_ref[...], wo1_ref[...], preferred_element_type=F32)
          + jnp.dot(conv_ref[...], wo2_ref[...], preferred_element_type=F32))
    x1 = x_ref[...] + mod_ref[2:3, :] * yo
    x1_ref[...] = x1
    ms = jnp.mean(x1 * x1, axis=1, keepdims=True)
    h2 = x1 * lax.rsqrt(ms + EPS) * gffn_ref[...] * (1.0 + mod_ref[4:5, :]) + mod_ref[3:4, :]
    hi = h2.astype(BF16)
    h2_ref[...] = hi
    lo = (h2 - hi.astype(F32)).astype(BF16)

    logits = (jnp.dot(hi, wrh_ref[...], preferred_element_type=F32)
              + jnp.dot(lo, wrh_ref[...], preferred_element_type=F32)
              + jnp.dot(hi, wrl_ref[...], preferred_element_type=F32)) + br_ref[...]
    lane_i = lax.broadcasted_iota(jnp.int32, logits.shape, 1)
    lane = lane_i.astype(F32)
    big = jnp.float32(1e9)
    ninf = jnp.float32(-jnp.inf)
    is_g = (lane_i >= n_exp) & (lane_i < n_exp + n_grp)
    gl = jnp.where(is_g, logits, ninf)
    gmax = jnp.max(gl, axis=1, keepdims=True)
    gsum = jnp.sum(jnp.where(is_g, jnp.exp(gl - gmax), 0.0), axis=1, keepdims=True)
    g_p = 1.0 / gsum
    gidx = jnp.min(jnp.where(gl == gmax, lane, big), axis=1, keepdims=True) - n_exp
    epg = n_exp // n_grp
    lo_l = gidx * epg
    in_grp = (lane >= lo_l) & (lane < lo_l + epg)
    el = jnp.where(in_grp, logits, ninf)
    m1 = jnp.max(el, axis=1, keepdims=True)
    i1 = jnp.min(jnp.where(el == m1, lane, big), axis=1, keepdims=True)
    el2 = jnp.where(lane == i1, ninf, el)
    m2 = jnp.max(el2, axis=1, keepdims=True)
    i2 = jnp.min(jnp.where(el2 == m2, lane, big), axis=1, keepdims=True)
    e2 = jnp.exp(m2 - m1)
    p1 = 1.0 / (1.0 + e2)
    p2 = e2 / (1.0 + e2)
    comb_ref[...] = jnp.where(lane == i1, g_p * p1, 0.0) + jnp.where(lane == i2, g_p * p2, 0.0)


def _mixout(attn, u, x2, mod, w_dw, b_dw, ln_g, ln_b, wo1, wo2, g_ffn, wr_hi, wr_lo, b_r,
            *, S, T, n_exp, n_grp):
    N, D = x2.shape
    aw = attn.shape[1]
    cw = u.shape[1]
    nS = S // T
    conv_k = w_dw.shape[0]
    hb = T // CONV_HALO
    kern = functools.partial(_mixout_kernel, nS=nS, conv_k=conv_k, n_exp=n_exp, n_grp=n_grp)
    full = lambda a: pl.BlockSpec(a.shape, lambda i: (0, 0))
    return pl.pallas_call(
        kern,
        grid=(N // T,),
        in_specs=[pl.BlockSpec((T, aw), lambda i: (i, 0)),
                  pl.BlockSpec((T, cw), lambda i: (i, 0)),
                  pl.BlockSpec((CONV_HALO, cw), lambda i: (jnp.maximum(i * hb - 1, 0), 0)),
                  pl.BlockSpec((T, D), lambda i: (i, 0)),
                  pl.BlockSpec((None, 6, D), lambda i: (i // nS, 0, 0)),
                  full(w_dw), full(b_dw), full(ln_g), full(ln_b), full(wo1), full(wo2), full(g_ffn),
                  full(wr_hi), full(wr_lo), full(b_r)],
        out_specs=[pl.BlockSpec((T, D), lambda i: (i, 0)),
                   pl.BlockSpec((T, D), lambda i: (i, 0)),
                   pl.BlockSpec((T, LANES), lambda i: (i, 0))],
        out_shape=[jax.ShapeDtypeStruct((N, D), F32),
                   jax.ShapeDtypeStruct((N, D), BF16),
                   jax.ShapeDtypeStruct((N, LANES), F32)],
        scratch_shapes=[pltpu.VMEM((CONV_HALO + T, cw), F32),
                        pltpu.VMEM((T, cw), BF16)],
        compiler_params=_cparams(("parallel",)),
        name="conv_outproj_router",
    )(attn, u, u, x2, mod, w_dw, b_dw, ln_g, ln_b, wo1, wo2, g_ffn, wr_hi, wr_lo, b_r)


def _moe_dense_kernel(h2_ref, comb_ref, wgu_ref, wd_ref, x1_ref, mod_ref, o_ref, acc_ref, *, ff):
    e = pl.program_id(1)

    @pl.when(e == 0)
    def _():
        acc_ref[...] = jnp.zeros_like(acc_ref)

    gu = jnp.dot(h2_ref[...], wgu_ref[...], preferred_element_type=F32)
    comb = comb_ref[...]
    lane = lax.broadcasted_iota(jnp.int32, comb.shape, 1)
    ce = jnp.sum(jnp.where(lane == e, comb, 0.0), axis=1, keepdims=True)
    hid = _silu(gu[:, :ff]) * gu[:, ff:] * ce
    acc_ref[...] += jnp.dot(hid.astype(BF16), wd_ref[...], preferred_element_type=F32)

    @pl.when(e == pl.num_programs(1) - 1)
    def _():
        o_ref[...] = x1_ref[...] + mod_ref[5:6, :] * acc_ref[...]


def _moe_dense(h2, comb, wgu, wd, x1, mod, *, S, T):
    N, D = x1.shape
    E, _, ff2 = wgu.shape
    nS = S // T
    kern = functools.partial(_moe_dense_kernel, ff=ff2 // 2)
    return pl.pallas_call(
        kern,
        grid=(N // T, E),
        in_specs=[pl.BlockSpec((T, D), lambda i, e: (i, 0)),
                  pl.BlockSpec((T, LANES), lambda i, e: (i, 0)),
                  pl.BlockSpec((None, D, ff2), lambda i, e: (e, 0, 0)),
                  pl.BlockSpec((None, ff2 // 2, D), lambda i, e: (e, 0, 0)),
                  pl.BlockSpec((T, D), lambda i, e: (i, 0)),
                  pl.BlockSpec((None, 6, D), lambda i, e: (i // nS, 0, 0))],
        out_specs=pl.BlockSpec((T, D), lambda i, e: (i, 0)),
        out_shape=jax.ShapeDtypeStruct((N, D), F32),
        scratch_shapes=[pltpu.VMEM((T, D), F32)],
        compiler_params=_cparams(("parallel", "arbitrary")),
        name="moe_dense",
    )(h2, comb, wgu, wd, x1, mod)


def _layer(x2, mod, pos_row, l, B, S, g_mix, w_in, q_norm_g, k_norm_g, lambda_q1, lambda_k1, lambda_q2,
           lambda_k2, subln_g, b_glu, w_dw, b_dw, conv_ln_g, conv_ln_b, w_out, g_ffn, w_group, b_group,
           w_router, b_router, w_gate, w_up, w_down):
    N, D = x2.shape
    dh = q_norm_g.shape[0]
    H = N_DIFF_HEADS
    aw = H * 2 * dh
    cw = w_dw.shape[1]
    rot = dh // 4
    n_grp = w_group.shape[1]
    n_exp = w_router.shape[1]
    lambda_init = 0.8 - 0.6 * math.exp(-0.3 * l)

    w_qk_t = w_in[:, :2 * aw].T.astype(BF16)
    w_vglu = w_in[:, 2 * aw:].astype(BF16)
    scale = dh ** -0.5
    gq = jnp.tile(q_norm_g * scale, aw // dh)
    gk = jnp.tile(k_norm_g, aw // dh)
    gqk_tab = jnp.broadcast_to(jnp.concatenate([gq, gk])[:, None], (2 * aw, LANES))
    inv_freq = ROPE_THETA ** (-jnp.arange(0, rot, 2, dtype=F32) / rot)
    invf_tab = jnp.broadcast_to(inv_freq[:, None], (rot // 2, LANES))

    q, kt, v, u = _inproj(x2, mod, g_mix.reshape(1, D), pos_row, w_vglu, w_qk_t, b_glu.reshape(1, 2 * cw),
                          gqk_tab, invf_tab, B=B, S=S, aw=aw, cw=cw, dh=dh, rot=rot, T=256)

    attn = _diff_attention(q, kt, v, lambda_q1.reshape(1, dh), lambda_k1.reshape(1, dh),
                           lambda_q2.reshape(1, dh), lambda_k2.reshape(1, dh), subln_g.reshape(1, 2 * dh),
                           B=B, S=S, H=H, dh=dh, lambda_init=lambda_init, tq=256)

    w_r = jnp.zeros((D, LANES), F32).at[:, :n_exp].set(w_router).at[:, n_exp:n_exp + n_grp].set(w_group)
    b_r = jnp.zeros((1, LANES), F32).at[0, :n_exp].set(b_router).at[0, n_exp:n_exp + n_grp].set(b_group)
    wr_hi = w_r.astype(BF16)
    wr_lo = (w_r - wr_hi.astype(F32)).astype(BF16)
    wo = w_out.astype(BF16)
    x1, h2, comb = _mixout(attn, u, x2, mod, w_dw, b_dw.reshape(1, cw), conv_ln_g.reshape(1, cw),
                           conv_ln_b.reshape(1, cw), wo[:aw], wo[aw:], g_ffn.reshape(1, D), wr_hi, wr_lo, b_r,
                           S=S, T=256, n_exp=n_exp, n_grp=n_grp)

    wgu = jnp.concatenate([w_gate, w_up], axis=-1).astype(BF16)
    wd = w_down.astype(BF16)
    return _moe_dense(h2, comb, wgu, wd, x1, mod, S=S, T=512)


def kernel(x, c, positions, w_ada, b_ada, g_mix, w_in, q_norm_g, k_norm_g, lambda_q1, lambda_k1, lambda_q2,
           lambda_k2, subln_g, b_glu, w_dw, b_dw, conv_ln_g, conv_ln_b, w_out, g_ffn, w_group, b_group,
           w_router, b_router, w_gate, w_up, w_down):
    B, S, D = x.shape
    depth = w_ada.shape[0]
    x2 = x.reshape(B * S, D)
    pos_row = positions.astype(F32).reshape(1, B * S)
    for l in range(depth):
        mod = _modulation(c, w_ada[l], b_ada[l])
        x2 = _layer(x2, mod, pos_row, l, B, S, g_mix[l], w_in[l], q_norm_g[l], k_norm_g[l], lambda_q1[l],
                    lambda_k1[l], lambda_q2[l], lambda_k2[l], subln_g[l], b_glu[l], w_dw[l], b_dw[l],
                    conv_ln_g[l], conv_ln_b[l], w_out[l], g_ffn[l], w_group[l], b_group[l], w_router[l],
                    b_router[l], w_gate[l], w_up[l], w_down[l])
    return x2.reshape(B, S, D)
```

```python
import functools
import math

import numpy as np
import jax
import jax.numpy as jnp
from jax import lax
from jax.experimental import pallas as pl
from jax.experimental.pallas import tpu as pltpu

F32 = jnp.float32
BF16 = jnp.bfloat16

EPS = 1e-6
ROPE_THETA = 500000.0
N_DIFF_HEADS = 4
TOP_K_INNER = 2

LANES = 128
CONV_HALO = 32
VMEM_LIMIT = 48 * 1024 * 1024


def _cparams(sem):
    return pltpu.CompilerParams(dimension_semantics=sem, vmem_limit_bytes=VMEM_LIMIT)


def _silu(x):
    return x * jax.nn.sigmoid(x)


def _mod_kernel(c_ref, w_ref, b_ref, o_ref):
    c = c_ref[...]
    o_ref[...] = jnp.dot(_silu(c), w_ref[...], preferred_element_type=F32,
                         precision=lax.Precision.HIGHEST) + b_ref[...]


def _modulation(c, w_ada, b_ada):
    B, D = c.shape
    n_out = w_ada.shape[1]
    rows = 8
    c_pad = jnp.pad(c, ((0, rows - B), (0, 0)))
    bn = 1024
    out = pl.pallas_call(
        _mod_kernel,
        grid=(n_out // bn,),
        in_specs=[pl.BlockSpec((rows, D), lambda j: (0, 0)),
                  pl.BlockSpec((D, bn), lambda j: (0, j)),
                  pl.BlockSpec((1, bn), lambda j: (0, j))],
        out_specs=pl.BlockSpec((rows, bn), lambda j: (0, j)),
        out_shape=jax.ShapeDtypeStruct((rows, n_out), F32),
        compiler_params=_cparams(("parallel",)),
        name="adaln_mod",
    )(c_pad, w_ada, b_ada.reshape(1, n_out))
    return out[:B].reshape(B, 6, D)


def _inproj_kernel(x_ref, mod_ref, g_ref, pos_ref, wv_ref, wqk_ref, bglu_ref, gqk_ref, invf_ref,
                   q_ref, kt_ref, v_ref, u_ref, *, aw, cw, dh, rot):
    T = x_ref.shape[0]
    half = rot // 2
    x = x_ref[...]
    ms = jnp.mean(x * x, axis=-1, keepdims=True)
    sh = mod_ref[0:1, :]
    sc = mod_ref[1:2, :]
    h = x * lax.rsqrt(ms + EPS) * g_ref[...] * (1.0 + sc) + sh
    hb = h.astype(BF16)

    pv = jnp.dot(hb, wv_ref[...], preferred_element_type=F32)
    v_ref[...] = pv[:, :aw].astype(BF16)
    a = pv[:, aw:aw + cw] + bglu_ref[:, :cw]
    gate = pv[:, aw + cw:] + bglu_ref[:, cw:]
    u_ref[...] = (a * jax.nn.sigmoid(gate)).astype(BF16)

    qkt = lax.dot_general(wqk_ref[...], hb, (((1,), (1,)), ((), ())), preferred_element_type=F32)
    nch = 2 * aw // dh
    for c in range(T // LANES):
        sl = slice(c * LANES, (c + 1) * LANES)
        s3 = qkt[:, sl].reshape(nch, dh, LANES)
        ssq = jnp.mean(s3 * s3, axis=1, keepdims=True)
        y = s3 * lax.rsqrt(ssq + EPS) * gqk_ref[...].reshape(nch, dh, LANES)
        ang = invf_ref[...] * pos_ref[:, sl]
        cs = jnp.cos(ang)
        sn = jnp.sin(ang)
        t1 = y[:, 0:half, :]
        t2 = y[:, half:rot, :]
        y = jnp.concatenate([t1 * cs - t2 * sn, t2 * cs + t1 * sn, y[:, rot:, :]], axis=1)
        y2 = y.reshape(2 * aw, LANES)
        kt_ref[:, sl] = y2[aw:].astype(BF16)
        q_ref[sl, :] = y2[:aw].T.astype(BF16)


def _inproj(x2, mod, g_mix, pos_row, w_vglu, w_qk_t, b_glu, gqk_tab, invf_tab, *, B, S, aw, cw, dh, rot, T):
    N, D = x2.shape
    nS = S // T
    kern = functools.partial(_inproj_kernel, aw=aw, cw=cw, dh=dh, rot=rot)
    return pl.pallas_call(
        kern,
        grid=(N // T,),
        in_specs=[pl.BlockSpec((T, D), lambda i: (i, 0)),
                  pl.BlockSpec((None, 6, D), lambda i: (i // nS, 0, 0)),
                  pl.BlockSpec((1, D), lambda i: (0, 0)),
                  pl.BlockSpec((1, T), lambda i: (0, i)),
                  pl.BlockSpec(w_vglu.shape, lambda i: (0, 0)),
                  pl.BlockSpec(w_qk_t.shape, lambda i: (0, 0)),
                  pl.BlockSpec((1, 2 * cw), lambda i: (0, 0)),
                  pl.BlockSpec(gqk_tab.shape, lambda i: (0, 0)),
                  pl.BlockSpec(invf_tab.shape, lambda i: (0, 0))],
        out_specs=[pl.BlockSpec((T, aw), lambda i: (i, 0)),
                   pl.BlockSpec((None, aw, T), lambda i: (i // nS, 0, i % nS)),
                   pl.BlockSpec((T, aw), lambda i: (i, 0)),
                   pl.BlockSpec((T, cw), lambda i: (i, 0))],
        out_shape=[jax.ShapeDtypeStruct((N, aw), BF16),
                   jax.ShapeDtypeStruct((B, aw, S), BF16),
                   jax.ShapeDtypeStruct((N, aw), BF16),
                   jax.ShapeDtypeStruct((N, cw), BF16)],
        compiler_params=_cparams(("parallel",)),
        name="inproj_qknorm_rope_glu",
    )(x2, mod, g_mix, pos_row, w_vglu, w_qk_t, b_glu, gqk_tab, invf_tab)


def _attn_kernel(q_ref, kt_ref, v_ref, lq1_ref, lk1_ref, lq2_ref, lk2_ref, sg_ref, o_ref, acc_ref,
                 s0_ref, s1_ref, *, tq, tk, dh, lambda_init):
    i = pl.program_id(2)
    hd = 2 * dh
    q = q_ref[...]
    lane = lax.broadcasted_iota(jnp.int32, q.shape, 1)
    zero = jnp.zeros_like(q)
    q2 = jnp.concatenate([jnp.where(lane < dh, q, zero), jnp.where(lane >= dh, q, zero)], axis=0)

    acc_ref[...] = jnp.zeros_like(acc_ref)
    ones_col = (lax.broadcasted_iota(jnp.int32, (tk, hd), 1) == 0).astype(BF16)

    def scores(t, s_ref):
        start = pl.multiple_of(t * tk, tk)
        s_ref[...] = jnp.dot(q2, kt_ref[:, pl.ds(start, tk)], preferred_element_type=F32)

    def softmax_pv(t, s_ref, m, masked):
        start = pl.multiple_of(t * tk, tk)
        vt = jnp.concatenate([v_ref[pl.ds(start, tk), :], ones_col], axis=1)
        s = s_ref[...]
        if masked:
            row = lax.broadcasted_iota(jnp.int32, s.shape, 0)
            col = lax.broadcasted_iota(jnp.int32, s.shape, 1)
            qpos = jnp.where(row >= tq, row - tq, row) + i * tq
            s = jnp.where(col + t * tk <= qpos, s, -jnp.inf)
        m_new = jnp.maximum(m, jnp.max(s, axis=1, keepdims=True))
        alpha = jnp.exp(m - m_new)
        p = jnp.exp(s - m_new).astype(BF16)
        acc_ref[...] = alpha * acc_ref[...] + jnp.dot(p, vt, preferred_element_type=F32)
        return m_new

    n_full = (i * tq) // tk
    m0 = jnp.full((2 * tq, 1), -jnp.inf, F32)
    scores(0, s0_ref)

    def pair(jj, m):
        t = 2 * jj
        scores(t + 1, s1_ref)
        m = softmax_pv(t, s0_ref, m, False)
        scores(t + 2, s0_ref)
        return softmax_pv(t + 1, s1_ref, m, False)

    m = lax.fori_loop(0, n_full // 2, pair, m0)

    @pl.when(n_full % 2 == 1)
    def _():
        scores(n_full, s1_ref)
        m1 = softmax_pv(n_full - 1, s0_ref, m, False)
        softmax_pv(n_full, s1_ref, m1, True)

    @pl.when(n_full % 2 == 0)
    def _():
        softmax_pv(n_full, s0_ref, m, True)

    lam = (jnp.exp(jnp.sum(lq1_ref[...] * lk1_ref[...], axis=1, keepdims=True))
           - jnp.exp(jnp.sum(lq2_ref[...] * lk2_ref[...], axis=1, keepdims=True)) + lambda_init)
    o = (acc_ref[0:tq, 0:hd] / acc_ref[0:tq, hd:hd + 1]
         - lam * (acc_ref[tq:, 0:hd] / acc_ref[tq:, hd:hd + 1]))
    ms = jnp.mean(o * o, axis=1, keepdims=True)
    o = o * lax.rsqrt(ms + EPS) * sg_ref[...] * (1.0 - lambda_init)
    o_ref[...] = o.astype(o_ref.dtype)


def _diff_attention(q, kt, v, lq1, lk1, lq2, lk2, subln_g, *, B, S, H, dh, lambda_init, tq, tk):
    aw = H * 2 * dh
    q3 = q.reshape(B, S, aw)
    v3 = v.reshape(B, S, aw)
    hd = 2 * dh
    tk = min(tk, S)
    kern = functools.partial(_attn_kernel, tq=tq, tk=tk, dh=dh, lambda_init=lambda_init)
    vec = pl.BlockSpec((1, dh), lambda b, h, i: (0, 0))
    out = pl.pallas_call(
        kern,
        grid=(B, H, S // tq),
        in_specs=[pl.BlockSpec((None, tq, hd), lambda b, h, i: (b, i, h)),
                  pl.BlockSpec((None, hd, S), lambda b, h, i: (b, h, 0)),
                  pl.BlockSpec((None, S, hd), lambda b, h, i: (b, 0, h)),
                  vec, vec, vec, vec,
                  pl.BlockSpec((1, hd), lambda b, h, i: (0, 0))],
        out_specs=pl.BlockSpec((None, tq, hd), lambda b, h, i: (b, i, h)),
        out_shape=jax.ShapeDtypeStruct((B, S, aw), BF16),
        scratch_shapes=[pltpu.VMEM((2 * tq, 2 * hd), F32),
                        pltpu.VMEM((2 * tq, tk), F32),
                        pltpu.VMEM((2 * tq, tk), F32)],
        compiler_params=_cparams(("parallel", "parallel", "parallel")),
        name="diff_flash_attention",
    )(q3, kt, v3, lq1, lk1, lq2, lk2, subln_g)
    return out.reshape(B * S, aw)


def _mixout_kernel(attn_ref, ucur_ref, uhalo_ref, x_ref, mod_ref, wdw_ref, bdw_ref, lng_ref, lnb_ref,
                   wo1_ref, wo2_ref, gffn_ref, wrh_ref, wrl_ref, br_ref,
                   x1_ref, h2_ref, comb_ref, ubuf_ref, conv_ref,
                   *, nS, conv_k, n_exp, n_grp):
    T = x_ref.shape[0]
    i = pl.program_id(0)
    first = (i % nS) == 0
    halo = uhalo_ref[...].astype(F32)
    ubuf_ref[0:CONV_HALO, :] = jnp.where(first, jnp.zeros_like(halo), halo)
    ubuf_ref[CONV_HALO:, :] = ucur_ref[...].astype(F32)

    off = CONV_HALO - (conv_k - 1)
    rows = 32
    for r0 in range(0, T, rows):
        acc = jnp.zeros((rows, ucur_ref.shape[1]), F32)
        for j in range(conv_k):
            acc = acc + wdw_ref[j:j + 1, :] * ubuf_ref[r0 + j + off:r0 + j + off + rows, :]
        y = acc + bdw_ref[...]
        mu = jnp.mean(y, axis=1, keepdims=True)
        d = y - mu
        var = jnp.mean(d * d, axis=1, keepdims=True)
        z = d * lax.rsqrt(var + EPS) * lng_ref[...] + lnb_ref[...]
        conv_ref[r0:r0 + rows, :] = _silu(z).astype(BF16)

    yo = (jnp.dot(attn_ref[...], wo1_ref[...], preferred_element_type=F32)
          + jnp.dot(conv_ref[...], wo2_ref[...], preferred_element_type=F32))
    x1 = x_ref[...] + mod_ref[2:3, :] * yo
    x1_ref[...] = x1
    ms = jnp.mean(x1 * x1, axis=1, keepdims=True)
    h2 = x1 * lax.rsqrt(ms + EPS) * gffn_ref[...] * (1.0 + mod_ref[4:5, :]) + mod_ref[3:4, :]
    hi = h2.astype(BF16)
    h2_ref[...] = hi
    lo = (h2 - hi.astype(F32)).astype(BF16)

    logits = (jnp.dot(hi, wrh_ref[...], preferred_element_type=F32)
              + jnp.dot(lo, wrh_ref[...], preferred_element_type=F32)
              + jnp.dot(hi, wrl_ref[...], preferred_element_type=F32)) + br_ref[...]
    lane_i = lax.broadcasted_iota(jnp.int32, logits.shape, 1)
    lane = lane_i.astype(F32)
    big = jnp.float32(1e9)
    ninf = jnp.float32(-jnp.inf)
    is_g = (lane_i >= n_exp) & (lane_i < n_exp + n_grp)
    gl = jnp.where(is_g, logits, ninf)
    gmax = jnp.max(gl, axis=1, keepdims=True)
    gsum = jnp.sum(jnp.where(is_g, jnp.exp(gl - gmax), 0.0), axis=1, keepdims=True)
    g_p = 1.0 / gsum
    gidx = jnp.min(jnp.where(gl == gmax, lane, big), axis=1, keepdims=True) - n_exp
    epg = n_exp // n_grp
    lo_l = gidx * epg
    in_grp = (lane >= lo_l) & (lane < lo_l + epg)
    el = jnp.where(in_grp, logits, ninf)
    m1 = jnp.max(el, axis=1, keepdims=True)
    i1 = jnp.min(jnp.where(el == m1, lane, big), axis=1, keepdims=True)
    el2 = jnp.where(lane == i1, ninf, el)
    m2 = jnp.max(el2, axis=1, keepdims=True)
    i2 = jnp.min(jnp.where(el2 == m2, lane, big), axis=1, keepdims=True)
    e2 = jnp.exp(m2 - m1)
    p1 = 1.0 / (1.0 + e2)
    p2 = e2 / (1.0 + e2)
    comb_ref[...] = jnp.where(lane == i1, g_p * p1, 0.0) + jnp.where(lane == i2, g_p * p2, 0.0)


def _mixout(attn, u, x2, mod, w_dw, b_dw, ln_g, ln_b, wo1, wo2, g_ffn, wr_hi, wr_lo, b_r,
            *, S, T, n_exp, n_grp):
    N, D = x2.shape
    aw = attn.shape[1]
    cw = u.shape[1]
    nS = S // T
    conv_k = w_dw.shape[0]
    hb = T // CONV_HALO
    kern = functools.partial(_mixout_kernel, nS=nS, conv_k=conv_k, n_exp=n_exp, n_grp=n_grp)
    full = lambda a: pl.BlockSpec(a.shape, lambda i: (0, 0))
    return pl.pallas_call(
        kern,
        grid=(N // T,),
        in_specs=[pl.BlockSpec((T, aw), lambda i: (i, 0)),
                  pl.BlockSpec((T, cw), lambda i: (i, 0)),
                  pl.BlockSpec((CONV_HALO, cw), lambda i: (jnp.maximum(i * hb - 1, 0), 0)),
                  pl.BlockSpec((T, D), lambda i: (i, 0)),
                  pl.BlockSpec((None, 6, D), lambda i: (i // nS, 0, 0)),
                  full(w_dw), full(b_dw), full(ln_g), full(ln_b), full(wo1), full(wo2), full(g_ffn),
                  full(wr_hi), full(wr_lo), full(b_r)],
        out_specs=[pl.BlockSpec((T, D), lambda i: (i, 0)),
                   pl.BlockSpec((T, D), lambda i: (i, 0)),
                   pl.BlockSpec((T, LANES), lambda i: (i, 0))],
        out_shape=[jax.ShapeDtypeStruct((N, D), F32),
                   jax.ShapeDtypeStruct((N, D), BF16),
                   jax.ShapeDtypeStruct((N, LANES), F32)],
        scratch_shapes=[pltpu.VMEM((CONV_HALO + T, cw), F32),
                        pltpu.VMEM((T, cw), BF16)],
        compiler_params=_cparams(("parallel",)),
        name="conv_outproj_router",
    )(attn, u, u, x2, mod, w_dw, b_dw, ln_g, ln_b, wo1, wo2, g_ffn, wr_hi, wr_lo, b_r)


def _moe_dense_kernel(h2_ref, comb_ref, wgu_ref, wd_ref, x1_ref, mod_ref, o_ref, acc_ref, *, ff):
    e = pl.program_id(1)

    @pl.when(e == 0)
    def _():
        acc_ref[...] = jnp.zeros_like(acc_ref)

    gu = jnp.dot(h2_ref[...], wgu_ref[...], preferred_element_type=F32)
    comb = comb_ref[...]
    lane = lax.broadcasted_iota(jnp.int32, comb.shape, 1)
    ce = jnp.sum(jnp.where(lane == e, comb, 0.0), axis=1, keepdims=True)
    hid = _silu(gu[:, :ff]) * gu[:, ff:] * ce
    acc_ref[...] += jnp.dot(hid.astype(BF16), wd_ref[...], preferred_element_type=F32)

    @pl.when(e == pl.num_programs(1) - 1)
    def _():
        o_ref[...] = x1_ref[...] + mod_ref[5:6, :] * acc_ref[...]


def _moe_dense(h2, comb, wgu, wd, x1, mod, *, S, T):
    N, D = x1.shape
    E, _, ff2 = wgu.shape
    nS = S // T
    kern = functools.partial(_moe_dense_kernel, ff=ff2 // 2)
    return pl.pallas_call(
        kern,
        grid=(N // T, E),
        in_specs=[pl.BlockSpec((T, D), lambda i, e: (i, 0)),
                  pl.BlockSpec((T, LANES), lambda i, e: (i, 0)),
                  pl.BlockSpec((None, D, ff2), lambda i, e: (e, 0, 0)),
                  pl.BlockSpec((None, ff2 // 2, D), lambda i, e: (e, 0, 0)),
                  pl.BlockSpec((T, D), lambda i, e: (i, 0)),
                  pl.BlockSpec((None, 6, D), lambda i, e: (i // nS, 0, 0))],
        out_specs=pl.BlockSpec((T, D), lambda i, e: (i, 0)),
        out_shape=jax.ShapeDtypeStruct((N, D), F32),
        scratch_shapes=[pltpu.VMEM((T, D), F32)],
        compiler_params=_cparams(("parallel", "arbitrary")),
        name="moe_dense",
    )(h2, comb, wgu, wd, x1, mod)


def _layer(x2, mod, pos_row, l, B, S, g_mix, w_in, q_norm_g, k_norm_g, lambda_q1, lambda_k1, lambda_q2,
           lambda_k2, subln_g, b_glu, w_dw, b_dw, conv_ln_g, conv_ln_b, w_out, g_ffn, w_group, b_group,
           w_router, b_router, w_gate, w_up, w_down):
    N, D = x2.shape
    dh = q_norm_g.shape[0]
    H = N_DIFF_HEADS
    aw = H * 2 * dh
    cw = w_dw.shape[1]
    rot = dh // 4
    n_grp = w_group.shape[1]
    n_exp = w_router.shape[1]
    lambda_init = 0.8 - 0.6 * math.exp(-0.3 * l)

    w_qk_t = w_in[:, :2 * aw].T.astype(BF16)
    w_vglu = w_in[:, 2 * aw:].astype(BF16)
    scale = dh ** -0.5
    gq = jnp.tile(q_norm_g * scale, aw // dh)
    gk = jnp.tile(k_norm_g, aw // dh)
    gqk_tab = jnp.broadcast_to(jnp.concatenate([gq, gk])[:, None], (2 * aw, LANES))
    inv_freq = ROPE_THETA ** (-jnp.arange(0, rot, 2, dtype=F32) / rot)
    invf_tab = jnp.broadcast_to(inv_freq[:, None], (rot // 2, LANES))

    q, kt, v, u = _inproj(x2, mod, g_mix.reshape(1, D), pos_row, w_vglu, w_qk_t, b_glu.reshape(1, 2 * cw),
                          gqk_tab, invf_tab, B=B, S=S, aw=aw, cw=cw, dh=dh, rot=rot, T=256)

    attn = _diff_attention(q, kt, v, lambda_q1.reshape(1, dh), lambda_k1.reshape(1, dh),
                           lambda_q2.reshape(1, dh), lambda_k2.reshape(1, dh), subln_g.reshape(1, 2 * dh),
                           B=B, S=S, H=H, dh=dh, lambda_init=lambda_init, tq=256, tk=1024)

    w_r = jnp.zeros((D, LANES), F32).at[:, :n_exp].set(w_router).at[:, n_exp:n_exp + n_grp].set(w_group)
    b_r = jnp.zeros((1, LANES), F32).at[0, :n_exp].set(b_router).at[0, n_exp:n_exp + n_grp].set(b_group)
    wr_hi = w_r.astype(BF16)
    wr_lo = (w_r - wr_hi.astype(F32)).astype(BF16)
    wo = w_out.astype(BF16)
    x1, h2, comb = _mixout(attn, u, x2, mod, w_dw, b_dw.reshape(1, cw), conv_ln_g.reshape(1, cw),
                           conv_ln_b.reshape(1, cw), wo[:aw], wo[aw:], g_ffn.reshape(1, D), wr_hi, wr_lo, b_r,
                           S=S, T=256, n_exp=n_exp, n_grp=n_grp)

    wgu = jnp.concatenate([w_gate, w_up], axis=-1).astype(BF16)
    wd = w_down.astype(BF16)
    return _moe_dense(h2, comb, wgu, wd, x1, mod, S=S, T=512)


def kernel(x, c, positions, w_ada, b_ada, g_mix, w_in, q_norm_g, k_norm_g, lambda_q1, lambda_k1, lambda_q2,
           lambda_k2, subln_g, b_glu, w_dw, b_dw, conv_ln_g, conv_ln_b, w_out, g_ffn, w_group, b_group,
           w_router, b_router, w_gate, w_up, w_down):
    B, S, D = x.shape
    depth = w_ada.shape[0]
    x2 = x.reshape(B * S, D)
    pos_row = positions.astype(F32).reshape(1, B * S)
    for l in range(depth):
        mod = _modulation(c, w_ada[l], b_ada[l])
        x2 = _layer(x2, mod, pos_row, l, B, S, g_mix[l], w_in[l], q_norm_g[l], k_norm_g[l], lambda_q1[l],
                    lambda_k1[l], lambda_q2[l], lambda_k2[l], subln_g[l], b_glu[l], w_dw[l], b_dw[l],
                    conv_ln_g[l], conv_ln_b[l], w_out[l], g_ffn[l], w_group[l], b_group[l], w_router[l],
                    b_router[l], w_gate[l], w_up[l], w_down[l])
    return x2.reshape(B, S, D)
```

```python
import functools
import math

import numpy as np
import jax
import jax.numpy as jnp
from jax import lax
from jax.experimental import pallas as pl
from jax.experimental.pallas import tpu as pltpu

F32 = jnp.float32
BF16 = jnp.bfloat16

EPS = 1e-6
ROPE_THETA = 500000.0
N_DIFF_HEADS = 4
TOP_K_INNER = 2

LANES = 128
SUBLANES = 8
CONV_HALO = 32
VMEM_LIMIT = 48 * 1024 * 1024


def _cparams(sem):
    return pltpu.CompilerParams(dimension_semantics=sem, vmem_limit_bytes=VMEM_LIMIT)


def _silu(x):
    return x * jax.nn.sigmoid(x)


def _mod_kernel(c_ref, w_ref, b_ref, o_ref):
    c = c_ref[...]
    o_ref[...] = jnp.dot(_silu(c), w_ref[...], preferred_element_type=F32,
                         precision=lax.Precision.HIGHEST) + b_ref[...]


def _modulation(c, w_ada, b_ada):
    B, D = c.shape
    n_out = w_ada.shape[1]
    rows = 8
    c_pad = jnp.pad(c, ((0, rows - B), (0, 0)))
    bn = 1024
    out = pl.pallas_call(
        _mod_kernel,
        grid=(n_out // bn,),
        in_specs=[pl.BlockSpec((rows, D), lambda j: (0, 0)),
                  pl.BlockSpec((D, bn), lambda j: (0, j)),
                  pl.BlockSpec((1, bn), lambda j: (0, j))],
        out_specs=pl.BlockSpec((rows, bn), lambda j: (0, j)),
        out_shape=jax.ShapeDtypeStruct((rows, n_out), F32),
        compiler_params=_cparams(("parallel",)),
        name="adaln_mod",
    )(c_pad, w_ada, b_ada.reshape(1, n_out))
    return out[:B].reshape(B, 6, D)


def _inproj_kernel(x_ref, mod_ref, g_ref, pos_ref, wv_ref, wqk_ref, bglu_ref, gqk_ref, invf_ref,
                   q_ref, kt_ref, v_ref, u_ref, *, aw, cw, dh, rot):
    T = x_ref.shape[0]
    half = rot // 2
    x = x_ref[...]
    ms = jnp.mean(x * x, axis=-1, keepdims=True)
    sh = mod_ref[0:1, :]
    sc = mod_ref[1:2, :]
    h = x * lax.rsqrt(ms + EPS) * g_ref[...] * (1.0 + sc) + sh
    hb = h.astype(BF16)

    pv = jnp.dot(hb, wv_ref[...], preferred_element_type=F32)
    v_ref[...] = pv[:, :aw].astype(BF16)
    a = pv[:, aw:aw + cw] + bglu_ref[:, :cw]
    gate = pv[:, aw + cw:] + bglu_ref[:, cw:]
    u_ref[...] = (a * jax.nn.sigmoid(gate)).astype(BF16)

    qkt = lax.dot_general(wqk_ref[...], hb, (((1,), (1,)), ((), ())), preferred_element_type=F32)
    nch = 2 * aw // dh
    for c in range(T // LANES):
        sl = slice(c * LANES, (c + 1) * LANES)
        s3 = qkt[:, sl].reshape(nch, dh, LANES)
        ssq = jnp.mean(s3 * s3, axis=1, keepdims=True)
        y = s3 * lax.rsqrt(ssq + EPS) * gqk_ref[...].reshape(nch, dh, LANES)
        ang = invf_ref[...] * pos_ref[:, sl]
        cs = jnp.cos(ang)
        sn = jnp.sin(ang)
        t1 = y[:, 0:half, :]
        t2 = y[:, half:rot, :]
        y = jnp.concatenate([t1 * cs - t2 * sn, t2 * cs + t1 * sn, y[:, rot:, :]], axis=1)
        y2 = y.reshape(2 * aw, LANES)
        kt_ref[:, sl] = y2[aw:].astype(BF16)
        q_ref[sl, :] = y2[:aw].T.astype(BF16)


def _inproj(x2, mod, g_mix, pos_row, w_vglu, w_qk_t, b_glu, gqk_tab, invf_tab, *, B, S, aw, cw, dh, rot, T):
    N, D = x2.shape
    nS = S // T
    kern = functools.partial(_inproj_kernel, aw=aw, cw=cw, dh=dh, rot=rot)
    return pl.pallas_call(
        kern,
        grid=(N // T,),
        in_specs=[pl.BlockSpec((T, D), lambda i: (i, 0)),
                  pl.BlockSpec((None, 6, D), lambda i: (i // nS, 0, 0)),
                  pl.BlockSpec((1, D), lambda i: (0, 0)),
                  pl.BlockSpec((1, T), lambda i: (0, i)),
                  pl.BlockSpec(w_vglu.shape, lambda i: (0, 0)),
                  pl.BlockSpec(w_qk_t.shape, lambda i: (0, 0)),
                  pl.BlockSpec((1, 2 * cw), lambda i: (0, 0)),
                  pl.BlockSpec(gqk_tab.shape, lambda i: (0, 0)),
                  pl.BlockSpec(invf_tab.shape, lambda i: (0, 0))],
        out_specs=[pl.BlockSpec((T, aw), lambda i: (i, 0)),
                   pl.BlockSpec((None, aw, T), lambda i: (i // nS, 0, i % nS)),
                   pl.BlockSpec((T, aw), lambda i: (i, 0)),
                   pl.BlockSpec((T, cw), lambda i: (i, 0))],
        out_shape=[jax.ShapeDtypeStruct((N, aw), BF16),
                   jax.ShapeDtypeStruct((B, aw, S), BF16),
                   jax.ShapeDtypeStruct((N, aw), BF16),
                   jax.ShapeDtypeStruct((N, cw), BF16)],
        compiler_params=_cparams(("parallel",)),
        name="inproj_qknorm_rope_glu",
    )(x2, mod, g_mix, pos_row, w_vglu, w_qk_t, b_glu, gqk_tab, invf_tab)


def _attn_kernel(q_ref, kt_ref, v_ref, lq1_ref, lk1_ref, lq2_ref, lk2_ref, sg_ref, o_ref, acc_ref,
                 s0_ref, s1_ref, *, tq, tk, dh, lambda_init):
    i = pl.program_id(2)
    hd = 2 * dh
    q = q_ref[...]
    lane = lax.broadcasted_iota(jnp.int32, q.shape, 1)
    zero = jnp.zeros_like(q)
    q2 = jnp.concatenate([jnp.where(lane < dh, q, zero), jnp.where(lane >= dh, q, zero)], axis=0)

    acc_ref[...] = jnp.zeros_like(acc_ref)
    ones_col = (lax.broadcasted_iota(jnp.int32, (tk, hd), 1) == 0).astype(BF16)

    def scores(t, s_ref):
        start = pl.multiple_of(t * tk, tk)
        s_ref[...] = jnp.dot(q2, kt_ref[:, pl.ds(start, tk)], preferred_element_type=F32)

    def softmax_pv(t, s_ref, m, masked):
        start = pl.multiple_of(t * tk, tk)
        vt = jnp.concatenate([v_ref[pl.ds(start, tk), :], ones_col], axis=1)
        s = s_ref[...]
        if masked:
            row = lax.broadcasted_iota(jnp.int32, s.shape, 0)
            col = lax.broadcasted_iota(jnp.int32, s.shape, 1)
            qpos = jnp.where(row >= tq, row - tq, row) + i * tq
            s = jnp.where(col + t * tk <= qpos, s, -jnp.inf)
        m_new = jnp.maximum(m, jnp.max(s, axis=1, keepdims=True))
        alpha = jnp.exp(m - m_new)
        p = jnp.exp(s - m_new).astype(BF16)
        acc_ref[...] = alpha * acc_ref[...] + jnp.dot(p, vt, preferred_element_type=F32)
        return m_new

    n_full = (i * tq) // tk
    m0 = jnp.full((2 * tq, 1), -jnp.inf, F32)
    scores(0, s0_ref)

    def pair(jj, m):
        t = 2 * jj
        scores(t + 1, s1_ref)
        m = softmax_pv(t, s0_ref, m, False)
        scores(t + 2, s0_ref)
        return softmax_pv(t + 1, s1_ref, m, False)

    m = lax.fori_loop(0, n_full // 2, pair, m0)

    @pl.when(n_full % 2 == 1)
    def _():
        scores(n_full, s1_ref)
        m1 = softmax_pv(n_full - 1, s0_ref, m, False)
        softmax_pv(n_full, s1_ref, m1, True)

    @pl.when(n_full % 2 == 0)
    def _():
        softmax_pv(n_full, s0_ref, m, True)

    lam = (jnp.exp(jnp.sum(lq1_ref[...] * lk1_ref[...], axis=1, keepdims=True))
           - jnp.exp(jnp.sum(lq2_ref[...] * lk2_ref[...], axis=1, keepdims=True)) + lambda_init)
    o = (acc_ref[0:tq, 0:hd] / acc_ref[0:tq, hd:hd + 1]
         - lam * (acc_ref[tq:, 0:hd] / acc_ref[tq:, hd:hd + 1]))
    ms = jnp.mean(o * o, axis=1, keepdims=True)
    o = o * lax.rsqrt(ms + EPS) * sg_ref[...] * (1.0 - lambda_init)
    o_ref[...] = o.astype(o_ref.dtype)


def _diff_attention(q, kt, v, lq1, lk1, lq2, lk2, subln_g, *, B, S, H, dh, lambda_init, tq, tk):
    aw = H * 2 * dh
    q3 = q.reshape(B, S, aw)
    v3 = v.reshape(B, S, aw)
    hd = 2 * dh
    tk = min(tk, S)
    kern = functools.partial(_attn_kernel, tq=tq, tk=tk, dh=dh, lambda_init=lambda_init)
    vec = pl.BlockSpec((1, dh), lambda b, h, i: (0, 0))
    out = pl.pallas_call(
        kern,
        grid=(B, H, S // tq),
        in_specs=[pl.BlockSpec((None, tq, hd), lambda b, h, i: (b, i, h)),
                  pl.BlockSpec((None, hd, S), lambda b, h, i: (b, h, 0)),
                  pl.BlockSpec((None, S, hd), lambda b, h, i: (b, 0, h)),
                  vec, vec, vec, vec,
                  pl.BlockSpec((1, hd), lambda b, h, i: (0, 0))],
        out_specs=pl.BlockSpec((None, tq, hd), lambda b, h, i: (b, i, h)),
        out_shape=jax.ShapeDtypeStruct((B, S, aw), BF16),
        scratch_shapes=[pltpu.VMEM((2 * tq, 2 * hd), F32),
                        pltpu.VMEM((2 * tq, tk), F32),
                        pltpu.VMEM((2 * tq, tk), F32)],
        compiler_params=_cparams(("parallel", "parallel", "parallel")),
        name="diff_flash_attention",
    )(q3, kt, v3, lq1, lk1, lq2, lk2, subln_g)
    return out.reshape(B * S, aw)


def _mixout_kernel(attn_ref, ucur_ref, uhalo_ref, x_ref, mod_ref, wdw_ref, bdw_ref, lng_ref, lnb_ref,
                   wo1_ref, wo2_ref, gffn_ref, wrh_ref, wrl_ref, br_ref,
                   x1_ref, h2_ref, route_ref, cnt_ref, ubuf_ref, conv_ref,
                   *, nS, conv_k, n_exp, n_grp):
    T = x_ref.shape[0]
    i = pl.program_id(0)

    @pl.when(i == 0)
    def _():
        cnt_ref[...] = jnp.zeros_like(cnt_ref)

    first = (i % nS) == 0
    halo = uhalo_ref[...].astype(F32)
    ubuf_ref[0, 0:CONV_HALO, :] = jnp.where(first, jnp.zeros_like(halo), halo)
    ubuf_ref[0, CONV_HALO:, :] = ucur_ref[...].astype(F32)
    span = T + CONV_HALO - SUBLANES
    for b in range(1, SUBLANES):
        ubuf_ref[b, 0:span, :] = ubuf_ref[0, b:b + span, :]

    off = CONV_HALO - (conv_k - 1)
    rows = 32
    for r0 in range(0, T, rows):
        acc = jnp.zeros((rows, ucur_ref.shape[1]), F32)
        for j in range(conv_k):
            a, b = divmod(j + off, SUBLANES)
            lo_r = r0 + a * SUBLANES
            acc = acc + wdw_ref[j:j + 1, :] * ubuf_ref[b, lo_r:lo_r + rows, :]
        y = acc + bdw_ref[...]
        mu = jnp.mean(y, axis=1, keepdims=True)
        d = y - mu
        var = jnp.mean(d * d, axis=1, keepdims=True)
        z = d * lax.rsqrt(var + EPS) * lng_ref[...] + lnb_ref[...]
        conv_ref[r0:r0 + rows, :] = _silu(z).astype(BF16)

    yo = (jnp.dot(attn_ref[...], wo1_ref[...], preferred_element_type=F32)
          + jnp.dot(conv_ref[...], wo2_ref[...], preferred_element_type=F32))
    x1 = x_ref[...] + mod_ref[2:3, :] * yo
    x1_ref[...] = x1
    ms = jnp.mean(x1 * x1, axis=1, keepdims=True)
    h2 = x1 * lax.rsqrt(ms + EPS) * gffn_ref[...] * (1.0 + mod_ref[4:5, :]) + mod_ref[3:4, :]
    h2_ref[...] = h2
    hi = h2.astype(BF16)
    lo = (h2 - hi.astype(F32)).astype(BF16)

    logits = (jnp.dot(hi, wrh_ref[...], preferred_element_type=F32)
              + jnp.dot(lo, wrh_ref[...], preferred_element_type=F32)
              + jnp.dot(hi, wrl_ref[...], preferred_element_type=F32)) + br_ref[...]
    lane_i = lax.broadcasted_iota(jnp.int32, logits.shape, 1)
    lane = lane_i.astype(F32)
    big = jnp.float32(1e9)
    ninf = jnp.float32(-jnp.inf)
    is_g = (lane_i >= n_exp) & (lane_i < n_exp + n_grp)
    gl = jnp.where(is_g, logits, ninf)
    gmax = jnp.max(gl, axis=1, keepdims=True)
    gsum = jnp.sum(jnp.where(is_g, jnp.exp(gl - gmax), 0.0), axis=1, keepdims=True)
    g_p = 1.0 / gsum
    gidx = jnp.min(jnp.where(gl == gmax, lane, big), axis=1, keepdims=True) - n_exp
    epg = n_exp // n_grp
    lo_l = gidx * epg
    in_grp = (lane >= lo_l) & (lane < lo_l + epg)
    el = jnp.where(in_grp, logits, ninf)
    m1 = jnp.max(el, axis=1, keepdims=True)
    i1 = jnp.min(jnp.where(el == m1, lane, big), axis=1, keepdims=True)
    el2 = jnp.where(lane == i1, ninf, el)
    m2 = jnp.max(el2, axis=1, keepdims=True)
    i2 = jnp.min(jnp.where(el2 == m2, lane, big), axis=1, keepdims=True)
    e2 = jnp.exp(m2 - m1)
    p1 = 1.0 / (1.0 + e2)
    p2 = e2 / (1.0 + e2)

    sel = (lane == i1) | (lane == i2)
    rr = lax.broadcasted_iota(jnp.int32, (T, T), 0)
    cc = lax.broadcasted_iota(jnp.int32, (T, T), 1)
    before = jnp.dot((rr > cc).astype(BF16), sel.astype(BF16), preferred_element_type=F32) + cnt_ref[...]
    r1 = jnp.sum(jnp.where(lane == i1, before, 0.0), axis=1, keepdims=True)
    r2 = jnp.sum(jnp.where(lane == i2, before, 0.0), axis=1, keepdims=True)
    cnt_ref[...] += jnp.sum(sel.astype(F32), axis=0, keepdims=True)

    route = jnp.zeros_like(logits)
    for k, val in enumerate((i1, i2, g_p * p1, g_p * p2, r1, r2)):
        route = jnp.where(lane_i == k, val, route)
    route_ref[...] = route


def _mixout(attn, u, x2, mod, w_dw, b_dw, ln_g, ln_b, wo1, wo2, g_ffn, wr_hi, wr_lo, b_r,
            *, S, T, n_exp, n_grp):
    N, D = x2.shape
    aw = attn.shape[1]
    cw = u.shape[1]
    nS = S // T
    conv_k = w_dw.shape[0]
    hb = T // CONV_HALO
    kern = functools.partial(_mixout_kernel, nS=nS, conv_k=conv_k, n_exp=n_exp, n_grp=n_grp)
    full = lambda a: pl.BlockSpec(a.shape, lambda i: (0, 0))
    return pl.pallas_call(
        kern,
        grid=(N // T,),
        in_specs=[pl.BlockSpec((T, aw), lambda i: (i, 0)),
                  pl.BlockSpec((T, cw), lambda i: (i, 0)),
                  pl.BlockSpec((CONV_HALO, cw), lambda i: (jnp.maximum(i * hb - 1, 0), 0)),
                  pl.BlockSpec((T, D), lambda i: (i, 0)),
                  pl.BlockSpec((None, 6, D), lambda i: (i // nS, 0, 0)),
                  full(w_dw), full(b_dw), full(ln_g), full(ln_b), full(wo1), full(wo2), full(g_ffn),
                  full(wr_hi), full(wr_lo), full(b_r)],
        out_specs=[pl.BlockSpec((T, D), lambda i: (i, 0)),
                   pl.BlockSpec((T, D), lambda i: (i, 0)),
                   pl.BlockSpec((T, LANES), lambda i: (i, 0)),
                   pl.BlockSpec((1, LANES), lambda i: (0, 0))],
        out_shape=[jax.ShapeDtypeStruct((N, D), F32),
                   jax.ShapeDtypeStruct((N, D), F32),
                   jax.ShapeDtypeStruct((N, LANES), F32),
                   jax.ShapeDtypeStruct((1, LANES), F32)],
        scratch_shapes=[pltpu.VMEM((SUBLANES, CONV_HALO + T, cw), F32),
                        pltpu.VMEM((T, cw), BF16)],
        compiler_params=_cparams(("arbitrary",)),
        name="conv_outproj_router",
    )(attn, u, u, x2, mod, w_dw, b_dw, ln_g, ln_b, wo1, wo2, g_ffn, wr_hi, wr_lo, b_r)


def _scatter_kernel(dest_ref, h2_hbm, xs_hbm, sem, *, tokens):
    step = pl.program_id(0)
    tok0 = step * tokens

    def issue(t, carry):
        for k in range(TOP_K_INNER):
            pltpu.make_async_copy(h2_hbm.at[pl.ds(tok0 + t, 1)],
                                  xs_hbm.at[pl.ds(dest_ref[0, t * TOP_K_INNER + k], 1)], sem).start()
        return carry

    lax.fori_loop(0, tokens, issue, 0, unroll=8)

    def drain():
        rows = tokens * TOP_K_INNER
        pltpu.make_async_copy(h2_hbm.at[pl.ds(0, rows)], xs_hbm.at[pl.ds(0, rows)], sem).wait()

    @pl.when(step > 0)
    def _():
        drain()

    @pl.when(step == pl.num_programs(0) - 1)
    def _():
        drain()


def _scatter_rows(h2, dest, *, tokens):
    N, C = h2.shape
    steps = N // tokens
    dest3 = dest.reshape(steps, 1, tokens * TOP_K_INNER)
    return pl.pallas_call(
        functools.partial(_scatter_kernel, tokens=tokens),
        grid=(steps,),
        in_specs=[pl.BlockSpec((None, 1, tokens * TOP_K_INNER), lambda s: (s, 0, 0), memory_space=pltpu.SMEM),
                  pl.BlockSpec(memory_space=pl.ANY)],
        out_specs=pl.BlockSpec(memory_space=pl.ANY),
        out_shape=jax.ShapeDtypeStruct((N * TOP_K_INNER, C), h2.dtype),
        scratch_shapes=[pltpu.SemaphoreType.DMA],
        compiler_params=_cparams(("arbitrary",)),
        name="moe_scatter_rows",
    )(dest3, h2)


def _experts_kernel(vt_ref, ve_ref, vlo_ref, vhi_ref, xs_ref, wg_ref, wu_ref, wd_ref, ys_ref):
    v = pl.program_id(0)
    lo = vlo_ref[v]
    hi = vhi_ref[v]

    @pl.when(hi > lo)
    def _():
        x = xs_ref[...].astype(BF16)
        g = jnp.dot(x, wg_ref[...].astype(BF16), preferred_element_type=F32)
        u = jnp.dot(x, wu_ref[...].astype(BF16), preferred_element_type=F32)
        hid = (_silu(g) * u).astype(BF16)
        y = jnp.dot(hid, wd_ref[...].astype(BF16), preferred_element_type=F32)

        @pl.when(lo == 0)
        def _():
            ys_ref[...] = y

        @pl.when(lo > 0)
        def _():
            row = lax.broadcasted_iota(jnp.int32, y.shape, 0)
            ys_ref[...] = jnp.where((row >= lo) & (row < hi), y, ys_ref[...])


def _experts(xs, w_gate, w_up, w_down, visits, *, tm):
    R, C = xs.shape
    E, D, ff = w_gate.shape
    vt, ve, vlo, vhi = visits
    grid_spec = pltpu.PrefetchScalarGridSpec(
        num_scalar_prefetch=4,
        grid=(vt.shape[0],),
        in_specs=[pl.BlockSpec((tm, C), lambda v, vt, ve, vlo, vhi: (vt[v], 0)),
                  pl.BlockSpec((None, D, ff), lambda v, vt, ve, vlo, vhi: (ve[v], 0, 0)),
                  pl.BlockSpec((None, D, ff), lambda v, vt, ve, vlo, vhi: (ve[v], 0, 0)),
                  pl.BlockSpec((None, ff, D), lambda v, vt, ve, vlo, vhi: (ve[v], 0, 0))],
        out_specs=pl.BlockSpec((tm, C), lambda v, vt, ve, vlo, vhi: (vt[v], 0)),
    )
    return pl.pallas_call(
        _experts_kernel,
        grid_spec=grid_spec,
        out_shape=jax.ShapeDtypeStruct((R, C), F32),
        compiler_params=_cparams(("arbitrary",)),
        name="moe_grouped_experts",
    )(vt, ve, vlo, vhi, xs, w_gate, w_up, w_down)


def _visit_tables(off, cnt, n_rows, tm):
    n_tiles = n_rows // tm
    n_exp = off.shape[0]
    n_visits = n_tiles + n_exp - 1
    tile_starts = jnp.arange(n_tiles, dtype=jnp.int32) * tm
    seg_starts = jnp.where((cnt > 0) & (off % tm != 0), off, n_rows)
    starts = jnp.sort(jnp.concatenate([tile_starts, seg_starts]))
    lo_abs = starts[:n_visits]
    hi_abs = starts[1:n_visits + 1]
    valid = lo_abs < n_rows
    tile = jnp.where(valid, lo_abs // tm, n_tiles - 1)
    ends = off + cnt
    probe = jnp.where(valid, lo_abs, n_rows - 1)
    expert = jnp.sum(ends[None, :] <= probe[:, None], axis=1).astype(jnp.int32)
    row_lo = jnp.where(valid, lo_abs - tile * tm, 0)
    row_hi = jnp.where(valid, hi_abs - tile * tm, 0)
    return tile, expert, row_lo, row_hi


def _combine_kernel(dcur_ref, dnxt_ref, ys_hbm, route_ref, x1_ref, mod_ref, o_ref, ybuf, sem):
    T = x1_ref.shape[0]
    i = pl.program_id(0)
    slot = lax.rem(i, 2)

    def gather(d_ref, sl):
        def body(t, carry):
            for k in range(TOP_K_INNER):
                pltpu.make_async_copy(ys_hbm.at[pl.ds(d_ref[0, t * TOP_K_INNER + k], 1)],
                                      ybuf.at[sl, k, pl.ds(t, 1)], sem.at[sl]).start()
            return carry
        lax.fori_loop(0, T, body, 0, unroll=8)

    @pl.when(i == 0)
    def _():
        gather(dcur_ref, 0)

    @pl.when(i + 1 < pl.num_programs(0))
    def _():
        gather(dnxt_ref, 1 - slot)

    for k in range(TOP_K_INNER):
        pltpu.make_async_copy(ys_hbm.at[pl.ds(0, T)], ybuf.at[slot, k], sem.at[slot]).wait()

    route = route_ref[...]
    moe = jnp.zeros(x1_ref.shape, F32)
    for k in range(TOP_K_INNER):
        moe = moe + route[:, TOP_K_INNER + k:TOP_K_INNER + k + 1] * ybuf[slot, k]
    o_ref[...] = x1_ref[...] + mod_ref[5:6, :] * moe


def _combine(ys, dest, route, x1, mod, *, S, T):
    N, D = x1.shape
    C = ys.shape[1]
    steps = N // T
    nS = S // T
    dest3 = dest.reshape(steps, 1, T * TOP_K_INNER)
    dspec = lambda f: pl.BlockSpec((None, 1, T * TOP_K_INNER), f, memory_space=pltpu.SMEM)
    return pl.pallas_call(
        _combine_kernel,
        grid=(steps,),
        in_specs=[dspec(lambda i: (i, 0, 0)),
                  dspec(lambda i: (jnp.minimum(i + 1, steps - 1), 0, 0)),
                  pl.BlockSpec(memory_space=pl.ANY),
                  pl.BlockSpec((T, LANES), lambda i: (i, 0)),
                  pl.BlockSpec((T, D), lambda i: (i, 0)),
                  pl.BlockSpec((None, 6, D), lambda i: (i // nS, 0, 0))],
        out_specs=pl.BlockSpec((T, D), lambda i: (i, 0)),
        out_shape=jax.ShapeDtypeStruct((N, D), F32),
        scratch_shapes=[pltpu.VMEM((2, TOP_K_INNER, T, C), F32),
                        pltpu.SemaphoreType.DMA((2,))],
        compiler_params=_cparams(("arbitrary",)),
        name="moe_gather_combine",
    )(dest3, dest3, ys, route, x1, mod)


def _layer(x2, mod, pos_row, l, B, S, g_mix, w_in, q_norm_g, k_norm_g, lambda_q1, lambda_k1, lambda_q2,
           lambda_k2, subln_g, b_glu, w_dw, b_dw, conv_ln_g, conv_ln_b, w_out, g_ffn, w_group, b_group,
           w_router, b_router, w_gate, w_up, w_down):
    N, D = x2.shape
    dh = q_norm_g.shape[0]
    H = N_DIFF_HEADS
    aw = H * 2 * dh
    cw = w_dw.shape[1]
    rot = dh // 4
    n_grp = w_group.shape[1]
    n_exp = w_router.shape[1]
    lambda_init = 0.8 - 0.6 * math.exp(-0.3 * l)

    w_qk_t = w_in[:, :2 * aw].T.astype(BF16)
    w_vglu = w_in[:, 2 * aw:].astype(BF16)
    scale = dh ** -0.5
    gq = jnp.tile(q_norm_g * scale, aw // dh)
    gk = jnp.tile(k_norm_g, aw // dh)
    gqk_tab = jnp.broadcast_to(jnp.concatenate([gq, gk])[:, None], (2 * aw, LANES))
    inv_freq = ROPE_THETA ** (-jnp.arange(0, rot, 2, dtype=F32) / rot)
    invf_tab = jnp.broadcast_to(inv_freq[:, None], (rot // 2, LANES))

    q, kt, v, u = _inproj(x2, mod, g_mix.reshape(1, D), pos_row, w_vglu, w_qk_t, b_glu.reshape(1, 2 * cw),
                          gqk_tab, invf_tab, B=B, S=S, aw=aw, cw=cw, dh=dh, rot=rot, T=256)

    attn = _diff_attention(q, kt, v, lambda_q1.reshape(1, dh), lambda_k1.reshape(1, dh),
                           lambda_q2.reshape(1, dh), lambda_k2.reshape(1, dh), subln_g.reshape(1, 2 * dh),
                           B=B, S=S, H=H, dh=dh, lambda_init=lambda_init, tq=256, tk=1024)

    w_r = jnp.zeros((D, LANES), F32).at[:, :n_exp].set(w_router).at[:, n_exp:n_exp + n_grp].set(w_group)
    b_r = jnp.zeros((1, LANES), F32).at[0, :n_exp].set(b_router).at[0, n_exp:n_exp + n_grp].set(b_group)
    wr_hi = w_r.astype(BF16)
    wr_lo = (w_r - wr_hi.astype(F32)).astype(BF16)
    wo = w_out.astype(BF16)
    x1, h2, route, counts = _mixout(attn, u, x2, mod, w_dw, b_dw.reshape(1, cw), conv_ln_g.reshape(1, cw),
                                     conv_ln_b.reshape(1, cw), wo[:aw], wo[aw:], g_ffn.reshape(1, D),
                                     wr_hi, wr_lo, b_r, S=S, T=256, n_exp=n_exp, n_grp=n_grp)

    cnt = counts[0, :n_exp].astype(jnp.int32)
    off = jnp.cumsum(cnt) - cnt
    dest = off[route[:, 0:TOP_K_INNER].astype(jnp.int32)] + route[:, 4:4 + TOP_K_INNER].astype(jnp.int32)
    tm = 256
    visits = _visit_tables(off, cnt, N * TOP_K_INNER, tm)

    xs = _scatter_rows(h2, dest, tokens=256)
    ys = _experts(xs, w_gate, w_up, w_down, visits, tm=tm)
    return _combine(ys, dest, route, x1, mod, S=S, T=256)


def kernel(x, c, positions, w_ada, b_ada, g_mix, w_in, q_norm_g, k_norm_g, lambda_q1, lambda_k1, lambda_q2,
           lambda_k2, subln_g, b_glu, w_dw, b_dw, conv_ln_g, conv_ln_b, w_out, g_ffn, w_group, b_group,
           w_router, b_router, w_gate, w_up, w_down):
    B, S, D = x.shape
    depth = w_ada.shape[0]
    x2 = x.reshape(B * S, D)
    pos_row = positions.astype(F32).reshape(1, B * S)
    for l in range(depth):
        mod = _modulation(c, w_ada[l], b_ada[l])
        x2 = _layer(x2, mod, pos_row, l, B, S, g_mix[l], w_in[l], q_norm_g[l], k_norm_g[l], lambda_q1[l],
                    lambda_k1[l], lambda_q2[l], lambda_k2[l], subln_g[l], b_glu[l], w_dw[l], b_dw[l],
                    conv_ln_g[l], conv_ln_b[l], w_out[l], g_ffn[l], w_group[l], b_group[l], w_router[l],
                    b_router[l], w_gate[l], w_up[l], w_down[l])
    return x2.reshape(B, S, D)
```

```python
import functools
import math

import numpy as np
import jax
import jax.numpy as jnp
from jax import lax
from jax.experimental import pallas as pl
from jax.experimental.pallas import tpu as pltpu

F32 = jnp.float32
BF16 = jnp.bfloat16

EPS = 1e-6
ROPE_THETA = 500000.0
N_DIFF_HEADS = 4
TOP_K_INNER = 2

LANES = 128
SUBLANES = 8
CONV_HALO = 32
VMEM_LIMIT = 48 * 1024 * 1024


def _cparams(sem):
    return pltpu.CompilerParams(dimension_semantics=sem, vmem_limit_bytes=VMEM_LIMIT)


def _silu(x):
    return x * jax.nn.sigmoid(x)


def _mod_kernel(c_ref, w_ref, b_ref, o_ref):
    c = c_ref[...]
    o_ref[...] = jnp.dot(_silu(c), w_ref[...], preferred_element_type=F32,
                         precision=lax.Precision.HIGHEST) + b_ref[...]


def _modulation(c, w_ada, b_ada):
    B, D = c.shape
    n_out = w_ada.shape[1]
    rows = 8
    c_pad = jnp.pad(c, ((0, rows - B), (0, 0)))
    bn = 1024
    out = pl.pallas_call(
        _mod_kernel,
        grid=(n_out // bn,),
        in_specs=[pl.BlockSpec((rows, D), lambda j: (0, 0)),
                  pl.BlockSpec((D, bn), lambda j: (0, j)),
                  pl.BlockSpec((1, bn), lambda j: (0, j))],
        out_specs=pl.BlockSpec((rows, bn), lambda j: (0, j)),
        out_shape=jax.ShapeDtypeStruct((rows, n_out), F32),
        compiler_params=_cparams(("parallel",)),
        name="adaln_mod",
    )(c_pad, w_ada, b_ada.reshape(1, n_out))
    return out[:B].reshape(B, 6, D)


def _inproj_kernel(x_ref, mod_ref, g_ref, pos_ref, wv_ref, wqk_ref, bglu_ref, gqk_ref, invf_ref,
                   q_ref, kt_ref, v_ref, u_ref, *, aw, cw, dh, rot):
    T = x_ref.shape[0]
    half = rot // 2
    x = x_ref[...]
    ms = jnp.mean(x * x, axis=-1, keepdims=True)
    sh = mod_ref[0:1, :]
    sc = mod_ref[1:2, :]
    h = x * lax.rsqrt(ms + EPS) * g_ref[...] * (1.0 + sc) + sh
    hb = h.astype(BF16)

    pv = jnp.dot(hb, wv_ref[...], preferred_element_type=F32)
    v_ref[...] = pv[:, :aw].astype(BF16)
    a = pv[:, aw:aw + cw] + bglu_ref[:, :cw]
    gate = pv[:, aw + cw:] + bglu_ref[:, cw:]
    u_ref[...] = (a * jax.nn.sigmoid(gate)).astype(BF16)

    qkt = lax.dot_general(wqk_ref[...], hb, (((1,), (1,)), ((), ())), preferred_element_type=F32)
    nch = 2 * aw // dh
    for c in range(T // LANES):
        sl = slice(c * LANES, (c + 1) * LANES)
        s3 = qkt[:, sl].reshape(nch, dh, LANES)
        ssq = jnp.mean(s3 * s3, axis=1, keepdims=True)
        y = s3 * lax.rsqrt(ssq + EPS) * gqk_ref[...].reshape(nch, dh, LANES)
        ang = invf_ref[...] * pos_ref[:, sl]
        cs = jnp.cos(ang)
        sn = jnp.sin(ang)
        t1 = y[:, 0:half, :]
        t2 = y[:, half:rot, :]
        y = jnp.concatenate([t1 * cs - t2 * sn, t2 * cs + t1 * sn, y[:, rot:, :]], axis=1)
        y2 = y.reshape(2 * aw, LANES)
        kt_ref[:, sl] = y2[aw:].astype(BF16)
        q_ref[sl, :] = y2[:aw].T.astype(BF16)


def _inproj(x2, mod, g_mix, pos_row, w_vglu, w_qk_t, b_glu, gqk_tab, invf_tab, *, B, S, aw, cw, dh, rot, T):
    N, D = x2.shape
    nS = S // T
    kern = functools.partial(_inproj_kernel, aw=aw, cw=cw, dh=dh, rot=rot)
    return pl.pallas_call(
        kern,
        grid=(N // T,),
        in_specs=[pl.BlockSpec((T, D), lambda i: (i, 0)),
                  pl.BlockSpec((None, 6, D), lambda i: (i // nS, 0, 0)),
                  pl.BlockSpec((1, D), lambda i: (0, 0)),
                  pl.BlockSpec((1, T), lambda i: (0, i)),
                  pl.BlockSpec(w_vglu.shape, lambda i: (0, 0)),
                  pl.BlockSpec(w_qk_t.shape, lambda i: (0, 0)),
                  pl.BlockSpec((1, 2 * cw), lambda i: (0, 0)),
                  pl.BlockSpec(gqk_tab.shape, lambda i: (0, 0)),
                  pl.BlockSpec(invf_tab.shape, lambda i: (0, 0))],
        out_specs=[pl.BlockSpec((T, aw), lambda i: (i, 0)),
                   pl.BlockSpec((None, aw, T), lambda i: (i // nS, 0, i % nS)),
                   pl.BlockSpec((T, aw), lambda i: (i, 0)),
                   pl.BlockSpec((T, cw), lambda i: (i, 0))],
        out_shape=[jax.ShapeDtypeStruct((N, aw), BF16),
                   jax.ShapeDtypeStruct((B, aw, S), BF16),
                   jax.ShapeDtypeStruct((N, aw), BF16),
                   jax.ShapeDtypeStruct((N, cw), BF16)],
        compiler_params=_cparams(("parallel",)),
        name="inproj_qknorm_rope_glu",
    )(x2, mod, g_mix, pos_row, w_vglu, w_qk_t, b_glu, gqk_tab, invf_tab)


def _attn_kernel(q_ref, kt_ref, v_ref, lq1_ref, lk1_ref, lq2_ref, lk2_ref, sg_ref, o_ref, acc_ref,
                 s0_ref, s1_ref, *, tq, tk, dh, lambda_init):
    i = pl.program_id(2)
    hd = 2 * dh
    q = q_ref[...]
    lane = lax.broadcasted_iota(jnp.int32, q.shape, 1)
    zero = jnp.zeros_like(q)
    q2 = jnp.concatenate([jnp.where(lane < dh, q, zero), jnp.where(lane >= dh, q, zero)], axis=0)

    acc_ref[...] = jnp.zeros_like(acc_ref)
    ones_col = (lax.broadcasted_iota(jnp.int32, (tk, hd), 1) == 0).astype(BF16)

    def scores(t, s_ref):
        start = pl.multiple_of(t * tk, tk)
        s_ref[...] = jnp.dot(q2, kt_ref[:, pl.ds(start, tk)], preferred_element_type=F32)

    def softmax_pv(t, s_ref, m, masked):
        start = pl.multiple_of(t * tk, tk)
        vt = jnp.concatenate([v_ref[pl.ds(start, tk), :], ones_col], axis=1)
        s = s_ref[...]
        if masked:
            row = lax.broadcasted_iota(jnp.int32, s.shape, 0)
            col = lax.broadcasted_iota(jnp.int32, s.shape, 1)
            qpos = jnp.where(row >= tq, row - tq, row) + i * tq
            s = jnp.where(col + t * tk <= qpos, s, -jnp.inf)
        m_new = jnp.maximum(m, jnp.max(s, axis=1, keepdims=True))
        alpha = jnp.exp2(m - m_new)
        p = jnp.exp2(s - m_new).astype(BF16)
        acc_ref[...] = alpha * acc_ref[...] + jnp.dot(p, vt, preferred_element_type=F32)
        return m_new

    n_full = (i * tq) // tk
    m0 = jnp.full((2 * tq, 1), -jnp.inf, F32)
    scores(0, s0_ref)

    def pair(jj, m):
        t = 2 * jj
        scores(t + 1, s1_ref)
        m = softmax_pv(t, s0_ref, m, False)
        scores(t + 2, s0_ref)
        return softmax_pv(t + 1, s1_ref, m, False)

    m = lax.fori_loop(0, n_full // 2, pair, m0)

    @pl.when(n_full % 2 == 1)
    def _():
        scores(n_full, s1_ref)
        m1 = softmax_pv(n_full - 1, s0_ref, m, False)
        softmax_pv(n_full, s1_ref, m1, True)

    @pl.when(n_full % 2 == 0)
    def _():
        softmax_pv(n_full, s0_ref, m, True)

    lam = (jnp.exp(jnp.sum(lq1_ref[...] * lk1_ref[...], axis=1, keepdims=True))
           - jnp.exp(jnp.sum(lq2_ref[...] * lk2_ref[...], axis=1, keepdims=True)) + lambda_init)
    o = (acc_ref[0:tq, 0:hd] / acc_ref[0:tq, hd:hd + 1]
         - lam * (acc_ref[tq:, 0:hd] / acc_ref[tq:, hd:hd + 1]))
    ms = jnp.mean(o * o, axis=1, keepdims=True)
    o = o * lax.rsqrt(ms + EPS) * sg_ref[...] * (1.0 - lambda_init)
    o_ref[...] = o.astype(o_ref.dtype)


def _diff_attention(q, kt, v, lq1, lk1, lq2, lk2, subln_g, *, B, S, H, dh, lambda_init, tq, tk):
    aw = H * 2 * dh
    q3 = q.reshape(B, S, aw)
    v3 = v.reshape(B, S, aw)
    hd = 2 * dh
    tk = min(tk, S)
    kern = functools.partial(_attn_kernel, tq=tq, tk=tk, dh=dh, lambda_init=lambda_init)
    vec = pl.BlockSpec((1, dh), lambda b, h, i: (0, 0))
    out = pl.pallas_call(
        kern,
        grid=(B, H, S // tq),
        in_specs=[pl.BlockSpec((None, tq, hd), lambda b, h, i: (b, i, h)),
                  pl.BlockSpec((None, hd, S), lambda b, h, i: (b, h, 0)),
                  pl.BlockSpec((None, S, hd), lambda b, h, i: (b, 0, h)),
                  vec, vec, vec, vec,
                  pl.BlockSpec((1, hd), lambda b, h, i: (0, 0))],
        out_specs=pl.BlockSpec((None, tq, hd), lambda b, h, i: (b, i, h)),
        out_shape=jax.ShapeDtypeStruct((B, S, aw), BF16),
        scratch_shapes=[pltpu.VMEM((2 * tq, 2 * hd), F32),
                        pltpu.VMEM((2 * tq, tk), F32),
                        pltpu.VMEM((2 * tq, tk), F32)],
        compiler_params=_cparams(("parallel", "parallel", "parallel")),
        name="diff_flash_attention",
    )(q3, kt, v3, lq1, lk1, lq2, lk2, subln_g)
    return out.reshape(B * S, aw)


def _mixout_kernel(attn_ref, ucur_ref, uhalo_ref, x_ref, mod_ref, wdw_ref, bdw_ref, lng_ref, lnb_ref,
                   wo1_ref, wo2_ref, gffn_ref, wrh_ref, wrl_ref, br_ref,
                   x1_ref, h2_ref, route_ref, cnt_ref, ubuf_ref, conv_ref,
                   *, nS, conv_k, n_exp, n_grp):
    T = x_ref.shape[0]
    i = pl.program_id(0)

    @pl.when(i == 0)
    def _():
        cnt_ref[...] = jnp.zeros_like(cnt_ref)

    first = (i % nS) == 0
    halo = uhalo_ref[...].astype(F32)
    ubuf_ref[0, 0:CONV_HALO, :] = jnp.where(first, jnp.zeros_like(halo), halo)
    ubuf_ref[0, CONV_HALO:, :] = ucur_ref[...].astype(F32)
    span = T + CONV_HALO - SUBLANES
    for b in range(1, SUBLANES):
        ubuf_ref[b, 0:span, :] = ubuf_ref[0, b:b + span, :]

    off = CONV_HALO - (conv_k - 1)
    rows = 32
    for r0 in range(0, T, rows):
        acc = jnp.zeros((rows, ucur_ref.shape[1]), F32)
        for j in range(conv_k):
            a, b = divmod(j + off, SUBLANES)
            lo_r = r0 + a * SUBLANES
            acc = acc + wdw_ref[j:j + 1, :] * ubuf_ref[b, lo_r:lo_r + rows, :]
        y = acc + bdw_ref[...]
        mu = jnp.mean(y, axis=1, keepdims=True)
        d = y - mu
        var = jnp.mean(d * d, axis=1, keepdims=True)
        z = d * lax.rsqrt(var + EPS) * lng_ref[...] + lnb_ref[...]
        conv_ref[r0:r0 + rows, :] = _silu(z).astype(BF16)

    yo = (jnp.dot(attn_ref[...], wo1_ref[...], preferred_element_type=F32)
          + jnp.dot(conv_ref[...], wo2_ref[...], preferred_element_type=F32))
    x1 = x_ref[...] + mod_ref[2:3, :] * yo
    x1_ref[...] = x1
    ms = jnp.mean(x1 * x1, axis=1, keepdims=True)
    h2 = x1 * lax.rsqrt(ms + EPS) * gffn_ref[...] * (1.0 + mod_ref[4:5, :]) + mod_ref[3:4, :]
    h2_ref[...] = h2
    hi = h2.astype(BF16)
    lo = (h2 - hi.astype(F32)).astype(BF16)

    logits = (jnp.dot(hi, wrh_ref[...], preferred_element_type=F32)
              + jnp.dot(lo, wrh_ref[...], preferred_element_type=F32)
              + jnp.dot(hi, wrl_ref[...], preferred_element_type=F32)) + br_ref[...]
    lane_i = lax.broadcasted_iota(jnp.int32, logits.shape, 1)
    lane = lane_i.astype(F32)
    big = jnp.float32(1e9)
    ninf = jnp.float32(-jnp.inf)
    is_g = (lane_i >= n_exp) & (lane_i < n_exp + n_grp)
    gl = jnp.where(is_g, logits, ninf)
    gmax = jnp.max(gl, axis=1, keepdims=True)
    gsum = jnp.sum(jnp.where(is_g, jnp.exp(gl - gmax), 0.0), axis=1, keepdims=True)
    g_p = 1.0 / gsum
    gidx = jnp.min(jnp.where(gl == gmax, lane, big), axis=1, keepdims=True) - n_exp
    epg = n_exp // n_grp
    lo_l = gidx * epg
    in_grp = (lane >= lo_l) & (lane < lo_l + epg)
    el = jnp.where(in_grp, logits, ninf)
    m1 = jnp.max(el, axis=1, keepdims=True)
    i1 = jnp.min(jnp.where(el == m1, lane, big), axis=1, keepdims=True)
    el2 = jnp.where(lane == i1, ninf, el)
    m2 = jnp.max(el2, axis=1, keepdims=True)
    i2 = jnp.min(jnp.where(el2 == m2, lane, big), axis=1, keepdims=True)
    e2 = jnp.exp(m2 - m1)
    p1 = 1.0 / (1.0 + e2)
    p2 = e2 / (1.0 + e2)

    sel = (lane == i1) | (lane == i2)
    rr = lax.broadcasted_iota(jnp.int32, (T, T), 0)
    cc = lax.broadcasted_iota(jnp.int32, (T, T), 1)
    before = jnp.dot((rr > cc).astype(BF16), sel.astype(BF16), preferred_element_type=F32) + cnt_ref[...]
    r1 = jnp.sum(jnp.where(lane == i1, before, 0.0), axis=1, keepdims=True)
    r2 = jnp.sum(jnp.where(lane == i2, before, 0.0), axis=1, keepdims=True)
    cnt_ref[...] += jnp.sum(sel.astype(F32), axis=0, keepdims=True)

    route = jnp.zeros_like(logits)
    for k, val in enumerate((i1, i2, g_p * p1, g_p * p2, r1, r2)):
        route = jnp.where(lane_i == k, val, route)
    route_ref[...] = route


def _mixout(attn, u, x2, mod, w_dw, b_dw, ln_g, ln_b, wo1, wo2, g_ffn, wr_hi, wr_lo, b_r,
            *, S, T, n_exp, n_grp):
    N, D = x2.shape
    aw = attn.shape[1]
    cw = u.shape[1]
    nS = S // T
    conv_k = w_dw.shape[0]
    hb = T // CONV_HALO
    kern = functools.partial(_mixout_kernel, nS=nS, conv_k=conv_k, n_exp=n_exp, n_grp=n_grp)
    full = lambda a: pl.BlockSpec(a.shape, lambda i: (0, 0))
    return pl.pallas_call(
        kern,
        grid=(N // T,),
        in_specs=[pl.BlockSpec((T, aw), lambda i: (i, 0)),
                  pl.BlockSpec((T, cw), lambda i: (i, 0)),
                  pl.BlockSpec((CONV_HALO, cw), lambda i: (jnp.maximum(i * hb - 1, 0), 0)),
                  pl.BlockSpec((T, D), lambda i: (i, 0)),
                  pl.BlockSpec((None, 6, D), lambda i: (i // nS, 0, 0)),
                  full(w_dw), full(b_dw), full(ln_g), full(ln_b), full(wo1), full(wo2), full(g_ffn),
                  full(wr_hi), full(wr_lo), full(b_r)],
        out_specs=[pl.BlockSpec((T, D), lambda i: (i, 0)),
                   pl.BlockSpec((T, D), lambda i: (i, 0)),
                   pl.BlockSpec((T, LANES), lambda i: (i, 0)),
                   pl.BlockSpec((1, LANES), lambda i: (0, 0))],
        out_shape=[jax.ShapeDtypeStruct((N, D), F32),
                   jax.ShapeDtypeStruct((N, D), F32),
                   jax.ShapeDtypeStruct((N, LANES), F32),
                   jax.ShapeDtypeStruct((1, LANES), F32)],
        scratch_shapes=[pltpu.VMEM((SUBLANES, CONV_HALO + T, cw), F32),
                        pltpu.VMEM((T, cw), BF16)],
        compiler_params=_cparams(("arbitrary",)),
        name="conv_outproj_router",
    )(attn, u, u, x2, mod, w_dw, b_dw, ln_g, ln_b, wo1, wo2, g_ffn, wr_hi, wr_lo, b_r)


def _scatter_kernel(dest_ref, h2_ref, xs_hbm, sem):
    tokens = h2_ref.shape[0]

    def issue(t, carry):
        for k in range(TOP_K_INNER):
            pltpu.make_async_copy(h2_ref.at[pl.ds(t, 1)],
                                  xs_hbm.at[pl.ds(dest_ref[0, t * TOP_K_INNER + k], 1)], sem).start()
        return carry

    lax.fori_loop(0, tokens, issue, 0, unroll=8)
    for k in range(TOP_K_INNER):
        pltpu.make_async_copy(h2_ref, xs_hbm.at[pl.ds(0, tokens)], sem).wait()


def _scatter_rows(h2, dest, *, tokens):
    N, C = h2.shape
    steps = N // tokens
    dest3 = dest.reshape(steps, 1, tokens * TOP_K_INNER)
    return pl.pallas_call(
        _scatter_kernel,
        grid=(steps,),
        in_specs=[pl.BlockSpec((None, 1, tokens * TOP_K_INNER), lambda s: (s, 0, 0), memory_space=pltpu.SMEM),
                  pl.BlockSpec((tokens, C), lambda s: (s, 0))],
        out_specs=pl.BlockSpec(memory_space=pl.ANY),
        out_shape=jax.ShapeDtypeStruct((N * TOP_K_INNER, C), h2.dtype),
        scratch_shapes=[pltpu.SemaphoreType.DMA],
        compiler_params=_cparams(("arbitrary",)),
        name="moe_scatter_rows",
    )(dest3, h2)


def _experts_kernel(vt_ref, ve_ref, vlo_ref, vhi_ref, xs_ref, wg_ref, wu_ref, wd_ref, ys_ref):
    v = pl.program_id(0)
    lo = vlo_ref[v]
    hi = vhi_ref[v]

    @pl.when(hi > lo)
    def _():
        x = xs_ref[...].astype(BF16)
        g = jnp.dot(x, wg_ref[...].astype(BF16), preferred_element_type=F32)
        u = jnp.dot(x, wu_ref[...].astype(BF16), preferred_element_type=F32)
        hid = (_silu(g) * u).astype(BF16)
        y = jnp.dot(hid, wd_ref[...].astype(BF16), preferred_element_type=F32)

        @pl.when(lo == 0)
        def _():
            ys_ref[...] = y

        @pl.when(lo > 0)
        def _():
            row = lax.broadcasted_iota(jnp.int32, y.shape, 0)
            ys_ref[...] = jnp.where((row >= lo) & (row < hi), y, ys_ref[...])


def _experts(xs, w_gate, w_up, w_down, visits, *, tm):
    R, C = xs.shape
    E, D, ff = w_gate.shape
    vt, ve, vlo, vhi = visits
    grid_spec = pltpu.PrefetchScalarGridSpec(
        num_scalar_prefetch=4,
        grid=(vt.shape[0],),
        in_specs=[pl.BlockSpec((tm, C), lambda v, vt, ve, vlo, vhi: (vt[v], 0)),
                  pl.BlockSpec((None, D, ff), lambda v, vt, ve, vlo, vhi: (ve[v], 0, 0)),
                  pl.BlockSpec((None, D, ff), lambda v, vt, ve, vlo, vhi: (ve[v], 0, 0)),
                  pl.BlockSpec((None, ff, D), lambda v, vt, ve, vlo, vhi: (ve[v], 0, 0))],
        out_specs=pl.BlockSpec((tm, C), lambda v, vt, ve, vlo, vhi: (vt[v], 0)),
    )
    return pl.pallas_call(
        _experts_kernel,
        grid_spec=grid_spec,
        out_shape=jax.ShapeDtypeStruct((R, C), F32),
        compiler_params=_cparams(("arbitrary",)),
        name="moe_grouped_experts",
    )(vt, ve, vlo, vhi, xs, w_gate, w_up, w_down)


def _visit_tables(off, cnt, n_rows, tm):
    n_tiles = n_rows // tm
    n_exp = off.shape[0]
    n_visits = n_tiles + n_exp - 1
    tile_starts = jnp.arange(n_tiles, dtype=jnp.int32) * tm
    seg_starts = jnp.where((cnt > 0) & (off % tm != 0), off, n_rows)
    starts = jnp.sort(jnp.concatenate([tile_starts, seg_starts]))
    lo_abs = starts[:n_visits]
    hi_abs = starts[1:n_visits + 1]
    valid = lo_abs < n_rows
    tile = jnp.where(valid, lo_abs // tm, n_tiles - 1)
    ends = off + cnt
    probe = jnp.where(valid, lo_abs, n_rows - 1)
    expert = jnp.sum(ends[None, :] <= probe[:, None], axis=1).astype(jnp.int32)
    row_lo = jnp.where(valid, lo_abs - tile * tm, 0)
    row_hi = jnp.where(valid, hi_abs - tile * tm, 0)
    return tile, expert, row_lo, row_hi


def _combine_kernel(dcur_ref, dnxt_ref, ys_hbm, route_ref, x1_ref, mod_ref, o_ref, ybuf, sem):
    T = x1_ref.shape[0]
    i = pl.program_id(0)
    slot = lax.rem(i, 2)

    def gather(d_ref, sl):
        def body(t, carry):
            for k in range(TOP_K_INNER):
                pltpu.make_async_copy(ys_hbm.at[pl.ds(d_ref[0, t * TOP_K_INNER + k], 1)],
                                      ybuf.at[sl, k, pl.ds(t, 1)], sem.at[sl]).start()
            return carry
        lax.fori_loop(0, T, body, 0, unroll=8)

    @pl.when(i == 0)
    def _():
        gather(dcur_ref, 0)

    @pl.when(i + 1 < pl.num_programs(0))
    def _():
        gather(dnxt_ref, 1 - slot)

    for k in range(TOP_K_INNER):
        pltpu.make_async_copy(ys_hbm.at[pl.ds(0, T)], ybuf.at[slot, k], sem.at[slot]).wait()

    route = route_ref[...]
    moe = jnp.zeros(x1_ref.shape, F32)
    for k in range(TOP_K_INNER):
        moe = moe + route[:, TOP_K_INNER + k:TOP_K_INNER + k + 1] * ybuf[slot, k]
    o_ref[...] = x1_ref[...] + mod_ref[5:6, :] * moe


def _combine(ys, dest, route, x1, mod, *, S, T):
    N, D = x1.shape
    C = ys.shape[1]
    steps = N // T
    nS = S // T
    dest3 = dest.reshape(steps, 1, T * TOP_K_INNER)
    dspec = lambda f: pl.BlockSpec((None, 1, T * TOP_K_INNER), f, memory_space=pltpu.SMEM)
    return pl.pallas_call(
        _combine_kernel,
        grid=(steps,),
        in_specs=[dspec(lambda i: (i, 0, 0)),
                  dspec(lambda i: (jnp.minimum(i + 1, steps - 1), 0, 0)),
                  pl.BlockSpec(memory_space=pl.ANY),
                  pl.BlockSpec((T, LANES), lambda i: (i, 0)),
                  pl.BlockSpec((T, D), lambda i: (i, 0)),
                  pl.BlockSpec((None, 6, D), lambda i: (i // nS, 0, 0))],
        out_specs=pl.BlockSpec((T, D), lambda i: (i, 0)),
        out_shape=jax.ShapeDtypeStruct((N, D), F32),
        scratch_shapes=[pltpu.VMEM((2, TOP_K_INNER, T, C), F32),
                        pltpu.SemaphoreType.DMA((2,))],
        compiler_params=_cparams(("arbitrary",)),
        name="moe_gather_combine",
    )(dest3, dest3, ys, route, x1, mod)


def _layer(x2, mod, pos_row, l, B, S, g_mix, w_in, q_norm_g, k_norm_g, lambda_q1, lambda_k1, lambda_q2,
           lambda_k2, subln_g, b_glu, w_dw, b_dw, conv_ln_g, conv_ln_b, w_out, g_ffn, w_group, b_group,
           w_router, b_router, w_gate, w_up, w_down):
    N, D = x2.shape
    dh = q_norm_g.shape[0]
    H = N_DIFF_HEADS
    aw = H * 2 * dh
    cw = w_dw.shape[1]
    rot = dh // 4
    n_grp = w_group.shape[1]
    n_exp = w_router.shape[1]
    lambda_init = 0.8 - 0.6 * math.exp(-0.3 * l)

    w_qk_t = w_in[:, :2 * aw].T.astype(BF16)
    w_vglu = w_in[:, 2 * aw:].astype(BF16)
    scale = dh ** -0.5 * math.log2(math.e)
    gq = jnp.tile(q_norm_g * scale, aw // dh)
    gk = jnp.tile(k_norm_g, aw // dh)
    gqk_tab = jnp.broadcast_to(jnp.concatenate([gq, gk])[:, None], (2 * aw, LANES))
    inv_freq = ROPE_THETA ** (-jnp.arange(0, rot, 2, dtype=F32) / rot)
    invf_tab = jnp.broadcast_to(inv_freq[:, None], (rot // 2, LANES))

    q, kt, v, u = _inproj(x2, mod, g_mix.reshape(1, D), pos_row, w_vglu, w_qk_t, b_glu.reshape(1, 2 * cw),
                          gqk_tab, invf_tab, B=B, S=S, aw=aw, cw=cw, dh=dh, rot=rot, T=256)

    attn = _diff_attention(q, kt, v, lambda_q1.reshape(1, dh), lambda_k1.reshape(1, dh),
                           lambda_q2.reshape(1, dh), lambda_k2.reshape(1, dh), subln_g.reshape(1, 2 * dh),
                           B=B, S=S, H=H, dh=dh, lambda_init=lambda_init, tq=256, tk=1024)

    w_r = jnp.zeros((D, LANES), F32).at[:, :n_exp].set(w_router).at[:, n_exp:n_exp + n_grp].set(w_group)
    b_r = jnp.zeros((1, LANES), F32).at[0, :n_exp].set(b_router).at[0, n_exp:n_exp + n_grp].set(b_group)
    wr_hi = w_r.astype(BF16)
    wr_lo = (w_r - wr_hi.astype(F32)).astype(BF16)
    wo = w_out.astype(BF16)
    x1, h2, route, counts = _mixout(attn, u, x2, mod, w_dw, b_dw.reshape(1, cw), conv_ln_g.reshape(1, cw),
                                     conv_ln_b.reshape(1, cw), wo[:aw], wo[aw:], g_ffn.reshape(1, D),
                                     wr_hi, wr_lo, b_r, S=S, T=256, n_exp=n_exp, n_grp=n_grp)

    cnt = counts[0, :n_exp].astype(jnp.int32)
    off = jnp.cumsum(cnt) - cnt
    dest = off[route[:, 0:TOP_K_INNER].astype(jnp.int32)] + route[:, 4:4 + TOP_K_INNER].astype(jnp.int32)
    tm = 256
    visits = _visit_tables(off, cnt, N * TOP_K_INNER, tm)

    xs = _scatter_rows(h2, dest, tokens=min(1024, N))
    ys = _experts(xs, w_gate, w_up, w_down, visits, tm=tm)
    return _combine(ys, dest, route, x1, mod, S=S, T=256)


def kernel(x, c, positions, w_ada, b_ada, g_mix, w_in, q_norm_g, k_norm_g, lambda_q1, lambda_k1, lambda_q2,
           lambda_k2, subln_g, b_glu, w_dw, b_dw, conv_ln_g, conv_ln_b, w_out, g_ffn, w_group, b_group,
           w_router, b_router, w_gate, w_up, w_down):
    B, S, D = x.shape
    depth = w_ada.shape[0]
    x2 = x.reshape(B * S, D)
    pos_row = positions.astype(F32).reshape(1, B * S)
    for l in range(depth):
        mod = _modulation(c, w_ada[l], b_ada[l])
        x2 = _layer(x2, mod, pos_row, l, B, S, g_mix[l], w_in[l], q_norm_g[l], k_norm_g[l], lambda_q1[l],
                    lambda_k1[l], lambda_q2[l], lambda_k2[l], subln_g[l], b_glu[l], w_dw[l], b_dw[l],
                    conv_ln_g[l], conv_ln_b[l], w_out[l], g_ffn[l], w_group[l], b_group[l], w_router[l],
                    b_router[l], w_gate[l], w_up[l], w_down[l])
    return x2.reshape(B, S, D)
```

```python
import functools
import math

import numpy as np
import jax
import jax.numpy as jnp
from jax import lax
from jax.experimental import pallas as pl
from jax.experimental.pallas import tpu as pltpu

F32 = jnp.float32
BF16 = jnp.bfloat16

EPS = 1e-6
ROPE_THETA = 500000.0
N_DIFF_HEADS = 4
TOP_K_INNER = 2

LANES = 128
SUBLANES = 8
CONV_HALO = 32
VMEM_LIMIT = 48 * 1024 * 1024


def _cparams(sem):
    return pltpu.CompilerParams(dimension_semantics=sem, vmem_limit_bytes=VMEM_LIMIT)


def _silu(x):
    return x * jax.nn.sigmoid(x)


def _mod_kernel(c_ref, w_ref, b_ref, o_ref):
    c = c_ref[...]
    o_ref[...] = jnp.dot(_silu(c), w_ref[...], preferred_element_type=F32,
                         precision=lax.Precision.HIGHEST) + b_ref[...]


def _modulation(c, w_ada, b_ada):
    B, D = c.shape
    n_out = w_ada.shape[1]
    rows = 8
    c_pad = jnp.pad(c, ((0, rows - B), (0, 0)))
    bn = 1024
    out = pl.pallas_call(
        _mod_kernel,
        grid=(n_out // bn,),
        in_specs=[pl.BlockSpec((rows, D), lambda j: (0, 0)),
                  pl.BlockSpec((D, bn), lambda j: (0, j)),
                  pl.BlockSpec((1, bn), lambda j: (0, j))],
        out_specs=pl.BlockSpec((rows, bn), lambda j: (0, j)),
        out_shape=jax.ShapeDtypeStruct((rows, n_out), F32),
        compiler_params=_cparams(("parallel",)),
        name="adaln_mod",
    )(c_pad, w_ada, b_ada.reshape(1, n_out))
    return out[:B].reshape(B, 6, D)


def _inproj_kernel(x_ref, mod_ref, g_ref, pos_ref, wv_ref, wqk_ref, bglu_ref, gqk_ref, invf_ref,
                   q_ref, kt_ref, v_ref, u_ref, *, aw, cw, dh, rot):
    T = x_ref.shape[0]
    half = rot // 2
    x = x_ref[...]
    ms = jnp.mean(x * x, axis=-1, keepdims=True)
    sh = mod_ref[0:1, :]
    sc = mod_ref[1:2, :]
    h = x * lax.rsqrt(ms + EPS) * g_ref[...] * (1.0 + sc) + sh
    hb = h.astype(BF16)

    pv = jnp.dot(hb, wv_ref[...], preferred_element_type=F32)
    v_ref[...] = pv[:, :aw].astype(BF16)
    a = pv[:, aw:aw + cw] + bglu_ref[:, :cw]
    gate = pv[:, aw + cw:] + bglu_ref[:, cw:]
    u_ref[...] = (a * jax.nn.sigmoid(gate)).astype(BF16)

    qkt = lax.dot_general(wqk_ref[...], hb, (((1,), (1,)), ((), ())), preferred_element_type=F32)
    nch = 2 * aw // dh
    for c in range(T // LANES):
        sl = slice(c * LANES, (c + 1) * LANES)
        s3 = qkt[:, sl].reshape(nch, dh, LANES)
        ssq = jnp.mean(s3 * s3, axis=1, keepdims=True)
        y = s3 * lax.rsqrt(ssq + EPS) * gqk_ref[...].reshape(nch, dh, LANES)
        ang = invf_ref[...] * pos_ref[:, sl]
        cs = jnp.cos(ang)
        sn = jnp.sin(ang)
        t1 = y[:, 0:half, :]
        t2 = y[:, half:rot, :]
        y = jnp.concatenate([t1 * cs - t2 * sn, t2 * cs + t1 * sn, y[:, rot:, :]], axis=1)
        y2 = y.reshape(2 * aw, LANES)
        kt_ref[:, sl] = y2[aw:].astype(BF16)
        q_ref[sl, :] = y2[:aw].T.astype(BF16)


def _inproj(x2, mod, g_mix, pos_row, w_vglu, w_qk_t, b_glu, gqk_tab, invf_tab, *, B, S, aw, cw, dh, rot, T):
    N, D = x2.shape
    nS = S // T
    kern = functools.partial(_inproj_kernel, aw=aw, cw=cw, dh=dh, rot=rot)
    return pl.pallas_call(
        kern,
        grid=(N // T,),
        in_specs=[pl.BlockSpec((T, D), lambda i: (i, 0)),
                  pl.BlockSpec((None, 6, D), lambda i: (i // nS, 0, 0)),
                  pl.BlockSpec((1, D), lambda i: (0, 0)),
                  pl.BlockSpec((1, T), lambda i: (0, i)),
                  pl.BlockSpec(w_vglu.shape, lambda i: (0, 0)),
                  pl.BlockSpec(w_qk_t.shape, lambda i: (0, 0)),
                  pl.BlockSpec((1, 2 * cw), lambda i: (0, 0)),
                  pl.BlockSpec(gqk_tab.shape, lambda i: (0, 0)),
                  pl.BlockSpec(invf_tab.shape, lambda i: (0, 0))],
        out_specs=[pl.BlockSpec((T, aw), lambda i: (i, 0)),
                   pl.BlockSpec((None, aw, T), lambda i: (i // nS, 0, i % nS)),
                   pl.BlockSpec((T, aw), lambda i: (i, 0)),
                   pl.BlockSpec((T, cw), lambda i: (i, 0))],
        out_shape=[jax.ShapeDtypeStruct((N, aw), BF16),
                   jax.ShapeDtypeStruct((B, aw, S), BF16),
                   jax.ShapeDtypeStruct((N, aw), BF16),
                   jax.ShapeDtypeStruct((N, cw), BF16)],
        compiler_params=_cparams(("parallel",)),
        name="inproj_qknorm_rope_glu",
    )(x2, mod, g_mix, pos_row, w_vglu, w_qk_t, b_glu, gqk_tab, invf_tab)


def _attn_kernel(q_ref, kt_ref, v_ref, lq1_ref, lk1_ref, lq2_ref, lk2_ref, sg_ref, o_ref, acc_ref,
                 s0_ref, s1_ref, *, tq, tk, dh, lambda_init):
    i = pl.program_id(2)
    hd = 2 * dh
    q = q_ref[...]
    lane = lax.broadcasted_iota(jnp.int32, q.shape, 1)
    zero = jnp.zeros_like(q)
    q2 = jnp.concatenate([jnp.where(lane < dh, q, zero), jnp.where(lane >= dh, q, zero)], axis=0)

    acc_ref[...] = jnp.zeros_like(acc_ref)
    n_sub = tk // tq
    ones_col = {w: (lax.broadcasted_iota(jnp.int32, (w, hd), 1) == 0).astype(BF16)
                for w in {tk} | {(r + 1) * tq for r in range(n_sub)}}

    def scores(t, s_ref):
        start = pl.multiple_of(t * tk, tk)
        s_ref[...] = jnp.dot(q2, kt_ref[:, pl.ds(start, tk)], preferred_element_type=F32)

    def softmax_pv(t, s_ref, m, width=tk, diagonal=False):
        start = pl.multiple_of(t * tk, tk)
        vt = jnp.concatenate([v_ref[pl.ds(start, width), :], ones_col[width]], axis=1)
        s = s_ref[:, :width]
        if diagonal:
            row = lax.broadcasted_iota(jnp.int32, (2 * tq, tq), 0)
            col = lax.broadcasted_iota(jnp.int32, (2 * tq, tq), 1)
            qrow = jnp.where(row >= tq, row - tq, row)
            tail = jnp.where(col <= qrow, s[:, width - tq:], -jnp.inf)
            s = tail if width == tq else jnp.concatenate([s[:, :width - tq], tail], axis=1)
        m_new = jnp.maximum(m, jnp.max(s, axis=1, keepdims=True))
        alpha = jnp.exp2(m - m_new)
        p = jnp.exp2(s - m_new).astype(BF16)
        acc_ref[...] = alpha * acc_ref[...] + jnp.dot(p, vt, preferred_element_type=F32)
        return m_new

    n_full = (i * tq) // tk
    odd = lax.rem(n_full, 2)
    m0 = jnp.full((2 * tq, 1), -jnp.inf, F32)

    @pl.when(odd == 0)
    def _():
        scores(0, s0_ref)

    @pl.when(odd == 1)
    def _():
        scores(0, s1_ref)

    def peeled(_, m):
        scores(1, s0_ref)
        return softmax_pv(0, s1_ref, m)

    m = lax.fori_loop(0, odd, peeled, m0)

    def pair(jj, m):
        t = odd + 2 * jj
        scores(t + 1, s1_ref)
        m = softmax_pv(t, s0_ref, m)
        scores(t + 2, s0_ref)
        return softmax_pv(t + 1, s1_ref, m)

    m = lax.fori_loop(0, (n_full - odd) // 2, pair, m)

    for r in range(n_sub):
        @pl.when(lax.rem(i, n_sub) == r)
        def _(r=r):
            softmax_pv(n_full, s0_ref, m, width=(r + 1) * tq, diagonal=True)

    lam = (jnp.exp(jnp.sum(lq1_ref[...] * lk1_ref[...], axis=1, keepdims=True))
           - jnp.exp(jnp.sum(lq2_ref[...] * lk2_ref[...], axis=1, keepdims=True)) + lambda_init)
    o = (acc_ref[0:tq, 0:hd] / acc_ref[0:tq, hd:hd + 1]
         - lam * (acc_ref[tq:, 0:hd] / acc_ref[tq:, hd:hd + 1]))
    ms = jnp.mean(o * o, axis=1, keepdims=True)
    o = o * lax.rsqrt(ms + EPS) * sg_ref[...] * (1.0 - lambda_init)
    o_ref[...] = o.astype(o_ref.dtype)


def _diff_attention(q, kt, v, lq1, lk1, lq2, lk2, subln_g, *, B, S, H, dh, lambda_init, tq, tk):
    aw = H * 2 * dh
    q3 = q.reshape(B, S, aw)
    v3 = v.reshape(B, S, aw)
    hd = 2 * dh
    tk = min(tk, S)
    kern = functools.partial(_attn_kernel, tq=tq, tk=tk, dh=dh, lambda_init=lambda_init)
    vec = pl.BlockSpec((1, dh), lambda b, h, i: (0, 0))
    out = pl.pallas_call(
        kern,
        grid=(B, H, S // tq),
        in_specs=[pl.BlockSpec((None, tq, hd), lambda b, h, i: (b, i, h)),
                  pl.BlockSpec((None, hd, S), lambda b, h, i: (b, h, 0)),
                  pl.BlockSpec((None, S, hd), lambda b, h, i: (b, 0, h)),
                  vec, vec, vec, vec,
                  pl.BlockSpec((1, hd), lambda b, h, i: (0, 0))],
        out_specs=pl.BlockSpec((None, tq, hd), lambda b, h, i: (b, i, h)),
        out_shape=jax.ShapeDtypeStruct((B, S, aw), BF16),
        scratch_shapes=[pltpu.VMEM((2 * tq, 2 * hd), F32),
                        pltpu.VMEM((2 * tq, tk), F32),
                        pltpu.VMEM((2 * tq, tk), F32)],
        compiler_params=_cparams(("parallel", "parallel", "parallel")),
        name="diff_flash_attention",
    )(q3, kt, v3, lq1, lk1, lq2, lk2, subln_g)
    return out.reshape(B * S, aw)


def _mixout_kernel(attn_ref, ucur_ref, uhalo_ref, x_ref, mod_ref, wdw_ref, bdw_ref, lng_ref, lnb_ref,
                   wo1_ref, wo2_ref, gffn_ref, wrh_ref, wrl_ref, br_ref,
                   x1_ref, h2_ref, route_ref, cnt_ref, ubuf_ref, conv_ref,
                   *, nS, conv_k, n_exp, n_grp):
    T = x_ref.shape[0]
    i = pl.program_id(0)

    @pl.when(i == 0)
    def _():
        cnt_ref[...] = jnp.zeros_like(cnt_ref)

    first = (i % nS) == 0
    halo = uhalo_ref[...].astype(F32)
    ubuf_ref[0, 0:CONV_HALO, :] = jnp.where(first, jnp.zeros_like(halo), halo)
    ubuf_ref[0, CONV_HALO:, :] = ucur_ref[...].astype(F32)
    span = T + CONV_HALO - SUBLANES
    for b in range(1, SUBLANES):
        ubuf_ref[b, 0:span, :] = ubuf_ref[0, b:b + span, :]

    off = CONV_HALO - (conv_k - 1)
    rows = 32
    for r0 in range(0, T, rows):
        acc = jnp.zeros((rows, ucur_ref.shape[1]), F32)
        for j in range(conv_k):
            a, b = divmod(j + off, SUBLANES)
            lo_r = r0 + a * SUBLANES
            acc = acc + wdw_ref[j:j + 1, :] * ubuf_ref[b, lo_r:lo_r + rows, :]
        y = acc + bdw_ref[...]
        mu = jnp.mean(y, axis=1, keepdims=True)
        d = y - mu
        var = jnp.mean(d * d, axis=1, keepdims=True)
        z = d * lax.rsqrt(var + EPS) * lng_ref[...] + lnb_ref[...]
        conv_ref[r0:r0 + rows, :] = _silu(z).astype(BF16)

    yo = (jnp.dot(attn_ref[...], wo1_ref[...], preferred_element_type=F32)
          + jnp.dot(conv_ref[...], wo2_ref[...], preferred_element_type=F32))
    x1 = x_ref[...] + mod_ref[2:3, :] * yo
    x1_ref[...] = x1
    ms = jnp.mean(x1 * x1, axis=1, keepdims=True)
    h2 = x1 * lax.rsqrt(ms + EPS) * gffn_ref[...] * (1.0 + mod_ref[4:5, :]) + mod_ref[3:4, :]
    h2_ref[...] = h2
    hi = h2.astype(BF16)
    lo = (h2 - hi.astype(F32)).astype(BF16)

    logits = (jnp.dot(hi, wrh_ref[...], preferred_element_type=F32)
              + jnp.dot(lo, wrh_ref[...], preferred_element_type=F32)
              + jnp.dot(hi, wrl_ref[...], preferred_element_type=F32)) + br_ref[...]
    lane_i = lax.broadcasted_iota(jnp.int32, logits.shape, 1)
    lane = lane_i.astype(F32)
    big = jnp.float32(1e9)
    ninf = jnp.float32(-jnp.inf)
    is_g = (lane_i >= n_exp) & (lane_i < n_exp + n_grp)
    gl = jnp.where(is_g, logits, ninf)
    gmax = jnp.max(gl, axis=1, keepdims=True)
    gsum = jnp.sum(jnp.where(is_g, jnp.exp(gl - gmax), 0.0), axis=1, keepdims=True)
    g_p = 1.0 / gsum
    gidx = jnp.min(jnp.where(gl == gmax, lane, big), axis=1, keepdims=True) - n_exp
    epg = n_exp // n_grp
    lo_l = gidx * epg
    in_grp = (lane >= lo_l) & (lane < lo_l + epg)
    el = jnp.where(in_grp, logits, ninf)
    m1 = jnp.max(el, axis=1, keepdims=True)
    i1 = jnp.min(jnp.where(el == m1, lane, big), axis=1, keepdims=True)
    el2 = jnp.where(lane == i1, ninf, el)
    m2 = jnp.max(el2, axis=1, keepdims=True)
    i2 = jnp.min(jnp.where(el2 == m2, lane, big), axis=1, keepdims=True)
    e2 = jnp.exp(m2 - m1)
    p1 = 1.0 / (1.0 + e2)
    p2 = e2 / (1.0 + e2)

    sel = (lane == i1) | (lane == i2)
    rr = lax.broadcasted_iota(jnp.int32, (T, T), 0)
    cc = lax.broadcasted_iota(jnp.int32, (T, T), 1)
    before = jnp.dot((rr > cc).astype(BF16), sel.astype(BF16), preferred_element_type=F32) + cnt_ref[...]
    r1 = jnp.sum(jnp.where(lane == i1, before, 0.0), axis=1, keepdims=True)
    r2 = jnp.sum(jnp.where(lane == i2, before, 0.0), axis=1, keepdims=True)
    cnt_ref[...] += jnp.sum(sel.astype(F32), axis=0, keepdims=True)

    route = jnp.zeros_like(logits)
    for k, val in enumerate((i1, i2, g_p * p1, g_p * p2, r1, r2)):
        route = jnp.where(lane_i == k, val, route)
    route_ref[...] = route


def _mixout(attn, u, x2, mod, w_dw, b_dw, ln_g, ln_b, wo1, wo2, g_ffn, wr_hi, wr_lo, b_r,
            *, S, T, n_exp, n_grp):
    N, D = x2.shape
    aw = attn.shape[1]
    cw = u.shape[1]
    nS = S // T
    conv_k = w_dw.shape[0]
    hb = T // CONV_HALO
    kern = functools.partial(_mixout_kernel, nS=nS, conv_k=conv_k, n_exp=n_exp, n_grp=n_grp)
    full = lambda a: pl.BlockSpec(a.shape, lambda i: (0, 0))
    return pl.pallas_call(
        kern,
        grid=(N // T,),
        in_specs=[pl.BlockSpec((T, aw), lambda i: (i, 0)),
                  pl.BlockSpec((T, cw), lambda i: (i, 0)),
                  pl.BlockSpec((CONV_HALO, cw), lambda i: (jnp.maximum(i * hb - 1, 0), 0)),
                  pl.BlockSpec((T, D), lambda i: (i, 0)),
                  pl.BlockSpec((None, 6, D), lambda i: (i // nS, 0, 0)),
                  full(w_dw), full(b_dw), full(ln_g), full(ln_b), full(wo1), full(wo2), full(g_ffn),
                  full(wr_hi), full(wr_lo), full(b_r)],
        out_specs=[pl.BlockSpec((T, D), lambda i: (i, 0)),
                   pl.BlockSpec((T, D), lambda i: (i, 0)),
                   pl.BlockSpec((T, LANES), lambda i: (i, 0)),
                   pl.BlockSpec((1, LANES), lambda i: (0, 0))],
        out_shape=[jax.ShapeDtypeStruct((N, D), F32),
                   jax.ShapeDtypeStruct((N, D), F32),
                   jax.ShapeDtypeStruct((N, LANES), F32),
                   jax.ShapeDtypeStruct((1, LANES), F32)],
        scratch_shapes=[pltpu.VMEM((SUBLANES, CONV_HALO + T, cw), F32),
                        pltpu.VMEM((T, cw), BF16)],
        compiler_params=_cparams(("arbitrary",)),
        name="conv_outproj_router",
    )(attn, u, u, x2, mod, w_dw, b_dw, ln_g, ln_b, wo1, wo2, g_ffn, wr_hi, wr_lo, b_r)


def _row_of(ref, r):
    if isinstance(r, int):
        return ref.at[r // SUBLANES, pl.ds(r % SUBLANES, 1)]
    return ref.at[lax.shift_right_logical(r, 3), pl.ds(r & (SUBLANES - 1), 1)]


def _scatter_kernel(dest_ref, h2_ref, xs_hbm, sem):
    groups = h2_ref.shape[0]

    def issue(g, carry):
        for j in range(SUBLANES):
            for k in range(TOP_K_INNER):
                d = dest_ref[0, g * (SUBLANES * TOP_K_INNER) + j * TOP_K_INNER + k]
                pltpu.make_async_copy(h2_ref.at[g, pl.ds(j, 1)], _row_of(xs_hbm, d), sem).start()
        return carry

    lax.fori_loop(0, groups, issue, 0)
    for k in range(TOP_K_INNER):
        pltpu.make_async_copy(h2_ref, xs_hbm.at[pl.ds(0, groups)], sem).wait()


def _scatter_rows(h2, dest, *, tokens):
    N, C = h2.shape
    steps = N // tokens
    dest3 = dest.reshape(steps, 1, tokens * TOP_K_INNER)
    rows = N * TOP_K_INNER
    xs = pl.pallas_call(
        _scatter_kernel,
        grid=(steps,),
        in_specs=[pl.BlockSpec((None, 1, tokens * TOP_K_INNER), lambda s: (s, 0, 0), memory_space=pltpu.SMEM),
                  pl.BlockSpec((tokens // SUBLANES, SUBLANES, C), lambda s: (s, 0, 0))],
        out_specs=pl.BlockSpec(memory_space=pl.ANY),
        out_shape=jax.ShapeDtypeStruct((rows // SUBLANES, SUBLANES, C), h2.dtype),
        scratch_shapes=[pltpu.SemaphoreType.DMA],
        compiler_params=_cparams(("arbitrary",)),
        name="moe_scatter_rows",
    )(dest3, h2.reshape(N // SUBLANES, SUBLANES, C))
    return xs.reshape(rows, C)


def _experts_kernel(vt_ref, ve_ref, vlo_ref, vhi_ref, xs_ref, wg_ref, wu_ref, wd_ref, ys_ref):
    v = pl.program_id(0)
    lo = vlo_ref[v]
    hi = vhi_ref[v]

    @pl.when(hi > lo)
    def _():
        x = xs_ref[...].astype(BF16)
        g = jnp.dot(x, wg_ref[...].astype(BF16), preferred_element_type=F32)
        u = jnp.dot(x, wu_ref[...].astype(BF16), preferred_element_type=F32)
        hid = (_silu(g) * u).astype(BF16)
        y = jnp.dot(hid, wd_ref[...].astype(BF16), preferred_element_type=F32)

        @pl.when(lo == 0)
        def _():
            ys_ref[...] = y

        @pl.when(lo > 0)
        def _():
            row = lax.broadcasted_iota(jnp.int32, y.shape, 0)
            ys_ref[...] = jnp.where((row >= lo) & (row < hi), y, ys_ref[...])


def _experts(xs, w_gate, w_up, w_down, visits, *, tm):
    R, C = xs.shape
    E, D, ff = w_gate.shape
    vt, ve, vlo, vhi = visits
    grid_spec = pltpu.PrefetchScalarGridSpec(
        num_scalar_prefetch=4,
        grid=(vt.shape[0],),
        in_specs=[pl.BlockSpec((tm, C), lambda v, vt, ve, vlo, vhi: (vt[v], 0)),
                  pl.BlockSpec((None, D, ff), lambda v, vt, ve, vlo, vhi: (ve[v], 0, 0)),
                  pl.BlockSpec((None, D, ff), lambda v, vt, ve, vlo, vhi: (ve[v], 0, 0)),
                  pl.BlockSpec((None, ff, D), lambda v, vt, ve, vlo, vhi: (ve[v], 0, 0))],
        out_specs=pl.BlockSpec((tm, C), lambda v, vt, ve, vlo, vhi: (vt[v], 0)),
    )
    return pl.pallas_call(
        _experts_kernel,
        grid_spec=grid_spec,
        out_shape=jax.ShapeDtypeStruct((R, C), F32),
        compiler_params=_cparams(("arbitrary",)),
        name="moe_grouped_experts",
    )(vt, ve, vlo, vhi, xs, w_gate, w_up, w_down)


def _visit_tables(off, cnt, n_rows, tm):
    n_tiles = n_rows // tm
    n_exp = off.shape[0]
    n_visits = n_tiles + n_exp - 1
    tile_starts = jnp.arange(n_tiles, dtype=jnp.int32) * tm
    seg_starts = jnp.where((cnt > 0) & (off % tm != 0), off, n_rows)
    starts = jnp.sort(jnp.concatenate([tile_starts, seg_starts]))
    lo_abs = starts[:n_visits]
    hi_abs = starts[1:n_visits + 1]
    valid = lo_abs < n_rows
    tile = jnp.where(valid, lo_abs // tm, n_tiles - 1)
    ends = off + cnt
    probe = jnp.where(valid, lo_abs, n_rows - 1)
    expert = jnp.sum(ends[None, :] <= probe[:, None], axis=1).astype(jnp.int32)
    row_lo = jnp.where(valid, lo_abs - tile * tm, 0)
    row_hi = jnp.where(valid, hi_abs - tile * tm, 0)
    return tile, expert, row_lo, row_hi


def _combine_kernel(dcur_ref, dnxt_ref, ys_hbm, route_ref, x1_ref, mod_ref, o_ref, ybuf, sem):
    T = x1_ref.shape[0]
    i = pl.program_id(0)
    slot = lax.rem(i, 2)

    def gather(d_ref, sl):
        def body(g, carry):
            for j in range(SUBLANES):
                for k in range(TOP_K_INNER):
                    d = d_ref[0, g * (SUBLANES * TOP_K_INNER) + j * TOP_K_INNER + k]
                    pltpu.make_async_copy(_row_of(ys_hbm, d), ybuf.at[sl, k, g, pl.ds(j, 1)],
                                          sem.at[sl]).start()
            return carry
        lax.fori_loop(0, T // SUBLANES, body, 0)

    @pl.when(i == 0)
    def _():
        gather(dcur_ref, 0)

    @pl.when(i + 1 < pl.num_programs(0))
    def _():
        gather(dnxt_ref, 1 - slot)

    for k in range(TOP_K_INNER):
        pltpu.make_async_copy(ys_hbm.at[pl.ds(0, T // SUBLANES)], ybuf.at[slot, k], sem.at[slot]).wait()

    route = route_ref[...]
    moe = jnp.zeros(x1_ref.shape, F32)
    for k in range(TOP_K_INNER):
        moe = moe + route[:, TOP_K_INNER + k:TOP_K_INNER + k + 1] * ybuf[slot, k].reshape(x1_ref.shape)
    o_ref[...] = x1_ref[...] + mod_ref[5:6, :] * moe


def _combine(ys, dest, route, x1, mod, *, S, T):
    N, D = x1.shape
    C = ys.shape[1]
    steps = N // T
    nS = S // T
    dest3 = dest.reshape(steps, 1, T * TOP_K_INNER)
    dspec = lambda f: pl.BlockSpec((None, 1, T * TOP_K_INNER), f, memory_space=pltpu.SMEM)
    return pl.pallas_call(
        _combine_kernel,
        grid=(steps,),
        in_specs=[dspec(lambda i: (i, 0, 0)),
                  dspec(lambda i: (jnp.minimum(i + 1, steps - 1), 0, 0)),
                  pl.BlockSpec(memory_space=pl.ANY),
                  pl.BlockSpec((T, LANES), lambda i: (i, 0)),
                  pl.BlockSpec((T, D), lambda i: (i, 0)),
                  pl.BlockSpec((None, 6, D), lambda i: (i // nS, 0, 0))],
        out_specs=pl.BlockSpec((T, D), lambda i: (i, 0)),
        out_shape=jax.ShapeDtypeStruct((N, D), F32),
        scratch_shapes=[pltpu.VMEM((2, TOP_K_INNER, T // SUBLANES, SUBLANES, C), F32),
                        pltpu.SemaphoreType.DMA((2,))],
        compiler_params=_cparams(("arbitrary",)),
        name="moe_gather_combine",
    )(dest3, dest3, ys.reshape(ys.shape[0] // SUBLANES, SUBLANES, C), route, x1, mod)


def _layer(x2, mod, pos_row, l, B, S, g_mix, w_in, q_norm_g, k_norm_g, lambda_q1, lambda_k1, lambda_q2,
           lambda_k2, subln_g, b_glu, w_dw, b_dw, conv_ln_g, conv_ln_b, w_out, g_ffn, w_group, b_group,
           w_router, b_router, w_gate, w_up, w_down):
    N, D = x2.shape
    dh = q_norm_g.shape[0]
    H = N_DIFF_HEADS
    aw = H * 2 * dh
    cw = w_dw.shape[1]
    rot = dh // 4
    n_grp = w_group.shape[1]
    n_exp = w_router.shape[1]
    lambda_init = 0.8 - 0.6 * math.exp(-0.3 * l)

    w_qk_t = w_in[:, :2 * aw].T.astype(BF16)
    w_vglu = w_in[:, 2 * aw:].astype(BF16)
    scale = dh ** -0.5 * math.log2(math.e)
    gq = jnp.tile(q_norm_g * scale, aw // dh)
    gk = jnp.tile(k_norm_g, aw // dh)
    gqk_tab = jnp.broadcast_to(jnp.concatenate([gq, gk])[:, None], (2 * aw, LANES))
    inv_freq = ROPE_THETA ** (-jnp.arange(0, rot, 2, dtype=F32) / rot)
    invf_tab = jnp.broadcast_to(inv_freq[:, None], (rot // 2, LANES))

    q, kt, v, u = _inproj(x2, mod, g_mix.reshape(1, D), pos_row, w_vglu, w_qk_t, b_glu.reshape(1, 2 * cw),
                          gqk_tab, invf_tab, B=B, S=S, aw=aw, cw=cw, dh=dh, rot=rot, T=256)

    attn = _diff_attention(q, kt, v, lambda_q1.reshape(1, dh), lambda_k1.reshape(1, dh),
                           lambda_q2.reshape(1, dh), lambda_k2.reshape(1, dh), subln_g.reshape(1, 2 * dh),
                           B=B, S=S, H=H, dh=dh, lambda_init=lambda_init, tq=256, tk=1024)

    w_r = jnp.zeros((D, LANES), F32).at[:, :n_exp].set(w_router).at[:, n_exp:n_exp + n_grp].set(w_group)
    b_r = jnp.zeros((1, LANES), F32).at[0, :n_exp].set(b_router).at[0, n_exp:n_exp + n_grp].set(b_group)
    wr_hi = w_r.astype(BF16)
    wr_lo = (w_r - wr_hi.astype(F32)).astype(BF16)
    wo = w_out.astype(BF16)
    x1, h2, route, counts = _mixout(attn, u, x2, mod, w_dw, b_dw.reshape(1, cw), conv_ln_g.reshape(1, cw),
                                     conv_ln_b.reshape(1, cw), wo[:aw], wo[aw:], g_ffn.reshape(1, D),
                                     wr_hi, wr_lo, b_r, S=S, T=256, n_exp=n_exp, n_grp=n_grp)

    cnt = counts[0, :n_exp].astype(jnp.int32)
    off = jnp.cumsum(cnt) - cnt
    e_sel = route[:, 0:TOP_K_INNER].astype(jnp.int32)[..., None] == jnp.arange(n_exp, dtype=jnp.int32)
    dest = jnp.sum(jnp.where(e_sel, off, 0), axis=-1) + route[:, 4:4 + TOP_K_INNER].astype(jnp.int32)
    tm = 256
    visits = _visit_tables(off, cnt, N * TOP_K_INNER, tm)

    xs = _scatter_rows(h2, dest, tokens=min(1024, N))
    ys = _experts(xs, w_gate, w_up, w_down, visits, tm=tm)
    return _combine(ys, dest, route, x1, mod, S=S, T=256)


def kernel(x, c, positions, w_ada, b_ada, g_mix, w_in, q_norm_g, k_norm_g, lambda_q1, lambda_k1, lambda_q2,
           lambda_k2, subln_g, b_glu, w_dw, b_dw, conv_ln_g, conv_ln_b, w_out, g_ffn, w_group, b_group,
           w_router, b_router, w_gate, w_up, w_down):
    B, S, D = x.shape
    depth = w_ada.shape[0]
    x2 = x.reshape(B * S, D)
    pos_row = positions.astype(F32).reshape(1, B * S)
    for l in range(depth):
        mod = _modulation(c, w_ada[l], b_ada[l])
        x2 = _layer(x2, mod, pos_row, l, B, S, g_mix[l], w_in[l], q_norm_g[l], k_norm_g[l], lambda_q1[l],
                    lambda_k1[l], lambda_q2[l], lambda_k2[l], subln_g[l], b_glu[l], w_dw[l], b_dw[l],
                    conv_ln_g[l], conv_ln_b[l], w_out[l], g_ffn[l], w_group[l], b_group[l], w_router[l],
                    b_router[l], w_gate[l], w_up[l], w_down[l])
    return x2.reshape(B, S, D)
```

```python
import functools
import math

import numpy as np
import jax
import jax.numpy as jnp
from jax import lax
from jax.experimental import pallas as pl
from jax.experimental.pallas import tpu as pltpu

F32 = jnp.float32
BF16 = jnp.bfloat16

EPS = 1e-6
ROPE_THETA = 500000.0
N_DIFF_HEADS = 4
TOP_K_INNER = 2

LANES = 128
SUBLANES = 8
CONV_HALO = 32
VMEM_LIMIT = 48 * 1024 * 1024


def _cparams(sem):
    return pltpu.CompilerParams(dimension_semantics=sem, vmem_limit_bytes=VMEM_LIMIT)


def _silu(x):
    return x * jax.nn.sigmoid(x)


def _mod_kernel(c_ref, w_ref, b_ref, o_ref):
    c = c_ref[...]
    o_ref[...] = jnp.dot(_silu(c), w_ref[...], preferred_element_type=F32,
                         precision=lax.Precision.HIGHEST) + b_ref[...]


def _modulation(c, w_ada, b_ada):
    B, D = c.shape
    n_out = w_ada.shape[1]
    rows = 8
    c_pad = jnp.pad(c, ((0, rows - B), (0, 0)))
    bn = 1024
    out = pl.pallas_call(
        _mod_kernel,
        grid=(n_out // bn,),
        in_specs=[pl.BlockSpec((rows, D), lambda j: (0, 0)),
                  pl.BlockSpec((D, bn), lambda j: (0, j)),
                  pl.BlockSpec((1, bn), lambda j: (0, j))],
        out_specs=pl.BlockSpec((rows, bn), lambda j: (0, j)),
        out_shape=jax.ShapeDtypeStruct((rows, n_out), F32),
        compiler_params=_cparams(("parallel",)),
        name="adaln_mod",
    )(c_pad, w_ada, b_ada.reshape(1, n_out))
    return out[:B].reshape(B, 6, D)


def _inproj_kernel(x_ref, mod_ref, g_ref, pos_ref, wv_ref, wqk_ref, bglu_ref, gqk_ref, invf_ref,
                   q_ref, kt_ref, v_ref, u_ref, *, aw, cw, dh, rot):
    T = x_ref.shape[0]
    half = rot // 2
    x = x_ref[...]
    ms = jnp.mean(x * x, axis=-1, keepdims=True)
    sh = mod_ref[0:1, :]
    sc = mod_ref[1:2, :]
    h = x * lax.rsqrt(ms + EPS) * g_ref[...] * (1.0 + sc) + sh
    hb = h.astype(BF16)

    pv = jnp.dot(hb, wv_ref[...], preferred_element_type=F32)
    v_ref[...] = pv[:, :aw].astype(BF16)
    a = pv[:, aw:aw + cw] + bglu_ref[:, :cw]
    gate = pv[:, aw + cw:] + bglu_ref[:, cw:]
    u_ref[...] = (a * jax.nn.sigmoid(gate)).astype(BF16)

    qkt = lax.dot_general(wqk_ref[...], hb, (((1,), (1,)), ((), ())), preferred_element_type=F32)
    nch = 2 * aw // dh
    for c in range(T // LANES):
        sl = slice(c * LANES, (c + 1) * LANES)
        s3 = qkt[:, sl].reshape(nch, dh, LANES)
        ssq = jnp.mean(s3 * s3, axis=1, keepdims=True)
        y = s3 * lax.rsqrt(ssq + EPS) * gqk_ref[...].reshape(nch, dh, LANES)
        ang = invf_ref[...] * pos_ref[:, sl]
        cs = jnp.cos(ang)
        sn = jnp.sin(ang)
        t1 = y[:, 0:half, :]
        t2 = y[:, half:rot, :]
        y = jnp.concatenate([t1 * cs - t2 * sn, t2 * cs + t1 * sn, y[:, rot:, :]], axis=1)
        y2 = y.reshape(2 * aw, LANES)
        kt_ref[:, sl] = y2[aw:].astype(BF16)
        q_ref[sl, :] = y2[:aw].T.astype(BF16)


def _inproj(x2, mod, g_mix, pos_row, w_vglu, w_qk_t, b_glu, gqk_tab, invf_tab, *, B, S, aw, cw, dh, rot, T):
    N, D = x2.shape
    nS = S // T
    kern = functools.partial(_inproj_kernel, aw=aw, cw=cw, dh=dh, rot=rot)
    return pl.pallas_call(
        kern,
        grid=(N // T,),
        in_specs=[pl.BlockSpec((T, D), lambda i: (i, 0)),
                  pl.BlockSpec((None, 6, D), lambda i: (i // nS, 0, 0)),
                  pl.BlockSpec((1, D), lambda i: (0, 0)),
                  pl.BlockSpec((1, T), lambda i: (0, i)),
                  pl.BlockSpec(w_vglu.shape, lambda i: (0, 0)),
                  pl.BlockSpec(w_qk_t.shape, lambda i: (0, 0)),
                  pl.BlockSpec((1, 2 * cw), lambda i: (0, 0)),
                  pl.BlockSpec(gqk_tab.shape, lambda i: (0, 0)),
                  pl.BlockSpec(invf_tab.shape, lambda i: (0, 0))],
        out_specs=[pl.BlockSpec((T, aw), lambda i: (i, 0)),
                   pl.BlockSpec((None, aw, T), lambda i: (i // nS, 0, i % nS)),
                   pl.BlockSpec((T, aw), lambda i: (i, 0)),
                   pl.BlockSpec((T, cw), lambda i: (i, 0))],
        out_shape=[jax.ShapeDtypeStruct((N, aw), BF16),
                   jax.ShapeDtypeStruct((B, aw, S), BF16),
                   jax.ShapeDtypeStruct((N, aw), BF16),
                   jax.ShapeDtypeStruct((N, cw), BF16)],
        compiler_params=_cparams(("parallel",)),
        name="inproj_qknorm_rope_glu",
    )(x2, mod, g_mix, pos_row, w_vglu, w_qk_t, b_glu, gqk_tab, invf_tab)


def _attn_kernel(q_ref, kt_ref, v_ref, lq1_ref, lk1_ref, lq2_ref, lk2_ref, sg_ref, o_ref, acc_ref,
                 s0_ref, s1_ref, *, tq, tk, dh, lambda_init):
    i = pl.program_id(2)
    hd = 2 * dh
    q = q_ref[...]
    lane = lax.broadcasted_iota(jnp.int32, q.shape, 1)
    zero = jnp.zeros_like(q)
    q2 = jnp.concatenate([jnp.where(lane < dh, q, zero), jnp.where(lane >= dh, q, zero)], axis=0)

    acc_ref[...] = jnp.zeros_like(acc_ref)
    n_sub = tk // tq
    ones_col = {w: (lax.broadcasted_iota(jnp.int32, (w, hd), 1) == 0).astype(BF16)
                for w in {tk} | {(r + 1) * tq for r in range(n_sub)}}

    def scores(t, s_ref):
        start = pl.multiple_of(t * tk, tk)
        s_ref[...] = jnp.dot(q2, kt_ref[:, pl.ds(start, tk)], preferred_element_type=F32)

    def softmax_pv(t, s_ref, m, width=tk, diagonal=False):
        start = pl.multiple_of(t * tk, tk)
        vt = jnp.concatenate([v_ref[pl.ds(start, width), :], ones_col[width]], axis=1)
        s = s_ref[:, :width]
        if diagonal:
            row = lax.broadcasted_iota(jnp.int32, (2 * tq, tq), 0)
            col = lax.broadcasted_iota(jnp.int32, (2 * tq, tq), 1)
            qrow = jnp.where(row >= tq, row - tq, row)
            tail = jnp.where(col <= qrow, s[:, width - tq:], -jnp.inf)
            s = tail if width == tq else jnp.concatenate([s[:, :width - tq], tail], axis=1)
        m_new = jnp.maximum(m, jnp.max(s, axis=1, keepdims=True))
        alpha = jnp.exp2(m - m_new)
        p = jnp.exp2(s - m_new).astype(BF16)
        acc_ref[...] = alpha * acc_ref[...] + jnp.dot(p, vt, preferred_element_type=F32)
        return m_new

    n_full = (i * tq) // tk
    odd = lax.rem(n_full, 2)
    m0 = jnp.full((2 * tq, 1), -jnp.inf, F32)

    @pl.when(odd == 0)
    def _():
        scores(0, s0_ref)

    @pl.when(odd == 1)
    def _():
        scores(0, s1_ref)

    def peeled(_, m):
        scores(1, s0_ref)
        return softmax_pv(0, s1_ref, m)

    m = lax.fori_loop(0, odd, peeled, m0)

    def pair(jj, m):
        t = odd + 2 * jj
        scores(t + 1, s1_ref)
        m = softmax_pv(t, s0_ref, m)
        scores(t + 2, s0_ref)
        return softmax_pv(t + 1, s1_ref, m)

    m = lax.fori_loop(0, (n_full - odd) // 2, pair, m)

    for r in range(n_sub):
        @pl.when(lax.rem(i, n_sub) == r)
        def _(r=r):
            softmax_pv(n_full, s0_ref, m, width=(r + 1) * tq, diagonal=True)

    lam = (jnp.exp(jnp.sum(lq1_ref[...] * lk1_ref[...], axis=1, keepdims=True))
           - jnp.exp(jnp.sum(lq2_ref[...] * lk2_ref[...], axis=1, keepdims=True)) + lambda_init)
    o = (acc_ref[0:tq, 0:hd] / acc_ref[0:tq, hd:hd + 1]
         - lam * (acc_ref[tq:, 0:hd] / acc_ref[tq:, hd:hd + 1]))
    ms = jnp.mean(o * o, axis=1, keepdims=True)
    o = o * lax.rsqrt(ms + EPS) * sg_ref[...] * (1.0 - lambda_init)
    o_ref[...] = o.astype(o_ref.dtype)


def _diff_attention(q, kt, v, lq1, lk1, lq2, lk2, subln_g, *, B, S, H, dh, lambda_init, tq, tk):
    aw = H * 2 * dh
    q3 = q.reshape(B, S, aw)
    v3 = v.reshape(B, S, aw)
    hd = 2 * dh
    tk = min(tk, S)
    kern = functools.partial(_attn_kernel, tq=tq, tk=tk, dh=dh, lambda_init=lambda_init)
    vec = pl.BlockSpec((1, dh), lambda b, h, i: (0, 0))
    out = pl.pallas_call(
        kern,
        grid=(B, H, S // tq),
        in_specs=[pl.BlockSpec((None, tq, hd), lambda b, h, i: (b, i, h)),
                  pl.BlockSpec((None, hd, S), lambda b, h, i: (b, h, 0)),
                  pl.BlockSpec((None, S, hd), lambda b, h, i: (b, 0, h)),
                  vec, vec, vec, vec,
                  pl.BlockSpec((1, hd), lambda b, h, i: (0, 0))],
        out_specs=pl.BlockSpec((None, tq, hd), lambda b, h, i: (b, i, h)),
        out_shape=jax.ShapeDtypeStruct((B, S, aw), BF16),
        scratch_shapes=[pltpu.VMEM((2 * tq, 2 * hd), F32),
                        pltpu.VMEM((2 * tq, tk), F32),
                        pltpu.VMEM((2 * tq, tk), F32)],
        compiler_params=_cparams(("parallel", "parallel", "parallel")),
        name="diff_flash_attention",
    )(q3, kt, v3, lq1, lk1, lq2, lk2, subln_g)
    return out.reshape(B * S, aw)


def _mixout_kernel(attn_ref, ucur_ref, uhalo_ref, x_ref, mod_ref, wdw_ref, bdw_ref, lng_ref, lnb_ref,
                   wo1_ref, wo2_ref, gffn_ref, wrh_ref, wrl_ref, br_ref,
                   x1_ref, h2_ref, route_ref, cnt_ref, ubuf_ref, conv_ref,
                   *, nS, conv_k, n_exp, n_grp):
    T = x_ref.shape[0]
    i = pl.program_id(0)

    @pl.when(i == 0)
    def _():
        cnt_ref[...] = jnp.zeros_like(cnt_ref)

    first = (i % nS) == 0
    halo = uhalo_ref[...].astype(F32)
    ubuf_ref[0, 0:CONV_HALO, :] = jnp.where(first, jnp.zeros_like(halo), halo)
    ubuf_ref[0, CONV_HALO:, :] = ucur_ref[...].astype(F32)
    span = T + CONV_HALO - SUBLANES
    for b in range(1, SUBLANES):
        ubuf_ref[b, 0:span, :] = ubuf_ref[0, b:b + span, :]

    off = CONV_HALO - (conv_k - 1)
    rows = 32
    for r0 in range(0, T, rows):
        acc = jnp.zeros((rows, ucur_ref.shape[1]), F32)
        for j in range(conv_k):
            a, b = divmod(j + off, SUBLANES)
            lo_r = r0 + a * SUBLANES
            acc = acc + wdw_ref[j:j + 1, :] * ubuf_ref[b, lo_r:lo_r + rows, :]
        y = acc + bdw_ref[...]
        mu = jnp.mean(y, axis=1, keepdims=True)
        d = y - mu
        var = jnp.mean(d * d, axis=1, keepdims=True)
        z = d * lax.rsqrt(var + EPS) * lng_ref[...] + lnb_ref[...]
        conv_ref[r0:r0 + rows, :] = _silu(z).astype(BF16)

    yo = (jnp.dot(attn_ref[...], wo1_ref[...], preferred_element_type=F32)
          + jnp.dot(conv_ref[...], wo2_ref[...], preferred_element_type=F32))
    x1 = x_ref[...] + mod_ref[2:3, :] * yo
    x1_ref[...] = x1
    ms = jnp.mean(x1 * x1, axis=1, keepdims=True)
    h2 = x1 * lax.rsqrt(ms + EPS) * gffn_ref[...] * (1.0 + mod_ref[4:5, :]) + mod_ref[3:4, :]
    h2_ref[...] = h2
    hi = h2.astype(BF16)
    lo = (h2 - hi.astype(F32)).astype(BF16)

    logits = (jnp.dot(hi, wrh_ref[...], preferred_element_type=F32)
              + jnp.dot(lo, wrh_ref[...], preferred_element_type=F32)
              + jnp.dot(hi, wrl_ref[...], preferred_element_type=F32)) + br_ref[...]
    lane_i = lax.broadcasted_iota(jnp.int32, logits.shape, 1)
    lane = lane_i.astype(F32)
    big = jnp.float32(1e9)
    ninf = jnp.float32(-jnp.inf)
    is_g = (lane_i >= n_exp) & (lane_i < n_exp + n_grp)
    gl = jnp.where(is_g, logits, ninf)
    gmax = jnp.max(gl, axis=1, keepdims=True)
    gsum = jnp.sum(jnp.where(is_g, jnp.exp(gl - gmax), 0.0), axis=1, keepdims=True)
    g_p = 1.0 / gsum
    gidx = jnp.min(jnp.where(gl == gmax, lane, big), axis=1, keepdims=True) - n_exp
    epg = n_exp // n_grp
    lo_l = gidx * epg
    in_grp = (lane >= lo_l) & (lane < lo_l + epg)
    el = jnp.where(in_grp, logits, ninf)
    m1 = jnp.max(el, axis=1, keepdims=True)
    i1 = jnp.min(jnp.where(el == m1, lane, big), axis=1, keepdims=True)
    el2 = jnp.where(lane == i1, ninf, el)
    m2 = jnp.max(el2, axis=1, keepdims=True)
    i2 = jnp.min(jnp.where(el2 == m2, lane, big), axis=1, keepdims=True)
    e2 = jnp.exp(m2 - m1)
    p1 = 1.0 / (1.0 + e2)
    p2 = e2 / (1.0 + e2)

    sel = (lane == i1) | (lane == i2)
    rr = lax.broadcasted_iota(jnp.int32, (T, T), 0)
    cc = lax.broadcasted_iota(jnp.int32, (T, T), 1)
    before = jnp.dot((rr > cc).astype(BF16), sel.astype(BF16), preferred_element_type=F32) + cnt_ref[...]
    r1 = jnp.sum(jnp.where(lane == i1, before, 0.0), axis=1, keepdims=True)
    r2 = jnp.sum(jnp.where(lane == i2, before, 0.0), axis=1, keepdims=True)
    cnt_ref[...] += jnp.sum(sel.astype(F32), axis=0, keepdims=True)

    route = jnp.zeros_like(logits)
    for k, val in enumerate((i1, i2, g_p * p1, g_p * p2, r1, r2)):
        route = jnp.where(lane_i == k, val, route)
    route_ref[...] = route


def _mixout(attn, u, x2, mod, w_dw, b_dw, ln_g, ln_b, wo1, wo2, g_ffn, wr_hi, wr_lo, b_r,
            *, S, T, n_exp, n_grp):
    N, D = x2.shape
    aw = attn.shape[1]
    cw = u.shape[1]
    nS = S // T
    conv_k = w_dw.shape[0]
    hb = T // CONV_HALO
    kern = functools.partial(_mixout_kernel, nS=nS, conv_k=conv_k, n_exp=n_exp, n_grp=n_grp)
    full = lambda a: pl.BlockSpec(a.shape, lambda i: (0, 0))
    return pl.pallas_call(
        kern,
        grid=(N // T,),
        in_specs=[pl.BlockSpec((T, aw), lambda i: (i, 0)),
                  pl.BlockSpec((T, cw), lambda i: (i, 0)),
                  pl.BlockSpec((CONV_HALO, cw), lambda i: (jnp.maximum(i * hb - 1, 0), 0)),
                  pl.BlockSpec((T, D), lambda i: (i, 0)),
                  pl.BlockSpec((None, 6, D), lambda i: (i // nS, 0, 0)),
                  full(w_dw), full(b_dw), full(ln_g), full(ln_b), full(wo1), full(wo2), full(g_ffn),
                  full(wr_hi), full(wr_lo), full(b_r)],
        out_specs=[pl.BlockSpec((T, D), lambda i: (i, 0)),
                   pl.BlockSpec((T, D), lambda i: (i, 0)),
                   pl.BlockSpec((T, LANES), lambda i: (i, 0)),
                   pl.BlockSpec((1, LANES), lambda i: (0, 0))],
        out_shape=[jax.ShapeDtypeStruct((N, D), F32),
                   jax.ShapeDtypeStruct((N, D), F32),
                   jax.ShapeDtypeStruct((N, LANES), F32),
                   jax.ShapeDtypeStruct((1, LANES), F32)],
        scratch_shapes=[pltpu.VMEM((SUBLANES, CONV_HALO + T, cw), F32),
                        pltpu.VMEM((T, cw), BF16)],
        compiler_params=_cparams(("arbitrary",)),
        name="conv_outproj_router",
    )(attn, u, u, x2, mod, w_dw, b_dw, ln_g, ln_b, wo1, wo2, g_ffn, wr_hi, wr_lo, b_r)


def _row_of(ref, r):
    if isinstance(r, int):
        return ref.at[r // SUBLANES, pl.ds(r % SUBLANES, 1)]
    return ref.at[lax.shift_right_logical(r, 3), pl.ds(r & (SUBLANES - 1), 1)]


def _scatter_kernel(dest_ref, h2_ref, xs_hbm, sem):
    groups = h2_ref.shape[0]

    def issue(g, carry):
        for j in range(SUBLANES):
            for k in range(TOP_K_INNER):
                d = dest_ref[0, g * (SUBLANES * TOP_K_INNER) + j * TOP_K_INNER + k]
                pltpu.make_async_copy(h2_ref.at[g, pl.ds(j, 1)], _row_of(xs_hbm, d), sem).start(priority=k % 2)
        return carry

    lax.fori_loop(0, groups, issue, 0)
    for k in range(TOP_K_INNER):
        pltpu.make_async_copy(h2_ref, xs_hbm.at[pl.ds(0, groups)], sem).wait()


def _scatter_rows(h2, dest, *, tokens):
    N, C = h2.shape
    steps = N // tokens
    dest3 = dest.reshape(steps, 1, tokens * TOP_K_INNER)
    rows = N * TOP_K_INNER
    xs = pl.pallas_call(
        _scatter_kernel,
        grid=(steps,),
        in_specs=[pl.BlockSpec((None, 1, tokens * TOP_K_INNER), lambda s: (s, 0, 0), memory_space=pltpu.SMEM),
                  pl.BlockSpec((tokens // SUBLANES, SUBLANES, C), lambda s: (s, 0, 0))],
        out_specs=pl.BlockSpec(memory_space=pl.ANY),
        out_shape=jax.ShapeDtypeStruct((rows // SUBLANES, SUBLANES, C), h2.dtype),
        scratch_shapes=[pltpu.SemaphoreType.DMA],
        compiler_params=_cparams(("arbitrary",)),
        name="moe_scatter_rows",
    )(dest3, h2.reshape(N // SUBLANES, SUBLANES, C))
    return xs.reshape(rows, C)


def _experts_kernel(vt_ref, ve_ref, vlo_ref, vhi_ref, xs_ref, wg_ref, wu_ref, wd_ref, ys_ref):
    v = pl.program_id(0)
    lo = vlo_ref[v]
    hi = vhi_ref[v]

    @pl.when(hi > lo)
    def _():
        x = xs_ref[...].astype(BF16)
        g = jnp.dot(x, wg_ref[...].astype(BF16), preferred_element_type=F32)
        u = jnp.dot(x, wu_ref[...].astype(BF16), preferred_element_type=F32)
        hid = (_silu(g) * u).astype(BF16)
        y = jnp.dot(hid, wd_ref[...].astype(BF16), preferred_element_type=F32)

        @pl.when(lo == 0)
        def _():
            ys_ref[...] = y

        @pl.when(lo > 0)
        def _():
            row = lax.broadcasted_iota(jnp.int32, y.shape, 0)
            ys_ref[...] = jnp.where((row >= lo) & (row < hi), y, ys_ref[...])


def _experts(xs, w_gate, w_up, w_down, visits, *, tm):
    R, C = xs.shape
    E, D, ff = w_gate.shape
    vt, ve, vlo, vhi = visits
    grid_spec = pltpu.PrefetchScalarGridSpec(
        num_scalar_prefetch=4,
        grid=(vt.shape[0],),
        in_specs=[pl.BlockSpec((tm, C), lambda v, vt, ve, vlo, vhi: (vt[v], 0)),
                  pl.BlockSpec((None, D, ff), lambda v, vt, ve, vlo, vhi: (ve[v], 0, 0)),
                  pl.BlockSpec((None, D, ff), lambda v, vt, ve, vlo, vhi: (ve[v], 0, 0)),
                  pl.BlockSpec((None, ff, D), lambda v, vt, ve, vlo, vhi: (ve[v], 0, 0))],
        out_specs=pl.BlockSpec((tm, C), lambda v, vt, ve, vlo, vhi: (vt[v], 0)),
    )
    return pl.pallas_call(
        _experts_kernel,
        grid_spec=grid_spec,
        out_shape=jax.ShapeDtypeStruct((R, C), F32),
        compiler_params=_cparams(("arbitrary",)),
        name="moe_grouped_experts",
    )(vt, ve, vlo, vhi, xs, w_gate, w_up, w_down)


def _visit_tables(off, cnt, n_rows, tm):
    n_tiles = n_rows // tm
    n_exp = off.shape[0]
    n_visits = n_tiles + n_exp - 1
    tile_starts = jnp.arange(n_tiles, dtype=jnp.int32) * tm
    seg_starts = jnp.where((cnt > 0) & (off % tm != 0), off, n_rows)
    starts = jnp.sort(jnp.concatenate([tile_starts, seg_starts]))
    lo_abs = starts[:n_visits]
    hi_abs = starts[1:n_visits + 1]
    valid = lo_abs < n_rows
    tile = jnp.where(valid, lo_abs // tm, n_tiles - 1)
    ends = off + cnt
    probe = jnp.where(valid, lo_abs, n_rows - 1)
    expert = jnp.sum(ends[None, :] <= probe[:, None], axis=1).astype(jnp.int32)
    row_lo = jnp.where(valid, lo_abs - tile * tm, 0)
    row_hi = jnp.where(valid, hi_abs - tile * tm, 0)
    return tile, expert, row_lo, row_hi


def _combine_kernel(dcur_ref, dnxt_ref, ys_hbm, route_ref, x1_ref, mod_ref, o_ref, ybuf, sem):
    T = x1_ref.shape[0]
    i = pl.program_id(0)
    slot = lax.rem(i, 2)

    def gather(d_ref, sl):
        def body(g, carry):
            for j in range(SUBLANES):
                for k in range(TOP_K_INNER):
                    d = d_ref[0, g * (SUBLANES * TOP_K_INNER) + j * TOP_K_INNER + k]
                    pltpu.make_async_copy(_row_of(ys_hbm, d), ybuf.at[sl, k, g, pl.ds(j, 1)],
                                          sem.at[sl]).start(priority=k % 2)
            return carry
        lax.fori_loop(0, T // SUBLANES, body, 0)

    @pl.when(i == 0)
    def _():
        gather(dcur_ref, 0)

    @pl.when(i + 1 < pl.num_programs(0))
    def _():
        gather(dnxt_ref, 1 - slot)

    for k in range(TOP_K_INNER):
        pltpu.make_async_copy(ys_hbm.at[pl.ds(0, T // SUBLANES)], ybuf.at[slot, k], sem.at[slot]).wait()

    route = route_ref[...]
    moe = jnp.zeros(x1_ref.shape, F32)
    for k in range(TOP_K_INNER):
        moe = moe + route[:, TOP_K_INNER + k:TOP_K_INNER + k + 1] * ybuf[slot, k].reshape(x1_ref.shape)
    o_ref[...] = x1_ref[...] + mod_ref[5:6, :] * moe


def _combine(ys, dest, route, x1, mod, *, S, T):
    N, D = x1.shape
    C = ys.shape[1]
    steps = N // T
    nS = S // T
    dest3 = dest.reshape(steps, 1, T * TOP_K_INNER)
    dspec = lambda f: pl.BlockSpec((None, 1, T * TOP_K_INNER), f, memory_space=pltpu.SMEM)
    return pl.pallas_call(
        _combine_kernel,
        grid=(steps,),
        in_specs=[dspec(lambda i: (i, 0, 0)),
                  dspec(lambda i: (jnp.minimum(i + 1, steps - 1), 0, 0)),
                  pl.BlockSpec(memory_space=pl.ANY),
                  pl.BlockSpec((T, LANES), lambda i: (i, 0)),
                  pl.BlockSpec((T, D), lambda i: (i, 0)),
                  pl.BlockSpec((None, 6, D), lambda i: (i // nS, 0, 0))],
        out_specs=pl.BlockSpec((T, D), lambda i: (i, 0)),
        out_shape=jax.ShapeDtypeStruct((N, D), F32),
        scratch_shapes=[pltpu.VMEM((2, TOP_K_INNER, T // SUBLANES, SUBLANES, C), F32),
                        pltpu.SemaphoreType.DMA((2,))],
        compiler_params=_cparams(("arbitrary",)),
        name="moe_gather_combine",
    )(dest3, dest3, ys.reshape(ys.shape[0] // SUBLANES, SUBLANES, C), route, x1, mod)


def _layer(x2, mod, pos_row, l, B, S, g_mix, w_in, q_norm_g, k_norm_g, lambda_q1, lambda_k1, lambda_q2,
           lambda_k2, subln_g, b_glu, w_dw, b_dw, conv_ln_g, conv_ln_b, w_out, g_ffn, w_group, b_group,
           w_router, b_router, w_gate, w_up, w_down):
    N, D = x2.shape
    dh = q_norm_g.shape[0]
    H = N_DIFF_HEADS
    aw = H * 2 * dh
    cw = w_dw.shape[1]
    rot = dh // 4
    n_grp = w_group.shape[1]
    n_exp = w_router.shape[1]
    lambda_init = 0.8 - 0.6 * math.exp(-0.3 * l)

    w_qk_t = w_in[:, :2 * aw].T.astype(BF16)
    w_vglu = w_in[:, 2 * aw:].astype(BF16)
    scale = dh ** -0.5 * math.log2(math.e)
    gq = jnp.tile(q_norm_g * scale, aw // dh)
    gk = jnp.tile(k_norm_g, aw // dh)
    gqk_tab = jnp.broadcast_to(jnp.concatenate([gq, gk])[:, None], (2 * aw, LANES))
    inv_freq = ROPE_THETA ** (-jnp.arange(0, rot, 2, dtype=F32) / rot)
    invf_tab = jnp.broadcast_to(inv_freq[:, None], (rot // 2, LANES))

    q, kt, v, u = _inproj(x2, mod, g_mix.reshape(1, D), pos_row, w_vglu, w_qk_t, b_glu.reshape(1, 2 * cw),
                          gqk_tab, invf_tab, B=B, S=S, aw=aw, cw=cw, dh=dh, rot=rot, T=256)

    attn = _diff_attention(q, kt, v, lambda_q1.reshape(1, dh), lambda_k1.reshape(1, dh),
                           lambda_q2.reshape(1, dh), lambda_k2.reshape(1, dh), subln_g.reshape(1, 2 * dh),
                           B=B, S=S, H=H, dh=dh, lambda_init=lambda_init, tq=256, tk=1024)

    w_r = jnp.zeros((D, LANES), F32).at[:, :n_exp].set(w_router).at[:, n_exp:n_exp + n_grp].set(w_group)
    b_r = jnp.zeros((1, LANES), F32).at[0, :n_exp].set(b_router).at[0, n_exp:n_exp + n_grp].set(b_group)
    wr_hi = w_r.astype(BF16)
    wr_lo = (w_r - wr_hi.astype(F32)).astype(BF16)
    wo = w_out.astype(BF16)
    x1, h2, route, counts = _mixout(attn, u, x2, mod, w_dw, b_dw.reshape(1, cw), conv_ln_g.reshape(1, cw),
                                     conv_ln_b.reshape(1, cw), wo[:aw], wo[aw:], g_ffn.reshape(1, D),
                                     wr_hi, wr_lo, b_r, S=S, T=256, n_exp=n_exp, n_grp=n_grp)

    cnt = counts[0, :n_exp].astype(jnp.int32)
    off = jnp.cumsum(cnt) - cnt
    e_sel = route[:, 0:TOP_K_INNER].astype(jnp.int32)[..., None] == jnp.arange(n_exp, dtype=jnp.int32)
    dest = jnp.sum(jnp.where(e_sel, off, 0), axis=-1) + route[:, 4:4 + TOP_K_INNER].astype(jnp.int32)
    tm = 256
    visits = _visit_tables(off, cnt, N * TOP_K_INNER, tm)

    xs = _scatter_rows(h2, dest, tokens=min(1024, N))
    ys = _experts(xs, w_gate, w_up, w_down, visits, tm=tm)
    return _combine(ys, dest, route, x1, mod, S=S, T=256)


def kernel(x, c, positions, w_ada, b_ada, g_mix, w_in, q_norm_g, k_norm_g, lambda_q1, lambda_k1, lambda_q2,
           lambda_k2, subln_g, b_glu, w_dw, b_dw, conv_ln_g, conv_ln_b, w_out, g_ffn, w_group, b_group,
           w_router, b_router, w_gate, w_up, w_down):
    B, S, D = x.shape
    depth = w_ada.shape[0]
    x2 = x.reshape(B * S, D)
    pos_row = positions.astype(F32).reshape(1, B * S)
    for l in range(depth):
        mod = _modulation(c, w_ada[l], b_ada[l])
        x2 = _layer(x2, mod, pos_row, l, B, S, g_mix[l], w_in[l], q_norm_g[l], k_norm_g[l], lambda_q1[l],
                    lambda_k1[l], lambda_q2[l], lambda_k2[l], subln_g[l], b_glu[l], w_dw[l], b_dw[l],
                    conv_ln_g[l], conv_ln_b[l], w_out[l], g_ffn[l], w_group[l], b_group[l], w_router[l],
                    b_router[l], w_gate[l], w_up[l], w_down[l])
    return x2.reshape(B, S, D)
```

```python
import functools
import math

import numpy as np
import jax
import jax.numpy as jnp
from jax import lax
from jax.experimental import pallas as pl
from jax.experimental.pallas import tpu as pltpu

F32 = jnp.float32
BF16 = jnp.bfloat16

EPS = 1e-6
ROPE_THETA = 500000.0
N_DIFF_HEADS = 4
TOP_K_INNER = 2

LANES = 128
SUBLANES = 8
CONV_HALO = 32
VMEM_LIMIT = 48 * 1024 * 1024


def _cparams(sem):
    return pltpu.CompilerParams(dimension_semantics=sem, vmem_limit_bytes=VMEM_LIMIT)


def _silu(x):
    return x * jax.nn.sigmoid(x)


def _mod_kernel(c_ref, w_ref, b_ref, o_ref):
    c = c_ref[...]
    o_ref[...] = jnp.dot(_silu(c), w_ref[...], preferred_element_type=F32,
                         precision=lax.Precision.HIGHEST) + b_ref[...]


def _modulation(c, w_ada, b_ada):
    B, D = c.shape
    n_out = w_ada.shape[1]
    rows = 8
    c_pad = jnp.pad(c, ((0, rows - B), (0, 0)))
    bn = 1024
    out = pl.pallas_call(
        _mod_kernel,
        grid=(n_out // bn,),
        in_specs=[pl.BlockSpec((rows, D), lambda j: (0, 0)),
                  pl.BlockSpec((D, bn), lambda j: (0, j)),
                  pl.BlockSpec((1, bn), lambda j: (0, j))],
        out_specs=pl.BlockSpec((rows, bn), lambda j: (0, j)),
        out_shape=jax.ShapeDtypeStruct((rows, n_out), F32),
        compiler_params=_cparams(("parallel",)),
        name="adaln_mod",
    )(c_pad, w_ada, b_ada.reshape(1, n_out))
    return out[:B].reshape(B, 6, D)


def _inproj_kernel(x_ref, mod_ref, g_ref, pos_ref, wv_ref, wqk_ref, bglu_ref, gqk_ref, invf_ref,
                   q_ref, kt_ref, v_ref, u_ref, *, aw, cw, dh, rot):
    T = x_ref.shape[0]
    half = rot // 2
    x = x_ref[...]
    ms = jnp.mean(x * x, axis=-1, keepdims=True)
    sh = mod_ref[0:1, :]
    sc = mod_ref[1:2, :]
    h = x * lax.rsqrt(ms + EPS) * g_ref[...] * (1.0 + sc) + sh
    hb = h.astype(BF16)

    pv = jnp.dot(hb, wv_ref[...], preferred_element_type=F32)
    v_ref[...] = pv[:, :aw].astype(BF16)
    a = pv[:, aw:aw + cw] + bglu_ref[:, :cw]
    gate = pv[:, aw + cw:] + bglu_ref[:, cw:]
    u_ref[...] = (a * jax.nn.sigmoid(gate)).astype(BF16)

    qkt = lax.dot_general(wqk_ref[...], hb, (((1,), (1,)), ((), ())), preferred_element_type=F32)
    nch = 2 * aw // dh
    for c in range(T // LANES):
        sl = slice(c * LANES, (c + 1) * LANES)
        s3 = qkt[:, sl].reshape(nch, dh, LANES)
        ssq = jnp.mean(s3 * s3, axis=1, keepdims=True)
        y = s3 * lax.rsqrt(ssq + EPS) * gqk_ref[...].reshape(nch, dh, LANES)
        ang = invf_ref[...] * pos_ref[:, sl]
        cs = jnp.cos(ang)
        sn = jnp.sin(ang)
        t1 = y[:, 0:half, :]
        t2 = y[:, half:rot, :]
        y = jnp.concatenate([t1 * cs - t2 * sn, t2 * cs + t1 * sn, y[:, rot:, :]], axis=1)
        y2 = y.reshape(2 * aw, LANES)
        kt_ref[:, sl] = y2[aw:].astype(BF16)
        q_ref[sl, :] = y2[:aw].T.astype(BF16)


def _inproj(x2, mod, g_mix, pos_row, w_vglu, w_qk_t, b_glu, gqk_tab, invf_tab, *, B, S, aw, cw, dh, rot, T):
    N, D = x2.shape
    nS = S // T
    kern = functools.partial(_inproj_kernel, aw=aw, cw=cw, dh=dh, rot=rot)
    return pl.pallas_call(
        kern,
        grid=(N // T,),
        in_specs=[pl.BlockSpec((T, D), lambda i: (i, 0)),
                  pl.BlockSpec((None, 6, D), lambda i: (i // nS, 0, 0)),
                  pl.BlockSpec((1, D), lambda i: (0, 0)),
                  pl.BlockSpec((1, T), lambda i: (0, i)),
                  pl.BlockSpec(w_vglu.shape, lambda i: (0, 0)),
                  pl.BlockSpec(w_qk_t.shape, lambda i: (0, 0)),
                  pl.BlockSpec((1, 2 * cw), lambda i: (0, 0)),
                  pl.BlockSpec(gqk_tab.shape, lambda i: (0, 0)),
                  pl.BlockSpec(invf_tab.shape, lambda i: (0, 0))],
        out_specs=[pl.BlockSpec((T, aw), lambda i: (i, 0)),
                   pl.BlockSpec((None, aw, T), lambda i: (i // nS, 0, i % nS)),
                   pl.BlockSpec((T, aw), lambda i: (i, 0)),
                   pl.BlockSpec((T, cw), lambda i: (i, 0))],
        out_shape=[jax.ShapeDtypeStruct((N, aw), BF16),
                   jax.ShapeDtypeStruct((B, aw, S), BF16),
                   jax.ShapeDtypeStruct((N, aw), BF16),
                   jax.ShapeDtypeStruct((N, cw), BF16)],
        compiler_params=_cparams(("parallel",)),
        name="inproj_qknorm_rope_glu",
    )(x2, mod, g_mix, pos_row, w_vglu, w_qk_t, b_glu, gqk_tab, invf_tab)


def _attn_kernel(q_ref, kt_ref, v_ref, lq1_ref, lk1_ref, lq2_ref, lk2_ref, sg_ref, o_ref, acc_ref,
                 s0_ref, s1_ref, *, tq, tk, dh, lambda_init):
    i = pl.program_id(2)
    hd = 2 * dh
    q = q_ref[...]
    lane = lax.broadcasted_iota(jnp.int32, q.shape, 1)
    zero = jnp.zeros_like(q)
    q2 = jnp.concatenate([jnp.where(lane < dh, q, zero), jnp.where(lane >= dh, q, zero)], axis=0)

    acc_ref[...] = jnp.zeros_like(acc_ref)
    n_sub = tk // tq
    ones_col = {w: jnp.ones((w, hd), BF16) for w in {tk} | {(r + 1) * tq for r in range(n_sub)}}

    def scores(t, s_ref):
        start = pl.multiple_of(t * tk, tk)
        s_ref[...] = jnp.dot(q2, kt_ref[:, pl.ds(start, tk)], preferred_element_type=F32)

    def softmax_pv(t, s_ref, m, width=tk, diagonal=False):
        start = pl.multiple_of(t * tk, tk)
        vt = jnp.concatenate([v_ref[pl.ds(start, width), :], ones_col[width]], axis=1)
        s = s_ref[:, :width]
        if diagonal:
            row = lax.broadcasted_iota(jnp.int32, (2 * tq, tq), 0)
            col = lax.broadcasted_iota(jnp.int32, (2 * tq, tq), 1)
            qrow = jnp.where(row >= tq, row - tq, row)
            tail = jnp.where(col <= qrow, s[:, width - tq:], -jnp.inf)
            s = tail if width == tq else jnp.concatenate([s[:, :width - tq], tail], axis=1)
        m_new = jnp.maximum(m, jnp.max(s, axis=1, keepdims=True))
        alpha = jnp.exp2(m - m_new)
        p = jnp.exp2(s - m_new).astype(BF16)
        acc_ref[...] = alpha * acc_ref[...] + jnp.dot(p, vt, preferred_element_type=F32)
        return m_new

    n_full = (i * tq) // tk
    odd = lax.rem(n_full, 2)
    m0 = jnp.full((2 * tq, 1), -jnp.inf, F32)

    @pl.when(odd == 0)
    def _():
        scores(0, s0_ref)

    @pl.when(odd == 1)
    def _():
        scores(0, s1_ref)

    def peeled(_, m):
        scores(1, s0_ref)
        return softmax_pv(0, s1_ref, m)

    m = lax.fori_loop(0, odd, peeled, m0)

    def pair(jj, m):
        t = odd + 2 * jj
        scores(t + 1, s1_ref)
        m = softmax_pv(t, s0_ref, m)
        scores(t + 2, s0_ref)
        return softmax_pv(t + 1, s1_ref, m)

    m = lax.fori_loop(0, (n_full - odd) // 2, pair, m)

    for r in range(n_sub):
        @pl.when(lax.rem(i, n_sub) == r)
        def _(r=r):
            softmax_pv(n_full, s0_ref, m, width=(r + 1) * tq, diagonal=True)

    lam = (jnp.exp(jnp.sum(lq1_ref[...] * lk1_ref[...], axis=1, keepdims=True))
           - jnp.exp(jnp.sum(lq2_ref[...] * lk2_ref[...], axis=1, keepdims=True)) + lambda_init)
    o = (acc_ref[0:tq, 0:hd] / acc_ref[0:tq, hd:]
         - lam * (acc_ref[tq:, 0:hd] / acc_ref[tq:, hd:]))
    ms = jnp.mean(o * o, axis=1, keepdims=True)
    o = o * lax.rsqrt(ms + EPS) * sg_ref[...] * (1.0 - lambda_init)
    o_ref[...] = o.astype(o_ref.dtype)


def _diff_attention(q, kt, v, lq1, lk1, lq2, lk2, subln_g, *, B, S, H, dh, lambda_init, tq, tk):
    aw = H * 2 * dh
    q3 = q.reshape(B, S, aw)
    v3 = v.reshape(B, S, aw)
    hd = 2 * dh
    tk = min(tk, S)
    kern = functools.partial(_attn_kernel, tq=tq, tk=tk, dh=dh, lambda_init=lambda_init)
    vec = pl.BlockSpec((1, dh), lambda b, h, i: (0, 0))
    out = pl.pallas_call(
        kern,
        grid=(B, H, S // tq),
        in_specs=[pl.BlockSpec((None, tq, hd), lambda b, h, i: (b, i, h)),
                  pl.BlockSpec((None, hd, S), lambda b, h, i: (b, h, 0)),
                  pl.BlockSpec((None, S, hd), lambda b, h, i: (b, 0, h)),
                  vec, vec, vec, vec,
                  pl.BlockSpec((1, hd), lambda b, h, i: (0, 0))],
        out_specs=pl.BlockSpec((None, tq, hd), lambda b, h, i: (b, i, h)),
        out_shape=jax.ShapeDtypeStruct((B, S, aw), BF16),
        scratch_shapes=[pltpu.VMEM((2 * tq, 2 * hd), F32),
                        pltpu.VMEM((2 * tq, tk), F32),
                        pltpu.VMEM((2 * tq, tk), F32)],
        compiler_params=_cparams(("parallel", "parallel", "parallel")),
        name="diff_flash_attention",
    )(q3, kt, v3, lq1, lk1, lq2, lk2, subln_g)
    return out.reshape(B * S, aw)


def _mixout_kernel(attn_ref, ucur_ref, uhalo_ref, x_ref, mod_ref, wdw_ref, bdw_ref, lng_ref, lnb_ref,
                   wo1_ref, wo2_ref, gffn_ref, wrh_ref, wrl_ref, br_ref,
                   x1_ref, h2_ref, route_ref, cnt_ref, ubuf_ref, conv_ref,
                   *, nS, conv_k, n_exp, n_grp):
    T = x_ref.shape[0]
    i = pl.program_id(0)

    @pl.when(i == 0)
    def _():
        cnt_ref[...] = jnp.zeros_like(cnt_ref)

    first = (i % nS) == 0
    halo = uhalo_ref[...].astype(F32)
    ubuf_ref[0, 0:CONV_HALO, :] = jnp.where(first, jnp.zeros_like(halo), halo)
    ubuf_ref[0, CONV_HALO:, :] = ucur_ref[...].astype(F32)
    span = T + CONV_HALO - SUBLANES
    for b in range(1, SUBLANES):
        ubuf_ref[b, 0:span, :] = ubuf_ref[0, b:b + span, :]

    off = CONV_HALO - (conv_k - 1)
    rows = 32
    for r0 in range(0, T, rows):
        acc = jnp.zeros((rows, ucur_ref.shape[1]), F32)
        for j in range(conv_k):
            a, b = divmod(j + off, SUBLANES)
            lo_r = r0 + a * SUBLANES
            acc = acc + wdw_ref[j:j + 1, :] * ubuf_ref[b, lo_r:lo_r + rows, :]
        y = acc + bdw_ref[...]
        mu = jnp.mean(y, axis=1, keepdims=True)
        d = y - mu
        var = jnp.mean(d * d, axis=1, keepdims=True)
        z = d * lax.rsqrt(var + EPS) * lng_ref[...] + lnb_ref[...]
        conv_ref[r0:r0 + rows, :] = _silu(z).astype(BF16)

    yo = (jnp.dot(attn_ref[...], wo1_ref[...], preferred_element_type=F32)
          + jnp.dot(conv_ref[...], wo2_ref[...], preferred_element_type=F32))
    x1 = x_ref[...] + mod_ref[2:3, :] * yo
    x1_ref[...] = x1
    ms = jnp.mean(x1 * x1, axis=1, keepdims=True)
    h2 = x1 * lax.rsqrt(ms + EPS) * gffn_ref[...] * (1.0 + mod_ref[4:5, :]) + mod_ref[3:4, :]
    h2_ref[...] = h2
    hi = h2.astype(BF16)
    lo = (h2 - hi.astype(F32)).astype(BF16)

    logits = (jnp.dot(hi, wrh_ref[...], preferred_element_type=F32)
              + jnp.dot(lo, wrh_ref[...], preferred_element_type=F32)
              + jnp.dot(hi, wrl_ref[...], preferred_element_type=F32)) + br_ref[...]
    lane_i = lax.broadcasted_iota(jnp.int32, logits.shape, 1)
    lane = lane_i.astype(F32)
    big = jnp.float32(1e9)
    ninf = jnp.float32(-jnp.inf)
    is_g = (lane_i >= n_exp) & (lane_i < n_exp + n_grp)
    gl = jnp.where(is_g, logits, ninf)
    gmax = jnp.max(gl, axis=1, keepdims=True)
    gsum = jnp.sum(jnp.where(is_g, jnp.exp(gl - gmax), 0.0), axis=1, keepdims=True)
    g_p = 1.0 / gsum
    gidx = jnp.min(jnp.where(gl == gmax, lane, big), axis=1, keepdims=True) - n_exp
    epg = n_exp // n_grp
    lo_l = gidx * epg
    in_grp = (lane >= lo_l) & (lane < lo_l + epg)
    el = jnp.where(in_grp, logits, ninf)
    m1 = jnp.max(el, axis=1, keepdims=True)
    i1 = jnp.min(jnp.where(el == m1, lane, big), axis=1, keepdims=True)
    el2 = jnp.where(lane == i1, ninf, el)
    m2 = jnp.max(el2, axis=1, keepdims=True)
    i2 = jnp.min(jnp.where(el2 == m2, lane, big), axis=1, keepdims=True)
    e2 = jnp.exp(m2 - m1)
    p1 = 1.0 / (1.0 + e2)
    p2 = e2 / (1.0 + e2)

    sel = (lane == i1) | (lane == i2)
    rr = lax.broadcasted_iota(jnp.int32, (T, T), 0)
    cc = lax.broadcasted_iota(jnp.int32, (T, T), 1)
    before = jnp.dot((rr > cc).astype(BF16), sel.astype(BF16), preferred_element_type=F32) + cnt_ref[...]
    r1 = jnp.sum(jnp.where(lane == i1, before, 0.0), axis=1, keepdims=True)
    r2 = jnp.sum(jnp.where(lane == i2, before, 0.0), axis=1, keepdims=True)
    cnt_ref[...] += jnp.sum(sel.astype(F32), axis=0, keepdims=True)

    route = jnp.zeros_like(logits)
    for k, val in enumerate((i1, i2, g_p * p1, g_p * p2, r1, r2)):
        route = jnp.where(lane_i == k, val, route)
    route_ref[...] = route


def _mixout(attn, u, x2, mod, w_dw, b_dw, ln_g, ln_b, wo1, wo2, g_ffn, wr_hi, wr_lo, b_r,
            *, S, T, n_exp, n_grp):
    N, D = x2.shape
    aw = attn.shape[1]
    cw = u.shape[1]
    nS = S // T
    conv_k = w_dw.shape[0]
    hb = T // CONV_HALO
    kern = functools.partial(_mixout_kernel, nS=nS, conv_k=conv_k, n_exp=n_exp, n_grp=n_grp)
    full = lambda a: pl.BlockSpec(a.shape, lambda i: (0, 0))
    return pl.pallas_call(
        kern,
        grid=(N // T,),
        in_specs=[pl.BlockSpec((T, aw), lambda i: (i, 0)),
                  pl.BlockSpec((T, cw), lambda i: (i, 0)),
                  pl.BlockSpec((CONV_HALO, cw), lambda i: (jnp.maximum(i * hb - 1, 0), 0)),
                  pl.BlockSpec((T, D), lambda i: (i, 0)),
                  pl.BlockSpec((None, 6, D), lambda i: (i // nS, 0, 0)),
                  full(w_dw), full(b_dw), full(ln_g), full(ln_b), full(wo1), full(wo2), full(g_ffn),
                  full(wr_hi), full(wr_lo), full(b_r)],
        out_specs=[pl.BlockSpec((T, D), lambda i: (i, 0)),
                   pl.BlockSpec((T, D), lambda i: (i, 0)),
                   pl.BlockSpec((T, LANES), lambda i: (i, 0)),
                   pl.BlockSpec((1, LANES), lambda i: (0, 0))],
        out_shape=[jax.ShapeDtypeStruct((N, D), F32),
                   jax.ShapeDtypeStruct((N, D), F32),
                   jax.ShapeDtypeStruct((N, LANES), F32),
                   jax.ShapeDtypeStruct((1, LANES), F32)],
        scratch_shapes=[pltpu.VMEM((SUBLANES, CONV_HALO + T, cw), F32),
                        pltpu.VMEM((T, cw), BF16)],
        compiler_params=_cparams(("arbitrary",)),
        name="conv_outproj_router",
    )(attn, u, u, x2, mod, w_dw, b_dw, ln_g, ln_b, wo1, wo2, g_ffn, wr_hi, wr_lo, b_r)


def _row_of(ref, r):
    if isinstance(r, int):
        return ref.at[r // SUBLANES, pl.ds(r % SUBLANES, 1)]
    return ref.at[lax.shift_right_logical(r, 3), pl.ds(r & (SUBLANES - 1), 1)]


def _scatter_kernel(dest_ref, h2_ref, xs_hbm, sem):
    groups = h2_ref.shape[0]

    def issue(g, carry):
        for j in range(SUBLANES):
            for k in range(TOP_K_INNER):
                d = dest_ref[0, g * (SUBLANES * TOP_K_INNER) + j * TOP_K_INNER + k]
                pltpu.make_async_copy(h2_ref.at[g, pl.ds(j, 1)], _row_of(xs_hbm, d), sem).start(priority=k % 2)
        return carry

    lax.fori_loop(0, groups, issue, 0)
    for k in range(TOP_K_INNER):
        pltpu.make_async_copy(h2_ref, xs_hbm.at[pl.ds(0, groups)], sem).wait()


def _scatter_rows(h2, dest, *, tokens):
    N, C = h2.shape
    steps = N // tokens
    dest3 = dest.reshape(steps, 1, tokens * TOP_K_INNER)
    rows = N * TOP_K_INNER
    xs = pl.pallas_call(
        _scatter_kernel,
        grid=(steps,),
        in_specs=[pl.BlockSpec((None, 1, tokens * TOP_K_INNER), lambda s: (s, 0, 0), memory_space=pltpu.SMEM),
                  pl.BlockSpec((tokens // SUBLANES, SUBLANES, C), lambda s: (s, 0, 0))],
        out_specs=pl.BlockSpec(memory_space=pl.ANY),
        out_shape=jax.ShapeDtypeStruct((rows // SUBLANES, SUBLANES, C), h2.dtype),
        scratch_shapes=[pltpu.SemaphoreType.DMA],
        compiler_params=_cparams(("arbitrary",)),
        name="moe_scatter_rows",
    )(dest3, h2.reshape(N // SUBLANES, SUBLANES, C))
    return xs.reshape(rows, C)


def _experts_kernel(vt_ref, ve_ref, vlo_ref, vhi_ref, xs_ref, wg_ref, wu_ref, wd_ref, ys_ref):
    v = pl.program_id(0)
    lo = vlo_ref[v]
    hi = vhi_ref[v]

    @pl.when(hi > lo)
    def _():
        x = xs_ref[...].astype(BF16)
        g = jnp.dot(x, wg_ref[...].astype(BF16), preferred_element_type=F32)
        u = jnp.dot(x, wu_ref[...].astype(BF16), preferred_element_type=F32)
        hid = (_silu(g) * u).astype(BF16)
        y = jnp.dot(hid, wd_ref[...].astype(BF16), preferred_element_type=F32)

        @pl.when(lo == 0)
        def _():
            ys_ref[...] = y

        @pl.when(lo > 0)
        def _():
            row = lax.broadcasted_iota(jnp.int32, y.shape, 0)
            ys_ref[...] = jnp.where((row >= lo) & (row < hi), y, ys_ref[...])


def _experts(xs, w_gate, w_up, w_down, visits, *, tm):
    R, C = xs.shape
    E, D, ff = w_gate.shape
    vt, ve, vlo, vhi = visits
    grid_spec = pltpu.PrefetchScalarGridSpec(
        num_scalar_prefetch=4,
        grid=(vt.shape[0],),
        in_specs=[pl.BlockSpec((tm, C), lambda v, vt, ve, vlo, vhi: (vt[v], 0)),
                  pl.BlockSpec((None, D, ff), lambda v, vt, ve, vlo, vhi: (ve[v], 0, 0)),
                  pl.BlockSpec((None, D, ff), lambda v, vt, ve, vlo, vhi: (ve[v], 0, 0)),
                  pl.BlockSpec((None, ff, D), lambda v, vt, ve, vlo, vhi: (ve[v], 0, 0))],
        out_specs=pl.BlockSpec((tm, C), lambda v, vt, ve, vlo, vhi: (vt[v], 0)),
    )
    return pl.pallas_call(
        _experts_kernel,
        grid_spec=grid_spec,
        out_shape=jax.ShapeDtypeStruct((R, C), F32),
        compiler_params=_cparams(("arbitrary",)),
        name="moe_grouped_experts",
    )(vt, ve, vlo, vhi, xs, w_gate, w_up, w_down)


def _visit_tables(off, cnt, n_rows, tm):
    n_tiles = n_rows // tm
    n_exp = off.shape[0]
    n_visits = n_tiles + n_exp - 1
    tile_starts = jnp.arange(n_tiles, dtype=jnp.int32) * tm
    seg_starts = jnp.where((cnt > 0) & (off % tm != 0), off, n_rows)
    starts = jnp.sort(jnp.concatenate([tile_starts, seg_starts]))
    lo_abs = starts[:n_visits]
    hi_abs = starts[1:n_visits + 1]
    valid = lo_abs < n_rows
    tile = jnp.where(valid, lo_abs // tm, n_tiles - 1)
    ends = off + cnt
    probe = jnp.where(valid, lo_abs, n_rows - 1)
    expert = jnp.sum(ends[None, :] <= probe[:, None], axis=1).astype(jnp.int32)
    row_lo = jnp.where(valid, lo_abs - tile * tm, 0)
    row_hi = jnp.where(valid, hi_abs - tile * tm, 0)
    return tile, expert, row_lo, row_hi


def _combine_kernel(dcur_ref, dnxt_ref, ys_hbm, route_ref, x1_ref, mod_ref, o_ref, ybuf, sem):
    T = x1_ref.shape[0]
    i = pl.program_id(0)
    slot = lax.rem(i, 2)

    def gather(d_ref, sl):
        def body(g, carry):
            for j in range(SUBLANES):
                for k in range(TOP_K_INNER):
                    d = d_ref[0, g * (SUBLANES * TOP_K_INNER) + j * TOP_K_INNER + k]
                    pltpu.make_async_copy(_row_of(ys_hbm, d), ybuf.at[sl, k, g, pl.ds(j, 1)],
                                          sem.at[sl]).start(priority=k % 2)
            return carry
        lax.fori_loop(0, T // SUBLANES, body, 0)

    @pl.when(i == 0)
    def _():
        gather(dcur_ref, 0)

    @pl.when(i + 1 < pl.num_programs(0))
    def _():
        gather(dnxt_ref, 1 - slot)

    for k in range(TOP_K_INNER):
        pltpu.make_async_copy(ys_hbm.at[pl.ds(0, T // SUBLANES)], ybuf.at[slot, k], sem.at[slot]).wait()

    route = route_ref[...]
    moe = jnp.zeros(x1_ref.shape, F32)
    for k in range(TOP_K_INNER):
        moe = moe + route[:, TOP_K_INNER + k:TOP_K_INNER + k + 1] * ybuf[slot, k].reshape(x1_ref.shape)
    o_ref[...] = x1_ref[...] + mod_ref[5:6, :] * moe


def _combine(ys, dest, route, x1, mod, *, S, T):
    N, D = x1.shape
    C = ys.shape[1]
    steps = N // T
    nS = S // T
    dest3 = dest.reshape(steps, 1, T * TOP_K_INNER)
    dspec = lambda f: pl.BlockSpec((None, 1, T * TOP_K_INNER), f, memory_space=pltpu.SMEM)
    return pl.pallas_call(
        _combine_kernel,
        grid=(steps,),
        in_specs=[dspec(lambda i: (i, 0, 0)),
                  dspec(lambda i: (jnp.minimum(i + 1, steps - 1), 0, 0)),
                  pl.BlockSpec(memory_space=pl.ANY),
                  pl.BlockSpec((T, LANES), lambda i: (i, 0)),
                  pl.BlockSpec((T, D), lambda i: (i, 0)),
                  pl.BlockSpec((None, 6, D), lambda i: (i // nS, 0, 0))],
        out_specs=pl.BlockSpec((T, D), lambda i: (i, 0)),
        out_shape=jax.ShapeDtypeStruct((N, D), F32),
        scratch_shapes=[pltpu.VMEM((2, TOP_K_INNER, T // SUBLANES, SUBLANES, C), F32),
                        pltpu.SemaphoreType.DMA((2,))],
        compiler_params=_cparams(("arbitrary",)),
        name="moe_gather_combine",
    )(dest3, dest3, ys.reshape(ys.shape[0] // SUBLANES, SUBLANES, C), route, x1, mod)


def _layer(x2, mod, pos_row, l, B, S, g_mix, w_in, q_norm_g, k_norm_g, lambda_q1, lambda_k1, lambda_q2,
           lambda_k2, subln_g, b_glu, w_dw, b_dw, conv_ln_g, conv_ln_b, w_out, g_ffn, w_group, b_group,
           w_router, b_router, w_gate, w_up, w_down):
    N, D = x2.shape
    dh = q_norm_g.shape[0]
    H = N_DIFF_HEADS
    aw = H * 2 * dh
    cw = w_dw.shape[1]
    rot = dh // 4
    n_grp = w_group.shape[1]
    n_exp = w_router.shape[1]
    lambda_init = 0.8 - 0.6 * math.exp(-0.3 * l)

    w_qk_t = w_in[:, :2 * aw].T.astype(BF16)
    w_vglu = w_in[:, 2 * aw:].astype(BF16)
    scale = dh ** -0.5 * math.log2(math.e)
    gq = jnp.tile(q_norm_g * scale, aw // dh)
    gk = jnp.tile(k_norm_g, aw // dh)
    gqk_tab = jnp.broadcast_to(jnp.concatenate([gq, gk])[:, None], (2 * aw, LANES))
    inv_freq = ROPE_THETA ** (-jnp.arange(0, rot, 2, dtype=F32) / rot)
    invf_tab = jnp.broadcast_to(inv_freq[:, None], (rot // 2, LANES))

    q, kt, v, u = _inproj(x2, mod, g_mix.reshape(1, D), pos_row, w_vglu, w_qk_t, b_glu.reshape(1, 2 * cw),
                          gqk_tab, invf_tab, B=B, S=S, aw=aw, cw=cw, dh=dh, rot=rot, T=256)

    attn = _diff_attention(q, kt, v, lambda_q1.reshape(1, dh), lambda_k1.reshape(1, dh),
                           lambda_q2.reshape(1, dh), lambda_k2.reshape(1, dh), subln_g.reshape(1, 2 * dh),
                           B=B, S=S, H=H, dh=dh, lambda_init=lambda_init, tq=512, tk=1024)

    w_r = jnp.zeros((D, LANES), F32).at[:, :n_exp].set(w_router).at[:, n_exp:n_exp + n_grp].set(w_group)
    b_r = jnp.zeros((1, LANES), F32).at[0, :n_exp].set(b_router).at[0, n_exp:n_exp + n_grp].set(b_group)
    wr_hi = w_r.astype(BF16)
    wr_lo = (w_r - wr_hi.astype(F32)).astype(BF16)
    wo = w_out.astype(BF16)
    x1, h2, route, counts = _mixout(attn, u, x2, mod, w_dw, b_dw.reshape(1, cw), conv_ln_g.reshape(1, cw),
                                     conv_ln_b.reshape(1, cw), wo[:aw], wo[aw:], g_ffn.reshape(1, D),
                                     wr_hi, wr_lo, b_r, S=S, T=256, n_exp=n_exp, n_grp=n_grp)

    cnt = counts[0, :n_exp].astype(jnp.int32)
    off = jnp.cumsum(cnt) - cnt
    e_sel = route[:, 0:TOP_K_INNER].astype(jnp.int32)[..., None] == jnp.arange(n_exp, dtype=jnp.int32)
    dest = jnp.sum(jnp.where(e_sel, off, 0), axis=-1) + route[:, 4:4 + TOP_K_INNER].astype(jnp.int32)
    tm = 256
    visits = _visit_tables(off, cnt, N * TOP_K_INNER, tm)

    xs = _scatter_rows(h2, dest, tokens=min(1024, N))
    ys = _experts(xs, w_gate, w_up, w_down, visits, tm=tm)
    return _combine(ys, dest, route, x1, mod, S=S, T=256)


def kernel(x, c, positions, w_ada, b_ada, g_mix, w_in, q_norm_g, k_norm_g, lambda_q1, lambda_k1, lambda_q2,
           lambda_k2, subln_g, b_glu, w_dw, b_dw, conv_ln_g, conv_ln_b, w_out, g_ffn, w_group, b_group,
           w_router, b_router, w_gate, w_up, w_down):
    B, S, D = x.shape
    depth = w_ada.shape[0]
    x2 = x.reshape(B * S, D)
    pos_row = positions.astype(F32).reshape(1, B * S)
    for l in range(depth):
        mod = _modulation(c, w_ada[l], b_ada[l])
        x2 = _layer(x2, mod, pos_row, l, B, S, g_mix[l], w_in[l], q_norm_g[l], k_norm_g[l], lambda_q1[l],
                    lambda_k1[l], lambda_q2[l], lambda_k2[l], subln_g[l], b_glu[l], w_dw[l], b_dw[l],
                    conv_ln_g[l], conv_ln_b[l], w_out[l], g_ffn[l], w_group[l], b_group[l], w_router[l],
                    b_router[l], w_gate[l], w_up[l], w_down[l])
    return x2.reshape(B, S, D)
```

```python
import functools
import math

import numpy as np
import jax
import jax.numpy as jnp
from jax import lax
from jax.experimental import pallas as pl
from jax.experimental.pallas import tpu as pltpu

F32 = jnp.float32
BF16 = jnp.bfloat16

EPS = 1e-6
ROPE_THETA = 500000.0
N_DIFF_HEADS = 4
TOP_K_INNER = 2

LANES = 128
SUBLANES = 8
CONV_HALO = 32
VMEM_LIMIT = 48 * 1024 * 1024


def _cparams(sem):
    return pltpu.CompilerParams(dimension_semantics=sem, vmem_limit_bytes=VMEM_LIMIT)


def _silu(x):
    return x * jax.nn.sigmoid(x)


def _mod_kernel(c_ref, w_ref, b_ref, o_ref):
    c = c_ref[...]
    o_ref[...] = jnp.dot(_silu(c), w_ref[...], preferred_element_type=F32,
                         precision=lax.Precision.HIGHEST) + b_ref[...]


def _modulation(c, w_ada, b_ada):
    B, D = c.shape
    n_out = w_ada.shape[1]
    rows = 8
    c_pad = jnp.pad(c, ((0, rows - B), (0, 0)))
    bn = 1024
    out = pl.pallas_call(
        _mod_kernel,
        grid=(n_out // bn,),
        in_specs=[pl.BlockSpec((rows, D), lambda j: (0, 0)),
                  pl.BlockSpec((D, bn), lambda j: (0, j)),
                  pl.BlockSpec((1, bn), lambda j: (0, j))],
        out_specs=pl.BlockSpec((rows, bn), lambda j: (0, j)),
        out_shape=jax.ShapeDtypeStruct((rows, n_out), F32),
        compiler_params=_cparams(("parallel",)),
        name="adaln_mod",
    )(c_pad, w_ada, b_ada.reshape(1, n_out))
    return out[:B].reshape(B, 6, D)


def _inproj_kernel(x_ref, mod_ref, g_ref, pos_ref, wv_ref, wqk_ref, bglu_ref, gqk_ref, invf_ref,
                   q_ref, kt_ref, v_ref, u_ref, *, aw, cw, dh, rot):
    T = x_ref.shape[0]
    half = rot // 2
    x = x_ref[...]
    ms = jnp.mean(x * x, axis=-1, keepdims=True)
    sh = mod_ref[0:1, :]
    sc = mod_ref[1:2, :]
    h = x * lax.rsqrt(ms + EPS) * g_ref[...] * (1.0 + sc) + sh
    hb = h.astype(BF16)

    pv = jnp.dot(hb, wv_ref[...], preferred_element_type=F32)
    v_ref[...] = pv[:, :aw].astype(BF16)
    a = pv[:, aw:aw + cw] + bglu_ref[:, :cw]
    gate = pv[:, aw + cw:] + bglu_ref[:, cw:]
    u_ref[...] = (a * jax.nn.sigmoid(gate)).astype(BF16)

    qkt = lax.dot_general(wqk_ref[...], hb, (((1,), (1,)), ((), ())), preferred_element_type=F32)
    nch = 2 * aw // dh
    for c in range(T // LANES):
        sl = slice(c * LANES, (c + 1) * LANES)
        s3 = qkt[:, sl].reshape(nch, dh, LANES)
        ssq = jnp.mean(s3 * s3, axis=1, keepdims=True)
        y = s3 * lax.rsqrt(ssq + EPS) * gqk_ref[...].reshape(nch, dh, LANES)
        ang = invf_ref[...] * pos_ref[:, sl]
        cs = jnp.cos(ang)
        sn = jnp.sin(ang)
        t1 = y[:, 0:half, :]
        t2 = y[:, half:rot, :]
        y = jnp.concatenate([t1 * cs - t2 * sn, t2 * cs + t1 * sn, y[:, rot:, :]], axis=1)
        y2 = y.reshape(2 * aw, LANES)
        kt_ref[:, sl] = y2[aw:].astype(BF16)
        q_ref[sl, :] = y2[:aw].T.astype(BF16)


def _inproj(x2, mod, g_mix, pos_row, w_vglu, w_qk_t, b_glu, gqk_tab, invf_tab, *, B, S, aw, cw, dh, rot, T):
    N, D = x2.shape
    nS = S // T
    kern = functools.partial(_inproj_kernel, aw=aw, cw=cw, dh=dh, rot=rot)
    return pl.pallas_call(
        kern,
        grid=(N // T,),
        in_specs=[pl.BlockSpec((T, D), lambda i: (i, 0)),
                  pl.BlockSpec((None, 6, D), lambda i: (i // nS, 0, 0)),
                  pl.BlockSpec((1, D), lambda i: (0, 0)),
                  pl.BlockSpec((1, T), lambda i: (0, i)),
                  pl.BlockSpec(w_vglu.shape, lambda i: (0, 0)),
                  pl.BlockSpec(w_qk_t.shape, lambda i: (0, 0)),
                  pl.BlockSpec((1, 2 * cw), lambda i: (0, 0)),
                  pl.BlockSpec(gqk_tab.shape, lambda i: (0, 0)),
                  pl.BlockSpec(invf_tab.shape, lambda i: (0, 0))],
        out_specs=[pl.BlockSpec((T, aw), lambda i: (i, 0)),
                   pl.BlockSpec((None, aw, T), lambda i: (i // nS, 0, i % nS)),
                   pl.BlockSpec((T, aw), lambda i: (i, 0)),
                   pl.BlockSpec((T, cw), lambda i: (i, 0))],
        out_shape=[jax.ShapeDtypeStruct((N, aw), BF16),
                   jax.ShapeDtypeStruct((B, aw, S), BF16),
                   jax.ShapeDtypeStruct((N, aw), BF16),
                   jax.ShapeDtypeStruct((N, cw), BF16)],
        compiler_params=_cparams(("parallel",)),
        name="inproj_qknorm_rope_glu",
    )(x2, mod, g_mix, pos_row, w_vglu, w_qk_t, b_glu, gqk_tab, invf_tab)


def _attn_kernel(q_ref, kt_ref, v_ref, lq1_ref, lk1_ref, lq2_ref, lk2_ref, sg_ref, o_ref, acc_ref,
                 s0_ref, s1_ref, *, tq, tk, dh, lambda_init):
    i = pl.program_id(2)
    hd = 2 * dh
    q = q_ref[...]
    lane = lax.broadcasted_iota(jnp.int32, q.shape, 1)
    zero = jnp.zeros_like(q)
    q2 = jnp.concatenate([jnp.where(lane < dh, q, zero), jnp.where(lane >= dh, q, zero)], axis=0)

    acc_ref[...] = jnp.zeros_like(acc_ref)
    n_sub = tk // tq
    ones_col = {w: jnp.ones((w, hd), BF16) for w in {tk} | {(r + 1) * tq for r in range(n_sub)}}

    def scores(t, s_ref):
        start = pl.multiple_of(t * tk, tk)
        s_ref[...] = jnp.dot(q2, kt_ref[:, pl.ds(start, tk)], preferred_element_type=F32)

    def softmax_pv(t, s_ref, m, width=tk, diagonal=False):
        start = pl.multiple_of(t * tk, tk)
        vt = jnp.concatenate([v_ref[pl.ds(start, width), :], ones_col[width]], axis=1)
        s = s_ref[:, :width]
        if diagonal:
            row = lax.broadcasted_iota(jnp.int32, (2 * tq, tq), 0)
            col = lax.broadcasted_iota(jnp.int32, (2 * tq, tq), 1)
            qrow = jnp.where(row >= tq, row - tq, row)
            tail = jnp.where(col <= qrow, s[:, width - tq:], -jnp.inf)
            s = tail if width == tq else jnp.concatenate([s[:, :width - tq], tail], axis=1)
        m_new = jnp.maximum(m, jnp.max(s, axis=1, keepdims=True))
        alpha = jnp.exp2(m - m_new)
        p = jnp.exp2(s - m_new).astype(BF16)
        acc_ref[...] = alpha * acc_ref[...] + jnp.dot(p, vt, preferred_element_type=F32)
        return m_new

    n_full = (i * tq) // tk
    odd = lax.rem(n_full, 2)
    m0 = jnp.full((2 * tq, 1), -jnp.inf, F32)

    @pl.when(odd == 0)
    def _():
        scores(0, s0_ref)

    @pl.when(odd == 1)
    def _():
        scores(0, s1_ref)

    def peeled(_, m):
        scores(1, s0_ref)
        return softmax_pv(0, s1_ref, m)

    m = lax.fori_loop(0, odd, peeled, m0)

    def pair(jj, m):
        t = odd + 2 * jj
        scores(t + 1, s1_ref)
        m = softmax_pv(t, s0_ref, m)
        scores(t + 2, s0_ref)
        return softmax_pv(t + 1, s1_ref, m)

    m = lax.fori_loop(0, (n_full - odd) // 2, pair, m)

    for r in range(n_sub):
        @pl.when(lax.rem(i, n_sub) == r)
        def _(r=r):
            softmax_pv(n_full, s0_ref, m, width=(r + 1) * tq, diagonal=True)

    lam = (jnp.exp(jnp.sum(lq1_ref[...] * lk1_ref[...], axis=1, keepdims=True))
           - jnp.exp(jnp.sum(lq2_ref[...] * lk2_ref[...], axis=1, keepdims=True)) + lambda_init)
    o = (acc_ref[0:tq, 0:hd] / acc_ref[0:tq, hd:]
         - lam * (acc_ref[tq:, 0:hd] / acc_ref[tq:, hd:]))
    ms = jnp.mean(o * o, axis=1, keepdims=True)
    o = o * lax.rsqrt(ms + EPS) * sg_ref[...] * (1.0 - lambda_init)
    o_ref[...] = o.astype(o_ref.dtype)


def _diff_attention(q, kt, v, lq1, lk1, lq2, lk2, subln_g, *, B, S, H, dh, lambda_init, tq, tk):
    aw = H * 2 * dh
    q3 = q.reshape(B, S, aw)
    v3 = v.reshape(B, S, aw)
    hd = 2 * dh
    tk = min(tk, S)
    kern = functools.partial(_attn_kernel, tq=tq, tk=tk, dh=dh, lambda_init=lambda_init)
    vec = pl.BlockSpec((1, dh), lambda b, h, i: (0, 0))
    out = pl.pallas_call(
        kern,
        grid=(B, H, S // tq),
        in_specs=[pl.BlockSpec((None, tq, hd), lambda b, h, i: (b, i, h)),
                  pl.BlockSpec((None, hd, S), lambda b, h, i: (b, h, 0)),
                  pl.BlockSpec((None, S, hd), lambda b, h, i: (b, 0, h)),
                  vec, vec, vec, vec,
                  pl.BlockSpec((1, hd), lambda b, h, i: (0, 0))],
        out_specs=pl.BlockSpec((None, tq, hd), lambda b, h, i: (b, i, h)),
        out_shape=jax.ShapeDtypeStruct((B, S, aw), BF16),
        scratch_shapes=[pltpu.VMEM((2 * tq, 2 * hd), F32),
                        pltpu.VMEM((2 * tq, tk), F32),
                        pltpu.VMEM((2 * tq, tk), F32)],
        compiler_params=_cparams(("parallel", "parallel", "parallel")),
        name="diff_flash_attention",
    )(q3, kt, v3, lq1, lk1, lq2, lk2, subln_g)
    return out.reshape(B * S, aw)


def _mixout_kernel(attn_ref, ucur_ref, uhalo_ref, x_ref, mod_ref, wdw_ref, bdw_ref, lng_ref, lnb_ref,
                   wo1_ref, wo2_ref, gffn_ref, wrh_ref, wrl_ref, br_ref,
                   x1_ref, h2_ref, route_ref, cnt_ref, ubuf_ref, conv_ref,
                   *, nS, conv_k, n_exp, n_grp):
    T = x_ref.shape[0]
    cw = ucur_ref.shape[1]
    i = pl.program_id(0)

    @pl.when(i == 0)
    def _():
        cnt_ref[...] = jnp.zeros_like(cnt_ref)

    first = (i % nS) == 0
    halo = uhalo_ref[...].astype(F32)
    ubuf_ref[0, 0:CONV_HALO, :] = jnp.where(first, jnp.zeros_like(halo), halo)
    ubuf_ref[0, CONV_HALO:, :] = ucur_ref[...].astype(F32)
    span = T + CONV_HALO - SUBLANES
    for b in range(1, SUBLANES):
        ubuf_ref[b, 0:span, :] = ubuf_ref[0, b:b + span, :]

    off = CONV_HALO - (conv_k - 1)
    rows = 32
    for r0 in range(0, T, rows):
        acc = jnp.zeros((rows // SUBLANES, SUBLANES, cw), F32)
        for j in range(conv_k):
            a, b = divmod(j + off, SUBLANES)
            lo_r = r0 + a * SUBLANES
            acc = acc + (wdw_ref[j * SUBLANES:(j + 1) * SUBLANES, :]
                         * ubuf_ref[b, lo_r:lo_r + rows, :].reshape(rows // SUBLANES, SUBLANES, cw))
        y = acc.reshape(rows, cw) + bdw_ref[...]
        mu = jnp.mean(y, axis=1, keepdims=True)
        d = y - mu
        var = jnp.mean(d * d, axis=1, keepdims=True)
        z = d * lax.rsqrt(var + EPS) * lng_ref[...] + lnb_ref[...]
        conv_ref[r0:r0 + rows, :] = _silu(z).astype(BF16)

    yo = (jnp.dot(attn_ref[...], wo1_ref[...], preferred_element_type=F32)
          + jnp.dot(conv_ref[...], wo2_ref[...], preferred_element_type=F32))
    x1 = x_ref[...] + mod_ref[2:3, :] * yo
    x1_ref[...] = x1
    ms = jnp.mean(x1 * x1, axis=1, keepdims=True)
    h2 = x1 * lax.rsqrt(ms + EPS) * gffn_ref[...] * (1.0 + mod_ref[4:5, :]) + mod_ref[3:4, :]
    h2_ref[...] = h2
    hi = h2.astype(BF16)
    lo = (h2 - hi.astype(F32)).astype(BF16)

    logits = (jnp.dot(hi, wrh_ref[...], preferred_element_type=F32)
              + jnp.dot(lo, wrh_ref[...], preferred_element_type=F32)
              + jnp.dot(hi, wrl_ref[...], preferred_element_type=F32)) + br_ref[...]
    lane_i = lax.broadcasted_iota(jnp.int32, logits.shape, 1)
    lane = lane_i.astype(F32)
    big = jnp.float32(1e9)
    ninf = jnp.float32(-jnp.inf)
    is_g = (lane_i >= n_exp) & (lane_i < n_exp + n_grp)
    gl = jnp.where(is_g, logits, ninf)
    gmax = jnp.max(gl, axis=1, keepdims=True)
    gsum = jnp.sum(jnp.where(is_g, jnp.exp(gl - gmax), 0.0), axis=1, keepdims=True)
    g_p = 1.0 / gsum
    gidx = jnp.min(jnp.where(gl == gmax, lane, big), axis=1, keepdims=True) - n_exp
    epg = n_exp // n_grp
    lo_l = gidx * epg
    in_grp = (lane >= lo_l) & (lane < lo_l + epg)
    el = jnp.where(in_grp, logits, ninf)
    m1 = jnp.max(el, axis=1, keepdims=True)
    i1 = jnp.min(jnp.where(el == m1, lane, big), axis=1, keepdims=True)
    el2 = jnp.where(lane == i1, ninf, el)
    m2 = jnp.max(el2, axis=1, keepdims=True)
    i2 = jnp.min(jnp.where(el2 == m2, lane, big), axis=1, keepdims=True)
    e2 = jnp.exp(m2 - m1)
    p1 = 1.0 / (1.0 + e2)
    p2 = e2 / (1.0 + e2)

    sel = (lane == i1) | (lane == i2)
    rr = lax.broadcasted_iota(jnp.int32, (T, T), 0)
    cc = lax.broadcasted_iota(jnp.int32, (T, T), 1)
    before = jnp.dot((rr > cc).astype(BF16), sel.astype(BF16), preferred_element_type=F32) + cnt_ref[...]
    r1 = jnp.sum(jnp.where(lane == i1, before, 0.0), axis=1, keepdims=True)
    r2 = jnp.sum(jnp.where(lane == i2, before, 0.0), axis=1, keepdims=True)
    cnt_ref[...] += jnp.sum(sel.astype(F32), axis=0, keepdims=True)

    route = jnp.zeros_like(logits)
    for k, val in enumerate((i1, i2, g_p * p1, g_p * p2, r1, r2)):
        route = jnp.where(lane_i == k, val, route)
    route_ref[...] = route


def _mixout(attn, u, x2, mod, w_dw, b_dw, ln_g, ln_b, wo1, wo2, g_ffn, wr_hi, wr_lo, b_r,
            *, S, T, n_exp, n_grp):
    N, D = x2.shape
    aw = attn.shape[1]
    cw = u.shape[1]
    nS = S // T
    conv_k = w_dw.shape[0]
    w_dw = jnp.repeat(w_dw, SUBLANES, axis=0)
    hb = T // CONV_HALO
    kern = functools.partial(_mixout_kernel, nS=nS, conv_k=conv_k, n_exp=n_exp, n_grp=n_grp)
    full = lambda a: pl.BlockSpec(a.shape, lambda i: (0, 0))
    return pl.pallas_call(
        kern,
        grid=(N // T,),
        in_specs=[pl.BlockSpec((T, aw), lambda i: (i, 0)),
                  pl.BlockSpec((T, cw), lambda i: (i, 0)),
                  pl.BlockSpec((CONV_HALO, cw), lambda i: (jnp.maximum(i * hb - 1, 0), 0)),
                  pl.BlockSpec((T, D), lambda i: (i, 0)),
                  pl.BlockSpec((None, 6, D), lambda i: (i // nS, 0, 0)),
                  full(w_dw), full(b_dw), full(ln_g), full(ln_b), full(wo1), full(wo2), full(g_ffn),
                  full(wr_hi), full(wr_lo), full(b_r)],
        out_specs=[pl.BlockSpec((T, D), lambda i: (i, 0)),
                   pl.BlockSpec((T, D), lambda i: (i, 0)),
                   pl.BlockSpec((T, LANES), lambda i: (i, 0)),
                   pl.BlockSpec((1, LANES), lambda i: (0, 0))],
        out_shape=[jax.ShapeDtypeStruct((N, D), F32),
                   jax.ShapeDtypeStruct((N, D), F32),
                   jax.ShapeDtypeStruct((N, LANES), F32),
                   jax.ShapeDtypeStruct((1, LANES), F32)],
        scratch_shapes=[pltpu.VMEM((SUBLANES, CONV_HALO + T, cw), F32),
                        pltpu.VMEM((T, cw), BF16)],
        compiler_params=_cparams(("arbitrary",)),
        name="conv_outproj_router",
    )(attn, u, u, x2, mod, w_dw, b_dw, ln_g, ln_b, wo1, wo2, g_ffn, wr_hi, wr_lo, b_r)


def _row_of(ref, r):
    if isinstance(r, int):
        return ref.at[r // SUBLANES, pl.ds(r % SUBLANES, 1)]
    return ref.at[lax.shift_right_logical(r, 3), pl.ds(r & (SUBLANES - 1), 1)]


def _scatter_kernel(dest_ref, h2_ref, xs_hbm, sem):
    groups = h2_ref.shape[0]

    def issue(g, carry):
        for j in range(SUBLANES):
            for k in range(TOP_K_INNER):
                d = dest_ref[k, g * SUBLANES + j]
                pltpu.make_async_copy(h2_ref.at[g, pl.ds(j, 1)], _row_of(xs_hbm, d), sem).start(priority=k % 2)
        return carry

    lax.fori_loop(0, groups, issue, 0)
    for k in range(TOP_K_INNER):
        pltpu.make_async_copy(h2_ref, xs_hbm.at[pl.ds(0, groups)], sem).wait()


def _dest_table(dest, tokens):
    return jnp.stack([d.reshape(-1, tokens) for d in dest], axis=1)


def _scatter_rows(h2, dest, *, tokens):
    N, C = h2.shape
    steps = N // tokens
    dest3 = _dest_table(dest, tokens)
    rows = N * TOP_K_INNER
    xs = pl.pallas_call(
        _scatter_kernel,
        grid=(steps,),
        in_specs=[pl.BlockSpec((None, TOP_K_INNER, tokens), lambda s: (s, 0, 0), memory_space=pltpu.SMEM),
                  pl.BlockSpec((tokens // SUBLANES, SUBLANES, C), lambda s: (s, 0, 0))],
        out_specs=pl.BlockSpec(memory_space=pl.ANY),
        out_shape=jax.ShapeDtypeStruct((rows // SUBLANES, SUBLANES, C), h2.dtype),
        scratch_shapes=[pltpu.SemaphoreType.DMA],
        compiler_params=_cparams(("arbitrary",)),
        name="moe_scatter_rows",
    )(dest3, h2.reshape(N // SUBLANES, SUBLANES, C))
    return xs.reshape(rows, C)


def _experts_kernel(vt_ref, ve_ref, vlo_ref, vhi_ref, xs_ref, wg_ref, wu_ref, wd_ref, ys_ref):
    v = pl.program_id(0)
    lo = vlo_ref[v]
    hi = vhi_ref[v]

    @pl.when(hi > lo)
    def _():
        x = xs_ref[...].astype(BF16)
        g = jnp.dot(x, wg_ref[...].astype(BF16), preferred_element_type=F32)
        u = jnp.dot(x, wu_ref[...].astype(BF16), preferred_element_type=F32)
        hid = (_silu(g) * u).astype(BF16)
        y = jnp.dot(hid, wd_ref[...].astype(BF16), preferred_element_type=F32)

        @pl.when(lo == 0)
        def _():
            ys_ref[...] = y

        @pl.when(lo > 0)
        def _():
            row = lax.broadcasted_iota(jnp.int32, y.shape, 0)
            ys_ref[...] = jnp.where((row >= lo) & (row < hi), y, ys_ref[...])


def _experts(xs, w_gate, w_up, w_down, visits, *, tm):
    R, C = xs.shape
    E, D, ff = w_gate.shape
    vt, ve, vlo, vhi = visits
    grid_spec = pltpu.PrefetchScalarGridSpec(
        num_scalar_prefetch=4,
        grid=(vt.shape[0],),
        in_specs=[pl.BlockSpec((tm, C), lambda v, vt, ve, vlo, vhi: (vt[v], 0)),
                  pl.BlockSpec((None, D, ff), lambda v, vt, ve, vlo, vhi: (ve[v], 0, 0)),
                  pl.BlockSpec((None, D, ff), lambda v, vt, ve, vlo, vhi: (ve[v], 0, 0)),
                  pl.BlockSpec((None, ff, D), lambda v, vt, ve, vlo, vhi: (ve[v], 0, 0))],
        out_specs=pl.BlockSpec((tm, C), lambda v, vt, ve, vlo, vhi: (vt[v], 0)),
    )
    return pl.pallas_call(
        _experts_kernel,
        grid_spec=grid_spec,
        out_shape=jax.ShapeDtypeStruct((R, C), F32),
        compiler_params=_cparams(("arbitrary",)),
        name="moe_grouped_experts",
    )(vt, ve, vlo, vhi, xs, w_gate, w_up, w_down)


def _visit_tables(off, cnt, n_rows, tm):
    n_tiles = n_rows // tm
    n_exp = off.shape[0]
    n_visits = n_tiles + n_exp - 1
    tile_starts = jnp.arange(n_tiles, dtype=jnp.int32) * tm
    seg_starts = jnp.where((cnt > 0) & (off % tm != 0), off, n_rows)
    starts = jnp.sort(jnp.concatenate([tile_starts, seg_starts]))
    lo_abs = starts[:n_visits]
    hi_abs = starts[1:n_visits + 1]
    valid = lo_abs < n_rows
    tile = jnp.where(valid, lo_abs // tm, n_tiles - 1)
    ends = off + cnt
    probe = jnp.where(valid, lo_abs, n_rows - 1)
    expert = jnp.sum(ends[None, :] <= probe[:, None], axis=1).astype(jnp.int32)
    row_lo = jnp.where(valid, lo_abs - tile * tm, 0)
    row_hi = jnp.where(valid, hi_abs - tile * tm, 0)
    return tile, expert, row_lo, row_hi


def _combine_kernel(dcur_ref, dnxt_ref, ys_hbm, route_ref, x1_ref, mod_ref, o_ref, ybuf, sem):
    T = x1_ref.shape[0]
    i = pl.program_id(0)
    slot = lax.rem(i, 2)

    def gather(d_ref, sl):
        def body(g, carry):
            for j in range(SUBLANES):
                for k in range(TOP_K_INNER):
                    d = d_ref[k, g * SUBLANES + j]
                    pltpu.make_async_copy(_row_of(ys_hbm, d), ybuf.at[sl, k, g, pl.ds(j, 1)],
                                          sem.at[sl]).start(priority=k % 2)
            return carry
        lax.fori_loop(0, T // SUBLANES, body, 0)

    @pl.when(i == 0)
    def _():
        gather(dcur_ref, 0)

    @pl.when(i + 1 < pl.num_programs(0))
    def _():
        gather(dnxt_ref, 1 - slot)

    for k in range(TOP_K_INNER):
        pltpu.make_async_copy(ys_hbm.at[pl.ds(0, T // SUBLANES)], ybuf.at[slot, k], sem.at[slot]).wait()

    route = route_ref[...]
    moe = jnp.zeros(x1_ref.shape, F32)
    for k in range(TOP_K_INNER):
        moe = moe + route[:, TOP_K_INNER + k:TOP_K_INNER + k + 1] * ybuf[slot, k].reshape(x1_ref.shape)
    o_ref[...] = x1_ref[...] + mod_ref[5:6, :] * moe


def _combine(ys, dest, route, x1, mod, *, S, T):
    N, D = x1.shape
    C = ys.shape[1]
    steps = N // T
    nS = S // T
    dest3 = _dest_table(dest, T)
    dspec = lambda f: pl.BlockSpec((None, TOP_K_INNER, T), f, memory_space=pltpu.SMEM)
    return pl.pallas_call(
        _combine_kernel,
        grid=(steps,),
        in_specs=[dspec(lambda i: (i, 0, 0)),
                  dspec(lambda i: (jnp.minimum(i + 1, steps - 1), 0, 0)),
                  pl.BlockSpec(memory_space=pl.ANY),
                  pl.BlockSpec((T, LANES), lambda i: (i, 0)),
                  pl.BlockSpec((T, D), lambda i: (i, 0)),
                  pl.BlockSpec((None, 6, D), lambda i: (i // nS, 0, 0))],
        out_specs=pl.BlockSpec((T, D), lambda i: (i, 0)),
        out_shape=jax.ShapeDtypeStruct((N, D), F32),
        scratch_shapes=[pltpu.VMEM((2, TOP_K_INNER, T // SUBLANES, SUBLANES, C), F32),
                        pltpu.SemaphoreType.DMA((2,))],
        compiler_params=_cparams(("arbitrary",)),
        name="moe_gather_combine",
    )(dest3, dest3, ys.reshape(ys.shape[0] // SUBLANES, SUBLANES, C), route, x1, mod)


def _layer(x2, mod, pos_row, l, B, S, g_mix, w_in, q_norm_g, k_norm_g, lambda_q1, lambda_k1, lambda_q2,
           lambda_k2, subln_g, b_glu, w_dw, b_dw, conv_ln_g, conv_ln_b, w_out, g_ffn, w_group, b_group,
           w_router, b_router, w_gate, w_up, w_down):
    N, D = x2.shape
    dh = q_norm_g.shape[0]
    H = N_DIFF_HEADS
    aw = H * 2 * dh
    cw = w_dw.shape[1]
    rot = dh // 4
    n_grp = w_group.shape[1]
    n_exp = w_router.shape[1]
    lambda_init = 0.8 - 0.6 * math.exp(-0.3 * l)

    w_qk_t = w_in[:, :2 * aw].T.astype(BF16)
    w_vglu = w_in[:, 2 * aw:].astype(BF16)
    scale = dh ** -0.5 * math.log2(math.e)
    gq = jnp.tile(q_norm_g * scale, aw // dh)
    gk = jnp.tile(k_norm_g, aw // dh)
    gqk_tab = jnp.broadcast_to(jnp.concatenate([gq, gk])[:, None], (2 * aw, LANES))
    inv_freq = ROPE_THETA ** (-jnp.arange(0, rot, 2, dtype=F32) / rot)
    invf_tab = jnp.broadcast_to(inv_freq[:, None], (rot // 2, LANES))

    q, kt, v, u = _inproj(x2, mod, g_mix.reshape(1, D), pos_row, w_vglu, w_qk_t, b_glu.reshape(1, 2 * cw),
                          gqk_tab, invf_tab, B=B, S=S, aw=aw, cw=cw, dh=dh, rot=rot, T=256)

    attn = _diff_attention(q, kt, v, lambda_q1.reshape(1, dh), lambda_k1.reshape(1, dh),
                           lambda_q2.reshape(1, dh), lambda_k2.reshape(1, dh), subln_g.reshape(1, 2 * dh),
                           B=B, S=S, H=H, dh=dh, lambda_init=lambda_init, tq=512, tk=1024)

    w_r = jnp.zeros((D, LANES), F32).at[:, :n_exp].set(w_router).at[:, n_exp:n_exp + n_grp].set(w_group)
    b_r = jnp.zeros((1, LANES), F32).at[0, :n_exp].set(b_router).at[0, n_exp:n_exp + n_grp].set(b_group)
    wr_hi = w_r.astype(BF16)
    wr_lo = (w_r - wr_hi.astype(F32)).astype(BF16)
    wo = w_out.astype(BF16)
    x1, h2, route, counts = _mixout(attn, u, x2, mod, w_dw, b_dw.reshape(1, cw), conv_ln_g.reshape(1, cw),
                                     conv_ln_b.reshape(1, cw), wo[:aw], wo[aw:], g_ffn.reshape(1, D),
                                     wr_hi, wr_lo, b_r, S=S, T=256, n_exp=n_exp, n_grp=n_grp)

    cnt = counts[0, :n_exp].astype(jnp.int32)
    off = jnp.cumsum(cnt) - cnt
    experts = jnp.arange(n_exp, dtype=jnp.int32)[:, None]
    dest = [jnp.sum(jnp.where(route[:, k].astype(jnp.int32)[None, :] == experts, off[:, None], 0), axis=0)
            + route[:, 4 + k].astype(jnp.int32) for k in range(TOP_K_INNER)]
    tm = 256
    visits = _visit_tables(off, cnt, N * TOP_K_INNER, tm)

    xs = _scatter_rows(h2, dest, tokens=min(1024, N))
    ys = _experts(xs, w_gate, w_up, w_down, visits, tm=tm)
    return _combine(ys, dest, route, x1, mod, S=S, T=256)


def kernel(x, c, positions, w_ada, b_ada, g_mix, w_in, q_norm_g, k_norm_g, lambda_q1, lambda_k1, lambda_q2,
           lambda_k2, subln_g, b_glu, w_dw, b_dw, conv_ln_g, conv_ln_b, w_out, g_ffn, w_group, b_group,
           w_router, b_router, w_gate, w_up, w_down):
    B, S, D = x.shape
    depth = w_ada.shape[0]
    x2 = x.reshape(B * S, D)
    pos_row = positions.astype(F32).reshape(1, B * S)
    for l in range(depth):
        mod = _modulation(c, w_ada[l], b_ada[l])
        x2 = _layer(x2, mod, pos_row, l, B, S, g_mix[l], w_in[l], q_norm_g[l], k_norm_g[l], lambda_q1[l],
                    lambda_k1[l], lambda_q2[l], lambda_k2[l], subln_g[l], b_glu[l], w_dw[l], b_dw[l],
                    conv_ln_g[l], conv_ln_b[l], w_out[l], g_ffn[l], w_group[l], b_group[l], w_router[l],
                    b_router[l], w_gate[l], w_up[l], w_down[l])
    return x2.reshape(B, S, D)
```

```python
import functools
import math

import numpy as np
import jax
import jax.numpy as jnp
from jax import lax
from jax.experimental import pallas as pl
from jax.experimental.pallas import tpu as pltpu

F32 = jnp.float32
BF16 = jnp.bfloat16

EPS = 1e-6
ROPE_THETA = 500000.0
N_DIFF_HEADS = 4
TOP_K_INNER = 2

LANES = 128
SUBLANES = 8
CONV_HALO = 32
VMEM_LIMIT = 48 * 1024 * 1024


def _cparams(sem):
    return pltpu.CompilerParams(dimension_semantics=sem, vmem_limit_bytes=VMEM_LIMIT)


def _silu(x):
    return x * jax.nn.sigmoid(x)


def _mod_kernel(c_ref, w_ref, b_ref, o_ref):
    c = c_ref[...]
    o_ref[...] = jnp.dot(_silu(c), w_ref[...], preferred_element_type=F32,
                         precision=lax.Precision.HIGHEST) + b_ref[...]


def _modulation(c, w_ada, b_ada):
    B, D = c.shape
    n_out = w_ada.shape[1]
    rows = 8
    c_pad = jnp.pad(c, ((0, rows - B), (0, 0)))
    bn = 1024
    out = pl.pallas_call(
        _mod_kernel,
        grid=(n_out // bn,),
        in_specs=[pl.BlockSpec((rows, D), lambda j: (0, 0)),
                  pl.BlockSpec((D, bn), lambda j: (0, j)),
                  pl.BlockSpec((1, bn), lambda j: (0, j))],
        out_specs=pl.BlockSpec((rows, bn), lambda j: (0, j)),
        out_shape=jax.ShapeDtypeStruct((rows, n_out), F32),
        compiler_params=_cparams(("parallel",)),
        name="adaln_mod",
    )(c_pad, w_ada, b_ada.reshape(1, n_out))
    return out[:B].reshape(B, 6, D)


def _inproj_kernel(x_ref, mod_ref, g_ref, pos_ref, wv_ref, wqk_ref, bglu_ref, gqk_ref, invf_ref,
                   q_ref, kt_ref, v_ref, u_ref, *, aw, cw, dh, rot):
    T = x_ref.shape[0]
    half = rot // 2
    x = x_ref[...]
    ms = jnp.mean(x * x, axis=-1, keepdims=True)
    sh = mod_ref[0:1, :]
    sc = mod_ref[1:2, :]
    h = x * lax.rsqrt(ms + EPS) * g_ref[...] * (1.0 + sc) + sh
    hb = h.astype(BF16)

    pv = jnp.dot(hb, wv_ref[...], preferred_element_type=F32)
    v_ref[...] = pv[:, :aw].astype(BF16)
    a = pv[:, aw:aw + cw] + bglu_ref[:, :cw]
    gate = pv[:, aw + cw:] + bglu_ref[:, cw:]
    u_ref[...] = (a * jax.nn.sigmoid(gate)).astype(BF16)

    qkt = lax.dot_general(wqk_ref[...], hb, (((1,), (1,)), ((), ())), preferred_element_type=F32)
    nch = 2 * aw // dh
    for c in range(T // LANES):
        sl = slice(c * LANES, (c + 1) * LANES)
        s3 = qkt[:, sl].reshape(nch, dh, LANES)
        ssq = jnp.mean(s3 * s3, axis=1, keepdims=True)
        y = s3 * lax.rsqrt(ssq + EPS) * gqk_ref[...].reshape(nch, dh, LANES)
        ang = invf_ref[...] * pos_ref[:, sl]
        cs = jnp.cos(ang)
        sn = jnp.sin(ang)
        t1 = y[:, 0:half, :]
        t2 = y[:, half:rot, :]
        y = jnp.concatenate([t1 * cs - t2 * sn, t2 * cs + t1 * sn, y[:, rot:, :]], axis=1)
        y2 = y.reshape(2 * aw, LANES)
        kt_ref[:, sl] = y2[aw:].astype(BF16)
        q_ref[sl, :] = y2[:aw].T.astype(BF16)


def _inproj(x2, mod, g_mix, pos_row, w_vglu, w_qk_t, b_glu, gqk_tab, invf_tab, *, B, S, aw, cw, dh, rot, T):
    N, D = x2.shape
    nS = S // T
    kern = functools.partial(_inproj_kernel, aw=aw, cw=cw, dh=dh, rot=rot)
    return pl.pallas_call(
        kern,
        grid=(N // T,),
        in_specs=[pl.BlockSpec((T, D), lambda i: (i, 0)),
                  pl.BlockSpec((None, 6, D), lambda i: (i // nS, 0, 0)),
                  pl.BlockSpec((1, D), lambda i: (0, 0)),
                  pl.BlockSpec((1, T), lambda i: (0, i)),
                  pl.BlockSpec(w_vglu.shape, lambda i: (0, 0)),
                  pl.BlockSpec(w_qk_t.shape, lambda i: (0, 0)),
                  pl.BlockSpec((1, 2 * cw), lambda i: (0, 0)),
                  pl.BlockSpec(gqk_tab.shape, lambda i: (0, 0)),
                  pl.BlockSpec(invf_tab.shape, lambda i: (0, 0))],
        out_specs=[pl.BlockSpec((T, aw), lambda i: (i, 0)),
                   pl.BlockSpec((None, aw, T), lambda i: (i // nS, 0, i % nS)),
                   pl.BlockSpec((T, aw), lambda i: (i, 0)),
                   pl.BlockSpec((T, cw), lambda i: (i, 0))],
        out_shape=[jax.ShapeDtypeStruct((N, aw), BF16),
                   jax.ShapeDtypeStruct((B, aw, S), BF16),
                   jax.ShapeDtypeStruct((N, aw), BF16),
                   jax.ShapeDtypeStruct((N, cw), BF16)],
        compiler_params=_cparams(("parallel",)),
        name="inproj_qknorm_rope_glu",
    )(x2, mod, g_mix, pos_row, w_vglu, w_qk_t, b_glu, gqk_tab, invf_tab)


def _attn_kernel(q_ref, kt_ref, v_ref, lq1_ref, lk1_ref, lq2_ref, lk2_ref, sg_ref, o_ref, acc_ref,
                 s0_ref, s1_ref, *, tq, tk, dh, lambda_init):
    i = pl.program_id(2)
    hd = 2 * dh
    q = q_ref[...]
    lane = lax.broadcasted_iota(jnp.int32, q.shape, 1)
    zero = jnp.zeros_like(q)
    q2 = jnp.concatenate([jnp.where(lane < dh, q, zero), jnp.where(lane >= dh, q, zero)], axis=0)

    acc_ref[...] = jnp.zeros_like(acc_ref)
    n_sub = tk // tq
    ones_col = {w: jnp.ones((w, hd), BF16) for w in {tk} | {(r + 1) * tq for r in range(n_sub)}}

    def scores(t, s_ref):
        start = pl.multiple_of(t * tk, tk)
        s_ref[...] = jnp.dot(q2, kt_ref[:, pl.ds(start, tk)], preferred_element_type=F32)

    def softmax_pv(t, s_ref, m, width=tk, diagonal=False):
        start = pl.multiple_of(t * tk, tk)
        vt = jnp.concatenate([v_ref[pl.ds(start, width), :], ones_col[width]], axis=1)
        s = s_ref[:, :width]
        if diagonal:
            row = lax.broadcasted_iota(jnp.int32, (2 * tq, tq), 0)
            col = lax.broadcasted_iota(jnp.int32, (2 * tq, tq), 1)
            qrow = jnp.where(row >= tq, row - tq, row)
            tail = jnp.where(col <= qrow, s[:, width - tq:], -jnp.inf)
            s = tail if width == tq else jnp.concatenate([s[:, :width - tq], tail], axis=1)
        m_new = jnp.maximum(m, jnp.max(s, axis=1, keepdims=True))
        alpha = jnp.exp2(m - m_new)
        p = jnp.exp2(s - m_new).astype(BF16)
        acc_ref[...] = alpha * acc_ref[...] + jnp.dot(p, vt, preferred_element_type=F32)
        return m_new

    n_full = (i * tq) // tk
    odd = lax.rem(n_full, 2)
    m0 = jnp.full((2 * tq, 1), -jnp.inf, F32)

    @pl.when(odd == 0)
    def _():
        scores(0, s0_ref)

    @pl.when(odd == 1)
    def _():
        scores(0, s1_ref)

    def peeled(_, m):
        scores(1, s0_ref)
        return softmax_pv(0, s1_ref, m)

    m = lax.fori_loop(0, odd, peeled, m0)

    def pair(jj, m):
        t = odd + 2 * jj
        scores(t + 1, s1_ref)
        m = softmax_pv(t, s0_ref, m)
        scores(t + 2, s0_ref)
        return softmax_pv(t + 1, s1_ref, m)

    m = lax.fori_loop(0, (n_full - odd) // 2, pair, m)

    for r in range(n_sub):
        @pl.when(lax.rem(i, n_sub) == r)
        def _(r=r):
            softmax_pv(n_full, s0_ref, m, width=(r + 1) * tq, diagonal=True)

    lam = (jnp.exp(jnp.sum(lq1_ref[...] * lk1_ref[...], axis=1, keepdims=True))
           - jnp.exp(jnp.sum(lq2_ref[...] * lk2_ref[...], axis=1, keepdims=True)) + lambda_init)
    o = (acc_ref[0:tq, 0:hd] / acc_ref[0:tq, hd:]
         - lam * (acc_ref[tq:, 0:hd] / acc_ref[tq:, hd:]))
    ms = jnp.mean(o * o, axis=1, keepdims=True)
    o = o * lax.rsqrt(ms + EPS) * sg_ref[...] * (1.0 - lambda_init)
    o_ref[...] = o.astype(o_ref.dtype)


def _diff_attention(q, kt, v, lq1, lk1, lq2, lk2, subln_g, *, B, S, H, dh, lambda_init, tq, tk):
    aw = H * 2 * dh
    q3 = q.reshape(B, S, aw)
    v3 = v.reshape(B, S, aw)
    hd = 2 * dh
    tk = min(tk, S)
    kern = functools.partial(_attn_kernel, tq=tq, tk=tk, dh=dh, lambda_init=lambda_init)
    vec = pl.BlockSpec((1, dh), lambda b, h, i: (0, 0))
    out = pl.pallas_call(
        kern,
        grid=(B, H, S // tq),
        in_specs=[pl.BlockSpec((None, tq, hd), lambda b, h, i: (b, i, h)),
                  pl.BlockSpec((None, hd, S), lambda b, h, i: (b, h, 0)),
                  pl.BlockSpec((None, S, hd), lambda b, h, i: (b, 0, h)),
                  vec, vec, vec, vec,
                  pl.BlockSpec((1, hd), lambda b, h, i: (0, 0))],
        out_specs=pl.BlockSpec((None, tq, hd), lambda b, h, i: (b, i, h)),
        out_shape=jax.ShapeDtypeStruct((B, S, aw), BF16),
        scratch_shapes=[pltpu.VMEM((2 * tq, 2 * hd), F32),
                        pltpu.VMEM((2 * tq, tk), F32),
                        pltpu.VMEM((2 * tq, tk), F32)],
        compiler_params=_cparams(("parallel", "parallel", "parallel")),
        name="diff_flash_attention",
    )(q3, kt, v3, lq1, lk1, lq2, lk2, subln_g)
    return out.reshape(B * S, aw)


def _mixout_kernel(attn_ref, ucur_ref, uhalo_ref, x_ref, mod_ref, wdw_ref, bdw_ref, lng_ref, lnb_ref,
                   wo1_ref, wo2_ref, gffn_ref, wrh_ref, wrl_ref, br_ref,
                   x1_ref, h2_ref, route_ref, route_t_ref, cnt_ref, ubuf_ref, conv_ref,
                   *, nS, conv_k, n_exp, n_grp):
    T = x_ref.shape[0]
    cw = ucur_ref.shape[1]
    i = pl.program_id(0)

    @pl.when(i == 0)
    def _():
        cnt_ref[...] = jnp.zeros_like(cnt_ref)

    first = (i % nS) == 0
    halo = uhalo_ref[...].astype(F32)
    ubuf_ref[0, 0:CONV_HALO, :] = jnp.where(first, jnp.zeros_like(halo), halo)
    ubuf_ref[0, CONV_HALO:, :] = ucur_ref[...].astype(F32)
    span = T + CONV_HALO - SUBLANES
    for b in range(1, SUBLANES):
        ubuf_ref[b, 0:span, :] = ubuf_ref[0, b:b + span, :]

    off = CONV_HALO - (conv_k - 1)
    rows = 32
    for r0 in range(0, T, rows):
        acc = jnp.zeros((rows // SUBLANES, SUBLANES, cw), F32)
        for j in range(conv_k):
            a, b = divmod(j + off, SUBLANES)
            lo_r = r0 + a * SUBLANES
            acc = acc + (wdw_ref[j * SUBLANES:(j + 1) * SUBLANES, :]
                         * ubuf_ref[b, lo_r:lo_r + rows, :].reshape(rows // SUBLANES, SUBLANES, cw))
        y = acc.reshape(rows, cw) + bdw_ref[...]
        mu = jnp.mean(y, axis=1, keepdims=True)
        d = y - mu
        var = jnp.mean(d * d, axis=1, keepdims=True)
        z = d * lax.rsqrt(var + EPS) * lng_ref[...] + lnb_ref[...]
        conv_ref[r0:r0 + rows, :] = _silu(z).astype(BF16)

    yo = (jnp.dot(attn_ref[...], wo1_ref[...], preferred_element_type=F32)
          + jnp.dot(conv_ref[...], wo2_ref[...], preferred_element_type=F32))
    x1 = x_ref[...] + mod_ref[2:3, :] * yo
    x1_ref[...] = x1
    ms = jnp.mean(x1 * x1, axis=1, keepdims=True)
    h2 = x1 * lax.rsqrt(ms + EPS) * gffn_ref[...] * (1.0 + mod_ref[4:5, :]) + mod_ref[3:4, :]
    h2_ref[...] = h2
    hi = h2.astype(BF16)
    lo = (h2 - hi.astype(F32)).astype(BF16)

    logits = (jnp.dot(hi, wrh_ref[...], preferred_element_type=F32)
              + jnp.dot(lo, wrh_ref[...], preferred_element_type=F32)
              + jnp.dot(hi, wrl_ref[...], preferred_element_type=F32)) + br_ref[...]
    lane_i = lax.broadcasted_iota(jnp.int32, logits.shape, 1)
    lane = lane_i.astype(F32)
    big = jnp.float32(1e9)
    ninf = jnp.float32(-jnp.inf)
    is_g = (lane_i >= n_exp) & (lane_i < n_exp + n_grp)
    gl = jnp.where(is_g, logits, ninf)
    gmax = jnp.max(gl, axis=1, keepdims=True)
    gsum = jnp.sum(jnp.where(is_g, jnp.exp(gl - gmax), 0.0), axis=1, keepdims=True)
    g_p = 1.0 / gsum
    gidx = jnp.min(jnp.where(gl == gmax, lane, big), axis=1, keepdims=True) - n_exp
    epg = n_exp // n_grp
    lo_l = gidx * epg
    in_grp = (lane >= lo_l) & (lane < lo_l + epg)
    el = jnp.where(in_grp, logits, ninf)
    m1 = jnp.max(el, axis=1, keepdims=True)
    i1 = jnp.min(jnp.where(el == m1, lane, big), axis=1, keepdims=True)
    el2 = jnp.where(lane == i1, ninf, el)
    m2 = jnp.max(el2, axis=1, keepdims=True)
    i2 = jnp.min(jnp.where(el2 == m2, lane, big), axis=1, keepdims=True)
    e2 = jnp.exp(m2 - m1)
    p1 = 1.0 / (1.0 + e2)
    p2 = e2 / (1.0 + e2)

    sel = (lane == i1) | (lane == i2)
    rr = lax.broadcasted_iota(jnp.int32, (T, T), 0)
    cc = lax.broadcasted_iota(jnp.int32, (T, T), 1)
    before = jnp.dot((rr > cc).astype(BF16), sel.astype(BF16), preferred_element_type=F32) + cnt_ref[...]
    r1 = jnp.sum(jnp.where(lane == i1, before, 0.0), axis=1, keepdims=True)
    r2 = jnp.sum(jnp.where(lane == i2, before, 0.0), axis=1, keepdims=True)
    cnt_ref[...] += jnp.sum(sel.astype(F32), axis=0, keepdims=True)

    route = jnp.zeros_like(logits)
    for k, val in enumerate((i1, i2, g_p * p1, g_p * p2, r1, r2)):
        route = jnp.where(lane_i == k, val, route)
    route_ref[...] = route
    route_t_ref[...] = route.T[:SUBLANES]


def _mixout(attn, u, x2, mod, w_dw, b_dw, ln_g, ln_b, wo1, wo2, g_ffn, wr_hi, wr_lo, b_r,
            *, S, T, n_exp, n_grp):
    N, D = x2.shape
    aw = attn.shape[1]
    cw = u.shape[1]
    nS = S // T
    conv_k = w_dw.shape[0]
    w_dw = jnp.repeat(w_dw, SUBLANES, axis=0)
    hb = T // CONV_HALO
    kern = functools.partial(_mixout_kernel, nS=nS, conv_k=conv_k, n_exp=n_exp, n_grp=n_grp)
    full = lambda a: pl.BlockSpec(a.shape, lambda i: (0, 0))
    return pl.pallas_call(
        kern,
        grid=(N // T,),
        in_specs=[pl.BlockSpec((T, aw), lambda i: (i, 0)),
                  pl.BlockSpec((T, cw), lambda i: (i, 0)),
                  pl.BlockSpec((CONV_HALO, cw), lambda i: (jnp.maximum(i * hb - 1, 0), 0)),
                  pl.BlockSpec((T, D), lambda i: (i, 0)),
                  pl.BlockSpec((None, 6, D), lambda i: (i // nS, 0, 0)),
                  full(w_dw), full(b_dw), full(ln_g), full(ln_b), full(wo1), full(wo2), full(g_ffn),
                  full(wr_hi), full(wr_lo), full(b_r)],
        out_specs=[pl.BlockSpec((T, D), lambda i: (i, 0)),
                   pl.BlockSpec((T, D), lambda i: (i, 0)),
                   pl.BlockSpec((T, LANES), lambda i: (i, 0)),
                   pl.BlockSpec((SUBLANES, T), lambda i: (0, i)),
                   pl.BlockSpec((1, LANES), lambda i: (0, 0))],
        out_shape=[jax.ShapeDtypeStruct((N, D), F32),
                   jax.ShapeDtypeStruct((N, D), F32),
                   jax.ShapeDtypeStruct((N, LANES), F32),
                   jax.ShapeDtypeStruct((SUBLANES, N), F32),
                   jax.ShapeDtypeStruct((1, LANES), F32)],
        scratch_shapes=[pltpu.VMEM((SUBLANES, CONV_HALO + T, cw), F32),
                        pltpu.VMEM((T, cw), BF16)],
        compiler_params=_cparams(("arbitrary",)),
        name="conv_outproj_router",
    )(attn, u, u, x2, mod, w_dw, b_dw, ln_g, ln_b, wo1, wo2, g_ffn, wr_hi, wr_lo, b_r)


def _row_of(ref, r):
    if isinstance(r, int):
        return ref.at[r // SUBLANES, pl.ds(r % SUBLANES, 1)]
    return ref.at[lax.shift_right_logical(r, 3), pl.ds(r & (SUBLANES - 1), 1)]


def _scatter_kernel(dest_ref, h2_ref, xs_hbm, sem):
    groups = h2_ref.shape[0]
    tokens = groups * SUBLANES

    def issue(g, carry):
        for j in range(SUBLANES):
            for k in range(TOP_K_INNER):
                d = dest_ref[0, k * tokens + g * SUBLANES + j]
                pltpu.make_async_copy(h2_ref.at[g, pl.ds(j, 1)], _row_of(xs_hbm, d), sem).start(priority=k % 2)
        return carry

    lax.fori_loop(0, groups, issue, 0)
    for k in range(TOP_K_INNER):
        pltpu.make_async_copy(h2_ref, xs_hbm.at[pl.ds(0, groups)], sem).wait()


def _dest_table(dest, tokens):
    tab = jnp.concatenate([d.reshape(-1, tokens) for d in dest], axis=1)
    return tab.reshape(tab.shape[0], 1, tab.shape[1])


def _scatter_rows(h2, dest, *, tokens):
    N, C = h2.shape
    steps = N // tokens
    dest3 = _dest_table(dest, tokens)
    rows = N * TOP_K_INNER
    xs = pl.pallas_call(
        _scatter_kernel,
        grid=(steps,),
        in_specs=[pl.BlockSpec((None, 1, TOP_K_INNER * tokens), lambda s: (s, 0, 0), memory_space=pltpu.SMEM),
                  pl.BlockSpec((tokens // SUBLANES, SUBLANES, C), lambda s: (s, 0, 0))],
        out_specs=pl.BlockSpec(memory_space=pl.ANY),
        out_shape=jax.ShapeDtypeStruct((rows // SUBLANES, SUBLANES, C), h2.dtype),
        scratch_shapes=[pltpu.SemaphoreType.DMA],
        compiler_params=_cparams(("arbitrary",)),
        name="moe_scatter_rows",
    )(dest3, h2.reshape(N // SUBLANES, SUBLANES, C))
    return xs.reshape(rows, C)


def _experts_kernel(vt_ref, ve_ref, vlo_ref, vhi_ref, xs_ref, wg_ref, wu_ref, wd_ref, ys_ref):
    v = pl.program_id(0)
    lo = vlo_ref[v]
    hi = vhi_ref[v]

    @pl.when(hi > lo)
    def _():
        x = xs_ref[...].astype(BF16)
        g = jnp.dot(x, wg_ref[...].astype(BF16), preferred_element_type=F32)
        u = jnp.dot(x, wu_ref[...].astype(BF16), preferred_element_type=F32)
        hid = (_silu(g) * u).astype(BF16)
        y = jnp.dot(hid, wd_ref[...].astype(BF16), preferred_element_type=F32)

        @pl.when(lo == 0)
        def _():
            ys_ref[...] = y

        @pl.when(lo > 0)
        def _():
            row = lax.broadcasted_iota(jnp.int32, y.shape, 0)
            ys_ref[...] = jnp.where((row >= lo) & (row < hi), y, ys_ref[...])


def _experts(xs, w_gate, w_up, w_down, visits, *, tm):
    R, C = xs.shape
    E, D, ff = w_gate.shape
    vt, ve, vlo, vhi = visits
    grid_spec = pltpu.PrefetchScalarGridSpec(
        num_scalar_prefetch=4,
        grid=(vt.shape[0],),
        in_specs=[pl.BlockSpec((tm, C), lambda v, vt, ve, vlo, vhi: (vt[v], 0)),
                  pl.BlockSpec((None, D, ff), lambda v, vt, ve, vlo, vhi: (ve[v], 0, 0)),
                  pl.BlockSpec((None, D, ff), lambda v, vt, ve, vlo, vhi: (ve[v], 0, 0)),
                  pl.BlockSpec((None, ff, D), lambda v, vt, ve, vlo, vhi: (ve[v], 0, 0))],
        out_specs=pl.BlockSpec((tm, C), lambda v, vt, ve, vlo, vhi: (vt[v], 0)),
    )
    return pl.pallas_call(
        _experts_kernel,
        grid_spec=grid_spec,
        out_shape=jax.ShapeDtypeStruct((R, C), F32),
        compiler_params=_cparams(("arbitrary",)),
        name="moe_grouped_experts",
    )(vt, ve, vlo, vhi, xs, w_gate, w_up, w_down)


def _visit_tables(off, cnt, n_rows, tm):
    n_tiles = n_rows // tm
    n_exp = off.shape[0]
    n_visits = n_tiles + n_exp - 1
    tile_starts = jnp.arange(n_tiles, dtype=jnp.int32) * tm
    seg_starts = jnp.where((cnt > 0) & (off % tm != 0), off, n_rows)
    starts = jnp.sort(jnp.concatenate([tile_starts, seg_starts]))
    lo_abs = starts[:n_visits]
    hi_abs = starts[1:n_visits + 1]
    valid = lo_abs < n_rows
    tile = jnp.where(valid, lo_abs // tm, n_tiles - 1)
    ends = off + cnt
    probe = jnp.where(valid, lo_abs, n_rows - 1)
    expert = jnp.sum(ends[None, :] <= probe[:, None], axis=1).astype(jnp.int32)
    row_lo = jnp.where(valid, lo_abs - tile * tm, 0)
    row_hi = jnp.where(valid, hi_abs - tile * tm, 0)
    return tile, expert, row_lo, row_hi


def _combine_kernel(dcur_ref, dnxt_ref, ys_hbm, route_ref, x1_ref, mod_ref, o_ref, ybuf, sem):
    T = x1_ref.shape[0]
    i = pl.program_id(0)
    slot = lax.rem(i, 2)

    def gather(d_ref, sl):
        def body(g, carry):
            for j in range(SUBLANES):
                for k in range(TOP_K_INNER):
                    d = d_ref[0, k * T + g * SUBLANES + j]
                    pltpu.make_async_copy(_row_of(ys_hbm, d), ybuf.at[sl, k, g, pl.ds(j, 1)],
                                          sem.at[sl]).start(priority=k % 2)
            return carry
        lax.fori_loop(0, T // SUBLANES, body, 0)

    @pl.when(i == 0)
    def _():
        gather(dcur_ref, 0)

    @pl.when(i + 1 < pl.num_programs(0))
    def _():
        gather(dnxt_ref, 1 - slot)

    for k in range(TOP_K_INNER):
        pltpu.make_async_copy(ys_hbm.at[pl.ds(0, T // SUBLANES)], ybuf.at[slot, k], sem.at[slot]).wait()

    route = route_ref[...]
    moe = jnp.zeros(x1_ref.shape, F32)
    for k in range(TOP_K_INNER):
        moe = moe + route[:, TOP_K_INNER + k:TOP_K_INNER + k + 1] * ybuf[slot, k].reshape(x1_ref.shape)
    o_ref[...] = x1_ref[...] + mod_ref[5:6, :] * moe


def _combine(ys, dest, route, x1, mod, *, S, T):
    N, D = x1.shape
    C = ys.shape[1]
    steps = N // T
    nS = S // T
    dest3 = _dest_table(dest, T)
    dspec = lambda f: pl.BlockSpec((None, 1, TOP_K_INNER * T), f, memory_space=pltpu.SMEM)
    return pl.pallas_call(
        _combine_kernel,
        grid=(steps,),
        in_specs=[dspec(lambda i: (i, 0, 0)),
                  dspec(lambda i: (jnp.minimum(i + 1, steps - 1), 0, 0)),
                  pl.BlockSpec(memory_space=pl.ANY),
                  pl.BlockSpec((T, LANES), lambda i: (i, 0)),
                  pl.BlockSpec((T, D), lambda i: (i, 0)),
                  pl.BlockSpec((None, 6, D), lambda i: (i // nS, 0, 0))],
        out_specs=pl.BlockSpec((T, D), lambda i: (i, 0)),
        out_shape=jax.ShapeDtypeStruct((N, D), F32),
        scratch_shapes=[pltpu.VMEM((2, TOP_K_INNER, T // SUBLANES, SUBLANES, C), F32),
                        pltpu.SemaphoreType.DMA((2,))],
        compiler_params=_cparams(("arbitrary",)),
        name="moe_gather_combine",
    )(dest3, dest3, ys.reshape(ys.shape[0] // SUBLANES, SUBLANES, C), route, x1, mod)


def _layer(x2, mod, pos_row, l, B, S, g_mix, w_in, q_norm_g, k_norm_g, lambda_q1, lambda_k1, lambda_q2,
           lambda_k2, subln_g, b_glu, w_dw, b_dw, conv_ln_g, conv_ln_b, w_out, g_ffn, w_group, b_group,
           w_router, b_router, w_gate, w_up, w_down):
    N, D = x2.shape
    dh = q_norm_g.shape[0]
    H = N_DIFF_HEADS
    aw = H * 2 * dh
    cw = w_dw.shape[1]
    rot = dh // 4
    n_grp = w_group.shape[1]
    n_exp = w_router.shape[1]
    lambda_init = 0.8 - 0.6 * math.exp(-0.3 * l)

    w_qk_t = w_in[:, :2 * aw].T.astype(BF16)
    w_vglu = w_in[:, 2 * aw:].astype(BF16)
    scale = dh ** -0.5 * math.log2(math.e)
    gq = jnp.tile(q_norm_g * scale, aw // dh)
    gk = jnp.tile(k_norm_g, aw // dh)
    gqk_tab = jnp.broadcast_to(jnp.concatenate([gq, gk])[:, None], (2 * aw, LANES))
    inv_freq = ROPE_THETA ** (-jnp.arange(0, rot, 2, dtype=F32) / rot)
    invf_tab = jnp.broadcast_to(inv_freq[:, None], (rot // 2, LANES))

    q, kt, v, u = _inproj(x2, mod, g_mix.reshape(1, D), pos_row, w_vglu, w_qk_t, b_glu.reshape(1, 2 * cw),
                          gqk_tab, invf_tab, B=B, S=S, aw=aw, cw=cw, dh=dh, rot=rot, T=256)

    attn = _diff_attention(q, kt, v, lambda_q1.reshape(1, dh), lambda_k1.reshape(1, dh),
                           lambda_q2.reshape(1, dh), lambda_k2.reshape(1, dh), subln_g.reshape(1, 2 * dh),
                           B=B, S=S, H=H, dh=dh, lambda_init=lambda_init, tq=512, tk=1024)

    w_r = jnp.zeros((D, LANES), F32).at[:, :n_exp].set(w_router).at[:, n_exp:n_exp + n_grp].set(w_group)
    b_r = jnp.zeros((1, LANES), F32).at[0, :n_exp].set(b_router).at[0, n_exp:n_exp + n_grp].set(b_group)
    wr_hi = w_r.astype(BF16)
    wr_lo = (w_r - wr_hi.astype(F32)).astype(BF16)
    wo = w_out.astype(BF16)
    x1, h2, route, route_t, counts = _mixout(attn, u, x2, mod, w_dw, b_dw.reshape(1, cw), conv_ln_g.reshape(1, cw),
                                     conv_ln_b.reshape(1, cw), wo[:aw], wo[aw:], g_ffn.reshape(1, D),
                                     wr_hi, wr_lo, b_r, S=S, T=256, n_exp=n_exp, n_grp=n_grp)

    cnt = counts[0, :n_exp].astype(jnp.int32)
    off = jnp.cumsum(cnt) - cnt
    experts = jnp.arange(n_exp, dtype=jnp.int32)[:, None]
    dest = [jnp.sum(jnp.where(route_t[k].astype(jnp.int32)[None, :] == experts, off[:, None], 0), axis=0)
            + route_t[4 + k].astype(jnp.int32) for k in range(TOP_K_INNER)]
    tm = 256
    visits = _visit_tables(off, cnt, N * TOP_K_INNER, tm)

    xs = _scatter_rows(h2, dest, tokens=min(1024, N))
    ys = _experts(xs, w_gate, w_up, w_down, visits, tm=tm)
    return _combine(ys, dest, route, x1, mod, S=S, T=256)


def kernel(x, c, positions, w_ada, b_ada, g_mix, w_in, q_norm_g, k_norm_g, lambda_q1, lambda_k1, lambda_q2,
           lambda_k2, subln_g, b_glu, w_dw, b_dw, conv_ln_g, conv_ln_b, w_out, g_ffn, w_group, b_group,
           w_router, b_router, w_gate, w_up, w_down):
    B, S, D = x.shape
    depth = w_ada.shape[0]
    x2 = x.reshape(B * S, D)
    pos_row = positions.astype(F32).reshape(1, B * S)
    for l in range(depth):
        mod = _modulation(c, w_ada[l], b_ada[l])
        x2 = _layer(x2, mod, pos_row, l, B, S, g_mix[l], w_in[l], q_norm_g[l], k_norm_g[l], lambda_q1[l],
                    lambda_k1[l], lambda_q2[l], lambda_k2[l], subln_g[l], b_glu[l], w_dw[l], b_dw[l],
                    conv_ln_g[l], conv_ln_b[l], w_out[l], g_ffn[l], w_group[l], b_group[l], w_router[l],
                    b_router[l], w_gate[l], w_up[l], w_down[l])
    return x2.reshape(B, S, D)
```

```python
import functools
import math

import numpy as np
import jax
import jax.numpy as jnp
from jax import lax
from jax.experimental import pallas as pl
from jax.experimental.pallas import tpu as pltpu

F32 = jnp.float32
BF16 = jnp.bfloat16

EPS = 1e-6
ROPE_THETA = 500000.0
N_DIFF_HEADS = 4
TOP_K_INNER = 2

LANES = 128
SUBLANES = 8
CONV_HALO = 32
VMEM_LIMIT = 48 * 1024 * 1024


def _cparams(sem):
    return pltpu.CompilerParams(dimension_semantics=sem, vmem_limit_bytes=VMEM_LIMIT)


def _silu(x):
    return x * jax.nn.sigmoid(x)


def _mod_kernel(c_ref, w_ref, b_ref, o_ref):
    c = c_ref[...]
    o_ref[...] = jnp.dot(_silu(c), w_ref[...], preferred_element_type=F32,
                         precision=lax.Precision.HIGHEST) + b_ref[...]


def _modulation(c, w_ada, b_ada):
    B, D = c.shape
    n_out = w_ada.shape[1]
    rows = 8
    c_pad = jnp.pad(c, ((0, rows - B), (0, 0)))
    bn = 1024
    out = pl.pallas_call(
        _mod_kernel,
        grid=(n_out // bn,),
        in_specs=[pl.BlockSpec((rows, D), lambda j: (0, 0)),
                  pl.BlockSpec((D, bn), lambda j: (0, j)),
                  pl.BlockSpec((1, bn), lambda j: (0, j))],
        out_specs=pl.BlockSpec((rows, bn), lambda j: (0, j)),
        out_shape=jax.ShapeDtypeStruct((rows, n_out), F32),
        compiler_params=_cparams(("parallel",)),
        name="adaln_mod",
    )(c_pad, w_ada, b_ada.reshape(1, n_out))
    return out[:B].reshape(B, 6, D)


def _inproj_kernel(x_ref, mod_ref, g_ref, pos_ref, wv_ref, wqk_ref, bglu_ref, gqk_ref, invf_ref,
                   q_ref, kt_ref, v_ref, u_ref, *, aw, cw, dh, rot):
    T = x_ref.shape[0]
    half = rot // 2
    x = x_ref[...]
    ms = jnp.mean(x * x, axis=-1, keepdims=True)
    sh = mod_ref[0:1, :]
    sc = mod_ref[1:2, :]
    h = x * lax.rsqrt(ms + EPS) * g_ref[...] * (1.0 + sc) + sh
    hb = h.astype(BF16)

    pv = jnp.dot(hb, wv_ref[...], preferred_element_type=F32)
    v_ref[...] = pv[:, :aw].astype(BF16)
    a = pv[:, aw:aw + cw] + bglu_ref[:, :cw]
    gate = pv[:, aw + cw:] + bglu_ref[:, cw:]
    u_ref[...] = (a * jax.nn.sigmoid(gate)).astype(BF16)

    qkt = lax.dot_general(wqk_ref[...], hb, (((1,), (1,)), ((), ())), preferred_element_type=F32)
    nch = 2 * aw // dh
    for c in range(T // LANES):
        sl = slice(c * LANES, (c + 1) * LANES)
        s3 = qkt[:, sl].reshape(nch, dh, LANES)
        ssq = jnp.mean(s3 * s3, axis=1, keepdims=True)
        y = s3 * lax.rsqrt(ssq + EPS) * gqk_ref[...].reshape(nch, dh, LANES)
        ang = invf_ref[...] * pos_ref[:, sl]
        cs = jnp.cos(ang)
        sn = jnp.sin(ang)
        t1 = y[:, 0:half, :]
        t2 = y[:, half:rot, :]
        y = jnp.concatenate([t1 * cs - t2 * sn, t2 * cs + t1 * sn, y[:, rot:, :]], axis=1)
        y2 = y.reshape(2 * aw, LANES)
        kt_ref[:, sl] = y2[aw:].astype(BF16)
        q_ref[sl, :] = y2[:aw].T.astype(BF16)


def _inproj(x2, mod, g_mix, pos_row, w_vglu, w_qk_t, b_glu, gqk_tab, invf_tab, *, B, S, aw, cw, dh, rot, T):
    N, D = x2.shape
    nS = S // T
    kern = functools.partial(_inproj_kernel, aw=aw, cw=cw, dh=dh, rot=rot)
    return pl.pallas_call(
        kern,
        grid=(N // T,),
        in_specs=[pl.BlockSpec((T, D), lambda i: (i, 0)),
                  pl.BlockSpec((None, 6, D), lambda i: (i // nS, 0, 0)),
                  pl.BlockSpec((1, D), lambda i: (0, 0)),
                  pl.BlockSpec((1, T), lambda i: (0, i)),
                  pl.BlockSpec(w_vglu.shape, lambda i: (0, 0)),
                  pl.BlockSpec(w_qk_t.shape, lambda i: (0, 0)),
                  pl.BlockSpec((1, 2 * cw), lambda i: (0, 0)),
                  pl.BlockSpec(gqk_tab.shape, lambda i: (0, 0)),
                  pl.BlockSpec(invf_tab.shape, lambda i: (0, 0))],
        out_specs=[pl.BlockSpec((T, aw), lambda i: (i, 0)),
                   pl.BlockSpec((None, aw, T), lambda i: (i // nS, 0, i % nS)),
                   pl.BlockSpec((T, aw), lambda i: (i, 0)),
                   pl.BlockSpec((T, cw), lambda i: (i, 0))],
        out_shape=[jax.ShapeDtypeStruct((N, aw), BF16),
                   jax.ShapeDtypeStruct((B, aw, S), BF16),
                   jax.ShapeDtypeStruct((N, aw), BF16),
                   jax.ShapeDtypeStruct((N, cw), BF16)],
        compiler_params=_cparams(("parallel",)),
        name="inproj_qknorm_rope_glu",
    )(x2, mod, g_mix, pos_row, w_vglu, w_qk_t, b_glu, gqk_tab, invf_tab)


def _attn_kernel(q_ref, kt_ref, v_ref, lq1_ref, lk1_ref, lq2_ref, lk2_ref, sg_ref, o_ref, acc_ref,
                 s0_ref, s1_ref, *, tq, tk, dh, lambda_init):
    i = pl.program_id(2)
    hd = 2 * dh
    q = q_ref[...]
    lane = lax.broadcasted_iota(jnp.int32, q.shape, 1)
    zero = jnp.zeros_like(q)
    q2 = jnp.concatenate([jnp.where(lane < dh, q, zero), jnp.where(lane >= dh, q, zero)], axis=0)

    acc_ref[...] = jnp.zeros_like(acc_ref)
    n_sub = tk // tq
    ones_col = {w: jnp.ones((w, hd), BF16) for w in {tk} | {(r + 1) * tq for r in range(n_sub)}}

    def scores(t, s_ref):
        start = pl.multiple_of(t * tk, tk)
        s_ref[...] = jnp.dot(q2, kt_ref[:, pl.ds(start, tk)], preferred_element_type=F32)

    def softmax_pv(t, s_ref, m, width=tk, diagonal=False):
        start = pl.multiple_of(t * tk, tk)
        vt = jnp.concatenate([v_ref[pl.ds(start, width), :], ones_col[width]], axis=1)
        s = s_ref[:, :width]
        if diagonal:
            row = lax.broadcasted_iota(jnp.int32, (2 * tq, tq), 0)
            col = lax.broadcasted_iota(jnp.int32, (2 * tq, tq), 1)
            qrow = jnp.where(row >= tq, row - tq, row)
            tail = jnp.where(col <= qrow, s[:, width - tq:], -jnp.inf)
            s = tail if width == tq else jnp.concatenate([s[:, :width - tq], tail], axis=1)
        m_new = jnp.maximum(m, jnp.max(s, axis=1, keepdims=True))
        alpha = jnp.exp2(m - m_new)
        p = jnp.exp2(s - m_new).astype(BF16)
        acc_ref[...] = alpha * acc_ref[...] + jnp.dot(p, vt, preferred_element_type=F32)
        return m_new

    n_full = (i * tq) // tk
    odd = lax.rem(n_full, 2)
    m0 = jnp.full((2 * tq, 1), -jnp.inf, F32)

    @pl.when(odd == 0)
    def _():
        scores(0, s0_ref)

    @pl.when(odd == 1)
    def _():
        scores(0, s1_ref)

    def peeled(_, m):
        scores(1, s0_ref)
        return softmax_pv(0, s1_ref, m)

    m = lax.fori_loop(0, odd, peeled, m0)

    def pair(jj, m):
        t = odd + 2 * jj
        scores(t + 1, s1_ref)
        m = softmax_pv(t, s0_ref, m)
        scores(t + 2, s0_ref)
        return softmax_pv(t + 1, s1_ref, m)

    m = lax.fori_loop(0, (n_full - odd) // 2, pair, m)

    for r in range(n_sub):
        @pl.when(lax.rem(i, n_sub) == r)
        def _(r=r):
            softmax_pv(n_full, s0_ref, m, width=(r + 1) * tq, diagonal=True)

    lam = (jnp.exp(jnp.sum(lq1_ref[...] * lk1_ref[...], axis=1, keepdims=True))
           - jnp.exp(jnp.sum(lq2_ref[...] * lk2_ref[...], axis=1, keepdims=True)) + lambda_init)
    o = (acc_ref[0:tq, 0:hd] / acc_ref[0:tq, hd:]
         - lam * (acc_ref[tq:, 0:hd] / acc_ref[tq:, hd:]))
    ms = jnp.mean(o * o, axis=1, keepdims=True)
    o = o * lax.rsqrt(ms + EPS) * sg_ref[...] * (1.0 - lambda_init)
    o_ref[...] = o.astype(o_ref.dtype)


def _diff_attention(q, kt, v, lq1, lk1, lq2, lk2, subln_g, *, B, S, H, dh, lambda_init, tq, tk):
    aw = H * 2 * dh
    q3 = q.reshape(B, S, aw)
    v3 = v.reshape(B, S, aw)
    hd = 2 * dh
    tk = min(tk, S)
    kern = functools.partial(_attn_kernel, tq=tq, tk=tk, dh=dh, lambda_init=lambda_init)
    vec = pl.BlockSpec((1, dh), lambda b, h, i: (0, 0))
    out = pl.pallas_call(
        kern,
        grid=(B, H, S // tq),
        in_specs=[pl.BlockSpec((None, tq, hd), lambda b, h, i: (b, i, h)),
                  pl.BlockSpec((None, hd, S), lambda b, h, i: (b, h, 0)),
                  pl.BlockSpec((None, S, hd), lambda b, h, i: (b, 0, h)),
                  vec, vec, vec, vec,
                  pl.BlockSpec((1, hd), lambda b, h, i: (0, 0))],
        out_specs=pl.BlockSpec((None, tq, hd), lambda b, h, i: (b, i, h)),
        out_shape=jax.ShapeDtypeStruct((B, S, aw), BF16),
        scratch_shapes=[pltpu.VMEM((2 * tq, 2 * hd), F32),
                        pltpu.VMEM((2 * tq, tk), F32),
                        pltpu.VMEM((2 * tq, tk), F32)],
        compiler_params=_cparams(("parallel", "parallel", "parallel")),
        name="diff_flash_attention",
    )(q3, kt, v3, lq1, lk1, lq2, lk2, subln_g)
    return out.reshape(B * S, aw)


def _mixout_kernel(attn_ref, ucur_ref, uhalo_ref, x_ref, mod_ref, wdw_ref, bdw_ref, lng_ref, lnb_ref,
                   wo1_ref, wo2_ref, gffn_ref, wrh_ref, wrl_ref, br_ref,
                   x1_ref, h2_ref, route_ref, route_t_ref, cnt_ref, ubuf_ref, conv_ref,
                   *, nS, conv_k, n_exp, n_grp):
    T = x_ref.shape[0]
    cw = ucur_ref.shape[1]
    i = pl.program_id(0)

    @pl.when(i == 0)
    def _():
        cnt_ref[...] = jnp.zeros_like(cnt_ref)

    first = (i % nS) == 0
    halo = uhalo_ref[...].astype(F32)
    ubuf_ref[0, 0:CONV_HALO, :] = jnp.where(first, jnp.zeros_like(halo), halo)
    ubuf_ref[0, CONV_HALO:, :] = ucur_ref[...].astype(F32)
    span = T + CONV_HALO - SUBLANES
    for b in range(1, SUBLANES):
        ubuf_ref[b, 0:span, :] = ubuf_ref[0, b:b + span, :]

    off = CONV_HALO - (conv_k - 1)
    rows = 32
    for r0 in range(0, T, rows):
        acc = jnp.zeros((rows // SUBLANES, SUBLANES, cw), F32)
        for j in range(conv_k):
            a, b = divmod(j + off, SUBLANES)
            lo_r = r0 + a * SUBLANES
            acc = acc + (wdw_ref[j * SUBLANES:(j + 1) * SUBLANES, :]
                         * ubuf_ref[b, lo_r:lo_r + rows, :].reshape(rows // SUBLANES, SUBLANES, cw))
        y = acc.reshape(rows, cw) + bdw_ref[...]
        mu = jnp.mean(y, axis=1, keepdims=True)
        d = y - mu
        var = jnp.mean(d * d, axis=1, keepdims=True)
        z = d * lax.rsqrt(var + EPS) * lng_ref[...] + lnb_ref[...]
        conv_ref[r0:r0 + rows, :] = _silu(z).astype(BF16)

    yo = (jnp.dot(attn_ref[...], wo1_ref[...], preferred_element_type=F32)
          + jnp.dot(conv_ref[...], wo2_ref[...], preferred_element_type=F32))
    x1 = x_ref[...] + mod_ref[2:3, :] * yo
    x1_ref[...] = x1
    ms = jnp.mean(x1 * x1, axis=1, keepdims=True)
    h2 = x1 * lax.rsqrt(ms + EPS) * gffn_ref[...] * (1.0 + mod_ref[4:5, :]) + mod_ref[3:4, :]
    h2_ref[...] = h2
    hi = h2.astype(BF16)
    lo = (h2 - hi.astype(F32)).astype(BF16)

    logits = (jnp.dot(hi, wrh_ref[...], preferred_element_type=F32)
              + jnp.dot(lo, wrh_ref[...], preferred_element_type=F32)
              + jnp.dot(hi, wrl_ref[...], preferred_element_type=F32)) + br_ref[...]
    lane_i = lax.broadcasted_iota(jnp.int32, logits.shape, 1)
    lane = lane_i.astype(F32)
    big = jnp.float32(1e9)
    ninf = jnp.float32(-jnp.inf)
    is_g = (lane_i >= n_exp) & (lane_i < n_exp + n_grp)
    gl = jnp.where(is_g, logits, ninf)
    gmax = jnp.max(gl, axis=1, keepdims=True)
    gsum = jnp.sum(jnp.where(is_g, jnp.exp(gl - gmax), 0.0), axis=1, keepdims=True)
    g_p = 1.0 / gsum
    gidx = jnp.min(jnp.where(gl == gmax, lane, big), axis=1, keepdims=True) - n_exp
    epg = n_exp // n_grp
    lo_l = gidx * epg
    in_grp = (lane >= lo_l) & (lane < lo_l + epg)
    el = jnp.where(in_grp, logits, ninf)
    m1 = jnp.max(el, axis=1, keepdims=True)
    i1 = jnp.min(jnp.where(el == m1, lane, big), axis=1, keepdims=True)
    el2 = jnp.where(lane == i1, ninf, el)
    m2 = jnp.max(el2, axis=1, keepdims=True)
    i2 = jnp.min(jnp.where(el2 == m2, lane, big), axis=1, keepdims=True)
    e2 = jnp.exp(m2 - m1)
    p1 = 1.0 / (1.0 + e2)
    p2 = e2 / (1.0 + e2)

    sel = (lane == i1) | (lane == i2)
    rr = lax.broadcasted_iota(jnp.int32, (T, T), 0)
    cc = lax.broadcasted_iota(jnp.int32, (T, T), 1)
    before = jnp.dot((rr > cc).astype(BF16), sel.astype(BF16), preferred_element_type=F32) + cnt_ref[...]
    r1 = jnp.sum(jnp.where(lane == i1, before, 0.0), axis=1, keepdims=True)
    r2 = jnp.sum(jnp.where(lane == i2, before, 0.0), axis=1, keepdims=True)
    cnt_ref[...] += jnp.sum(sel.astype(F32), axis=0, keepdims=True)

    route = jnp.zeros_like(logits)
    for k, val in enumerate((i1, i2, g_p * p1, g_p * p2, r1, r2)):
        route = jnp.where(lane_i == k, val, route)
    route_ref[...] = route
    route_t_ref[...] = route.T[:SUBLANES]


def _mixout(attn, u, x2, mod, w_dw, b_dw, ln_g, ln_b, wo1, wo2, g_ffn, wr_hi, wr_lo, b_r,
            *, S, T, n_exp, n_grp):
    N, D = x2.shape
    aw = attn.shape[1]
    cw = u.shape[1]
    nS = S // T
    conv_k = w_dw.shape[0]
    w_dw = jnp.repeat(w_dw, SUBLANES, axis=0)
    hb = T // CONV_HALO
    kern = functools.partial(_mixout_kernel, nS=nS, conv_k=conv_k, n_exp=n_exp, n_grp=n_grp)
    full = lambda a: pl.BlockSpec(a.shape, lambda i: (0, 0))
    return pl.pallas_call(
        kern,
        grid=(N // T,),
        in_specs=[pl.BlockSpec((T, aw), lambda i: (i, 0)),
                  pl.BlockSpec((T, cw), lambda i: (i, 0)),
                  pl.BlockSpec((CONV_HALO, cw), lambda i: (jnp.maximum(i * hb - 1, 0), 0)),
                  pl.BlockSpec((T, D), lambda i: (i, 0)),
                  pl.BlockSpec((None, 6, D), lambda i: (i // nS, 0, 0)),
                  full(w_dw), full(b_dw), full(ln_g), full(ln_b), full(wo1), full(wo2), full(g_ffn),
                  full(wr_hi), full(wr_lo), full(b_r)],
        out_specs=[pl.BlockSpec((T, D), lambda i: (i, 0)),
                   pl.BlockSpec((T, D), lambda i: (i, 0)),
                   pl.BlockSpec((T, LANES), lambda i: (i, 0)),
                   pl.BlockSpec((SUBLANES, T), lambda i: (0, i)),
                   pl.BlockSpec((1, LANES), lambda i: (0, 0))],
        out_shape=[jax.ShapeDtypeStruct((N, D), F32),
                   jax.ShapeDtypeStruct((N, D), F32),
                   jax.ShapeDtypeStruct((N, LANES), F32),
                   jax.ShapeDtypeStruct((SUBLANES, N), F32),
                   jax.ShapeDtypeStruct((1, LANES), F32)],
        scratch_shapes=[pltpu.VMEM((SUBLANES, CONV_HALO + T, cw), F32),
                        pltpu.VMEM((T, cw), BF16)],
        compiler_params=_cparams(("arbitrary",)),
        name="conv_outproj_router",
    )(attn, u, u, x2, mod, w_dw, b_dw, ln_g, ln_b, wo1, wo2, g_ffn, wr_hi, wr_lo, b_r)


def _row_of(ref, r):
    if isinstance(r, int):
        return ref.at[r // SUBLANES, pl.ds(r % SUBLANES, 1)]
    return ref.at[lax.shift_right_logical(r, 3), pl.ds(r & (SUBLANES - 1), 1)]


def _scatter_kernel(dest_ref, h2_ref, xs_hbm, sem):
    groups = h2_ref.shape[0]
    tokens = groups * SUBLANES

    def issue(g, carry):
        for j in range(SUBLANES):
            for k in range(TOP_K_INNER):
                d = dest_ref[0, k * tokens + g * SUBLANES + j]
                pltpu.make_async_copy(h2_ref.at[g, pl.ds(j, 1)], _row_of(xs_hbm, d), sem).start(priority=k % 2)
        return carry

    lax.fori_loop(0, groups, issue, 0)
    for k in range(TOP_K_INNER):
        pltpu.make_async_copy(h2_ref, xs_hbm.at[pl.ds(0, groups)], sem).wait()


def _dest_table(dest, tokens):
    tab = jnp.concatenate([d.reshape(-1, tokens) for d in dest], axis=1)
    return tab.reshape(tab.shape[0], 1, tab.shape[1])


def _scatter_rows(h2, dest, *, tokens):
    N, C = h2.shape
    steps = N // tokens
    dest3 = _dest_table(dest, tokens)
    rows = N * TOP_K_INNER
    xs = pl.pallas_call(
        _scatter_kernel,
        grid=(steps,),
        in_specs=[pl.BlockSpec((None, 1, TOP_K_INNER * tokens), lambda s: (s, 0, 0), memory_space=pltpu.SMEM),
                  pl.BlockSpec((tokens // SUBLANES, SUBLANES, C), lambda s: (s, 0, 0))],
        out_specs=pl.BlockSpec(memory_space=pl.ANY),
        out_shape=jax.ShapeDtypeStruct((rows // SUBLANES, SUBLANES, C), h2.dtype),
        scratch_shapes=[pltpu.SemaphoreType.DMA],
        compiler_params=_cparams(("arbitrary",)),
        name="moe_scatter_rows",
    )(dest3, h2.reshape(N // SUBLANES, SUBLANES, C))
    return xs.reshape(rows, C)


def _experts_kernel(vt_ref, ve_ref, vlo_ref, vhi_ref, xs_ref, wg_ref, wu_ref, wd_ref, ys_ref):
    v = pl.program_id(0)
    lo = vlo_ref[v]
    hi = vhi_ref[v]

    @pl.when(hi > lo)
    def _():
        x = xs_ref[...].astype(BF16)
        g = jnp.dot(x, wg_ref[...].astype(BF16), preferred_element_type=F32)
        u = jnp.dot(x, wu_ref[...].astype(BF16), preferred_element_type=F32)
        hid = (_silu(g) * u).astype(BF16)
        y = jnp.dot(hid, wd_ref[...].astype(BF16), preferred_element_type=F32)

        @pl.when(lo == 0)
        def _():
            ys_ref[...] = y

        @pl.when(lo > 0)
        def _():
            row = lax.broadcasted_iota(jnp.int32, y.shape, 0)
            ys_ref[...] = jnp.where((row >= lo) & (row < hi), y, ys_ref[...])


def _experts(xs, w_gate, w_up, w_down, visits, *, tm):
    R, C = xs.shape
    E, D, ff = w_gate.shape
    vt, ve, vlo, vhi = visits
    grid_spec = pltpu.PrefetchScalarGridSpec(
        num_scalar_prefetch=4,
        grid=(vt.shape[0],),
        in_specs=[pl.BlockSpec((tm, C), lambda v, vt, ve, vlo, vhi: (vt[v], 0)),
                  pl.BlockSpec((None, D, ff), lambda v, vt, ve, vlo, vhi: (ve[v], 0, 0)),
                  pl.BlockSpec((None, D, ff), lambda v, vt, ve, vlo, vhi: (ve[v], 0, 0)),
                  pl.BlockSpec((None, ff, D), lambda v, vt, ve, vlo, vhi: (ve[v], 0, 0))],
        out_specs=pl.BlockSpec((tm, C), lambda v, vt, ve, vlo, vhi: (vt[v], 0)),
    )
    return pl.pallas_call(
        _experts_kernel,
        grid_spec=grid_spec,
        out_shape=jax.ShapeDtypeStruct((R, C), F32),
        compiler_params=_cparams(("arbitrary",)),
        name="moe_grouped_experts",
    )(vt, ve, vlo, vhi, xs, w_gate, w_up, w_down)


def _visit_tables(off, cnt, n_rows, tm):
    n_tiles = n_rows // tm
    n_exp = off.shape[0]
    n_visits = n_tiles + n_exp - 1
    tile_starts = jnp.arange(n_tiles, dtype=jnp.int32) * tm
    seg_starts = jnp.where((cnt > 0) & (off % tm != 0), off, n_rows)
    starts = jnp.sort(jnp.concatenate([tile_starts, seg_starts]))
    lo_abs = starts[:n_visits]
    hi_abs = starts[1:n_visits + 1]
    valid = lo_abs < n_rows
    tile = jnp.where(valid, lo_abs // tm, n_tiles - 1)
    ends = off + cnt
    probe = jnp.where(valid, lo_abs, n_rows - 1)
    expert = jnp.sum(ends[None, :] <= probe[:, None], axis=1).astype(jnp.int32)
    row_lo = jnp.where(valid, lo_abs - tile * tm, 0)
    row_hi = jnp.where(valid, hi_abs - tile * tm, 0)
    return tile, expert, row_lo, row_hi


def _combine_kernel(dcur_ref, dnxt_ref, ys_hbm, route_ref, x1_ref, mod_ref, o_ref, ybuf, sem):
    T = x1_ref.shape[0]
    i = pl.program_id(0)
    slot = lax.rem(i, 2)

    def gather(d_ref, sl):
        def body(g, carry):
            for j in range(SUBLANES):
                for k in range(TOP_K_INNER):
                    d = d_ref[0, k * T + g * SUBLANES + j]
                    pltpu.make_async_copy(_row_of(ys_hbm, d), ybuf.at[sl, k, g, pl.ds(j, 1)],
                                          sem.at[sl]).start(priority=k % 2)
            return carry
        lax.fori_loop(0, T // SUBLANES, body, 0)

    @pl.when(i == 0)
    def _():
        gather(dcur_ref, 0)

    @pl.when(i + 1 < pl.num_programs(0))
    def _():
        gather(dnxt_ref, 1 - slot)

    for k in range(TOP_K_INNER):
        pltpu.make_async_copy(ys_hbm.at[pl.ds(0, T // SUBLANES)], ybuf.at[slot, k], sem.at[slot]).wait()

    route = route_ref[...]
    moe = jnp.zeros(x1_ref.shape, F32)
    for k in range(TOP_K_INNER):
        moe = moe + route[:, TOP_K_INNER + k:TOP_K_INNER + k + 1] * ybuf[slot, k].reshape(x1_ref.shape)
    o_ref[...] = x1_ref[...] + mod_ref[5:6, :] * moe


def _combine(ys, dest, route, x1, mod, *, S, T):
    N, D = x1.shape
    C = ys.shape[1]
    steps = N // T
    nS = S // T
    dest3 = _dest_table(dest, T)
    dspec = lambda f: pl.BlockSpec((None, 1, TOP_K_INNER * T), f, memory_space=pltpu.SMEM)
    return pl.pallas_call(
        _combine_kernel,
        grid=(steps,),
        in_specs=[dspec(lambda i: (i, 0, 0)),
                  dspec(lambda i: (jnp.minimum(i + 1, steps - 1), 0, 0)),
                  pl.BlockSpec(memory_space=pl.ANY),
                  pl.BlockSpec((T, LANES), lambda i: (i, 0)),
                  pl.BlockSpec((T, D), lambda i: (i, 0)),
                  pl.BlockSpec((None, 6, D), lambda i: (i // nS, 0, 0))],
        out_specs=pl.BlockSpec((T, D), lambda i: (i, 0)),
        out_shape=jax.ShapeDtypeStruct((N, D), F32),
        scratch_shapes=[pltpu.VMEM((2, TOP_K_INNER, T // SUBLANES, SUBLANES, C), F32),
                        pltpu.SemaphoreType.DMA((2,))],
        compiler_params=_cparams(("arbitrary",)),
        name="moe_gather_combine",
    )(dest3, dest3, ys.reshape(ys.shape[0] // SUBLANES, SUBLANES, C), route, x1, mod)


def _layer(x2, mod, pos_row, l, B, S, g_mix, w_in, q_norm_g, k_norm_g, lambda_q1, lambda_k1, lambda_q2,
           lambda_k2, subln_g, b_glu, w_dw, b_dw, conv_ln_g, conv_ln_b, w_out, g_ffn, w_group, b_group,
           w_router, b_router, w_gate, w_up, w_down):
    N, D = x2.shape
    dh = q_norm_g.shape[0]
    H = N_DIFF_HEADS
    aw = H * 2 * dh
    cw = w_dw.shape[1]
    rot = dh // 4
    n_grp = w_group.shape[1]
    n_exp = w_router.shape[1]
    lambda_init = 0.8 - 0.6 * math.exp(-0.3 * l)

    w_qk_t = w_in[:, :2 * aw].T.astype(BF16)
    w_vglu = w_in[:, 2 * aw:].astype(BF16)
    scale = dh ** -0.5 * math.log2(math.e)
    gq = jnp.tile(q_norm_g * scale, aw // dh)
    gk = jnp.tile(k_norm_g, aw // dh)
    gqk_tab = jnp.broadcast_to(jnp.concatenate([gq, gk])[:, None], (2 * aw, LANES))
    inv_freq = ROPE_THETA ** (-jnp.arange(0, rot, 2, dtype=F32) / rot)
    invf_tab = jnp.broadcast_to(inv_freq[:, None], (rot // 2, LANES))

    q, kt, v, u = _inproj(x2, mod, g_mix.reshape(1, D), pos_row, w_vglu, w_qk_t, b_glu.reshape(1, 2 * cw),
                          gqk_tab, invf_tab, B=B, S=S, aw=aw, cw=cw, dh=dh, rot=rot, T=512)

    attn = _diff_attention(q, kt, v, lambda_q1.reshape(1, dh), lambda_k1.reshape(1, dh),
                           lambda_q2.reshape(1, dh), lambda_k2.reshape(1, dh), subln_g.reshape(1, 2 * dh),
                           B=B, S=S, H=H, dh=dh, lambda_init=lambda_init, tq=1024, tk=1024)

    w_r = jnp.zeros((D, LANES), F32).at[:, :n_exp].set(w_router).at[:, n_exp:n_exp + n_grp].set(w_group)
    b_r = jnp.zeros((1, LANES), F32).at[0, :n_exp].set(b_router).at[0, n_exp:n_exp + n_grp].set(b_group)
    wr_hi = w_r.astype(BF16)
    wr_lo = (w_r - wr_hi.astype(F32)).astype(BF16)
    wo = w_out.astype(BF16)
    x1, h2, route, route_t, counts = _mixout(attn, u, x2, mod, w_dw, b_dw.reshape(1, cw), conv_ln_g.reshape(1, cw),
                                     conv_ln_b.reshape(1, cw), wo[:aw], wo[aw:], g_ffn.reshape(1, D),
                                     wr_hi, wr_lo, b_r, S=S, T=512, n_exp=n_exp, n_grp=n_grp)

    cnt = counts[0, :n_exp].astype(jnp.int32)
    off = jnp.cumsum(cnt) - cnt
    experts = jnp.arange(n_exp, dtype=jnp.int32)[:, None]
    dest = [jnp.sum(jnp.where(route_t[k].astype(jnp.int32)[None, :] == experts, off[:, None], 0), axis=0)
            + route_t[4 + k].astype(jnp.int32) for k in range(TOP_K_INNER)]
    tm = 256
    visits = _visit_tables(off, cnt, N * TOP_K_INNER, tm)

    xs = _scatter_rows(h2, dest, tokens=min(1024, N))
    ys = _experts(xs, w_gate, w_up, w_down, visits, tm=tm)
    return _combine(ys, dest, route, x1, mod, S=S, T=256)


def kernel(x, c, positions, w_ada, b_ada, g_mix, w_in, q_norm_g, k_norm_g, lambda_q1, lambda_k1, lambda_q2,
           lambda_k2, subln_g, b_glu, w_dw, b_dw, conv_ln_g, conv_ln_b, w_out, g_ffn, w_group, b_group,
           w_router, b_router, w_gate, w_up, w_down):
    B, S, D = x.shape
    depth = w_ada.shape[0]
    x2 = x.reshape(B * S, D)
    pos_row = positions.astype(F32).reshape(1, B * S)
    for l in range(depth):
        mod = _modulation(c, w_ada[l], b_ada[l])
        x2 = _layer(x2, mod, pos_row, l, B, S, g_mix[l], w_in[l], q_norm_g[l], k_norm_g[l], lambda_q1[l],
                    lambda_k1[l], lambda_q2[l], lambda_k2[l], subln_g[l], b_glu[l], w_dw[l], b_dw[l],
                    conv_ln_g[l], conv_ln_b[l], w_out[l], g_ffn[l], w_group[l], b_group[l], w_router[l],
                    b_router[l], w_gate[l], w_up[l], w_down[l])
    return x2.reshape(B, S, D)
```

```python
import functools
import math

import numpy as np
import jax
import jax.numpy as jnp
from jax import lax
from jax.experimental import pallas as pl
from jax.experimental.pallas import tpu as pltpu

F32 = jnp.float32
BF16 = jnp.bfloat16

EPS = 1e-6
ROPE_THETA = 500000.0
N_DIFF_HEADS = 4
TOP_K_INNER = 2

LANES = 128
SUBLANES = 8
CONV_HALO = 32
VMEM_LIMIT = 48 * 1024 * 1024


def _cparams(sem):
    return pltpu.CompilerParams(dimension_semantics=sem, vmem_limit_bytes=VMEM_LIMIT)


def _silu(x):
    return x * jax.nn.sigmoid(x)


def _mod_kernel(c_ref, w_ref, b_ref, o_ref):
    c = c_ref[...]
    o_ref[...] = jnp.dot(_silu(c), w_ref[...], preferred_element_type=F32,
                         precision=lax.Precision.HIGHEST) + b_ref[...]


def _modulation(c, w_ada, b_ada):
    B, D = c.shape
    n_out = w_ada.shape[1]
    rows = 8
    c_pad = jnp.pad(c, ((0, rows - B), (0, 0)))
    bn = 1024
    out = pl.pallas_call(
        _mod_kernel,
        grid=(n_out // bn,),
        in_specs=[pl.BlockSpec((rows, D), lambda j: (0, 0)),
                  pl.BlockSpec((D, bn), lambda j: (0, j)),
                  pl.BlockSpec((1, bn), lambda j: (0, j))],
        out_specs=pl.BlockSpec((rows, bn), lambda j: (0, j)),
        out_shape=jax.ShapeDtypeStruct((rows, n_out), F32),
        compiler_params=_cparams(("parallel",)),
        name="adaln_mod",
    )(c_pad, w_ada, b_ada.reshape(1, n_out))
    return out[:B].reshape(B, 6, D)


def _inproj_kernel(x_ref, mod_ref, g_ref, pos_ref, wv_ref, wqk_ref, bglu_ref, gqk_ref, invf_ref,
                   q_ref, kt_ref, v_ref, u_ref, *, aw, cw, dh, rot):
    T = x_ref.shape[0]
    half = rot // 2
    x = x_ref[...]
    ms = jnp.mean(x * x, axis=-1, keepdims=True)
    sh = mod_ref[0:1, :]
    sc = mod_ref[1:2, :]
    h = x * lax.rsqrt(ms + EPS) * g_ref[...] * (1.0 + sc) + sh
    hb = h.astype(BF16)

    pv = jnp.dot(hb, wv_ref[...], preferred_element_type=F32)
    v_ref[...] = pv[:, :aw].astype(BF16)
    a = pv[:, aw:aw + cw] + bglu_ref[:, :cw]
    gate = pv[:, aw + cw:] + bglu_ref[:, cw:]
    u_ref[...] = (a * jax.nn.sigmoid(gate)).astype(BF16)

    qkt = lax.dot_general(wqk_ref[...], hb, (((1,), (1,)), ((), ())), preferred_element_type=F32)
    nch = 2 * aw // dh
    for c in range(T // LANES):
        sl = slice(c * LANES, (c + 1) * LANES)
        s3 = qkt[:, sl].reshape(nch, dh, LANES)
        ssq = jnp.mean(s3 * s3, axis=1, keepdims=True)
        y = s3 * lax.rsqrt(ssq + EPS) * gqk_ref[...].reshape(nch, dh, LANES)
        ang = invf_ref[...] * pos_ref[:, sl]
        cs = jnp.cos(ang)
        sn = jnp.sin(ang)
        t1 = y[:, 0:half, :]
        t2 = y[:, half:rot, :]
        y = jnp.concatenate([t1 * cs - t2 * sn, t2 * cs + t1 * sn, y[:, rot:, :]], axis=1)
        y2 = y.reshape(2 * aw, LANES)
        kt_ref[:, sl] = y2[aw:].astype(BF16)
        q_ref[sl, :] = y2[:aw].T.astype(BF16)


def _inproj(x2, mod, g_mix, pos_row, w_vglu, w_qk_t, b_glu, gqk_tab, invf_tab, *, B, S, aw, cw, dh, rot, T):
    N, D = x2.shape
    nS = S // T
    kern = functools.partial(_inproj_kernel, aw=aw, cw=cw, dh=dh, rot=rot)
    return pl.pallas_call(
        kern,
        grid=(N // T,),
        in_specs=[pl.BlockSpec((T, D), lambda i: (i, 0)),
                  pl.BlockSpec((None, 6, D), lambda i: (i // nS, 0, 0)),
                  pl.BlockSpec((1, D), lambda i: (0, 0)),
                  pl.BlockSpec((1, T), lambda i: (0, i)),
                  pl.BlockSpec(w_vglu.shape, lambda i: (0, 0)),
                  pl.BlockSpec(w_qk_t.shape, lambda i: (0, 0)),
                  pl.BlockSpec((1, 2 * cw), lambda i: (0, 0)),
                  pl.BlockSpec(gqk_tab.shape, lambda i: (0, 0)),
                  pl.BlockSpec(invf_tab.shape, lambda i: (0, 0))],
        out_specs=[pl.BlockSpec((T, aw), lambda i: (i, 0)),
                   pl.BlockSpec((None, aw, T), lambda i: (i // nS, 0, i % nS)),
                   pl.BlockSpec((T, aw), lambda i: (i, 0)),
                   pl.BlockSpec((T, cw), lambda i: (i, 0))],
        out_shape=[jax.ShapeDtypeStruct((N, aw), BF16),
                   jax.ShapeDtypeStruct((B, aw, S), BF16),
                   jax.ShapeDtypeStruct((N, aw), BF16),
                   jax.ShapeDtypeStruct((N, cw), BF16)],
        compiler_params=_cparams(("parallel",)),
        name="inproj_qknorm_rope_glu",
    )(x2, mod, g_mix, pos_row, w_vglu, w_qk_t, b_glu, gqk_tab, invf_tab)


def _attn_kernel(q_ref, kt_ref, v_ref, lq1_ref, lk1_ref, lq2_ref, lk2_ref, sg_ref, o_ref, acc_ref,
                 s0_ref, s1_ref, *, tq, tk, dh, lambda_init):
    i = pl.program_id(2)
    hd = 2 * dh
    q = q_ref[...]
    lane = lax.broadcasted_iota(jnp.int32, q.shape, 1)
    zero = jnp.zeros_like(q)
    q2 = jnp.concatenate([jnp.where(lane < dh, q, zero), jnp.where(lane >= dh, q, zero)], axis=0)

    acc_ref[...] = jnp.zeros_like(acc_ref)
    n_sub = tk // tq
    ones_col = {w: jnp.ones((w, hd), BF16) for w in {tk} | {(r + 1) * tq for r in range(n_sub)}}

    def scores(t, s_ref):
        start = pl.multiple_of(t * tk, tk)
        s_ref[...] = jnp.dot(q2, kt_ref[:, pl.ds(start, tk)], preferred_element_type=F32)

    def softmax_pv(t, s_ref, m, width=tk, diagonal=False):
        start = pl.multiple_of(t * tk, tk)
        vt = jnp.concatenate([v_ref[pl.ds(start, width), :], ones_col[width]], axis=1)
        s = s_ref[:, :width]
        if diagonal:
            row = lax.broadcasted_iota(jnp.int32, (2 * tq, tq), 0)
            col = lax.broadcasted_iota(jnp.int32, (2 * tq, tq), 1)
            qrow = jnp.where(row >= tq, row - tq, row)
            tail = jnp.where(col <= qrow, s[:, width - tq:], -jnp.inf)
            s = tail if width == tq else jnp.concatenate([s[:, :width - tq], tail], axis=1)
        m_new = jnp.maximum(m, jnp.max(s, axis=1, keepdims=True))
        alpha = jnp.exp2(m - m_new)
        p = jnp.exp2(s - m_new).astype(BF16)
        acc_ref[...] = alpha * acc_ref[...] + jnp.dot(p, vt, preferred_element_type=F32)
        return m_new

    n_full = (i * tq) // tk
    odd = lax.rem(n_full, 2)
    m0 = jnp.full((2 * tq, 1), -jnp.inf, F32)

    @pl.when(odd == 0)
    def _():
        scores(0, s0_ref)

    @pl.when(odd == 1)
    def _():
        scores(0, s1_ref)

    def peeled(_, m):
        scores(1, s0_ref)
        return softmax_pv(0, s1_ref, m)

    m = lax.fori_loop(0, odd, peeled, m0)

    def pair(jj, m):
        t = odd + 2 * jj
        scores(t + 1, s1_ref)
        m = softmax_pv(t, s0_ref, m)
        scores(t + 2, s0_ref)
        return softmax_pv(t + 1, s1_ref, m)

    m = lax.fori_loop(0, (n_full - odd) // 2, pair, m)

    for r in range(n_sub):
        @pl.when(lax.rem(i, n_sub) == r)
        def _(r=r):
            softmax_pv(n_full, s0_ref, m, width=(r + 1) * tq, diagonal=True)

    lam = (jnp.exp(jnp.sum(lq1_ref[...] * lk1_ref[...], axis=1, keepdims=True))
           - jnp.exp(jnp.sum(lq2_ref[...] * lk2_ref[...], axis=1, keepdims=True)) + lambda_init)
    o = (acc_ref[0:tq, 0:hd] / acc_ref[0:tq, hd:]
         - lam * (acc_ref[tq:, 0:hd] / acc_ref[tq:, hd:]))
    ms = jnp.mean(o * o, axis=1, keepdims=True)
    o = o * lax.rsqrt(ms + EPS) * sg_ref[...] * (1.0 - lambda_init)
    o_ref[...] = o.astype(o_ref.dtype)


def _diff_attention(q, kt, v, lq1, lk1, lq2, lk2, subln_g, *, B, S, H, dh, lambda_init, tq, tk):
    aw = H * 2 * dh
    q3 = q.reshape(B, S, aw)
    v3 = v.reshape(B, S, aw)
    hd = 2 * dh
    tk = min(tk, S)
    tq = min(tq, tk)
    kern = functools.partial(_attn_kernel, tq=tq, tk=tk, dh=dh, lambda_init=lambda_init)
    vec = pl.BlockSpec((1, dh), lambda b, h, i: (0, 0))
    out = pl.pallas_call(
        kern,
        grid=(B, H, S // tq),
        in_specs=[pl.BlockSpec((None, tq, hd), lambda b, h, i: (b, i, h)),
                  pl.BlockSpec((None, hd, S), lambda b, h, i: (b, h, 0)),
                  pl.BlockSpec((None, S, hd), lambda b, h, i: (b, 0, h)),
                  vec, vec, vec, vec,
                  pl.BlockSpec((1, hd), lambda b, h, i: (0, 0))],
        out_specs=pl.BlockSpec((None, tq, hd), lambda b, h, i: (b, i, h)),
        out_shape=jax.ShapeDtypeStruct((B, S, aw), BF16),
        scratch_shapes=[pltpu.VMEM((2 * tq, 2 * hd), F32),
                        pltpu.VMEM((2 * tq, tk), F32),
                        pltpu.VMEM((2 * tq, tk), F32)],
        compiler_params=_cparams(("parallel", "parallel", "parallel")),
        name="diff_flash_attention",
    )(q3, kt, v3, lq1, lk1, lq2, lk2, subln_g)
    return out.reshape(B * S, aw)


def _mixout_kernel(attn_ref, ucur_ref, uhalo_ref, x_ref, mod_ref, wdw_ref, bdw_ref, lng_ref, lnb_ref,
                   wo1_ref, wo2_ref, gffn_ref, wrh_ref, wrl_ref, br_ref,
                   x1_ref, h2_ref, route_ref, route_t_ref, cnt_ref, ubuf_ref, conv_ref,
                   *, nS, conv_k, n_exp, n_grp):
    T = x_ref.shape[0]
    cw = ucur_ref.shape[1]
    i = pl.program_id(0)

    @pl.when(i == 0)
    def _():
        cnt_ref[...] = jnp.zeros_like(cnt_ref)

    first = (i % nS) == 0
    halo = uhalo_ref[...].astype(F32)
    ubuf_ref[0, 0:CONV_HALO, :] = jnp.where(first, jnp.zeros_like(halo), halo)
    ubuf_ref[0, CONV_HALO:, :] = ucur_ref[...].astype(F32)
    span = T + CONV_HALO - SUBLANES
    for b in range(1, SUBLANES):
        ubuf_ref[b, 0:span, :] = ubuf_ref[0, b:b + span, :]

    off = CONV_HALO - (conv_k - 1)
    rows = 32
    for r0 in range(0, T, rows):
        acc = jnp.zeros((rows // SUBLANES, SUBLANES, cw), F32)
        for j in range(conv_k):
            a, b = divmod(j + off, SUBLANES)
            lo_r = r0 + a * SUBLANES
            acc = acc + (wdw_ref[j * SUBLANES:(j + 1) * SUBLANES, :]
                         * ubuf_ref[b, lo_r:lo_r + rows, :].reshape(rows // SUBLANES, SUBLANES, cw))
        y = acc.reshape(rows, cw) + bdw_ref[...]
        mu = jnp.mean(y, axis=1, keepdims=True)
        d = y - mu
        var = jnp.mean(d * d, axis=1, keepdims=True)
        z = d * lax.rsqrt(var + EPS) * lng_ref[...] + lnb_ref[...]
        conv_ref[r0:r0 + rows, :] = _silu(z).astype(BF16)

    yo = (jnp.dot(attn_ref[...], wo1_ref[...], preferred_element_type=F32)
          + jnp.dot(conv_ref[...], wo2_ref[...], preferred_element_type=F32))
    x1 = x_ref[...] + mod_ref[2:3, :] * yo
    x1_ref[...] = x1
    ms = jnp.mean(x1 * x1, axis=1, keepdims=True)
    h2 = x1 * lax.rsqrt(ms + EPS) * gffn_ref[...] * (1.0 + mod_ref[4:5, :]) + mod_ref[3:4, :]
    h2_ref[...] = h2
    hi = h2.astype(BF16)
    lo = (h2 - hi.astype(F32)).astype(BF16)

    logits = (jnp.dot(hi, wrh_ref[...], preferred_element_type=F32)
              + jnp.dot(lo, wrh_ref[...], preferred_element_type=F32)
              + jnp.dot(hi, wrl_ref[...], preferred_element_type=F32)) + br_ref[...]
    lane_i = lax.broadcasted_iota(jnp.int32, logits.shape, 1)
    lane = lane_i.astype(F32)
    big = jnp.float32(1e9)
    ninf = jnp.float32(-jnp.inf)
    is_g = (lane_i >= n_exp) & (lane_i < n_exp + n_grp)
    gl = jnp.where(is_g, logits, ninf)
    gmax = jnp.max(gl, axis=1, keepdims=True)
    gsum = jnp.sum(jnp.where(is_g, jnp.exp(gl - gmax), 0.0), axis=1, keepdims=True)
    g_p = 1.0 / gsum
    gidx = jnp.min(jnp.where(gl == gmax, lane, big), axis=1, keepdims=True) - n_exp
    epg = n_exp // n_grp
    lo_l = gidx * epg
    in_grp = (lane >= lo_l) & (lane < lo_l + epg)
    el = jnp.where(in_grp, logits, ninf)
    m1 = jnp.max(el, axis=1, keepdims=True)
    i1 = jnp.min(jnp.where(el == m1, lane, big), axis=1, keepdims=True)
    el2 = jnp.where(lane == i1, ninf, el)
    m2 = jnp.max(el2, axis=1, keepdims=True)
    i2 = jnp.min(jnp.where(el2 == m2, lane, big), axis=1, keepdims=True)
    e2 = jnp.exp(m2 - m1)
    p1 = 1.0 / (1.0 + e2)
    p2 = e2 / (1.0 + e2)

    sel = (lane == i1) | (lane == i2)
    rr = lax.broadcasted_iota(jnp.int32, (T, T), 0)
    cc = lax.broadcasted_iota(jnp.int32, (T, T), 1)
    before = jnp.dot((rr > cc).astype(BF16), sel.astype(BF16), preferred_element_type=F32) + cnt_ref[...]
    r1 = jnp.sum(jnp.where(lane == i1, before, 0.0), axis=1, keepdims=True)
    r2 = jnp.sum(jnp.where(lane == i2, before, 0.0), axis=1, keepdims=True)
    cnt_ref[...] += jnp.sum(sel.astype(F32), axis=0, keepdims=True)

    route = jnp.zeros_like(logits)
    for k, val in enumerate((i1, i2, g_p * p1, g_p * p2, r1, r2)):
        route = jnp.where(lane_i == k, val, route)
    route_ref[...] = route
    route_t_ref[...] = route.T[:SUBLANES]


def _mixout(attn, u, x2, mod, w_dw, b_dw, ln_g, ln_b, wo1, wo2, g_ffn, wr_hi, wr_lo, b_r,
            *, S, T, n_exp, n_grp):
    N, D = x2.shape
    aw = attn.shape[1]
    cw = u.shape[1]
    nS = S // T
    conv_k = w_dw.shape[0]
    w_dw = jnp.repeat(w_dw, SUBLANES, axis=0)
    hb = T // CONV_HALO
    kern = functools.partial(_mixout_kernel, nS=nS, conv_k=conv_k, n_exp=n_exp, n_grp=n_grp)
    full = lambda a: pl.BlockSpec(a.shape, lambda i: (0, 0))
    return pl.pallas_call(
        kern,
        grid=(N // T,),
        in_specs=[pl.BlockSpec((T, aw), lambda i: (i, 0)),
                  pl.BlockSpec((T, cw), lambda i: (i, 0)),
                  pl.BlockSpec((CONV_HALO, cw), lambda i: (jnp.maximum(i * hb - 1, 0), 0)),
                  pl.BlockSpec((T, D), lambda i: (i, 0)),
                  pl.BlockSpec((None, 6, D), lambda i: (i // nS, 0, 0)),
                  full(w_dw), full(b_dw), full(ln_g), full(ln_b), full(wo1), full(wo2), full(g_ffn),
                  full(wr_hi), full(wr_lo), full(b_r)],
        out_specs=[pl.BlockSpec((T, D), lambda i: (i, 0)),
                   pl.BlockSpec((T, D), lambda i: (i, 0)),
                   pl.BlockSpec((T, LANES), lambda i: (i, 0)),
                   pl.BlockSpec((SUBLANES, T), lambda i: (0, i)),
                   pl.BlockSpec((1, LANES), lambda i: (0, 0))],
        out_shape=[jax.ShapeDtypeStruct((N, D), F32),
                   jax.ShapeDtypeStruct((N, D), F32),
                   jax.ShapeDtypeStruct((N, LANES), F32),
                   jax.ShapeDtypeStruct((SUBLANES, N), F32),
                   jax.ShapeDtypeStruct((1, LANES), F32)],
        scratch_shapes=[pltpu.VMEM((SUBLANES, CONV_HALO + T, cw), F32),
                        pltpu.VMEM((T, cw), BF16)],
        compiler_params=_cparams(("arbitrary",)),
        name="conv_outproj_router",
    )(attn, u, u, x2, mod, w_dw, b_dw, ln_g, ln_b, wo1, wo2, g_ffn, wr_hi, wr_lo, b_r)


def _row_of(ref, r):
    if isinstance(r, int):
        return ref.at[r // SUBLANES, pl.ds(r % SUBLANES, 1)]
    return ref.at[lax.shift_right_logical(r, 3), pl.ds(r & (SUBLANES - 1), 1)]


def _scatter_kernel(dest_ref, h2_ref, xs_hbm, sem):
    groups = h2_ref.shape[0]
    tokens = groups * SUBLANES

    def issue(g, carry):
        for j in range(SUBLANES):
            for k in range(TOP_K_INNER):
                d = dest_ref[0, k * tokens + g * SUBLANES + j]
                pltpu.make_async_copy(h2_ref.at[g, pl.ds(j, 1)], _row_of(xs_hbm, d), sem).start(priority=k % 2)
        return carry

    lax.fori_loop(0, groups, issue, 0)
    for k in range(TOP_K_INNER):
        pltpu.make_async_copy(h2_ref, xs_hbm.at[pl.ds(0, groups)], sem).wait()


def _dest_table(dest, tokens):
    tab = jnp.concatenate([d.reshape(-1, tokens) for d in dest], axis=1)
    return tab.reshape(tab.shape[0], 1, tab.shape[1])


def _scatter_rows(h2, dest, *, tokens):
    N, C = h2.shape
    steps = N // tokens
    dest3 = _dest_table(dest, tokens)
    rows = N * TOP_K_INNER
    xs = pl.pallas_call(
        _scatter_kernel,
        grid=(steps,),
        in_specs=[pl.BlockSpec((None, 1, TOP_K_INNER * tokens), lambda s: (s, 0, 0), memory_space=pltpu.SMEM),
                  pl.BlockSpec((tokens // SUBLANES, SUBLANES, C), lambda s: (s, 0, 0))],
        out_specs=pl.BlockSpec(memory_space=pl.ANY),
        out_shape=jax.ShapeDtypeStruct((rows // SUBLANES, SUBLANES, C), h2.dtype),
        scratch_shapes=[pltpu.SemaphoreType.DMA],
        compiler_params=_cparams(("arbitrary",)),
        name="moe_scatter_rows",
    )(dest3, h2.reshape(N // SUBLANES, SUBLANES, C))
    return xs.reshape(rows, C)


XS_RING = 3


def _experts_kernel(vt_ref, ve_ref, vlo_ref, vhi_ref, xs_hbm, wg_ref, wu_ref, wd_ref, ys_ref, xbuf, sem,
                    *, n_tiles):
    v = pl.program_id(0)
    lo = vlo_ref[v]
    hi = vhi_ref[v]
    t = vt_ref[v]
    tm = xbuf.shape[1]

    def tile_copy(tile):
        slot = lax.rem(tile, XS_RING)
        return pltpu.make_async_copy(xs_hbm.at[pl.ds(pl.multiple_of(tile * tm, tm), tm)], xbuf.at[slot],
                                     sem.at[slot])

    @pl.when(v == 0)
    def _():
        for tile in range(min(XS_RING - 1, n_tiles)):
            tile_copy(tile).start()

    @pl.when((lo == 0) & (hi > lo))
    def _():
        @pl.when(t + (XS_RING - 1) < n_tiles)
        def _():
            tile_copy(t + (XS_RING - 1)).start()

        tile_copy(t).wait()

    @pl.when(hi > lo)
    def _():
        x = xbuf[lax.rem(t, XS_RING)].astype(BF16)
        g = jnp.dot(x, wg_ref[...].astype(BF16), preferred_element_type=F32)
        u = jnp.dot(x, wu_ref[...].astype(BF16), preferred_element_type=F32)
        hid = (_silu(g) * u).astype(BF16)
        y = jnp.dot(hid, wd_ref[...].astype(BF16), preferred_element_type=F32)

        @pl.when(lo == 0)
        def _():
            ys_ref[...] = y

        @pl.when(lo > 0)
        def _():
            row = lax.broadcasted_iota(jnp.int32, y.shape, 0)
            ys_ref[...] = jnp.where((row >= lo) & (row < hi), y, ys_ref[...])


def _experts(xs, w_gate, w_up, w_down, visits, *, tm):
    R, C = xs.shape
    E, D, ff = w_gate.shape
    vt, ve, vlo, vhi = visits
    grid_spec = pltpu.PrefetchScalarGridSpec(
        num_scalar_prefetch=4,
        grid=(vt.shape[0],),
        in_specs=[pl.BlockSpec(memory_space=pl.ANY),
                  pl.BlockSpec((None, D, ff), lambda v, vt, ve, vlo, vhi: (ve[v], 0, 0)),
                  pl.BlockSpec((None, D, ff), lambda v, vt, ve, vlo, vhi: (ve[v], 0, 0)),
                  pl.BlockSpec((None, ff, D), lambda v, vt, ve, vlo, vhi: (ve[v], 0, 0))],
        out_specs=pl.BlockSpec((tm, C), lambda v, vt, ve, vlo, vhi: (vt[v], 0)),
        scratch_shapes=[pltpu.VMEM((XS_RING, tm, C), F32),
                        pltpu.SemaphoreType.DMA((XS_RING,))],
    )
    return pl.pallas_call(
        functools.partial(_experts_kernel, n_tiles=R // tm),
        grid_spec=grid_spec,
        out_shape=jax.ShapeDtypeStruct((R, C), F32),
        compiler_params=_cparams(("arbitrary",)),
        name="moe_grouped_experts",
    )(vt, ve, vlo, vhi, xs, w_gate, w_up, w_down)


def _visit_tables(off, cnt, n_rows, tm):
    n_tiles = n_rows // tm
    n_exp = off.shape[0]
    n_visits = n_tiles + n_exp - 1
    tile_starts = jnp.arange(n_tiles, dtype=jnp.int32) * tm
    seg_starts = jnp.where((cnt > 0) & (off % tm != 0), off, n_rows)
    starts = jnp.sort(jnp.concatenate([tile_starts, seg_starts]))
    lo_abs = starts[:n_visits]
    hi_abs = starts[1:n_visits + 1]
    valid = lo_abs < n_rows
    tile = jnp.where(valid, lo_abs // tm, n_tiles - 1)
    ends = off + cnt
    probe = jnp.where(valid, lo_abs, n_rows - 1)
    expert = jnp.sum(ends[None, :] <= probe[:, None], axis=1).astype(jnp.int32)
    row_lo = jnp.where(valid, lo_abs - tile * tm, 0)
    row_hi = jnp.where(valid, hi_abs - tile * tm, 0)
    return tile, expert, row_lo, row_hi


def _combine_kernel(dcur_ref, dnxt_ref, ys_hbm, route_ref, x1_ref, mod_ref, o_ref, ybuf, sem):
    T = x1_ref.shape[0]
    i = pl.program_id(0)
    slot = lax.rem(i, 2)

    def gather(d_ref, sl):
        def body(g, carry):
            for j in range(SUBLANES):
                for k in range(TOP_K_INNER):
                    d = d_ref[0, k * T + g * SUBLANES + j]
                    pltpu.make_async_copy(_row_of(ys_hbm, d), ybuf.at[sl, k, g, pl.ds(j, 1)],
                                          sem.at[sl]).start(priority=k % 2)
            return carry
        lax.fori_loop(0, T // SUBLANES, body, 0)

    @pl.when(i == 0)
    def _():
        gather(dcur_ref, 0)

    @pl.when(i + 1 < pl.num_programs(0))
    def _():
        gather(dnxt_ref, 1 - slot)

    for k in range(TOP_K_INNER):
        pltpu.make_async_copy(ys_hbm.at[pl.ds(0, T // SUBLANES)], ybuf.at[slot, k], sem.at[slot]).wait()

    route = route_ref[...]
    moe = jnp.zeros(x1_ref.shape, F32)
    for k in range(TOP_K_INNER):
        moe = moe + route[:, TOP_K_INNER + k:TOP_K_INNER + k + 1] * ybuf[slot, k].reshape(x1_ref.shape)
    o_ref[...] = x1_ref[...] + mod_ref[5:6, :] * moe


def _combine(ys, dest, route, x1, mod, *, S, T):
    N, D = x1.shape
    C = ys.shape[1]
    steps = N // T
    nS = S // T
    dest3 = _dest_table(dest, T)
    dspec = lambda f: pl.BlockSpec((None, 1, TOP_K_INNER * T), f, memory_space=pltpu.SMEM)
    return pl.pallas_call(
        _combine_kernel,
        grid=(steps,),
        in_specs=[dspec(lambda i: (i, 0, 0)),
                  dspec(lambda i: (jnp.minimum(i + 1, steps - 1), 0, 0)),
                  pl.BlockSpec(memory_space=pl.ANY),
                  pl.BlockSpec((T, LANES), lambda i: (i, 0)),
                  pl.BlockSpec((T, D), lambda i: (i, 0)),
                  pl.BlockSpec((None, 6, D), lambda i: (i // nS, 0, 0))],
        out_specs=pl.BlockSpec((T, D), lambda i: (i, 0)),
        out_shape=jax.ShapeDtypeStruct((N, D), F32),
        scratch_shapes=[pltpu.VMEM((2, TOP_K_INNER, T // SUBLANES, SUBLANES, C), F32),
                        pltpu.SemaphoreType.DMA((2,))],
        compiler_params=_cparams(("arbitrary",)),
        name="moe_gather_combine",
    )(dest3, dest3, ys.reshape(ys.shape[0] // SUBLANES, SUBLANES, C), route, x1, mod)


def _layer(x2, mod, pos_row, l, B, S, g_mix, w_in, q_norm_g, k_norm_g, lambda_q1, lambda_k1, lambda_q2,
           lambda_k2, subln_g, b_glu, w_dw, b_dw, conv_ln_g, conv_ln_b, w_out, g_ffn, w_group, b_group,
           w_router, b_router, w_gate, w_up, w_down):
    N, D = x2.shape
    dh = q_norm_g.shape[0]
    H = N_DIFF_HEADS
    aw = H * 2 * dh
    cw = w_dw.shape[1]
    rot = dh // 4
    n_grp = w_group.shape[1]
    n_exp = w_router.shape[1]
    lambda_init = 0.8 - 0.6 * math.exp(-0.3 * l)

    w_qk_t = w_in[:, :2 * aw].T.astype(BF16)
    w_vglu = w_in[:, 2 * aw:].astype(BF16)
    scale = dh ** -0.5 * math.log2(math.e)
    gq = jnp.tile(q_norm_g * scale, aw // dh)
    gk = jnp.tile(k_norm_g, aw // dh)
    gqk_tab = jnp.broadcast_to(jnp.concatenate([gq, gk])[:, None], (2 * aw, LANES))
    inv_freq = ROPE_THETA ** (-jnp.arange(0, rot, 2, dtype=F32) / rot)
    invf_tab = jnp.broadcast_to(inv_freq[:, None], (rot // 2, LANES))

    q, kt, v, u = _inproj(x2, mod, g_mix.reshape(1, D), pos_row, w_vglu, w_qk_t, b_glu.reshape(1, 2 * cw),
                          gqk_tab, invf_tab, B=B, S=S, aw=aw, cw=cw, dh=dh, rot=rot, T=512)

    attn = _diff_attention(q, kt, v, lambda_q1.reshape(1, dh), lambda_k1.reshape(1, dh),
                           lambda_q2.reshape(1, dh), lambda_k2.reshape(1, dh), subln_g.reshape(1, 2 * dh),
                           B=B, S=S, H=H, dh=dh, lambda_init=lambda_init, tq=1024, tk=1024)

    w_r = jnp.zeros((D, LANES), F32).at[:, :n_exp].set(w_router).at[:, n_exp:n_exp + n_grp].set(w_group)
    b_r = jnp.zeros((1, LANES), F32).at[0, :n_exp].set(b_router).at[0, n_exp:n_exp + n_grp].set(b_group)
    wr_hi = w_r.astype(BF16)
    wr_lo = (w_r - wr_hi.astype(F32)).astype(BF16)
    wo = w_out.astype(BF16)
    x1, h2, route, route_t, counts = _mixout(attn, u, x2, mod, w_dw, b_dw.reshape(1, cw), conv_ln_g.reshape(1, cw),
                                     conv_ln_b.reshape(1, cw), wo[:aw], wo[aw:], g_ffn.reshape(1, D),
                                     wr_hi, wr_lo, b_r, S=S, T=512, n_exp=n_exp, n_grp=n_grp)

    cnt = counts[0, :n_exp].astype(jnp.int32)
    off = jnp.cumsum(cnt) - cnt
    experts = jnp.arange(n_exp, dtype=jnp.int32)[:, None]
    dest = [jnp.sum(jnp.where(route_t[k].astype(jnp.int32)[None, :] == experts, off[:, None], 0), axis=0)
            + route_t[4 + k].astype(jnp.int32) for k in range(TOP_K_INNER)]
    tm = 256
    visits = _visit_tables(off, cnt, N * TOP_K_INNER, tm)

    xs = _scatter_rows(h2, dest, tokens=min(1024, N))
    ys = _experts(xs, w_gate, w_up, w_down, visits, tm=tm)
    return _combine(ys, dest, route, x1, mod, S=S, T=256)


def kernel(x, c, positions, w_ada, b_ada, g_mix, w_in, q_norm_g, k_norm_g, lambda_q1, lambda_k1, lambda_q2,
           lambda_k2, subln_g, b_glu, w_dw, b_dw, conv_ln_g, conv_ln_b, w_out, g_ffn, w_group, b_group,
           w_router, b_router, w_gate, w_up, w_down):
    B, S, D = x.shape
    depth = w_ada.shape[0]
    x2 = x.reshape(B * S, D)
    pos_row = positions.astype(F32).reshape(1, B * S)
    for l in range(depth):
        mod = _modulation(c, w_ada[l], b_ada[l])
        x2 = _layer(x2, mod, pos_row, l, B, S, g_mix[l], w_in[l], q_norm_g[l], k_norm_g[l], lambda_q1[l],
                    lambda_k1[l], lambda_q2[l], lambda_k2[l], subln_g[l], b_glu[l], w_dw[l], b_dw[l],
                    conv_ln_g[l], conv_ln_b[l], w_out[l], g_ffn[l], w_group[l], b_group[l], w_router[l],
                    b_router[l], w_gate[l], w_up[l], w_down[l])
    return x2.reshape(B, S, D)
```

```python
import functools
import math

import numpy as np
import jax
import jax.numpy as jnp
from jax import lax
from jax.experimental import pallas as pl
from jax.experimental.pallas import tpu as pltpu

F32 = jnp.float32
BF16 = jnp.bfloat16

EPS = 1e-6
ROPE_THETA = 500000.0
N_DIFF_HEADS = 4
TOP_K_INNER = 2

LANES = 128
SUBLANES = 8
CONV_HALO = 32
VMEM_LIMIT = 48 * 1024 * 1024


def _cparams(sem):
    return pltpu.CompilerParams(dimension_semantics=sem, vmem_limit_bytes=VMEM_LIMIT)


def _silu(x):
    return x * jax.nn.sigmoid(x)


def _mod_kernel(c_ref, w_ref, b_ref, o_ref):
    c = c_ref[...]
    o_ref[...] = jnp.dot(_silu(c), w_ref[...], preferred_element_type=F32,
                         precision=lax.Precision.HIGHEST) + b_ref[...]


def _modulation(c, w_ada, b_ada):
    B, D = c.shape
    n_out = w_ada.shape[1]
    rows = 8
    c_pad = jnp.pad(c, ((0, rows - B), (0, 0)))
    bn = 1024
    out = pl.pallas_call(
        _mod_kernel,
        grid=(n_out // bn,),
        in_specs=[pl.BlockSpec((rows, D), lambda j: (0, 0)),
                  pl.BlockSpec((D, bn), lambda j: (0, j)),
                  pl.BlockSpec((1, bn), lambda j: (0, j))],
        out_specs=pl.BlockSpec((rows, bn), lambda j: (0, j)),
        out_shape=jax.ShapeDtypeStruct((rows, n_out), F32),
        compiler_params=_cparams(("parallel",)),
        name="adaln_mod",
    )(c_pad, w_ada, b_ada.reshape(1, n_out))
    return out[:B].reshape(B, 6, D)


def _inproj_kernel(x_ref, mod_ref, g_ref, pos_ref, wv_ref, wqk_ref, bglu_ref, gqk_ref, invf_ref,
                   q_ref, kt_ref, v_ref, u_ref, *, aw, cw, dh, rot):
    T = x_ref.shape[0]
    half = rot // 2
    x = x_ref[...]
    ms = jnp.mean(x * x, axis=-1, keepdims=True)
    sh = mod_ref[0:1, :]
    sc = mod_ref[1:2, :]
    h = x * lax.rsqrt(ms + EPS) * g_ref[...] * (1.0 + sc) + sh
    hb = h.astype(BF16)

    pv = jnp.dot(hb, wv_ref[...], preferred_element_type=F32)
    v_ref[...] = pv[:, :aw].astype(BF16)
    a = pv[:, aw:aw + cw] + bglu_ref[:, :cw]
    gate = pv[:, aw + cw:] + bglu_ref[:, cw:]
    u_ref[...] = (a * jax.nn.sigmoid(gate)).astype(BF16)

    qkt = lax.dot_general(wqk_ref[...], hb, (((1,), (1,)), ((), ())), preferred_element_type=F32)
    nch = 2 * aw // dh
    for c in range(T // LANES):
        sl = slice(c * LANES, (c + 1) * LANES)
        s3 = qkt[:, sl].reshape(nch, dh, LANES)
        ssq = jnp.mean(s3 * s3, axis=1, keepdims=True)
        y = s3 * lax.rsqrt(ssq + EPS) * gqk_ref[...].reshape(nch, dh, LANES)
        ang = invf_ref[...] * pos_ref[:, sl]
        cs = jnp.cos(ang)
        sn = jnp.sin(ang)
        t1 = y[:, 0:half, :]
        t2 = y[:, half:rot, :]
        y = jnp.concatenate([t1 * cs - t2 * sn, t2 * cs + t1 * sn, y[:, rot:, :]], axis=1)
        y2 = y.reshape(2 * aw, LANES)
        kt_ref[:, sl] = y2[aw:].astype(BF16)
        q_ref[sl, :] = y2[:aw].T.astype(BF16)


def _inproj(x2, mod, g_mix, pos_row, w_vglu, w_qk_t, b_glu, gqk_tab, invf_tab, *, B, S, aw, cw, dh, rot, T):
    N, D = x2.shape
    nS = S // T
    kern = functools.partial(_inproj_kernel, aw=aw, cw=cw, dh=dh, rot=rot)
    return pl.pallas_call(
        kern,
        grid=(N // T,),
        in_specs=[pl.BlockSpec((T, D), lambda i: (i, 0)),
                  pl.BlockSpec((None, 6, D), lambda i: (i // nS, 0, 0)),
                  pl.BlockSpec((1, D), lambda i: (0, 0)),
                  pl.BlockSpec((1, T), lambda i: (0, i)),
                  pl.BlockSpec(w_vglu.shape, lambda i: (0, 0)),
                  pl.BlockSpec(w_qk_t.shape, lambda i: (0, 0)),
                  pl.BlockSpec((1, 2 * cw), lambda i: (0, 0)),
                  pl.BlockSpec(gqk_tab.shape, lambda i: (0, 0)),
                  pl.BlockSpec(invf_tab.shape, lambda i: (0, 0))],
        out_specs=[pl.BlockSpec((T, aw), lambda i: (i, 0)),
                   pl.BlockSpec((None, aw, T), lambda i: (i // nS, 0, i % nS)),
                   pl.BlockSpec((T, aw), lambda i: (i, 0)),
                   pl.BlockSpec((T, cw), lambda i: (i, 0))],
        out_shape=[jax.ShapeDtypeStruct((N, aw), BF16),
                   jax.ShapeDtypeStruct((B, aw, S), BF16),
                   jax.ShapeDtypeStruct((N, aw), BF16),
                   jax.ShapeDtypeStruct((N, cw), BF16)],
        compiler_params=_cparams(("parallel",)),
        name="inproj_qknorm_rope_glu",
    )(x2, mod, g_mix, pos_row, w_vglu, w_qk_t, b_glu, gqk_tab, invf_tab)


def _attn_kernel(q_ref, kt_ref, v_ref, lq1_ref, lk1_ref, lq2_ref, lk2_ref, sg_ref, o_ref, acc_ref,
                 s0_ref, s1_ref, *, tq, tk, dh, lambda_init):
    i = pl.program_id(2)
    hd = 2 * dh
    q = q_ref[...]
    lane = lax.broadcasted_iota(jnp.int32, q.shape, 1)
    zero = jnp.zeros_like(q)
    q2 = jnp.concatenate([jnp.where(lane < dh, q, zero), jnp.where(lane >= dh, q, zero)], axis=0)

    acc_ref[...] = jnp.zeros_like(acc_ref)
    n_sub = tk // tq
    ones_col = {w: jnp.ones((w, hd), BF16) for w in {tk} | {(r + 1) * tq for r in range(n_sub)}}

    def scores(t, s_ref):
        start = pl.multiple_of(t * tk, tk)
        s_ref[...] = jnp.dot(q2, kt_ref[:, pl.ds(start, tk)], preferred_element_type=F32)

    def softmax_pv(t, s_ref, m, width=tk, diagonal=False):
        start = pl.multiple_of(t * tk, tk)
        vt = jnp.concatenate([v_ref[pl.ds(start, width), :], ones_col[width]], axis=1)
        s = s_ref[:, :width]
        if diagonal:
            row = lax.broadcasted_iota(jnp.int32, (2 * tq, tq), 0)
            col = lax.broadcasted_iota(jnp.int32, (2 * tq, tq), 1)
            qrow = jnp.where(row >= tq, row - tq, row)
            tail = jnp.where(col <= qrow, s[:, width - tq:], -jnp.inf)
            s = tail if width == tq else jnp.concatenate([s[:, :width - tq], tail], axis=1)
        m_new = jnp.maximum(m, jnp.max(s, axis=1, keepdims=True))
        alpha = jnp.exp2(m - m_new)
        p = jnp.exp2(s - m_new).astype(BF16)
        acc_ref[...] = alpha * acc_ref[...] + jnp.dot(p, vt, preferred_element_type=F32)
        return m_new

    n_full = (i * tq) // tk
    odd = lax.rem(n_full, 2)
    m0 = jnp.full((2 * tq, 1), -jnp.inf, F32)

    @pl.when(odd == 0)
    def _():
        scores(0, s0_ref)

    @pl.when(odd == 1)
    def _():
        scores(0, s1_ref)

    def peeled(_, m):
        scores(1, s0_ref)
        return softmax_pv(0, s1_ref, m)

    m = lax.fori_loop(0, odd, peeled, m0)

    def pair(jj, m):
        t = odd + 2 * jj
        scores(t + 1, s1_ref)
        m = softmax_pv(t, s0_ref, m)
        scores(t + 2, s0_ref)
        return softmax_pv(t + 1, s1_ref, m)

    m = lax.fori_loop(0, (n_full - odd) // 2, pair, m)

    for r in range(n_sub):
        @pl.when(lax.rem(i, n_sub) == r)
        def _(r=r):
            softmax_pv(n_full, s0_ref, m, width=(r + 1) * tq, diagonal=True)

    lam = (jnp.exp(jnp.sum(lq1_ref[...] * lk1_ref[...], axis=1, keepdims=True))
           - jnp.exp(jnp.sum(lq2_ref[...] * lk2_ref[...], axis=1, keepdims=True)) + lambda_init)
    o = (acc_ref[0:tq, 0:hd] / acc_ref[0:tq, hd:]
         - lam * (acc_ref[tq:, 0:hd] / acc_ref[tq:, hd:]))
    ms = jnp.mean(o * o, axis=1, keepdims=True)
    o = o * lax.rsqrt(ms + EPS) * sg_ref[...] * (1.0 - lambda_init)
    o_ref[...] = o.astype(o_ref.dtype)


def _diff_attention(q, kt, v, lq1, lk1, lq2, lk2, subln_g, *, B, S, H, dh, lambda_init, tq, tk):
    aw = H * 2 * dh
    q3 = q.reshape(B, S, aw)
    v3 = v.reshape(B, S, aw)
    hd = 2 * dh
    tk = min(tk, S)
    tq = min(tq, tk)
    kern = functools.partial(_attn_kernel, tq=tq, tk=tk, dh=dh, lambda_init=lambda_init)
    vec = pl.BlockSpec((1, dh), lambda b, h, i: (0, 0))
    out = pl.pallas_call(
        kern,
        grid=(B, H, S // tq),
        in_specs=[pl.BlockSpec((None, tq, hd), lambda b, h, i: (b, i, h)),
                  pl.BlockSpec((None, hd, S), lambda b, h, i: (b, h, 0)),
                  pl.BlockSpec((None, S, hd), lambda b, h, i: (b, 0, h)),
                  vec, vec, vec, vec,
                  pl.BlockSpec((1, hd), lambda b, h, i: (0, 0))],
        out_specs=pl.BlockSpec((None, tq, hd), lambda b, h, i: (b, i, h)),
        out_shape=jax.ShapeDtypeStruct((B, S, aw), BF16),
        scratch_shapes=[pltpu.VMEM((2 * tq, 2 * hd), F32),
                        pltpu.VMEM((2 * tq, tk), F32),
                        pltpu.VMEM((2 * tq, tk), F32)],
        compiler_params=_cparams(("parallel", "parallel", "parallel")),
        name="diff_flash_attention",
    )(q3, kt, v3, lq1, lk1, lq2, lk2, subln_g)
    return out.reshape(B * S, aw)


def _mixout_kernel(attn_ref, ucur_ref, uhalo_ref, x_ref, mod_ref, wdw_ref, bdw_ref, lng_ref, lnb_ref,
                   wo1_ref, wo2_ref, gffn_ref, wrh_ref, wrl_ref, br_ref,
                   x1_ref, h2_ref, route_ref, route_t_ref, cnt_ref, ubuf_ref, conv_ref,
                   *, nS, conv_k, n_exp, n_grp):
    T = x_ref.shape[0]
    cw = ucur_ref.shape[1]
    i = pl.program_id(0)

    @pl.when(i == 0)
    def _():
        cnt_ref[...] = jnp.zeros_like(cnt_ref)

    first = (i % nS) == 0
    halo = uhalo_ref[...].astype(F32)
    ubuf_ref[0, 0:CONV_HALO, :] = jnp.where(first, jnp.zeros_like(halo), halo)
    ubuf_ref[0, CONV_HALO:, :] = ucur_ref[...].astype(F32)
    span = T + CONV_HALO - SUBLANES
    for b in range(1, SUBLANES):
        ubuf_ref[b, 0:span, :] = ubuf_ref[0, b:b + span, :]

    off = CONV_HALO - (conv_k - 1)
    rows = 32
    for r0 in range(0, T, rows):
        acc = jnp.zeros((rows // SUBLANES, SUBLANES, cw), F32)
        for j in range(conv_k):
            a, b = divmod(j + off, SUBLANES)
            lo_r = r0 + a * SUBLANES
            acc = acc + (wdw_ref[j * SUBLANES:(j + 1) * SUBLANES, :]
                         * ubuf_ref[b, lo_r:lo_r + rows, :].reshape(rows // SUBLANES, SUBLANES, cw))
        y = acc.reshape(rows, cw) + bdw_ref[...]
        mu = jnp.mean(y, axis=1, keepdims=True)
        d = y - mu
        var = jnp.mean(d * d, axis=1, keepdims=True)
        z = d * lax.rsqrt(var + EPS) * lng_ref[...] + lnb_ref[...]
        conv_ref[r0:r0 + rows, :] = _silu(z).astype(BF16)

    yo = (jnp.dot(attn_ref[...], wo1_ref[...], preferred_element_type=F32)
          + jnp.dot(conv_ref[...], wo2_ref[...], preferred_element_type=F32))
    x1 = x_ref[...] + mod_ref[2:3, :] * yo
    x1_ref[...] = x1
    ms = jnp.mean(x1 * x1, axis=1, keepdims=True)
    h2 = x1 * lax.rsqrt(ms + EPS) * gffn_ref[...] * (1.0 + mod_ref[4:5, :]) + mod_ref[3:4, :]
    h2_ref[...] = h2
    hi = h2.astype(BF16)
    lo = (h2 - hi.astype(F32)).astype(BF16)

    logits = (jnp.dot(hi, wrh_ref[...], preferred_element_type=F32)
              + jnp.dot(lo, wrh_ref[...], preferred_element_type=F32)
              + jnp.dot(hi, wrl_ref[...], preferred_element_type=F32)) + br_ref[...]
    lane_i = lax.broadcasted_iota(jnp.int32, logits.shape, 1)
    lane = lane_i.astype(F32)
    big = jnp.float32(1e9)
    ninf = jnp.float32(-jnp.inf)
    is_g = (lane_i >= n_exp) & (lane_i < n_exp + n_grp)
    gl = jnp.where(is_g, logits, ninf)
    gmax = jnp.max(gl, axis=1, keepdims=True)
    gsum = jnp.sum(jnp.where(is_g, jnp.exp(gl - gmax), 0.0), axis=1, keepdims=True)
    g_p = 1.0 / gsum
    gidx = jnp.min(jnp.where(gl == gmax, lane, big), axis=1, keepdims=True) - n_exp
    epg = n_exp // n_grp
    lo_l = gidx * epg
    in_grp = (lane >= lo_l) & (lane < lo_l + epg)
    el = jnp.where(in_grp, logits, ninf)
    m1 = jnp.max(el, axis=1, keepdims=True)
    i1 = jnp.min(jnp.where(el == m1, lane, big), axis=1, keepdims=True)
    el2 = jnp.where(lane == i1, ninf, el)
    m2 = jnp.max(el2, axis=1, keepdims=True)
    i2 = jnp.min(jnp.where(el2 == m2, lane, big), axis=1, keepdims=True)
    e2 = jnp.exp(m2 - m1)
    p1 = 1.0 / (1.0 + e2)
    p2 = e2 / (1.0 + e2)

    sel = (lane == i1) | (lane == i2)
    rr = lax.broadcasted_iota(jnp.int32, (T, T), 0)
    cc = lax.broadcasted_iota(jnp.int32, (T, T), 1)
    before = jnp.dot((rr > cc).astype(BF16), sel.astype(BF16), preferred_element_type=F32) + cnt_ref[...]
    r1 = jnp.sum(jnp.where(lane == i1, before, 0.0), axis=1, keepdims=True)
    r2 = jnp.sum(jnp.where(lane == i2, before, 0.0), axis=1, keepdims=True)
    cnt_ref[...] += jnp.sum(sel.astype(F32), axis=0, keepdims=True)

    route = jnp.zeros_like(logits)
    for k, val in enumerate((i1, i2, g_p * p1, g_p * p2, r1, r2)):
        route = jnp.where(lane_i == k, val, route)
    route_ref[...] = route
    route_t_ref[...] = route.T[:SUBLANES]


def _mixout(attn, u, x2, mod, w_dw, b_dw, ln_g, ln_b, wo1, wo2, g_ffn, wr_hi, wr_lo, b_r,
            *, S, T, n_exp, n_grp):
    N, D = x2.shape
    aw = attn.shape[1]
    cw = u.shape[1]
    nS = S // T
    conv_k = w_dw.shape[0]
    w_dw = jnp.repeat(w_dw, SUBLANES, axis=0)
    hb = T // CONV_HALO
    kern = functools.partial(_mixout_kernel, nS=nS, conv_k=conv_k, n_exp=n_exp, n_grp=n_grp)
    full = lambda a: pl.BlockSpec(a.shape, lambda i: (0, 0))
    return pl.pallas_call(
        kern,
        grid=(N // T,),
        in_specs=[pl.BlockSpec((T, aw), lambda i: (i, 0)),
                  pl.BlockSpec((T, cw), lambda i: (i, 0)),
                  pl.BlockSpec((CONV_HALO, cw), lambda i: (jnp.maximum(i * hb - 1, 0), 0)),
                  pl.BlockSpec((T, D), lambda i: (i, 0)),
                  pl.BlockSpec((None, 6, D), lambda i: (i // nS, 0, 0)),
                  full(w_dw), full(b_dw), full(ln_g), full(ln_b), full(wo1), full(wo2), full(g_ffn),
                  full(wr_hi), full(wr_lo), full(b_r)],
        out_specs=[pl.BlockSpec((T, D), lambda i: (i, 0)),
                   pl.BlockSpec((T, D), lambda i: (i, 0)),
                   pl.BlockSpec((T, LANES), lambda i: (i, 0)),
                   pl.BlockSpec((SUBLANES, T), lambda i: (0, i)),
                   pl.BlockSpec((1, LANES), lambda i: (0, 0))],
        out_shape=[jax.ShapeDtypeStruct((N, D), F32),
                   jax.ShapeDtypeStruct((N, D), F32),
                   jax.ShapeDtypeStruct((N, LANES), F32),
                   jax.ShapeDtypeStruct((SUBLANES, N), F32),
                   jax.ShapeDtypeStruct((1, LANES), F32)],
        scratch_shapes=[pltpu.VMEM((SUBLANES, CONV_HALO + T, cw), F32),
                        pltpu.VMEM((T, cw), BF16)],
        compiler_params=_cparams(("arbitrary",)),
        name="conv_outproj_router",
    )(attn, u, u, x2, mod, w_dw, b_dw, ln_g, ln_b, wo1, wo2, g_ffn, wr_hi, wr_lo, b_r)


def _row_of(ref, r):
    if isinstance(r, int):
        return ref.at[r // SUBLANES, pl.ds(r % SUBLANES, 1)]
    return ref.at[lax.shift_right_logical(r, 3), pl.ds(r & (SUBLANES - 1), 1)]


def _scatter_kernel(dest_ref, h2_ref, xs_hbm, sem):
    groups = h2_ref.shape[0]
    tokens = groups * SUBLANES

    def issue(g, carry):
        for j in range(SUBLANES):
            for k in range(TOP_K_INNER):
                d = dest_ref[0, k * tokens + g * SUBLANES + j]
                pltpu.make_async_copy(h2_ref.at[g, pl.ds(j, 1)], _row_of(xs_hbm, d), sem).start(priority=k % 2)
        return carry

    lax.fori_loop(0, groups, issue, 0)
    for k in range(TOP_K_INNER):
        pltpu.make_async_copy(h2_ref, xs_hbm.at[pl.ds(0, groups)], sem).wait()


def _dest_table(dest, tokens):
    tab = jnp.concatenate([d.reshape(-1, tokens) for d in dest], axis=1)
    return tab.reshape(tab.shape[0], 1, tab.shape[1])


def _scatter_rows(h2, dest, *, tokens):
    N, C = h2.shape
    steps = N // tokens
    dest3 = _dest_table(dest, tokens)
    rows = N * TOP_K_INNER
    xs = pl.pallas_call(
        _scatter_kernel,
        grid=(steps,),
        in_specs=[pl.BlockSpec((None, 1, TOP_K_INNER * tokens), lambda s: (s, 0, 0), memory_space=pltpu.SMEM),
                  pl.BlockSpec((tokens // SUBLANES, SUBLANES, C), lambda s: (s, 0, 0))],
        out_specs=pl.BlockSpec(memory_space=pl.ANY),
        out_shape=jax.ShapeDtypeStruct((rows // SUBLANES, SUBLANES, C), h2.dtype),
        scratch_shapes=[pltpu.SemaphoreType.DMA],
        compiler_params=_cparams(("arbitrary",)),
        name="moe_scatter_rows",
    )(dest3, h2.reshape(N // SUBLANES, SUBLANES, C))
    return xs.reshape(rows, C)


XS_RING = 3


def _experts_kernel(vt_ref, ve_ref, vlo_ref, vhi_ref, vnew_ref, vnext_ref, vslot_ref,
                    xs_hbm, wg_hbm, wu_hbm, wd_hbm, ys_ref, xbuf, wg_buf, wu_buf, wd_buf, sem, wsem,
                    *, n_tiles):
    v = pl.program_id(0)
    lo = vlo_ref[v]
    hi = vhi_ref[v]
    t = vt_ref[v]
    tm = xbuf.shape[1]
    wslot = vslot_ref[v]

    def weight_copies(expert, slot):
        return [pltpu.make_async_copy(hbm.at[expert], buf.at[slot], wsem.at[slot, j])
                for j, (hbm, buf) in enumerate(((wg_hbm, wg_buf), (wu_hbm, wu_buf), (wd_hbm, wd_buf)))]

    @pl.when(v == 0)
    def _():
        for c in weight_copies(ve_ref[0], 0):
            c.start()

    @pl.when(vnew_ref[v] == 1)
    def _():
        @pl.when(vnext_ref[v] >= 0)
        def _():
            for c in weight_copies(vnext_ref[v], 1 - wslot):
                c.start()

        for c in weight_copies(ve_ref[v], wslot):
            c.wait()

    def tile_copy(tile):
        slot = lax.rem(tile, XS_RING)
        return pltpu.make_async_copy(xs_hbm.at[pl.ds(pl.multiple_of(tile * tm, tm), tm)], xbuf.at[slot],
                                     sem.at[slot])

    @pl.when(v == 0)
    def _():
        for tile in range(min(XS_RING - 1, n_tiles)):
            tile_copy(tile).start()

    @pl.when((lo == 0) & (hi > lo))
    def _():
        @pl.when(t + (XS_RING - 1) < n_tiles)
        def _():
            tile_copy(t + (XS_RING - 1)).start()

        tile_copy(t).wait()

    @pl.when(hi > lo)
    def _():
        x = xbuf[lax.rem(t, XS_RING)].astype(BF16)
        g = jnp.dot(x, wg_buf[wslot].astype(BF16), preferred_element_type=F32)
        u = jnp.dot(x, wu_buf[wslot].astype(BF16), preferred_element_type=F32)
        hid = (_silu(g) * u).astype(BF16)
        y = jnp.dot(hid, wd_buf[wslot].astype(BF16), preferred_element_type=F32)

        @pl.when(lo == 0)
        def _():
            ys_ref[...] = y

        @pl.when(lo > 0)
        def _():
            row = lax.broadcasted_iota(jnp.int32, y.shape, 0)
            ys_ref[...] = jnp.where((row >= lo) & (row < hi), y, ys_ref[...])


def _experts(xs, w_gate, w_up, w_down, visits, *, tm):
    R, C = xs.shape
    E, D, ff = w_gate.shape
    vt = visits[0]
    any_spec = pl.BlockSpec(memory_space=pl.ANY)
    grid_spec = pltpu.PrefetchScalarGridSpec(
        num_scalar_prefetch=len(visits),
        grid=(vt.shape[0],),
        in_specs=[any_spec, any_spec, any_spec, any_spec],
        out_specs=pl.BlockSpec((tm, C), lambda v, vt, *_: (vt[v], 0)),
        scratch_shapes=[pltpu.VMEM((XS_RING, tm, C), F32),
                        pltpu.VMEM((2, D, ff), w_gate.dtype),
                        pltpu.VMEM((2, D, ff), w_up.dtype),
                        pltpu.VMEM((2, ff, D), w_down.dtype),
                        pltpu.SemaphoreType.DMA((XS_RING,)),
                        pltpu.SemaphoreType.DMA((2, 3))],
    )
    return pl.pallas_call(
        functools.partial(_experts_kernel, n_tiles=R // tm),
        grid_spec=grid_spec,
        out_shape=jax.ShapeDtypeStruct((R, C), F32),
        compiler_params=_cparams(("arbitrary",)),
        name="moe_grouped_experts",
    )(*visits, xs, w_gate, w_up, w_down)


def _visit_tables(off, cnt, n_rows, tm):
    n_tiles = n_rows // tm
    n_exp = off.shape[0]
    n_visits = n_tiles + n_exp - 1
    tile_starts = jnp.arange(n_tiles, dtype=jnp.int32) * tm
    seg_starts = jnp.where((cnt > 0) & (off % tm != 0), off, n_rows)
    starts = jnp.sort(jnp.concatenate([tile_starts, seg_starts]))
    lo_abs = starts[:n_visits]
    hi_abs = starts[1:n_visits + 1]
    valid = lo_abs < n_rows
    tile = jnp.where(valid, lo_abs // tm, n_tiles - 1)
    ends = off + cnt
    probe = jnp.where(valid, lo_abs, n_rows - 1)
    expert = jnp.sum(ends[None, :] <= probe[:, None], axis=1).astype(jnp.int32)
    row_lo = jnp.where(valid, lo_abs - tile * tm, 0)
    row_hi = jnp.where(valid, hi_abs - tile * tm, 0)
    new = jnp.concatenate([jnp.ones((1,), jnp.int32), (expert[1:] != expert[:-1]).astype(jnp.int32)])
    slot = (jnp.cumsum(new) - 1) % 2
    nxt = jnp.searchsorted(expert, expert, side="right")
    nxt_expert = jnp.where(nxt < n_visits, expert[jnp.minimum(nxt, n_visits - 1)], -1)
    return tile, expert, row_lo, row_hi, new, nxt_expert.astype(jnp.int32), slot.astype(jnp.int32)


def _combine_kernel(dcur_ref, dnxt_ref, ys_hbm, route_ref, x1_ref, mod_ref, o_ref, ybuf, sem):
    T = x1_ref.shape[0]
    i = pl.program_id(0)
    slot = lax.rem(i, 2)

    def gather(d_ref, sl):
        def body(g, carry):
            for j in range(SUBLANES):
                for k in range(TOP_K_INNER):
                    d = d_ref[0, k * T + g * SUBLANES + j]
                    pltpu.make_async_copy(_row_of(ys_hbm, d), ybuf.at[sl, k, g, pl.ds(j, 1)],
                                          sem.at[sl]).start(priority=k % 2)
            return carry
        lax.fori_loop(0, T // SUBLANES, body, 0)

    @pl.when(i == 0)
    def _():
        gather(dcur_ref, 0)

    @pl.when(i + 1 < pl.num_programs(0))
    def _():
        gather(dnxt_ref, 1 - slot)

    for k in range(TOP_K_INNER):
        pltpu.make_async_copy(ys_hbm.at[pl.ds(0, T // SUBLANES)], ybuf.at[slot, k], sem.at[slot]).wait()

    route = route_ref[...]
    moe = jnp.zeros(x1_ref.shape, F32)
    for k in range(TOP_K_INNER):
        moe = moe + route[:, TOP_K_INNER + k:TOP_K_INNER + k + 1] * ybuf[slot, k].reshape(x1_ref.shape)
    o_ref[...] = x1_ref[...] + mod_ref[5:6, :] * moe


def _combine(ys, dest, route, x1, mod, *, S, T):
    N, D = x1.shape
    C = ys.shape[1]
    steps = N // T
    nS = S // T
    dest3 = _dest_table(dest, T)
    dspec = lambda f: pl.BlockSpec((None, 1, TOP_K_INNER * T), f, memory_space=pltpu.SMEM)
    return pl.pallas_call(
        _combine_kernel,
        grid=(steps,),
        in_specs=[dspec(lambda i: (i, 0, 0)),
                  dspec(lambda i: (jnp.minimum(i + 1, steps - 1), 0, 0)),
                  pl.BlockSpec(memory_space=pl.ANY),
                  pl.BlockSpec((T, LANES), lambda i: (i, 0)),
                  pl.BlockSpec((T, D), lambda i: (i, 0)),
                  pl.BlockSpec((None, 6, D), lambda i: (i // nS, 0, 0))],
        out_specs=pl.BlockSpec((T, D), lambda i: (i, 0)),
        out_shape=jax.ShapeDtypeStruct((N, D), F32),
        scratch_shapes=[pltpu.VMEM((2, TOP_K_INNER, T // SUBLANES, SUBLANES, C), F32),
                        pltpu.SemaphoreType.DMA((2,))],
        compiler_params=_cparams(("arbitrary",)),
        name="moe_gather_combine",
    )(dest3, dest3, ys.reshape(ys.shape[0] // SUBLANES, SUBLANES, C), route, x1, mod)


def _layer(x2, mod, pos_row, l, B, S, g_mix, w_in, q_norm_g, k_norm_g, lambda_q1, lambda_k1, lambda_q2,
           lambda_k2, subln_g, b_glu, w_dw, b_dw, conv_ln_g, conv_ln_b, w_out, g_ffn, w_group, b_group,
           w_router, b_router, w_gate, w_up, w_down):
    N, D = x2.shape
    dh = q_norm_g.shape[0]
    H = N_DIFF_HEADS
    aw = H * 2 * dh
    cw = w_dw.shape[1]
    rot = dh // 4
    n_grp = w_group.shape[1]
    n_exp = w_router.shape[1]
    lambda_init = 0.8 - 0.6 * math.exp(-0.3 * l)

    w_qk_t = w_in[:, :2 * aw].T.astype(BF16)
    w_vglu = w_in[:, 2 * aw:].astype(BF16)
    scale = dh ** -0.5 * math.log2(math.e)
    gq = jnp.tile(q_norm_g * scale, aw // dh)
    gk = jnp.tile(k_norm_g, aw // dh)
    gqk_tab = jnp.broadcast_to(jnp.concatenate([gq, gk])[:, None], (2 * aw, LANES))
    inv_freq = ROPE_THETA ** (-jnp.arange(0, rot, 2, dtype=F32) / rot)
    invf_tab = jnp.broadcast_to(inv_freq[:, None], (rot // 2, LANES))

    q, kt, v, u = _inproj(x2, mod, g_mix.reshape(1, D), pos_row, w_vglu, w_qk_t, b_glu.reshape(1, 2 * cw),
                          gqk_tab, invf_tab, B=B, S=S, aw=aw, cw=cw, dh=dh, rot=rot, T=512)

    attn = _diff_attention(q, kt, v, lambda_q1.reshape(1, dh), lambda_k1.reshape(1, dh),
                           lambda_q2.reshape(1, dh), lambda_k2.reshape(1, dh), subln_g.reshape(1, 2 * dh),
                           B=B, S=S, H=H, dh=dh, lambda_init=lambda_init, tq=1024, tk=1024)

    w_r = jnp.zeros((D, LANES), F32).at[:, :n_exp].set(w_router).at[:, n_exp:n_exp + n_grp].set(w_group)
    b_r = jnp.zeros((1, LANES), F32).at[0, :n_exp].set(b_router).at[0, n_exp:n_exp + n_grp].set(b_group)
    wr_hi = w_r.astype(BF16)
    wr_lo = (w_r - wr_hi.astype(F32)).astype(BF16)
    wo = w_out.astype(BF16)
    x1, h2, route, route_t, counts = _mixout(attn, u, x2, mod, w_dw, b_dw.reshape(1, cw), conv_ln_g.reshape(1, cw),
                                     conv_ln_b.reshape(1, cw), wo[:aw], wo[aw:], g_ffn.reshape(1, D),
                                     wr_hi, wr_lo, b_r, S=S, T=512, n_exp=n_exp, n_grp=n_grp)

    cnt = counts[0, :n_exp].astype(jnp.int32)
    off = jnp.cumsum(cnt) - cnt
    experts = jnp.arange(n_exp, dtype=jnp.int32)[:, None]
    dest = [jnp.sum(jnp.where(route_t[k].astype(jnp.int32)[None, :] == experts, off[:, None], 0), axis=0)
            + route_t[4 + k].astype(jnp.int32) for k in range(TOP_K_INNER)]
    tm = 256
    visits = _visit_tables(off, cnt, N * TOP_K_INNER, tm)

    xs = _scatter_rows(h2, dest, tokens=min(1024, N))
    ys = _experts(xs, w_gate, w_up, w_down, visits, tm=tm)
    return _combine(ys, dest, route, x1, mod, S=S, T=256)


def kernel(x, c, positions, w_ada, b_ada, g_mix, w_in, q_norm_g, k_norm_g, lambda_q1, lambda_k1, lambda_q2,
           lambda_k2, subln_g, b_glu, w_dw, b_dw, conv_ln_g, conv_ln_b, w_out, g_ffn, w_group, b_group,
           w_router, b_router, w_gate, w_up, w_down):
    B, S, D = x.shape
    depth = w_ada.shape[0]
    x2 = x.reshape(B * S, D)
    pos_row = positions.astype(F32).reshape(1, B * S)
    for l in range(depth):
        mod = _modulation(c, w_ada[l], b_ada[l])
        x2 = _layer(x2, mod, pos_row, l, B, S, g_mix[l], w_in[l], q_norm_g[l], k_norm_g[l], lambda_q1[l],
                    lambda_k1[l], lambda_q2[l], lambda_k2[l], subln_g[l], b_glu[l], w_dw[l], b_dw[l],
                    conv_ln_g[l], conv_ln_b[l], w_out[l], g_ffn[l], w_group[l], b_group[l], w_router[l],
                    b_router[l], w_gate[l], w_up[l], w_down[l])
    return x2.reshape(B, S, D)
```

```python
import functools
import math

import numpy as np
import jax
import jax.numpy as jnp
from jax import lax
from jax.experimental import pallas as pl
from jax.experimental.pallas import tpu as pltpu

F32 = jnp.float32
BF16 = jnp.bfloat16

EPS = 1e-6
ROPE_THETA = 500000.0
N_DIFF_HEADS = 4
TOP_K_INNER = 2

LANES = 128
SUBLANES = 8
CONV_HALO = 32
VMEM_LIMIT = 48 * 1024 * 1024


def _cparams(sem):
    return pltpu.CompilerParams(dimension_semantics=sem, vmem_limit_bytes=VMEM_LIMIT)


def _silu(x):
    return x * jax.nn.sigmoid(x)


def _mod_kernel(c_ref, w_ref, b_ref, o_ref):
    c = c_ref[...]
    o_ref[...] = jnp.dot(_silu(c), w_ref[...], preferred_element_type=F32,
                         precision=lax.Precision.HIGHEST) + b_ref[...]


def _modulation(c, w_ada, b_ada):
    B, D = c.shape
    n_out = w_ada.shape[1]
    rows = 8
    c_pad = jnp.pad(c, ((0, rows - B), (0, 0)))
    bn = 1024
    out = pl.pallas_call(
        _mod_kernel,
        grid=(n_out // bn,),
        in_specs=[pl.BlockSpec((rows, D), lambda j: (0, 0)),
                  pl.BlockSpec((D, bn), lambda j: (0, j)),
                  pl.BlockSpec((1, bn), lambda j: (0, j))],
        out_specs=pl.BlockSpec((rows, bn), lambda j: (0, j)),
        out_shape=jax.ShapeDtypeStruct((rows, n_out), F32),
        compiler_params=_cparams(("parallel",)),
        name="adaln_mod",
    )(c_pad, w_ada, b_ada.reshape(1, n_out))
    return out[:B].reshape(B, 6, D)


def _inproj_kernel(x_ref, mod_ref, g_ref, pos_ref, wv_ref, wqk_ref, bglu_ref, gqk_ref, invf_ref,
                   q_ref, kt_ref, v_ref, u_ref, *, aw, cw, dh, rot):
    T = x_ref.shape[0]
    half = rot // 2
    x = x_ref[...]
    ms = jnp.mean(x * x, axis=-1, keepdims=True)
    sh = mod_ref[0:1, :]
    sc = mod_ref[1:2, :]
    h = x * lax.rsqrt(ms + EPS) * g_ref[...] * (1.0 + sc) + sh
    hb = h.astype(BF16)

    pv = jnp.dot(hb, wv_ref[...], preferred_element_type=F32)
    v_ref[...] = pv[:, :aw].astype(BF16)
    a = pv[:, aw:aw + cw] + bglu_ref[:, :cw]
    gate = pv[:, aw + cw:] + bglu_ref[:, cw:]
    u_ref[...] = (a * jax.nn.sigmoid(gate)).astype(BF16)

    qkt = lax.dot_general(wqk_ref[...], hb, (((1,), (1,)), ((), ())), preferred_element_type=F32)
    nch = 2 * aw // dh
    for c in range(T // LANES):
        sl = slice(c * LANES, (c + 1) * LANES)
        s3 = qkt[:, sl].reshape(nch, dh, LANES)
        ssq = jnp.mean(s3 * s3, axis=1, keepdims=True)
        y = s3 * lax.rsqrt(ssq + EPS) * gqk_ref[...].reshape(nch, dh, LANES)
        ang = invf_ref[...] * pos_ref[:, sl]
        cs = jnp.cos(ang)
        sn = jnp.sin(ang)
        t1 = y[:, 0:half, :]
        t2 = y[:, half:rot, :]
        y = jnp.concatenate([t1 * cs - t2 * sn, t2 * cs + t1 * sn, y[:, rot:, :]], axis=1)
        y2 = y.reshape(2 * aw, LANES)
        kt_ref[:, sl] = y2[aw:].astype(BF16)
        q_ref[sl, :] = y2[:aw].T.astype(BF16)


def _inproj(x2, mod, g_mix, pos_row, w_vglu, w_qk_t, b_glu, gqk_tab, invf_tab, *, B, S, aw, cw, dh, rot, T):
    N, D = x2.shape
    nS = S // T
    kern = functools.partial(_inproj_kernel, aw=aw, cw=cw, dh=dh, rot=rot)
    return pl.pallas_call(
        kern,
        grid=(N // T,),
        in_specs=[pl.BlockSpec((T, D), lambda i: (i, 0)),
                  pl.BlockSpec((None, 6, D), lambda i: (i // nS, 0, 0)),
                  pl.BlockSpec((1, D), lambda i: (0, 0)),
                  pl.BlockSpec((1, T), lambda i: (0, i)),
                  pl.BlockSpec(w_vglu.shape, lambda i: (0, 0)),
                  pl.BlockSpec(w_qk_t.shape, lambda i: (0, 0)),
                  pl.BlockSpec((1, 2 * cw), lambda i: (0, 0)),
                  pl.BlockSpec(gqk_tab.shape, lambda i: (0, 0)),
                  pl.BlockSpec(invf_tab.shape, lambda i: (0, 0))],
        out_specs=[pl.BlockSpec((T, aw), lambda i: (i, 0)),
                   pl.BlockSpec((None, aw, T), lambda i: (i // nS, 0, i % nS)),
                   pl.BlockSpec((T, aw), lambda i: (i, 0)),
                   pl.BlockSpec((T, cw), lambda i: (i, 0))],
        out_shape=[jax.ShapeDtypeStruct((N, aw), BF16),
                   jax.ShapeDtypeStruct((B, aw, S), BF16),
                   jax.ShapeDtypeStruct((N, aw), BF16),
                   jax.ShapeDtypeStruct((N, cw), BF16)],
        compiler_params=_cparams(("parallel",)),
        name="inproj_qknorm_rope_glu",
    )(x2, mod, g_mix, pos_row, w_vglu, w_qk_t, b_glu, gqk_tab, invf_tab)


def _attn_kernel(q_ref, kt_ref, v_ref, lq1_ref, lk1_ref, lq2_ref, lk2_ref, sg_ref, o_ref, acc_ref,
                 s0_ref, s1_ref, *, tq, tk, dh, lambda_init):
    i = pl.program_id(2)
    hd = 2 * dh
    q = q_ref[...]
    lane = lax.broadcasted_iota(jnp.int32, q.shape, 1)
    zero = jnp.zeros_like(q)
    q2 = jnp.concatenate([jnp.where(lane < dh, q, zero), jnp.where(lane >= dh, q, zero)], axis=0)

    acc_ref[...] = jnp.zeros_like(acc_ref)
    n_sub = tk // tq
    ones_col = {w: jnp.ones((w, hd), BF16) for w in {tk} | {(r + 1) * tq for r in range(n_sub)}}

    def scores(t, s_ref):
        start = pl.multiple_of(t * tk, tk)
        s_ref[...] = jnp.dot(q2, kt_ref[:, pl.ds(start, tk)], preferred_element_type=F32)

    def softmax_pv(t, s_ref, m, width=tk, diagonal=False):
        start = pl.multiple_of(t * tk, tk)
        vt = jnp.concatenate([v_ref[pl.ds(start, width), :], ones_col[width]], axis=1)
        s = s_ref[:, :width]
        if diagonal:
            row = lax.broadcasted_iota(jnp.int32, (2 * tq, tq), 0)
            col = lax.broadcasted_iota(jnp.int32, (2 * tq, tq), 1)
            qrow = jnp.where(row >= tq, row - tq, row)
            tail = jnp.where(col <= qrow, s[:, width - tq:], -jnp.inf)
            s = tail if width == tq else jnp.concatenate([s[:, :width - tq], tail], axis=1)
        m_new = jnp.maximum(m, jnp.max(s, axis=1, keepdims=True))
        alpha = jnp.exp2(m - m_new)
        p = jnp.exp2(s - m_new).astype(BF16)
        acc_ref[...] = alpha * acc_ref[...] + jnp.dot(p, vt, preferred_element_type=F32)
        return m_new

    n_full = (i * tq) // tk
    odd = lax.rem(n_full, 2)
    m0 = jnp.full((2 * tq, 1), -jnp.inf, F32)

    @pl.when(odd == 0)
    def _():
        scores(0, s0_ref)

    @pl.when(odd == 1)
    def _():
        scores(0, s1_ref)

    def peeled(_, m):
        scores(1, s0_ref)
        return softmax_pv(0, s1_ref, m)

    m = lax.fori_loop(0, odd, peeled, m0)

    def pair(jj, m):
        t = odd + 2 * jj
        scores(t + 1, s1_ref)
        m = softmax_pv(t, s0_ref, m)
        scores(t + 2, s0_ref)
        return softmax_pv(t + 1, s1_ref, m)

    m = lax.fori_loop(0, (n_full - odd) // 2, pair, m)

    for r in range(n_sub):
        @pl.when(lax.rem(i, n_sub) == r)
        def _(r=r):
            softmax_pv(n_full, s0_ref, m, width=(r + 1) * tq, diagonal=True)

    lam = (jnp.exp(jnp.sum(lq1_ref[...] * lk1_ref[...], axis=1, keepdims=True))
           - jnp.exp(jnp.sum(lq2_ref[...] * lk2_ref[...], axis=1, keepdims=True)) + lambda_init)
    o = (acc_ref[0:tq, 0:hd] / acc_ref[0:tq, hd:]
         - lam * (acc_ref[tq:, 0:hd] / acc_ref[tq:, hd:]))
    ms = jnp.mean(o * o, axis=1, keepdims=True)
    o = o * lax.rsqrt(ms + EPS) * sg_ref[...] * (1.0 - lambda_init)
    o_ref[...] = o.astype(o_ref.dtype)


def _diff_attention(q, kt, v, lq1, lk1, lq2, lk2, subln_g, *, B, S, H, dh, lambda_init, tq, tk):
    aw = H * 2 * dh
    q3 = q.reshape(B, S, aw)
    v3 = v.reshape(B, S, aw)
    hd = 2 * dh
    tk = min(tk, S)
    tq = min(tq, tk)
    kern = functools.partial(_attn_kernel, tq=tq, tk=tk, dh=dh, lambda_init=lambda_init)
    vec = pl.BlockSpec((1, dh), lambda b, h, i: (0, 0))
    out = pl.pallas_call(
        kern,
        grid=(B, H, S // tq),
        in_specs=[pl.BlockSpec((None, tq, hd), lambda b, h, i: (b, i, h)),
                  pl.BlockSpec((None, hd, S), lambda b, h, i: (b, h, 0)),
                  pl.BlockSpec((None, S, hd), lambda b, h, i: (b, 0, h)),
                  vec, vec, vec, vec,
                  pl.BlockSpec((1, hd), lambda b, h, i: (0, 0))],
        out_specs=pl.BlockSpec((None, tq, hd), lambda b, h, i: (b, i, h)),
        out_shape=jax.ShapeDtypeStruct((B, S, aw), BF16),
        scratch_shapes=[pltpu.VMEM((2 * tq, 2 * hd), F32),
                        pltpu.VMEM((2 * tq, tk), F32),
                        pltpu.VMEM((2 * tq, tk), F32)],
        compiler_params=_cparams(("parallel", "parallel", "parallel")),
        name="diff_flash_attention",
    )(q3, kt, v3, lq1, lk1, lq2, lk2, subln_g)
    return out.reshape(B * S, aw)


def _mixout_kernel(attn_ref, ucur_ref, uhalo_ref, x_ref, mod_ref, wdw_ref, bdw_ref, lng_ref, lnb_ref,
                   wo1_ref, wo2_ref, gffn_ref, wrh_ref, wrl_ref, br_ref,
                   x1_ref, h2_ref, route_ref, route_t_ref, cnt_ref, ubuf_ref, conv_ref,
                   *, nS, conv_k, n_exp, n_grp):
    T = x_ref.shape[0]
    cw = ucur_ref.shape[1]
    i = pl.program_id(0)

    @pl.when(i == 0)
    def _():
        cnt_ref[...] = jnp.zeros_like(cnt_ref)

    first = (i % nS) == 0
    halo = uhalo_ref[...].astype(F32)
    ubuf_ref[0, 0:CONV_HALO, :] = jnp.where(first, jnp.zeros_like(halo), halo)
    ubuf_ref[0, CONV_HALO:, :] = ucur_ref[...].astype(F32)
    span = T + CONV_HALO - SUBLANES
    for b in range(1, SUBLANES):
        ubuf_ref[b, 0:span, :] = ubuf_ref[0, b:b + span, :]

    off = CONV_HALO - (conv_k - 1)
    rows = 32
    for r0 in range(0, T, rows):
        acc = jnp.zeros((rows // SUBLANES, SUBLANES, cw), F32)
        for j in range(conv_k):
            a, b = divmod(j + off, SUBLANES)
            lo_r = r0 + a * SUBLANES
            acc = acc + (wdw_ref[j * SUBLANES:(j + 1) * SUBLANES, :]
                         * ubuf_ref[b, lo_r:lo_r + rows, :].reshape(rows // SUBLANES, SUBLANES, cw))
        y = acc.reshape(rows, cw) + bdw_ref[...]
        mu = jnp.mean(y, axis=1, keepdims=True)
        d = y - mu
        var = jnp.mean(d * d, axis=1, keepdims=True)
        z = d * lax.rsqrt(var + EPS) * lng_ref[...] + lnb_ref[...]
        conv_ref[r0:r0 + rows, :] = _silu(z).astype(BF16)

    yo = (jnp.dot(attn_ref[...], wo1_ref[...], preferred_element_type=F32)
          + jnp.dot(conv_ref[...], wo2_ref[...], preferred_element_type=F32))
    x1 = x_ref[...] + mod_ref[2:3, :] * yo
    x1_ref[...] = x1
    ms = jnp.mean(x1 * x1, axis=1, keepdims=True)
    h2 = x1 * lax.rsqrt(ms + EPS) * gffn_ref[...] * (1.0 + mod_ref[4:5, :]) + mod_ref[3:4, :]
    h2_ref[...] = h2
    hi = h2.astype(BF16)
    lo = (h2 - hi.astype(F32)).astype(BF16)

    logits = (jnp.dot(hi, wrh_ref[...], preferred_element_type=F32)
              + jnp.dot(lo, wrh_ref[...], preferred_element_type=F32)
              + jnp.dot(hi, wrl_ref[...], preferred_element_type=F32)) + br_ref[...]
    lane_i = lax.broadcasted_iota(jnp.int32, logits.shape, 1)
    lane = lane_i.astype(F32)
    big = jnp.float32(1e9)
    ninf = jnp.float32(-jnp.inf)
    is_g = (lane_i >= n_exp) & (lane_i < n_exp + n_grp)
    gl = jnp.where(is_g, logits, ninf)
    gmax = jnp.max(gl, axis=1, keepdims=True)
    gsum = jnp.sum(jnp.where(is_g, jnp.exp(gl - gmax), 0.0), axis=1, keepdims=True)
    g_p = 1.0 / gsum
    gidx = jnp.min(jnp.where(gl == gmax, lane, big), axis=1, keepdims=True) - n_exp
    epg = n_exp // n_grp
    lo_l = gidx * epg
    in_grp = (lane >= lo_l) & (lane < lo_l + epg)
    el = jnp.where(in_grp, logits, ninf)
    m1 = jnp.max(el, axis=1, keepdims=True)
    i1 = jnp.min(jnp.where(el == m1, lane, big), axis=1, keepdims=True)
    el2 = jnp.where(lane == i1, ninf, el)
    m2 = jnp.max(el2, axis=1, keepdims=True)
    i2 = jnp.min(jnp.where(el2 == m2, lane, big), axis=1, keepdims=True)
    e2 = jnp.exp(m2 - m1)
    p1 = 1.0 / (1.0 + e2)
    p2 = e2 / (1.0 + e2)

    sel = (lane == i1) | (lane == i2)
    rr = lax.broadcasted_iota(jnp.int32, (T, T), 0)
    cc = lax.broadcasted_iota(jnp.int32, (T, T), 1)
    before = jnp.dot((rr > cc).astype(BF16), sel.astype(BF16), preferred_element_type=F32) + cnt_ref[...]
    r1 = jnp.sum(jnp.where(lane == i1, before, 0.0), axis=1, keepdims=True)
    r2 = jnp.sum(jnp.where(lane == i2, before, 0.0), axis=1, keepdims=True)
    cnt_ref[...] += jnp.sum(sel.astype(F32), axis=0, keepdims=True)

    route = jnp.zeros_like(logits)
    for k, val in enumerate((i1, i2, g_p * p1, g_p * p2, r1, r2)):
        route = jnp.where(lane_i == k, val, route)
    route_ref[...] = route
    route_t_ref[...] = route.T[:SUBLANES]


def _mixout(attn, u, x2, mod, w_dw, b_dw, ln_g, ln_b, wo1, wo2, g_ffn, wr_hi, wr_lo, b_r,
            *, S, T, n_exp, n_grp):
    N, D = x2.shape
    aw = attn.shape[1]
    cw = u.shape[1]
    nS = S // T
    conv_k = w_dw.shape[0]
    w_dw = jnp.repeat(w_dw, SUBLANES, axis=0)
    hb = T // CONV_HALO
    kern = functools.partial(_mixout_kernel, nS=nS, conv_k=conv_k, n_exp=n_exp, n_grp=n_grp)
    full = lambda a: pl.BlockSpec(a.shape, lambda i: (0, 0))
    return pl.pallas_call(
        kern,
        grid=(N // T,),
        in_specs=[pl.BlockSpec((T, aw), lambda i: (i, 0)),
                  pl.BlockSpec((T, cw), lambda i: (i, 0)),
                  pl.BlockSpec((CONV_HALO, cw), lambda i: (jnp.maximum(i * hb - 1, 0), 0)),
                  pl.BlockSpec((T, D), lambda i: (i, 0)),
                  pl.BlockSpec((None, 6, D), lambda i: (i // nS, 0, 0)),
                  full(w_dw), full(b_dw), full(ln_g), full(ln_b), full(wo1), full(wo2), full(g_ffn),
                  full(wr_hi), full(wr_lo), full(b_r)],
        out_specs=[pl.BlockSpec((T, D), lambda i: (i, 0)),
                   pl.BlockSpec((T, D), lambda i: (i, 0)),
                   pl.BlockSpec((T, LANES), lambda i: (i, 0)),
                   pl.BlockSpec((SUBLANES, T), lambda i: (0, i)),
                   pl.BlockSpec((1, LANES), lambda i: (0, 0))],
        out_shape=[jax.ShapeDtypeStruct((N, D), F32),
                   jax.ShapeDtypeStruct((N, D), F32),
                   jax.ShapeDtypeStruct((N, LANES), F32),
                   jax.ShapeDtypeStruct((SUBLANES, N), F32),
                   jax.ShapeDtypeStruct((1, LANES), F32)],
        scratch_shapes=[pltpu.VMEM((SUBLANES, CONV_HALO + T, cw), F32),
                        pltpu.VMEM((T, cw), BF16)],
        compiler_params=_cparams(("arbitrary",)),
        name="conv_outproj_router",
    )(attn, u, u, x2, mod, w_dw, b_dw, ln_g, ln_b, wo1, wo2, g_ffn, wr_hi, wr_lo, b_r)


def _row_of(ref, r):
    if isinstance(r, int):
        return ref.at[r // SUBLANES, pl.ds(r % SUBLANES, 1)]
    return ref.at[lax.shift_right_logical(r, 3), pl.ds(r & (SUBLANES - 1), 1)]


def _scatter_kernel(dest_ref, h2_ref, xs_hbm, sem):
    groups = h2_ref.shape[0]
    tokens = groups * SUBLANES

    def issue(g, carry):
        for j in range(SUBLANES):
            for k in range(TOP_K_INNER):
                d = dest_ref[0, k * tokens + g * SUBLANES + j]
                pltpu.make_async_copy(h2_ref.at[g, pl.ds(j, 1)], _row_of(xs_hbm, d), sem).start(priority=k % 2)
        return carry

    lax.fori_loop(0, groups, issue, 0)
    for k in range(TOP_K_INNER):
        pltpu.make_async_copy(h2_ref, xs_hbm.at[pl.ds(0, groups)], sem).wait()


def _dest_table(dest, tokens):
    tab = jnp.concatenate([d.reshape(-1, tokens) for d in dest], axis=1)
    return tab.reshape(tab.shape[0], 1, tab.shape[1])


def _scatter_rows(h2, dest, *, tokens):
    N, C = h2.shape
    steps = N // tokens
    dest3 = _dest_table(dest, tokens)
    rows = N * TOP_K_INNER
    xs = pl.pallas_call(
        _scatter_kernel,
        grid=(steps,),
        in_specs=[pl.BlockSpec((None, 1, TOP_K_INNER * tokens), lambda s: (s, 0, 0), memory_space=pltpu.SMEM),
                  pl.BlockSpec((tokens // SUBLANES, SUBLANES, C), lambda s: (s, 0, 0))],
        out_specs=pl.BlockSpec(memory_space=pl.ANY),
        out_shape=jax.ShapeDtypeStruct((rows // SUBLANES, SUBLANES, C), h2.dtype),
        scratch_shapes=[pltpu.SemaphoreType.DMA],
        compiler_params=_cparams(("arbitrary",)),
        name="moe_scatter_rows",
    )(dest3, h2.reshape(N // SUBLANES, SUBLANES, C))
    return xs.reshape(rows, C)


XS_RING = 3


def _experts_kernel(vt_ref, ve_ref, vlo_ref, vhi_ref, vnew_ref, vnext_ref, vslot_ref,
                    xs_hbm, wg_hbm, wu_hbm, wd_hbm, ys_ref, xbuf, wg_buf, wu_buf, wd_buf, sem, wsem,
                    *, n_tiles):
    v = pl.program_id(0)
    lo = vlo_ref[v]
    hi = vhi_ref[v]
    t = vt_ref[v]
    tm = xbuf.shape[1]
    wslot = vslot_ref[v]

    def weight_copies(expert, slot):
        return [pltpu.make_async_copy(hbm.at[expert], buf.at[slot], wsem.at[slot, j])
                for j, (hbm, buf) in enumerate(((wg_hbm, wg_buf), (wu_hbm, wu_buf), (wd_hbm, wd_buf)))]

    @pl.when(v == 0)
    def _():
        for c in weight_copies(ve_ref[0], 0):
            c.start()

    @pl.when(vnew_ref[v] == 1)
    def _():
        @pl.when(vnext_ref[v] >= 0)
        def _():
            for c in weight_copies(vnext_ref[v], 1 - wslot):
                c.start()

        for c in weight_copies(ve_ref[v], wslot):
            c.wait()

    def tile_copy(tile):
        slot = lax.rem(tile, XS_RING)
        return pltpu.make_async_copy(xs_hbm.at[pl.ds(pl.multiple_of(tile * tm, tm), tm)], xbuf.at[slot],
                                     sem.at[slot])

    @pl.when(v == 0)
    def _():
        for tile in range(min(XS_RING - 1, n_tiles)):
            tile_copy(tile).start()

    @pl.when((lo == 0) & (hi > lo))
    def _():
        @pl.when(t + (XS_RING - 1) < n_tiles)
        def _():
            tile_copy(t + (XS_RING - 1)).start()

        tile_copy(t).wait()

    @pl.when(hi > lo)
    def _():
        x = xbuf[lax.rem(t, XS_RING)].astype(BF16)
        g = jnp.dot(x, wg_buf[wslot].astype(BF16), preferred_element_type=F32)
        u = jnp.dot(x, wu_buf[wslot].astype(BF16), preferred_element_type=F32)
        hid = (_silu(g) * u).astype(BF16)
        y = jnp.dot(hid, wd_buf[wslot].astype(BF16), preferred_element_type=F32)

        @pl.when(lo == 0)
        def _():
            ys_ref[...] = y

        @pl.when(lo > 0)
        def _():
            row = lax.broadcasted_iota(jnp.int32, y.shape, 0)
            ys_ref[...] = jnp.where((row >= lo) & (row < hi), y, ys_ref[...])


def _experts(xs, w_gate, w_up, w_down, visits, *, tm):
    R, C = xs.shape
    E, D, ff = w_gate.shape
    vt = visits[0]
    any_spec = pl.BlockSpec(memory_space=pl.ANY)
    grid_spec = pltpu.PrefetchScalarGridSpec(
        num_scalar_prefetch=len(visits),
        grid=(vt.shape[0],),
        in_specs=[any_spec, any_spec, any_spec, any_spec],
        out_specs=pl.BlockSpec((tm, C), lambda v, vt, *_: (vt[v], 0)),
        scratch_shapes=[pltpu.VMEM((XS_RING, tm, C), F32),
                        pltpu.VMEM((2, D, ff), w_gate.dtype),
                        pltpu.VMEM((2, D, ff), w_up.dtype),
                        pltpu.VMEM((2, ff, D), w_down.dtype),
                        pltpu.SemaphoreType.DMA((XS_RING,)),
                        pltpu.SemaphoreType.DMA((2, 3))],
    )
    return pl.pallas_call(
        functools.partial(_experts_kernel, n_tiles=R // tm),
        grid_spec=grid_spec,
        out_shape=jax.ShapeDtypeStruct((R, C), F32),
        compiler_params=_cparams(("arbitrary",)),
        name="moe_grouped_experts",
    )(*visits, xs, w_gate, w_up, w_down)


def _visit_tables(off, cnt, n_rows, tm):
    n_tiles = n_rows // tm
    n_exp = off.shape[0]
    n_visits = n_tiles + n_exp - 1
    tile_starts = jnp.arange(n_tiles, dtype=jnp.int32) * tm
    seg_starts = jnp.where((cnt > 0) & (off % tm != 0), off, n_rows)
    starts = jnp.sort(jnp.concatenate([tile_starts, seg_starts]))
    lo_abs = starts[:n_visits]
    hi_abs = starts[1:n_visits + 1]
    valid = lo_abs < n_rows
    tile = jnp.where(valid, lo_abs // tm, n_tiles - 1)
    ends = off + cnt
    probe = jnp.where(valid, lo_abs, n_rows - 1)
    expert = jnp.sum(ends[None, :] <= probe[:, None], axis=1).astype(jnp.int32)
    row_lo = jnp.where(valid, lo_abs - tile * tm, 0)
    row_hi = jnp.where(valid, hi_abs - tile * tm, 0)
    new = jnp.concatenate([jnp.ones((1,), jnp.int32), (expert[1:] != expert[:-1]).astype(jnp.int32)])
    slot = (jnp.cumsum(new) - 1) % 2
    nxt = jnp.sum(expert[None, :] <= expert[:, None], axis=1)
    nxt_expert = jnp.where(nxt < n_visits, expert[jnp.minimum(nxt, n_visits - 1)], -1)
    return tile, expert, row_lo, row_hi, new, nxt_expert.astype(jnp.int32), slot.astype(jnp.int32)


def _combine_kernel(dcur_ref, dnxt_ref, ys_hbm, route_ref, x1_ref, mod_ref, o_ref, ybuf, sem):
    T = x1_ref.shape[0]
    i = pl.program_id(0)
    slot = lax.rem(i, 2)

    def gather(d_ref, sl):
        def body(g, carry):
            for j in range(SUBLANES):
                for k in range(TOP_K_INNER):
                    d = d_ref[0, k * T + g * SUBLANES + j]
                    pltpu.make_async_copy(_row_of(ys_hbm, d), ybuf.at[sl, k, g, pl.ds(j, 1)],
                                          sem.at[sl]).start(priority=k % 2)
            return carry
        lax.fori_loop(0, T // SUBLANES, body, 0)

    @pl.when(i == 0)
    def _():
        gather(dcur_ref, 0)

    @pl.when(i + 1 < pl.num_programs(0))
    def _():
        gather(dnxt_ref, 1 - slot)

    for k in range(TOP_K_INNER):
        pltpu.make_async_copy(ys_hbm.at[pl.ds(0, T // SUBLANES)], ybuf.at[slot, k], sem.at[slot]).wait()

    route = route_ref[...]
    moe = jnp.zeros(x1_ref.shape, F32)
    for k in range(TOP_K_INNER):
        moe = moe + route[:, TOP_K_INNER + k:TOP_K_INNER + k + 1] * ybuf[slot, k].reshape(x1_ref.shape)
    o_ref[...] = x1_ref[...] + mod_ref[5:6, :] * moe


def _combine(ys, dest, route, x1, mod, *, S, T):
    N, D = x1.shape
    C = ys.shape[1]
    steps = N // T
    nS = S // T
    dest3 = _dest_table(dest, T)
    dspec = lambda f: pl.BlockSpec((None, 1, TOP_K_INNER * T), f, memory_space=pltpu.SMEM)
    return pl.pallas_call(
        _combine_kernel,
        grid=(steps,),
        in_specs=[dspec(lambda i: (i, 0, 0)),
                  dspec(lambda i: (jnp.minimum(i + 1, steps - 1), 0, 0)),
                  pl.BlockSpec(memory_space=pl.ANY),
                  pl.BlockSpec((T, LANES), lambda i: (i, 0)),
                  pl.BlockSpec((T, D), lambda i: (i, 0)),
                  pl.BlockSpec((None, 6, D), lambda i: (i // nS, 0, 0))],
        out_specs=pl.BlockSpec((T, D), lambda i: (i, 0)),
        out_shape=jax.ShapeDtypeStruct((N, D), F32),
        scratch_shapes=[pltpu.VMEM((2, TOP_K_INNER, T // SUBLANES, SUBLANES, C), F32),
                        pltpu.SemaphoreType.DMA((2,))],
        compiler_params=_cparams(("arbitrary",)),
        name="moe_gather_combine",
    )(dest3, dest3, ys.reshape(ys.shape[0] // SUBLANES, SUBLANES, C), route, x1, mod)


def _layer(x2, mod, pos_row, l, B, S, g_mix, w_in, q_norm_g, k_norm_g, lambda_q1, lambda_k1, lambda_q2,
           lambda_k2, subln_g, b_glu, w_dw, b_dw, conv_ln_g, conv_ln_b, w_out, g_ffn, w_group, b_group,
           w_router, b_router, w_gate, w_up, w_down):
    N, D = x2.shape
    dh = q_norm_g.shape[0]
    H = N_DIFF_HEADS
    aw = H * 2 * dh
    cw = w_dw.shape[1]
    rot = dh // 4
    n_grp = w_group.shape[1]
    n_exp = w_router.shape[1]
    lambda_init = 0.8 - 0.6 * math.exp(-0.3 * l)

    w_qk_t = w_in[:, :2 * aw].T.astype(BF16)
    w_vglu = w_in[:, 2 * aw:].astype(BF16)
    scale = dh ** -0.5 * math.log2(math.e)
    gq = jnp.tile(q_norm_g * scale, aw // dh)
    gk = jnp.tile(k_norm_g, aw // dh)
    gqk_tab = jnp.broadcast_to(jnp.concatenate([gq, gk])[:, None], (2 * aw, LANES))
    inv_freq = ROPE_THETA ** (-jnp.arange(0, rot, 2, dtype=F32) / rot)
    invf_tab = jnp.broadcast_to(inv_freq[:, None], (rot // 2, LANES))

    q, kt, v, u = _inproj(x2, mod, g_mix.reshape(1, D), pos_row, w_vglu, w_qk_t, b_glu.reshape(1, 2 * cw),
                          gqk_tab, invf_tab, B=B, S=S, aw=aw, cw=cw, dh=dh, rot=rot, T=512)

    attn = _diff_attention(q, kt, v, lambda_q1.reshape(1, dh), lambda_k1.reshape(1, dh),
                           lambda_q2.reshape(1, dh), lambda_k2.reshape(1, dh), subln_g.reshape(1, 2 * dh),
                           B=B, S=S, H=H, dh=dh, lambda_init=lambda_init, tq=1024, tk=1024)

    w_r = jnp.zeros((D, LANES), F32).at[:, :n_exp].set(w_router).at[:, n_exp:n_exp + n_grp].set(w_group)
    b_r = jnp.zeros((1, LANES), F32).at[0, :n_exp].set(b_router).at[0, n_exp:n_exp + n_grp].set(b_group)
    wr_hi = w_r.astype(BF16)
    wr_lo = (w_r - wr_hi.astype(F32)).astype(BF16)
    wo = w_out.astype(BF16)
    x1, h2, route, route_t, counts = _mixout(attn, u, x2, mod, w_dw, b_dw.reshape(1, cw), conv_ln_g.reshape(1, cw),
                                     conv_ln_b.reshape(1, cw), wo[:aw], wo[aw:], g_ffn.reshape(1, D),
                                     wr_hi, wr_lo, b_r, S=S, T=512, n_exp=n_exp, n_grp=n_grp)

    cnt = counts[0, :n_exp].astype(jnp.int32)
    off = jnp.cumsum(cnt) - cnt
    experts = jnp.arange(n_exp, dtype=jnp.int32)[:, None]
    dest = [jnp.sum(jnp.where(route_t[k].astype(jnp.int32)[None, :] == experts, off[:, None], 0), axis=0)
            + route_t[4 + k].astype(jnp.int32) for k in range(TOP_K_INNER)]
    tm = 256
    visits = _visit_tables(off, cnt, N * TOP_K_INNER, tm)

    xs = _scatter_rows(h2, dest, tokens=min(1024, N))
    ys = _experts(xs, w_gate, w_up, w_down, visits, tm=tm)
    return _combine(ys, dest, route, x1, mod, S=S, T=256)


def kernel(x, c, positions, w_ada, b_ada, g_mix, w_in, q_norm_g, k_norm_g, lambda_q1, lambda_k1, lambda_q2,
           lambda_k2, subln_g, b_glu, w_dw, b_dw, conv_ln_g, conv_ln_b, w_out, g_ffn, w_group, b_group,
           w_router, b_router, w_gate, w_up, w_down):
    B, S, D = x.shape
    depth = w_ada.shape[0]
    x2 = x.reshape(B * S, D)
    pos_row = positions.astype(F32).reshape(1, B * S)
    for l in range(depth):
        mod = _modulation(c, w_ada[l], b_ada[l])
        x2 = _layer(x2, mod, pos_row, l, B, S, g_mix[l], w_in[l], q_norm_g[l], k_norm_g[l], lambda_q1[l],
                    lambda_k1[l], lambda_q2[l], lambda_k2[l], subln_g[l], b_glu[l], w_dw[l], b_dw[l],
                    conv_ln_g[l], conv_ln_b[l], w_out[l], g_ffn[l], w_group[l], b_group[l], w_router[l],
                    b_router[l], w_gate[l], w_up[l], w_down[l])
    return x2.reshape(B, S, D)
```

```python
import functools
import math

import numpy as np
import jax
import jax.numpy as jnp
from jax import lax
from jax.experimental import pallas as pl
from jax.experimental.pallas import tpu as pltpu

F32 = jnp.float32
BF16 = jnp.bfloat16

EPS = 1e-6
ROPE_THETA = 500000.0
N_DIFF_HEADS = 4
TOP_K_INNER = 2

LANES = 128
SUBLANES = 8
CONV_HALO = 32
VMEM_LIMIT = 48 * 1024 * 1024


def _cparams(sem):
    return pltpu.CompilerParams(dimension_semantics=sem, vmem_limit_bytes=VMEM_LIMIT)


def _silu(x):
    return x * jax.nn.sigmoid(x)


def _mod_kernel(c_ref, w_ref, b_ref, o_ref):
    c = c_ref[...]
    o_ref[...] = jnp.dot(_silu(c), w_ref[...], preferred_element_type=F32,
                         precision=lax.Precision.HIGHEST) + b_ref[...]


def _modulation(c, w_ada, b_ada):
    B, D = c.shape
    n_out = w_ada.shape[1]
    rows = 8
    c_pad = jnp.pad(c, ((0, rows - B), (0, 0)))
    bn = 1024
    out = pl.pallas_call(
        _mod_kernel,
        grid=(n_out // bn,),
        in_specs=[pl.BlockSpec((rows, D), lambda j: (0, 0)),
                  pl.BlockSpec((D, bn), lambda j: (0, j)),
                  pl.BlockSpec((1, bn), lambda j: (0, j))],
        out_specs=pl.BlockSpec((rows, bn), lambda j: (0, j)),
        out_shape=jax.ShapeDtypeStruct((rows, n_out), F32),
        compiler_params=_cparams(("parallel",)),
        name="adaln_mod",
    )(c_pad, w_ada, b_ada.reshape(1, n_out))
    return out[:B].reshape(B, 6, D)


def _inproj_kernel(x_ref, mod_ref, g_ref, pos_ref, wv_ref, wqk_ref, bglu_ref, gqk_ref, invf_ref,
                   q_ref, kt_ref, v_ref, u_ref, *, aw, cw, dh, rot):
    T = x_ref.shape[0]
    half = rot // 2
    x = x_ref[...]
    ms = jnp.mean(x * x, axis=-1, keepdims=True)
    sh = mod_ref[0:1, :]
    sc = mod_ref[1:2, :]
    h = x * lax.rsqrt(ms + EPS) * g_ref[...] * (1.0 + sc) + sh
    hb = h.astype(BF16)

    pv = jnp.dot(hb, wv_ref[...], preferred_element_type=F32)
    v_ref[...] = pv[:, :aw].astype(BF16)
    a = pv[:, aw:aw + cw] + bglu_ref[:, :cw]
    gate = pv[:, aw + cw:] + bglu_ref[:, cw:]
    u_ref[...] = (a * jax.nn.sigmoid(gate)).astype(BF16)

    qkt = lax.dot_general(wqk_ref[...], hb, (((1,), (1,)), ((), ())), preferred_element_type=F32)
    nch = 2 * aw // dh
    for c in range(T // LANES):
        sl = slice(c * LANES, (c + 1) * LANES)
        s3 = qkt[:, sl].reshape(nch, dh, LANES)
        ssq = jnp.mean(s3 * s3, axis=1, keepdims=True)
        y = s3 * lax.rsqrt(ssq + EPS) * gqk_ref[...].reshape(nch, dh, LANES)
        ang = invf_ref[...] * pos_ref[:, sl]
        cs = jnp.cos(ang)
        sn = jnp.sin(ang)
        t1 = y[:, 0:half, :]
        t2 = y[:, half:rot, :]
        y = jnp.concatenate([t1 * cs - t2 * sn, t2 * cs + t1 * sn, y[:, rot:, :]], axis=1)
        y2 = y.reshape(2 * aw, LANES)
        kt_ref[:, sl] = y2[aw:].astype(BF16)
        q_ref[sl, :] = y2[:aw].T.astype(BF16)


def _inproj(x2, mod, g_mix, pos_row, w_vglu, w_qk_t, b_glu, gqk_tab, invf_tab, *, B, S, aw, cw, dh, rot, T):
    N, D = x2.shape
    nS = S // T
    kern = functools.partial(_inproj_kernel, aw=aw, cw=cw, dh=dh, rot=rot)
    return pl.pallas_call(
        kern,
        grid=(N // T,),
        in_specs=[pl.BlockSpec((T, D), lambda i: (i, 0)),
                  pl.BlockSpec((None, 6, D), lambda i: (i // nS, 0, 0)),
                  pl.BlockSpec((1, D), lambda i: (0, 0)),
                  pl.BlockSpec((1, T), lambda i: (0, i)),
                  pl.BlockSpec(w_vglu.shape, lambda i: (0, 0)),
                  pl.BlockSpec(w_qk_t.shape, lambda i: (0, 0)),
                  pl.BlockSpec((1, 2 * cw), lambda i: (0, 0)),
                  pl.BlockSpec(gqk_tab.shape, lambda i: (0, 0)),
                  pl.BlockSpec(invf_tab.shape, lambda i: (0, 0))],
        out_specs=[pl.BlockSpec((T, aw), lambda i: (i, 0)),
                   pl.BlockSpec((None, aw, T), lambda i: (i // nS, 0, i % nS)),
                   pl.BlockSpec((T, aw), lambda i: (i, 0)),
                   pl.BlockSpec((T, cw), lambda i: (i, 0))],
        out_shape=[jax.ShapeDtypeStruct((N, aw), BF16),
                   jax.ShapeDtypeStruct((B, aw, S), BF16),
                   jax.ShapeDtypeStruct((N, aw), BF16),
                   jax.ShapeDtypeStruct((N, cw), BF16)],
        compiler_params=_cparams(("parallel",)),
        name="inproj_qknorm_rope_glu",
    )(x2, mod, g_mix, pos_row, w_vglu, w_qk_t, b_glu, gqk_tab, invf_tab)


def _attn_kernel(q_ref, kt_ref, v_ref, lq1_ref, lk1_ref, lq2_ref, lk2_ref, sg_ref, o_ref, acc_ref,
                 sa_ref, sb_ref, *, tile, dh, lambda_init):
    i = pl.program_id(2)
    hd = 2 * dh
    half = tile // 2

    def stacked(q):
        lane = lax.broadcasted_iota(jnp.int32, q.shape, 1)
        zero = jnp.zeros_like(q)
        return jnp.concatenate([jnp.where(lane < dh, q, zero), jnp.where(lane >= dh, q, zero)], axis=0)

    qa = stacked(q_ref[0:half, :])
    qb = stacked(q_ref[half:, :])

    acc_ref[...] = jnp.zeros_like(acc_ref)
    ones = {w: jnp.ones((w, hd), BF16) for w in (half, tile)}

    def scores(t, lhs, s_ref):
        start = pl.multiple_of(t * tile, tile)
        s_ref[...] = jnp.dot(lhs, kt_ref[:, pl.ds(start, tile)], preferred_element_type=F32)

    def softmax_pv(t, s_ref, c, m, width=tile, diagonal=False):
        start = pl.multiple_of(t * tile, tile)
        vt = jnp.concatenate([v_ref[pl.ds(start, width), :], ones[width]], axis=1)
        s = s_ref[:, :width]
        if diagonal:
            row = lax.broadcasted_iota(jnp.int32, (2 * half, half), 0)
            col = lax.broadcasted_iota(jnp.int32, (2 * half, half), 1)
            qrow = jnp.where(row >= half, row - half, row)
            tail = jnp.where(col <= qrow, s[:, width - half:], -jnp.inf)
            s = tail if width == half else jnp.concatenate([s[:, :width - half], tail], axis=1)
        m_new = jnp.maximum(m, jnp.max(s, axis=1, keepdims=True))
        alpha = jnp.exp2(m - m_new)
        p = jnp.exp2(s - m_new).astype(BF16)
        acc_ref[c] = alpha * acc_ref[c] + jnp.dot(p, vt, preferred_element_type=F32)
        return m_new

    m0 = jnp.full((2 * half, 1), -jnp.inf, F32)
    scores(0, qa, sa_ref)

    def full_tile(t, carry):
        ma, mb = carry
        scores(t, qb, sb_ref)
        ma = softmax_pv(t, sa_ref, 0, ma)
        scores(t + 1, qa, sa_ref)
        mb = softmax_pv(t, sb_ref, 1, mb)
        return ma, mb

    ma, mb = lax.fori_loop(0, i, full_tile, (m0, m0))

    scores(i, qb, sb_ref)
    softmax_pv(i, sa_ref, 0, ma, width=half, diagonal=True)
    softmax_pv(i, sb_ref, 1, mb, width=tile, diagonal=True)

    lam = (jnp.exp(jnp.sum(lq1_ref[...] * lk1_ref[...], axis=1, keepdims=True))
           - jnp.exp(jnp.sum(lq2_ref[...] * lk2_ref[...], axis=1, keepdims=True)) + lambda_init)
    for c in range(2):
        o = (acc_ref[c, 0:half, 0:hd] / acc_ref[c, 0:half, hd:]
             - lam * (acc_ref[c, half:, 0:hd] / acc_ref[c, half:, hd:]))
        ms = jnp.mean(o * o, axis=1, keepdims=True)
        o = o * lax.rsqrt(ms + EPS) * sg_ref[...] * (1.0 - lambda_init)
        o_ref[c * half:(c + 1) * half, :] = o.astype(o_ref.dtype)


def _diff_attention(q, kt, v, lq1, lk1, lq2, lk2, subln_g, *, B, S, H, dh, lambda_init, tile):
    aw = H * 2 * dh
    q3 = q.reshape(B, S, aw)
    v3 = v.reshape(B, S, aw)
    hd = 2 * dh
    tile = min(tile, S)
    kern = functools.partial(_attn_kernel, tile=tile, dh=dh, lambda_init=lambda_init)
    vec = pl.BlockSpec((1, dh), lambda b, h, i: (0, 0))
    out = pl.pallas_call(
        kern,
        grid=(B, H, S // tile),
        in_specs=[pl.BlockSpec((None, tile, hd), lambda b, h, i: (b, i, h)),
                  pl.BlockSpec((None, hd, S), lambda b, h, i: (b, h, 0)),
                  pl.BlockSpec((None, S, hd), lambda b, h, i: (b, 0, h)),
                  vec, vec, vec, vec,
                  pl.BlockSpec((1, hd), lambda b, h, i: (0, 0))],
        out_specs=pl.BlockSpec((None, tile, hd), lambda b, h, i: (b, i, h)),
        out_shape=jax.ShapeDtypeStruct((B, S, aw), BF16),
        scratch_shapes=[pltpu.VMEM((2, tile, 2 * hd), F32),
                        pltpu.VMEM((tile, tile), F32),
                        pltpu.VMEM((tile, tile), F32)],
        compiler_params=_cparams(("parallel", "parallel", "parallel")),
        name="diff_flash_attention",
    )(q3, kt, v3, lq1, lk1, lq2, lk2, subln_g)
    return out.reshape(B * S, aw)


def _mixout_kernel(attn_ref, ucur_ref, uhalo_ref, x_ref, mod_ref, wdw_ref, bdw_ref, lng_ref, lnb_ref,
                   wo1_ref, wo2_ref, gffn_ref, wrh_ref, wrl_ref, br_ref,
                   x1_ref, h2_ref, route_ref, route_t_ref, cnt_ref, ubuf_ref, conv_ref,
                   *, nS, conv_k, n_exp, n_grp):
    T = x_ref.shape[0]
    cw = ucur_ref.shape[1]
    i = pl.program_id(0)

    @pl.when(i == 0)
    def _():
        cnt_ref[...] = jnp.zeros_like(cnt_ref)

    first = (i % nS) == 0
    halo = uhalo_ref[...].astype(F32)
    ubuf_ref[0, 0:CONV_HALO, :] = jnp.where(first, jnp.zeros_like(halo), halo)
    ubuf_ref[0, CONV_HALO:, :] = ucur_ref[...].astype(F32)
    span = T + CONV_HALO - SUBLANES
    for b in range(1, SUBLANES):
        ubuf_ref[b, 0:span, :] = ubuf_ref[0, b:b + span, :]

    off = CONV_HALO - (conv_k - 1)
    rows = 32
    for r0 in range(0, T, rows):
        acc = jnp.zeros((rows // SUBLANES, SUBLANES, cw), F32)
        for j in range(conv_k):
            a, b = divmod(j + off, SUBLANES)
            lo_r = r0 + a * SUBLANES
            acc = acc + (wdw_ref[j * SUBLANES:(j + 1) * SUBLANES, :]
                         * ubuf_ref[b, lo_r:lo_r + rows, :].reshape(rows // SUBLANES, SUBLANES, cw))
        y = acc.reshape(rows, cw) + bdw_ref[...]
        mu = jnp.mean(y, axis=1, keepdims=True)
        d = y - mu
        var = jnp.mean(d * d, axis=1, keepdims=True)
        z = d * lax.rsqrt(var + EPS) * lng_ref[...] + lnb_ref[...]
        conv_ref[r0:r0 + rows, :] = _silu(z).astype(BF16)

    yo = (jnp.dot(attn_ref[...], wo1_ref[...], preferred_element_type=F32)
          + jnp.dot(conv_ref[...], wo2_ref[...], preferred_element_type=F32))
    x1 = x_ref[...] + mod_ref[2:3, :] * yo
    x1_ref[...] = x1
    ms = jnp.mean(x1 * x1, axis=1, keepdims=True)
    h2 = x1 * lax.rsqrt(ms + EPS) * gffn_ref[...] * (1.0 + mod_ref[4:5, :]) + mod_ref[3:4, :]
    h2_ref[...] = h2
    hi = h2.astype(BF16)
    lo = (h2 - hi.astype(F32)).astype(BF16)

    logits = (jnp.dot(hi, wrh_ref[...], preferred_element_type=F32)
              + jnp.dot(lo, wrh_ref[...], preferred_element_type=F32)
              + jnp.dot(hi, wrl_ref[...], preferred_element_type=F32)) + br_ref[...]
    lane_i = lax.broadcasted_iota(jnp.int32, logits.shape, 1)
    lane = lane_i.astype(F32)
    big = jnp.float32(1e9)
    ninf = jnp.float32(-jnp.inf)
    is_g = (lane_i >= n_exp) & (lane_i < n_exp + n_grp)
    gl = jnp.where(is_g, logits, ninf)
    gmax = jnp.max(gl, axis=1, keepdims=True)
    gsum = jnp.sum(jnp.where(is_g, jnp.exp(gl - gmax), 0.0), axis=1, keepdims=True)
    g_p = 1.0 / gsum
    gidx = jnp.min(jnp.where(gl == gmax, lane, big), axis=1, keepdims=True) - n_exp
    epg = n_exp // n_grp
    lo_l = gidx * epg
    in_grp = (lane >= lo_l) & (lane < lo_l + epg)
    el = jnp.where(in_grp, logits, ninf)
    m1 = jnp.max(el, axis=1, keepdims=True)
    i1 = jnp.min(jnp.where(el == m1, lane, big), axis=1, keepdims=True)
    el2 = jnp.where(lane == i1, ninf, el)
    m2 = jnp.max(el2, axis=1, keepdims=True)
    i2 = jnp.min(jnp.where(el2 == m2, lane, big), axis=1, keepdims=True)
    e2 = jnp.exp(m2 - m1)
    p1 = 1.0 / (1.0 + e2)
    p2 = e2 / (1.0 + e2)

    sel = (lane == i1) | (lane == i2)
    rr = lax.broadcasted_iota(jnp.int32, (T, T), 0)
    cc = lax.broadcasted_iota(jnp.int32, (T, T), 1)
    before = jnp.dot((rr > cc).astype(BF16), sel.astype(BF16), preferred_element_type=F32) + cnt_ref[...]
    r1 = jnp.sum(jnp.where(lane == i1, before, 0.0), axis=1, keepdims=True)
    r2 = jnp.sum(jnp.where(lane == i2, before, 0.0), axis=1, keepdims=True)
    cnt_ref[...] += jnp.sum(sel.astype(F32), axis=0, keepdims=True)

    route = jnp.zeros_like(logits)
    for k, val in enumerate((i1, i2, g_p * p1, g_p * p2, r1, r2)):
        route = jnp.where(lane_i == k, val, route)
    route_ref[...] = route
    route_t_ref[...] = route.T[:SUBLANES]


def _mixout(attn, u, x2, mod, w_dw, b_dw, ln_g, ln_b, wo1, wo2, g_ffn, wr_hi, wr_lo, b_r,
            *, S, T, n_exp, n_grp):
    N, D = x2.shape
    aw = attn.shape[1]
    cw = u.shape[1]
    nS = S // T
    conv_k = w_dw.shape[0]
    w_dw = jnp.repeat(w_dw, SUBLANES, axis=0)
    hb = T // CONV_HALO
    kern = functools.partial(_mixout_kernel, nS=nS, conv_k=conv_k, n_exp=n_exp, n_grp=n_grp)
    full = lambda a: pl.BlockSpec(a.shape, lambda i: (0, 0))
    return pl.pallas_call(
        kern,
        grid=(N // T,),
        in_specs=[pl.BlockSpec((T, aw), lambda i: (i, 0)),
                  pl.BlockSpec((T, cw), lambda i: (i, 0)),
                  pl.BlockSpec((CONV_HALO, cw), lambda i: (jnp.maximum(i * hb - 1, 0), 0)),
                  pl.BlockSpec((T, D), lambda i: (i, 0)),
                  pl.BlockSpec((None, 6, D), lambda i: (i // nS, 0, 0)),
                  full(w_dw), full(b_dw), full(ln_g), full(ln_b), full(wo1), full(wo2), full(g_ffn),
                  full(wr_hi), full(wr_lo), full(b_r)],
        out_specs=[pl.BlockSpec((T, D), lambda i: (i, 0)),
                   pl.BlockSpec((T, D), lambda i: (i, 0)),
                   pl.BlockSpec((T, LANES), lambda i: (i, 0)),
                   pl.BlockSpec((SUBLANES, T), lambda i: (0, i)),
                   pl.BlockSpec((1, LANES), lambda i: (0, 0))],
        out_shape=[jax.ShapeDtypeStruct((N, D), F32),
                   jax.ShapeDtypeStruct((N, D), F32),
                   jax.ShapeDtypeStruct((N, LANES), F32),
                   jax.ShapeDtypeStruct((SUBLANES, N), F32),
                   jax.ShapeDtypeStruct((1, LANES), F32)],
        scratch_shapes=[pltpu.VMEM((SUBLANES, CONV_HALO + T, cw), F32),
                        pltpu.VMEM((T, cw), BF16)],
        compiler_params=_cparams(("arbitrary",)),
        name="conv_outproj_router",
    )(attn, u, u, x2, mod, w_dw, b_dw, ln_g, ln_b, wo1, wo2, g_ffn, wr_hi, wr_lo, b_r)


def _row_of(ref, r):
    if isinstance(r, int):
        return ref.at[r // SUBLANES, pl.ds(r % SUBLANES, 1)]
    return ref.at[lax.shift_right_logical(r, 3), pl.ds(r & (SUBLANES - 1), 1)]


def _scatter_kernel(dest_ref, h2_ref, xs_hbm, sem):
    groups = h2_ref.shape[0]
    tokens = groups * SUBLANES

    def issue(g, carry):
        for j in range(SUBLANES):
            for k in range(TOP_K_INNER):
                d = dest_ref[0, k * tokens + g * SUBLANES + j]
                pltpu.make_async_copy(h2_ref.at[g, pl.ds(j, 1)], _row_of(xs_hbm, d), sem).start(priority=k % 2)
        return carry

    lax.fori_loop(0, groups, issue, 0)
    for k in range(TOP_K_INNER):
        pltpu.make_async_copy(h2_ref, xs_hbm.at[pl.ds(0, groups)], sem).wait()


def _dest_table(dest, tokens):
    tab = jnp.concatenate([d.reshape(-1, tokens) for d in dest], axis=1)
    return tab.reshape(tab.shape[0], 1, tab.shape[1])


def _scatter_rows(h2, dest, *, tokens):
    N, C = h2.shape
    steps = N // tokens
    dest3 = _dest_table(dest, tokens)
    rows = N * TOP_K_INNER
    xs = pl.pallas_call(
        _scatter_kernel,
        grid=(steps,),
        in_specs=[pl.BlockSpec((None, 1, TOP_K_INNER * tokens), lambda s: (s, 0, 0), memory_space=pltpu.SMEM),
                  pl.BlockSpec((tokens // SUBLANES, SUBLANES, C), lambda s: (s, 0, 0))],
        out_specs=pl.BlockSpec(memory_space=pl.ANY),
        out_shape=jax.ShapeDtypeStruct((rows // SUBLANES, SUBLANES, C), h2.dtype),
        scratch_shapes=[pltpu.SemaphoreType.DMA],
        compiler_params=_cparams(("arbitrary",)),
        name="moe_scatter_rows",
    )(dest3, h2.reshape(N // SUBLANES, SUBLANES, C))
    return xs.reshape(rows, C)


XS_RING = 3


def _experts_kernel(vt_ref, ve_ref, vlo_ref, vhi_ref, vnew_ref, vnext_ref, vslot_ref,
                    xs_hbm, wg_hbm, wu_hbm, wd_hbm, ys_ref, xbuf, wg_buf, wu_buf, wd_buf, sem, wsem,
                    *, n_tiles):
    v = pl.program_id(0)
    lo = vlo_ref[v]
    hi = vhi_ref[v]
    t = vt_ref[v]
    tm = xbuf.shape[1]
    wslot = vslot_ref[v]

    def weight_copies(expert, slot):
        return [pltpu.make_async_copy(hbm.at[expert], buf.at[slot], wsem.at[slot, j])
                for j, (hbm, buf) in enumerate(((wg_hbm, wg_buf), (wu_hbm, wu_buf), (wd_hbm, wd_buf)))]

    @pl.when(v == 0)
    def _():
        for c in weight_copies(ve_ref[0], 0):
            c.start()

    @pl.when(vnew_ref[v] == 1)
    def _():
        @pl.when(vnext_ref[v] >= 0)
        def _():
            for c in weight_copies(vnext_ref[v], 1 - wslot):
                c.start()

        for c in weight_copies(ve_ref[v], wslot):
            c.wait()

    def tile_copy(tile):
        slot = lax.rem(tile, XS_RING)
        return pltpu.make_async_copy(xs_hbm.at[pl.ds(pl.multiple_of(tile * tm, tm), tm)], xbuf.at[slot],
                                     sem.at[slot])

    @pl.when(v == 0)
    def _():
        for tile in range(min(XS_RING - 1, n_tiles)):
            tile_copy(tile).start()

    @pl.when((lo == 0) & (hi > lo))
    def _():
        @pl.when(t + (XS_RING - 1) < n_tiles)
        def _():
            tile_copy(t + (XS_RING - 1)).start()

        tile_copy(t).wait()

    @pl.when(hi > lo)
    def _():
        x = xbuf[lax.rem(t, XS_RING)].astype(BF16)
        g = jnp.dot(x, wg_buf[wslot].astype(BF16), preferred_element_type=F32)
        u = jnp.dot(x, wu_buf[wslot].astype(BF16), preferred_element_type=F32)
        hid = (_silu(g) * u).astype(BF16)
        y = jnp.dot(hid, wd_buf[wslot].astype(BF16), preferred_element_type=F32)

        @pl.when(lo == 0)
        def _():
            ys_ref[...] = y

        @pl.when(lo > 0)
        def _():
            row = lax.broadcasted_iota(jnp.int32, y.shape, 0)
            ys_ref[...] = jnp.where((row >= lo) & (row < hi), y, ys_ref[...])


def _experts(xs, w_gate, w_up, w_down, visits, *, tm):
    R, C = xs.shape
    E, D, ff = w_gate.shape
    vt = visits[0]
    any_spec = pl.BlockSpec(memory_space=pl.ANY)
    grid_spec = pltpu.PrefetchScalarGridSpec(
        num_scalar_prefetch=len(visits),
        grid=(vt.shape[0],),
        in_specs=[any_spec, any_spec, any_spec, any_spec],
        out_specs=pl.BlockSpec((tm, C), lambda v, vt, *_: (vt[v], 0)),
        scratch_shapes=[pltpu.VMEM((XS_RING, tm, C), F32),
                        pltpu.VMEM((2, D, ff), w_gate.dtype),
                        pltpu.VMEM((2, D, ff), w_up.dtype),
                        pltpu.VMEM((2, ff, D), w_down.dtype),
                        pltpu.SemaphoreType.DMA((XS_RING,)),
                        pltpu.SemaphoreType.DMA((2, 3))],
    )
    return pl.pallas_call(
        functools.partial(_experts_kernel, n_tiles=R // tm),
        grid_spec=grid_spec,
        out_shape=jax.ShapeDtypeStruct((R, C), F32),
        compiler_params=_cparams(("arbitrary",)),
        name="moe_grouped_experts",
    )(*visits, xs, w_gate, w_up, w_down)


def _visit_tables(off, cnt, n_rows, tm):
    n_tiles = n_rows // tm
    n_exp = off.shape[0]
    n_visits = n_tiles + n_exp - 1
    tile_starts = jnp.arange(n_tiles, dtype=jnp.int32) * tm
    seg_starts = jnp.where((cnt > 0) & (off % tm != 0), off, n_rows)
    starts = jnp.sort(jnp.concatenate([tile_starts, seg_starts]))
    lo_abs = starts[:n_visits]
    hi_abs = starts[1:n_visits + 1]
    valid = lo_abs < n_rows
    tile = jnp.where(valid, lo_abs // tm, n_tiles - 1)
    ends = off + cnt
    probe = jnp.where(valid, lo_abs, n_rows - 1)
    expert = jnp.sum(ends[None, :] <= probe[:, None], axis=1).astype(jnp.int32)
    row_lo = jnp.where(valid, lo_abs - tile * tm, 0)
    row_hi = jnp.where(valid, hi_abs - tile * tm, 0)
    new = jnp.concatenate([jnp.ones((1,), jnp.int32), (expert[1:] != expert[:-1]).astype(jnp.int32)])
    slot = (jnp.cumsum(new) - 1) % 2
    nxt = jnp.sum(expert[None, :] <= expert[:, None], axis=1)
    nxt_expert = jnp.where(nxt < n_visits, expert[jnp.minimum(nxt, n_visits - 1)], -1)
    return tile, expert, row_lo, row_hi, new, nxt_expert.astype(jnp.int32), slot.astype(jnp.int32)


def _combine_kernel(dcur_ref, dnxt_ref, ys_hbm, route_ref, x1_ref, mod_ref, o_ref, ybuf, sem):
    T = x1_ref.shape[0]
    i = pl.program_id(0)
    slot = lax.rem(i, 2)

    def gather(d_ref, sl):
        def body(g, carry):
            for j in range(SUBLANES):
                for k in range(TOP_K_INNER):
                    d = d_ref[0, k * T + g * SUBLANES + j]
                    pltpu.make_async_copy(_row_of(ys_hbm, d), ybuf.at[sl, k, g, pl.ds(j, 1)],
                                          sem.at[sl]).start(priority=k % 2)
            return carry
        lax.fori_loop(0, T // SUBLANES, body, 0)

    @pl.when(i == 0)
    def _():
        gather(dcur_ref, 0)

    @pl.when(i + 1 < pl.num_programs(0))
    def _():
        gather(dnxt_ref, 1 - slot)

    for k in range(TOP_K_INNER):
        pltpu.make_async_copy(ys_hbm.at[pl.ds(0, T // SUBLANES)], ybuf.at[slot, k], sem.at[slot]).wait()

    route = route_ref[...]
    moe = jnp.zeros(x1_ref.shape, F32)
    for k in range(TOP_K_INNER):
        moe = moe + route[:, TOP_K_INNER + k:TOP_K_INNER + k + 1] * ybuf[slot, k].reshape(x1_ref.shape)
    o_ref[...] = x1_ref[...] + mod_ref[5:6, :] * moe


def _combine(ys, dest, route, x1, mod, *, S, T):
    N, D = x1.shape
    C = ys.shape[1]
    steps = N // T
    nS = S // T
    dest3 = _dest_table(dest, T)
    dspec = lambda f: pl.BlockSpec((None, 1, TOP_K_INNER * T), f, memory_space=pltpu.SMEM)
    return pl.pallas_call(
        _combine_kernel,
        grid=(steps,),
        in_specs=[dspec(lambda i: (i, 0, 0)),
                  dspec(lambda i: (jnp.minimum(i + 1, steps - 1), 0, 0)),
                  pl.BlockSpec(memory_space=pl.ANY),
                  pl.BlockSpec((T, LANES), lambda i: (i, 0)),
                  pl.BlockSpec((T, D), lambda i: (i, 0)),
                  pl.BlockSpec((None, 6, D), lambda i: (i // nS, 0, 0))],
        out_specs=pl.BlockSpec((T, D), lambda i: (i, 0)),
        out_shape=jax.ShapeDtypeStruct((N, D), F32),
        scratch_shapes=[pltpu.VMEM((2, TOP_K_INNER, T // SUBLANES, SUBLANES, C), F32),
                        pltpu.SemaphoreType.DMA((2,))],
        compiler_params=_cparams(("arbitrary",)),
        name="moe_gather_combine",
    )(dest3, dest3, ys.reshape(ys.shape[0] // SUBLANES, SUBLANES, C), route, x1, mod)


def _layer(x2, mod, pos_row, l, B, S, g_mix, w_in, q_norm_g, k_norm_g, lambda_q1, lambda_k1, lambda_q2,
           lambda_k2, subln_g, b_glu, w_dw, b_dw, conv_ln_g, conv_ln_b, w_out, g_ffn, w_group, b_group,
           w_router, b_router, w_gate, w_up, w_down):
    N, D = x2.shape
    dh = q_norm_g.shape[0]
    H = N_DIFF_HEADS
    aw = H * 2 * dh
    cw = w_dw.shape[1]
    rot = dh // 4
    n_grp = w_group.shape[1]
    n_exp = w_router.shape[1]
    lambda_init = 0.8 - 0.6 * math.exp(-0.3 * l)

    w_qk_t = w_in[:, :2 * aw].T.astype(BF16)
    w_vglu = w_in[:, 2 * aw:].astype(BF16)
    scale = dh ** -0.5 * math.log2(math.e)
    gq = jnp.tile(q_norm_g * scale, aw // dh)
    gk = jnp.tile(k_norm_g, aw // dh)
    gqk_tab = jnp.broadcast_to(jnp.concatenate([gq, gk])[:, None], (2 * aw, LANES))
    inv_freq = ROPE_THETA ** (-jnp.arange(0, rot, 2, dtype=F32) / rot)
    invf_tab = jnp.broadcast_to(inv_freq[:, None], (rot // 2, LANES))

    q, kt, v, u = _inproj(x2, mod, g_mix.reshape(1, D), pos_row, w_vglu, w_qk_t, b_glu.reshape(1, 2 * cw),
                          gqk_tab, invf_tab, B=B, S=S, aw=aw, cw=cw, dh=dh, rot=rot, T=512)

    attn = _diff_attention(q, kt, v, lambda_q1.reshape(1, dh), lambda_k1.reshape(1, dh),
                           lambda_q2.reshape(1, dh), lambda_k2.reshape(1, dh), subln_g.reshape(1, 2 * dh),
                           B=B, S=S, H=H, dh=dh, lambda_init=lambda_init, tile=1024)

    w_r = jnp.zeros((D, LANES), F32).at[:, :n_exp].set(w_router).at[:, n_exp:n_exp + n_grp].set(w_group)
    b_r = jnp.zeros((1, LANES), F32).at[0, :n_exp].set(b_router).at[0, n_exp:n_exp + n_grp].set(b_group)
    wr_hi = w_r.astype(BF16)
    wr_lo = (w_r - wr_hi.astype(F32)).astype(BF16)
    wo = w_out.astype(BF16)
    x1, h2, route, route_t, counts = _mixout(attn, u, x2, mod, w_dw, b_dw.reshape(1, cw), conv_ln_g.reshape(1, cw),
                                     conv_ln_b.reshape(1, cw), wo[:aw], wo[aw:], g_ffn.reshape(1, D),
                                     wr_hi, wr_lo, b_r, S=S, T=512, n_exp=n_exp, n_grp=n_grp)

    cnt = counts[0, :n_exp].astype(jnp.int32)
    off = jnp.cumsum(cnt) - cnt
    experts = jnp.arange(n_exp, dtype=jnp.int32)[:, None]
    dest = [jnp.sum(jnp.where(route_t[k].astype(jnp.int32)[None, :] == experts, off[:, None], 0), axis=0)
            + route_t[4 + k].astype(jnp.int32) for k in range(TOP_K_INNER)]
    tm = 256
    visits = _visit_tables(off, cnt, N * TOP_K_INNER, tm)

    xs = _scatter_rows(h2, dest, tokens=min(1024, N))
    ys = _experts(xs, w_gate, w_up, w_down, visits, tm=tm)
    return _combine(ys, dest, route, x1, mod, S=S, T=256)


def kernel(x, c, positions, w_ada, b_ada, g_mix, w_in, q_norm_g, k_norm_g, lambda_q1, lambda_k1, lambda_q2,
           lambda_k2, subln_g, b_glu, w_dw, b_dw, conv_ln_g, conv_ln_b, w_out, g_ffn, w_group, b_group,
           w_router, b_router, w_gate, w_up, w_down):
    B, S, D = x.shape
    depth = w_ada.shape[0]
    x2 = x.reshape(B * S, D)
    pos_row = positions.astype(F32).reshape(1, B * S)
    for l in range(depth):
        mod = _modulation(c, w_ada[l], b_ada[l])
        x2 = _layer(x2, mod, pos_row, l, B, S, g_mix[l], w_in[l], q_norm_g[l], k_norm_g[l], lambda_q1[l],
                    lambda_k1[l], lambda_q2[l], lambda_k2[l], subln_g[l], b_glu[l], w_dw[l], b_dw[l],
                    conv_ln_g[l], conv_ln_b[l], w_out[l], g_ffn[l], w_group[l], b_group[l], w_router[l],
                    b_router[l], w_gate[l], w_up[l], w_down[l])
    return x2.reshape(B, S, D)
```

```python
import functools
import math

import numpy as np
import jax
import jax.numpy as jnp
from jax import lax
from jax.experimental import pallas as pl
from jax.experimental.pallas import tpu as pltpu

F32 = jnp.float32
BF16 = jnp.bfloat16

EPS = 1e-6
ROPE_THETA = 500000.0
N_DIFF_HEADS = 4
TOP_K_INNER = 2

LANES = 128
SUBLANES = 8
CONV_HALO = 32
VMEM_LIMIT = 48 * 1024 * 1024


def _cparams(sem):
    return pltpu.CompilerParams(dimension_semantics=sem, vmem_limit_bytes=VMEM_LIMIT)


def _silu(x):
    return x * jax.nn.sigmoid(x)


def _mod_kernel(c_ref, w_ref, b_ref, o_ref):
    c = c_ref[...]
    o_ref[...] = jnp.dot(_silu(c), w_ref[...], preferred_element_type=F32,
                         precision=lax.Precision.HIGHEST) + b_ref[...]


def _modulation(c, w_ada, b_ada):
    B, D = c.shape
    n_out = w_ada.shape[1]
    rows = 8
    c_pad = jnp.pad(c, ((0, rows - B), (0, 0)))
    bn = 1024
    out = pl.pallas_call(
        _mod_kernel,
        grid=(n_out // bn,),
        in_specs=[pl.BlockSpec((rows, D), lambda j: (0, 0)),
                  pl.BlockSpec((D, bn), lambda j: (0, j)),
                  pl.BlockSpec((1, bn), lambda j: (0, j))],
        out_specs=pl.BlockSpec((rows, bn), lambda j: (0, j)),
        out_shape=jax.ShapeDtypeStruct((rows, n_out), F32),
        compiler_params=_cparams(("parallel",)),
        name="adaln_mod",
    )(c_pad, w_ada, b_ada.reshape(1, n_out))
    return out[:B].reshape(B, 6, D)


def _inproj_kernel(x_ref, mod_ref, g_ref, pos_ref, wv_ref, wqk_ref, bglu_ref, gqk_ref, invf_ref,
                   q_ref, kt_ref, v_ref, u_ref, *, aw, cw, dh, rot):
    T = x_ref.shape[0]
    half = rot // 2
    x = x_ref[...]
    ms = jnp.mean(x * x, axis=-1, keepdims=True)
    sh = mod_ref[0:1, :]
    sc = mod_ref[1:2, :]
    h = x * lax.rsqrt(ms + EPS) * g_ref[...] * (1.0 + sc) + sh
    hb = h.astype(BF16)

    pv = jnp.dot(hb, wv_ref[...], preferred_element_type=F32)
    v_ref[...] = pv[:, :aw].astype(BF16)
    a = pv[:, aw:aw + cw] + bglu_ref[:, :cw]
    gate = pv[:, aw + cw:] + bglu_ref[:, cw:]
    u_ref[...] = (a * jax.nn.sigmoid(gate)).astype(BF16)

    qkt = lax.dot_general(wqk_ref[...], hb, (((1,), (1,)), ((), ())), preferred_element_type=F32)
    nch = 2 * aw // dh
    for c in range(T // LANES):
        sl = slice(c * LANES, (c + 1) * LANES)
        s3 = qkt[:, sl].reshape(nch, dh, LANES)
        ssq = jnp.mean(s3 * s3, axis=1, keepdims=True)
        y = s3 * lax.rsqrt(ssq + EPS) * gqk_ref[...].reshape(nch, dh, LANES)
        ang = invf_ref[...] * pos_ref[:, sl]
        cs = jnp.cos(ang)
        sn = jnp.sin(ang)
        t1 = y[:, 0:half, :]
        t2 = y[:, half:rot, :]
        y = jnp.concatenate([t1 * cs - t2 * sn, t2 * cs + t1 * sn, y[:, rot:, :]], axis=1)
        y2 = y.reshape(2 * aw, LANES)
        kt_ref[:, sl] = y2[aw:].astype(BF16)
        q_ref[sl, :] = y2[:aw].T.astype(BF16)


def _inproj(x2, mod, g_mix, pos_row, w_vglu, w_qk_t, b_glu, gqk_tab, invf_tab, *, B, S, aw, cw, dh, rot, T):
    N, D = x2.shape
    nS = S // T
    kern = functools.partial(_inproj_kernel, aw=aw, cw=cw, dh=dh, rot=rot)
    return pl.pallas_call(
        kern,
        grid=(N // T,),
        in_specs=[pl.BlockSpec((T, D), lambda i: (i, 0)),
                  pl.BlockSpec((None, 6, D), lambda i: (i // nS, 0, 0)),
                  pl.BlockSpec((1, D), lambda i: (0, 0)),
                  pl.BlockSpec((1, T), lambda i: (0, i)),
                  pl.BlockSpec(w_vglu.shape, lambda i: (0, 0)),
                  pl.BlockSpec(w_qk_t.shape, lambda i: (0, 0)),
                  pl.BlockSpec((1, 2 * cw), lambda i: (0, 0)),
                  pl.BlockSpec(gqk_tab.shape, lambda i: (0, 0)),
                  pl.BlockSpec(invf_tab.shape, lambda i: (0, 0))],
        out_specs=[pl.BlockSpec((T, aw), lambda i: (i, 0)),
                   pl.BlockSpec((None, aw, T), lambda i: (i // nS, 0, i % nS)),
                   pl.BlockSpec((T, aw), lambda i: (i, 0)),
                   pl.BlockSpec((T, cw), lambda i: (i, 0))],
        out_shape=[jax.ShapeDtypeStruct((N, aw), BF16),
                   jax.ShapeDtypeStruct((B, aw, S), BF16),
                   jax.ShapeDtypeStruct((N, aw), BF16),
                   jax.ShapeDtypeStruct((N, cw), BF16)],
        compiler_params=_cparams(("parallel",)),
        name="inproj_qknorm_rope_glu",
    )(x2, mod, g_mix, pos_row, w_vglu, w_qk_t, b_glu, gqk_tab, invf_tab)


def _attn_kernel(q_ref, kt_ref, v_ref, lq1_ref, lk1_ref, lq2_ref, lk2_ref, sg_ref, o_ref, acc_ref, s_ref,
                 *, tile, chains, dh, lambda_init):
    i = pl.program_id(2)
    hd = 2 * dh
    rows = tile // chains

    def stacked(q):
        lane = lax.broadcasted_iota(jnp.int32, q.shape, 1)
        zero = jnp.zeros_like(q)
        return jnp.concatenate([jnp.where(lane < dh, q, zero), jnp.where(lane >= dh, q, zero)], axis=0)

    qs = [stacked(q_ref[c * rows:(c + 1) * rows, :]) for c in range(chains)]

    acc_ref[...] = jnp.zeros_like(acc_ref)
    ones = {w: jnp.ones((w, hd), BF16) for w in {tile} | {(c + 1) * rows for c in range(chains)}}

    def scores(t, c):
        start = pl.multiple_of(t * tile, tile)
        s_ref[c] = jnp.dot(qs[c], kt_ref[:, pl.ds(start, tile)], preferred_element_type=F32)

    def softmax_pv(t, c, m, width=tile, diagonal=False):
        start = pl.multiple_of(t * tile, tile)
        vt = jnp.concatenate([v_ref[pl.ds(start, width), :], ones[width]], axis=1)
        s = s_ref[c, :, :width]
        if diagonal:
            row = lax.broadcasted_iota(jnp.int32, (2 * rows, rows), 0)
            col = lax.broadcasted_iota(jnp.int32, (2 * rows, rows), 1)
            qrow = jnp.where(row >= rows, row - rows, row)
            tail = jnp.where(col <= qrow, s[:, width - rows:], -jnp.inf)
            s = tail if width == rows else jnp.concatenate([s[:, :width - rows], tail], axis=1)
        m_new = jnp.maximum(m, jnp.max(s, axis=1, keepdims=True))
        alpha = jnp.exp2(m - m_new)
        p = jnp.exp2(s - m_new).astype(BF16)
        acc_ref[c] = alpha * acc_ref[c] + jnp.dot(p, vt, preferred_element_type=F32)
        return m_new

    m0 = jnp.full((2 * rows, 1), -jnp.inf, F32)
    scores(0, 0)

    def full_tile(t, ms):
        ms = list(ms)
        for c in range(chains):
            if c + 1 < chains:
                scores(t, c + 1)
            else:
                scores(t + 1, 0)
            ms[c] = softmax_pv(t, c, ms[c])
        return tuple(ms)

    ms = lax.fori_loop(0, i, full_tile, (m0,) * chains)

    for c in range(chains):
        if c + 1 < chains:
            scores(i, c + 1)
        softmax_pv(i, c, ms[c], width=(c + 1) * rows, diagonal=True)

    lam = (jnp.exp(jnp.sum(lq1_ref[...] * lk1_ref[...], axis=1, keepdims=True))
           - jnp.exp(jnp.sum(lq2_ref[...] * lk2_ref[...], axis=1, keepdims=True)) + lambda_init)
    for c in range(chains):
        o = (acc_ref[c, 0:rows, 0:hd] / acc_ref[c, 0:rows, hd:]
             - lam * (acc_ref[c, rows:, 0:hd] / acc_ref[c, rows:, hd:]))
        ms_o = jnp.mean(o * o, axis=1, keepdims=True)
        o = o * lax.rsqrt(ms_o + EPS) * sg_ref[...] * (1.0 - lambda_init)
        o_ref[c * rows:(c + 1) * rows, :] = o.astype(o_ref.dtype)


def _diff_attention(q, kt, v, lq1, lk1, lq2, lk2, subln_g, *, B, S, H, dh, lambda_init, tile, chains):
    aw = H * 2 * dh
    q3 = q.reshape(B, S, aw)
    v3 = v.reshape(B, S, aw)
    hd = 2 * dh
    tile = min(tile, S)
    rows = tile // chains
    kern = functools.partial(_attn_kernel, tile=tile, chains=chains, dh=dh, lambda_init=lambda_init)
    vec = pl.BlockSpec((1, dh), lambda b, h, i: (0, 0))
    out = pl.pallas_call(
        kern,
        grid=(B, H, S // tile),
        in_specs=[pl.BlockSpec((None, tile, hd), lambda b, h, i: (b, i, h)),
                  pl.BlockSpec((None, hd, S), lambda b, h, i: (b, h, 0)),
                  pl.BlockSpec((None, S, hd), lambda b, h, i: (b, 0, h)),
                  vec, vec, vec, vec,
                  pl.BlockSpec((1, hd), lambda b, h, i: (0, 0))],
        out_specs=pl.BlockSpec((None, tile, hd), lambda b, h, i: (b, i, h)),
        out_shape=jax.ShapeDtypeStruct((B, S, aw), BF16),
        scratch_shapes=[pltpu.VMEM((chains, 2 * rows, 2 * hd), F32),
                        pltpu.VMEM((chains, 2 * rows, tile), F32)],
        compiler_params=_cparams(("parallel", "parallel", "parallel")),
        name="diff_flash_attention",
    )(q3, kt, v3, lq1, lk1, lq2, lk2, subln_g)
    return out.reshape(B * S, aw)


def _mixout_kernel(attn_ref, ucur_ref, uhalo_ref, x_ref, mod_ref, wdw_ref, bdw_ref, lng_ref, lnb_ref,
                   wo1_ref, wo2_ref, gffn_ref, wrh_ref, wrl_ref, br_ref,
                   x1_ref, h2_ref, route_ref, route_t_ref, cnt_ref, ubuf_ref, conv_ref,
                   *, nS, conv_k, n_exp, n_grp):
    T = x_ref.shape[0]
    cw = ucur_ref.shape[1]
    i = pl.program_id(0)

    @pl.when(i == 0)
    def _():
        cnt_ref[...] = jnp.zeros_like(cnt_ref)

    first = (i % nS) == 0
    halo = uhalo_ref[...].astype(F32)
    ubuf_ref[0, 0:CONV_HALO, :] = jnp.where(first, jnp.zeros_like(halo), halo)
    ubuf_ref[0, CONV_HALO:, :] = ucur_ref[...].astype(F32)
    span = T + CONV_HALO - SUBLANES
    for b in range(1, SUBLANES):
        ubuf_ref[b, 0:span, :] = ubuf_ref[0, b:b + span, :]

    off = CONV_HALO - (conv_k - 1)
    rows = 32
    for r0 in range(0, T, rows):
        acc = jnp.zeros((rows // SUBLANES, SUBLANES, cw), F32)
        for j in range(conv_k):
            a, b = divmod(j + off, SUBLANES)
            lo_r = r0 + a * SUBLANES
            acc = acc + (wdw_ref[j * SUBLANES:(j + 1) * SUBLANES, :]
                         * ubuf_ref[b, lo_r:lo_r + rows, :].reshape(rows // SUBLANES, SUBLANES, cw))
        y = acc.reshape(rows, cw) + bdw_ref[...]
        mu = jnp.mean(y, axis=1, keepdims=True)
        d = y - mu
        var = jnp.mean(d * d, axis=1, keepdims=True)
        z = d * lax.rsqrt(var + EPS) * lng_ref[...] + lnb_ref[...]
        conv_ref[r0:r0 + rows, :] = _silu(z).astype(BF16)

    yo = (jnp.dot(attn_ref[...], wo1_ref[...], preferred_element_type=F32)
          + jnp.dot(conv_ref[...], wo2_ref[...], preferred_element_type=F32))
    x1 = x_ref[...] + mod_ref[2:3, :] * yo
    x1_ref[...] = x1
    ms = jnp.mean(x1 * x1, axis=1, keepdims=True)
    h2 = x1 * lax.rsqrt(ms + EPS) * gffn_ref[...] * (1.0 + mod_ref[4:5, :]) + mod_ref[3:4, :]
    h2_ref[...] = h2
    hi = h2.astype(BF16)
    lo = (h2 - hi.astype(F32)).astype(BF16)

    logits = (jnp.dot(hi, wrh_ref[...], preferred_element_type=F32)
              + jnp.dot(lo, wrh_ref[...], preferred_element_type=F32)
              + jnp.dot(hi, wrl_ref[...], preferred_element_type=F32)) + br_ref[...]
    lane_i = lax.broadcasted_iota(jnp.int32, logits.shape, 1)
    lane = lane_i.astype(F32)
    big = jnp.float32(1e9)
    ninf = jnp.float32(-jnp.inf)
    is_g = (lane_i >= n_exp) & (lane_i < n_exp + n_grp)
    gl = jnp.where(is_g, logits, ninf)
    gmax = jnp.max(gl, axis=1, keepdims=True)
    gsum = jnp.sum(jnp.where(is_g, jnp.exp(gl - gmax), 0.0), axis=1, keepdims=True)
    g_p = 1.0 / gsum
    gidx = jnp.min(jnp.where(gl == gmax, lane, big), axis=1, keepdims=True) - n_exp
    epg = n_exp // n_grp
    lo_l = gidx * epg
    in_grp = (lane >= lo_l) & (lane < lo_l + epg)
    el = jnp.where(in_grp, logits, ninf)
    m1 = jnp.max(el, axis=1, keepdims=True)
    i1 = jnp.min(jnp.where(el == m1, lane, big), axis=1, keepdims=True)
    el2 = jnp.where(lane == i1, ninf, el)
    m2 = jnp.max(el2, axis=1, keepdims=True)
    i2 = jnp.min(jnp.where(el2 == m2, lane, big), axis=1, keepdims=True)
    e2 = jnp.exp(m2 - m1)
    p1 = 1.0 / (1.0 + e2)
    p2 = e2 / (1.0 + e2)

    sel = (lane == i1) | (lane == i2)
    rr = lax.broadcasted_iota(jnp.int32, (T, T), 0)
    cc = lax.broadcasted_iota(jnp.int32, (T, T), 1)
    before = jnp.dot((rr > cc).astype(BF16), sel.astype(BF16), preferred_element_type=F32) + cnt_ref[...]
    r1 = jnp.sum(jnp.where(lane == i1, before, 0.0), axis=1, keepdims=True)
    r2 = jnp.sum(jnp.where(lane == i2, before, 0.0), axis=1, keepdims=True)
    cnt_ref[...] += jnp.sum(sel.astype(F32), axis=0, keepdims=True)

    route = jnp.zeros_like(logits)
    for k, val in enumerate((i1, i2, g_p * p1, g_p * p2, r1, r2)):
        route = jnp.where(lane_i == k, val, route)
    route_ref[...] = route
    route_t_ref[...] = route.T[:SUBLANES]


def _mixout(attn, u, x2, mod, w_dw, b_dw, ln_g, ln_b, wo1, wo2, g_ffn, wr_hi, wr_lo, b_r,
            *, S, T, n_exp, n_grp):
    N, D = x2.shape
    aw = attn.shape[1]
    cw = u.shape[1]
    nS = S // T
    conv_k = w_dw.shape[0]
    w_dw = jnp.repeat(w_dw, SUBLANES, axis=0)
    hb = T // CONV_HALO
    kern = functools.partial(_mixout_kernel, nS=nS, conv_k=conv_k, n_exp=n_exp, n_grp=n_grp)
    full = lambda a: pl.BlockSpec(a.shape, lambda i: (0, 0))
    return pl.pallas_call(
        kern,
        grid=(N // T,),
        in_specs=[pl.BlockSpec((T, aw), lambda i: (i, 0)),
                  pl.BlockSpec((T, cw), lambda i: (i, 0)),
                  pl.BlockSpec((CONV_HALO, cw), lambda i: (jnp.maximum(i * hb - 1, 0), 0)),
                  pl.BlockSpec((T, D), lambda i: (i, 0)),
                  pl.BlockSpec((None, 6, D), lambda i: (i // nS, 0, 0)),
                  full(w_dw), full(b_dw), full(ln_g), full(ln_b), full(wo1), full(wo2), full(g_ffn),
                  full(wr_hi), full(wr_lo), full(b_r)],
        out_specs=[pl.BlockSpec((T, D), lambda i: (i, 0)),
                   pl.BlockSpec((T, D), lambda i: (i, 0)),
                   pl.BlockSpec((T, LANES), lambda i: (i, 0)),
                   pl.BlockSpec((SUBLANES, T), lambda i: (0, i)),
                   pl.BlockSpec((1, LANES), lambda i: (0, 0))],
        out_shape=[jax.ShapeDtypeStruct((N, D), F32),
                   jax.ShapeDtypeStruct((N, D), F32),
                   jax.ShapeDtypeStruct((N, LANES), F32),
                   jax.ShapeDtypeStruct((SUBLANES, N), F32),
                   jax.ShapeDtypeStruct((1, LANES), F32)],
        scratch_shapes=[pltpu.VMEM((SUBLANES, CONV_HALO + T, cw), F32),
                        pltpu.VMEM((T, cw), BF16)],
        compiler_params=_cparams(("arbitrary",)),
        name="conv_outproj_router",
    )(attn, u, u, x2, mod, w_dw, b_dw, ln_g, ln_b, wo1, wo2, g_ffn, wr_hi, wr_lo, b_r)


def _row_of(ref, r):
    if isinstance(r, int):
        return ref.at[r // SUBLANES, pl.ds(r % SUBLANES, 1)]
    return ref.at[lax.shift_right_logical(r, 3), pl.ds(r & (SUBLANES - 1), 1)]


def _scatter_kernel(dest_ref, h2_ref, xs_hbm, sem):
    groups = h2_ref.shape[0]
    tokens = groups * SUBLANES

    def issue(g, carry):
        for j in range(SUBLANES):
            for k in range(TOP_K_INNER):
                d = dest_ref[0, k * tokens + g * SUBLANES + j]
                pltpu.make_async_copy(h2_ref.at[g, pl.ds(j, 1)], _row_of(xs_hbm, d), sem).start(priority=k % 2)
        return carry

    lax.fori_loop(0, groups, issue, 0)
    for k in range(TOP_K_INNER):
        pltpu.make_async_copy(h2_ref, xs_hbm.at[pl.ds(0, groups)], sem).wait()


def _dest_table(dest, tokens):
    tab = jnp.concatenate([d.reshape(-1, tokens) for d in dest], axis=1)
    return tab.reshape(tab.shape[0], 1, tab.shape[1])


def _scatter_rows(h2, dest, *, tokens):
    N, C = h2.shape
    steps = N // tokens
    dest3 = _dest_table(dest, tokens)
    rows = N * TOP_K_INNER
    xs = pl.pallas_call(
        _scatter_kernel,
        grid=(steps,),
        in_specs=[pl.BlockSpec((None, 1, TOP_K_INNER * tokens), lambda s: (s, 0, 0), memory_space=pltpu.SMEM),
                  pl.BlockSpec((tokens // SUBLANES, SUBLANES, C), lambda s: (s, 0, 0))],
        out_specs=pl.BlockSpec(memory_space=pl.ANY),
        out_shape=jax.ShapeDtypeStruct((rows // SUBLANES, SUBLANES, C), h2.dtype),
        scratch_shapes=[pltpu.SemaphoreType.DMA],
        compiler_params=_cparams(("arbitrary",)),
        name="moe_scatter_rows",
    )(dest3, h2.reshape(N // SUBLANES, SUBLANES, C))
    return xs.reshape(rows, C)


XS_RING = 3


def _experts_kernel(vt_ref, ve_ref, vlo_ref, vhi_ref, vnew_ref, vnext_ref, vslot_ref,
                    xs_hbm, wg_hbm, wu_hbm, wd_hbm, ys_ref, xbuf, wg_buf, wu_buf, wd_buf, sem, wsem,
                    *, n_tiles):
    v = pl.program_id(0)
    lo = vlo_ref[v]
    hi = vhi_ref[v]
    t = vt_ref[v]
    tm = xbuf.shape[1]
    wslot = vslot_ref[v]

    def weight_copies(expert, slot):
        return [pltpu.make_async_copy(hbm.at[expert], buf.at[slot], wsem.at[slot, j])
                for j, (hbm, buf) in enumerate(((wg_hbm, wg_buf), (wu_hbm, wu_buf), (wd_hbm, wd_buf)))]

    @pl.when(v == 0)
    def _():
        for c in weight_copies(ve_ref[0], 0):
            c.start()

    @pl.when(vnew_ref[v] == 1)
    def _():
        @pl.when(vnext_ref[v] >= 0)
        def _():
            for c in weight_copies(vnext_ref[v], 1 - wslot):
                c.start()

        for c in weight_copies(ve_ref[v], wslot):
            c.wait()

    def tile_copy(tile):
        slot = lax.rem(tile, XS_RING)
        return pltpu.make_async_copy(xs_hbm.at[pl.ds(pl.multiple_of(tile * tm, tm), tm)], xbuf.at[slot],
                                     sem.at[slot])

    @pl.when(v == 0)
    def _():
        for tile in range(min(XS_RING - 1, n_tiles)):
            tile_copy(tile).start()

    @pl.when((lo == 0) & (hi > lo))
    def _():
        @pl.when(t + (XS_RING - 1) < n_tiles)
        def _():
            tile_copy(t + (XS_RING - 1)).start()

        tile_copy(t).wait()

    @pl.when(hi > lo)
    def _():
        x = xbuf[lax.rem(t, XS_RING)].astype(BF16)
        g = jnp.dot(x, wg_buf[wslot].astype(BF16), preferred_element_type=F32)
        u = jnp.dot(x, wu_buf[wslot].astype(BF16), preferred_element_type=F32)
        hid = (_silu(g) * u).astype(BF16)
        y = jnp.dot(hid, wd_buf[wslot].astype(BF16), preferred_element_type=F32)

        @pl.when(lo == 0)
        def _():
            ys_ref[...] = y

        @pl.when(lo > 0)
        def _():
            row = lax.broadcasted_iota(jnp.int32, y.shape, 0)
            ys_ref[...] = jnp.where((row >= lo) & (row < hi), y, ys_ref[...])


def _experts(xs, w_gate, w_up, w_down, visits, *, tm):
    R, C = xs.shape
    E, D, ff = w_gate.shape
    vt = visits[0]
    any_spec = pl.BlockSpec(memory_space=pl.ANY)
    grid_spec = pltpu.PrefetchScalarGridSpec(
        num_scalar_prefetch=len(visits),
        grid=(vt.shape[0],),
        in_specs=[any_spec, any_spec, any_spec, any_spec],
        out_specs=pl.BlockSpec((tm, C), lambda v, vt, *_: (vt[v], 0)),
        scratch_shapes=[pltpu.VMEM((XS_RING, tm, C), F32),
                        pltpu.VMEM((2, D, ff), w_gate.dtype),
                        pltpu.VMEM((2, D, ff), w_up.dtype),
                        pltpu.VMEM((2, ff, D), w_down.dtype),
                        pltpu.SemaphoreType.DMA((XS_RING,)),
                        pltpu.SemaphoreType.DMA((2, 3))],
    )
    return pl.pallas_call(
        functools.partial(_experts_kernel, n_tiles=R // tm),
        grid_spec=grid_spec,
        out_shape=jax.ShapeDtypeStruct((R, C), F32),
        compiler_params=_cparams(("arbitrary",)),
        name="moe_grouped_experts",
    )(*visits, xs, w_gate, w_up, w_down)


def _visit_tables(off, cnt, n_rows, tm):
    n_tiles = n_rows // tm
    n_exp = off.shape[0]
    n_visits = n_tiles + n_exp - 1
    tile_starts = jnp.arange(n_tiles, dtype=jnp.int32) * tm
    seg_starts = jnp.where((cnt > 0) & (off % tm != 0), off, n_rows)
    starts = jnp.sort(jnp.concatenate([tile_starts, seg_starts]))
    lo_abs = starts[:n_visits]
    hi_abs = starts[1:n_visits + 1]
    valid = lo_abs < n_rows
    tile = jnp.where(valid, lo_abs // tm, n_tiles - 1)
    ends = off + cnt
    probe = jnp.where(valid, lo_abs, n_rows - 1)
    expert = jnp.sum(ends[None, :] <= probe[:, None], axis=1).astype(jnp.int32)
    row_lo = jnp.where(valid, lo_abs - tile * tm, 0)
    row_hi = jnp.where(valid, hi_abs - tile * tm, 0)
    new = jnp.concatenate([jnp.ones((1,), jnp.int32), (expert[1:] != expert[:-1]).astype(jnp.int32)])
    slot = (jnp.cumsum(new) - 1) % 2
    nxt = jnp.sum(expert[None, :] <= expert[:, None], axis=1)
    nxt_expert = jnp.where(nxt < n_visits, expert[jnp.minimum(nxt, n_visits - 1)], -1)
    return tile, expert, row_lo, row_hi, new, nxt_expert.astype(jnp.int32), slot.astype(jnp.int32)


def _combine_kernel(dcur_ref, dnxt_ref, ys_hbm, route_ref, x1_ref, mod_ref, o_ref, ybuf, sem):
    T = x1_ref.shape[0]
    i = pl.program_id(0)
    slot = lax.rem(i, 2)

    def gather(d_ref, sl):
        def body(g, carry):
            for j in range(SUBLANES):
                for k in range(TOP_K_INNER):
                    d = d_ref[0, k * T + g * SUBLANES + j]
                    pltpu.make_async_copy(_row_of(ys_hbm, d), ybuf.at[sl, k, g, pl.ds(j, 1)],
                                          sem.at[sl]).start(priority=k % 2)
            return carry
        lax.fori_loop(0, T // SUBLANES, body, 0)

    @pl.when(i == 0)
    def _():
        gather(dcur_ref, 0)

    @pl.when(i + 1 < pl.num_programs(0))
    def _():
        gather(dnxt_ref, 1 - slot)

    for k in range(TOP_K_INNER):
        pltpu.make_async_copy(ys_hbm.at[pl.ds(0, T // SUBLANES)], ybuf.at[slot, k], sem.at[slot]).wait()

    route = route_ref[...]
    moe = jnp.zeros(x1_ref.shape, F32)
    for k in range(TOP_K_INNER):
        moe = moe + route[:, TOP_K_INNER + k:TOP_K_INNER + k + 1] * ybuf[slot, k].reshape(x1_ref.shape)
    o_ref[...] = x1_ref[...] + mod_ref[5:6, :] * moe


def _combine(ys, dest, route, x1, mod, *, S, T):
    N, D = x1.shape
    C = ys.shape[1]
    steps = N // T
    nS = S // T
    dest3 = _dest_table(dest, T)
    dspec = lambda f: pl.BlockSpec((None, 1, TOP_K_INNER * T), f, memory_space=pltpu.SMEM)
    return pl.pallas_call(
        _combine_kernel,
        grid=(steps,),
        in_specs=[dspec(lambda i: (i, 0, 0)),
                  dspec(lambda i: (jnp.minimum(i + 1, steps - 1), 0, 0)),
                  pl.BlockSpec(memory_space=pl.ANY),
                  pl.BlockSpec((T, LANES), lambda i: (i, 0)),
                  pl.BlockSpec((T, D), lambda i: (i, 0)),
                  pl.BlockSpec((None, 6, D), lambda i: (i // nS, 0, 0))],
        out_specs=pl.BlockSpec((T, D), lambda i: (i, 0)),
        out_shape=jax.ShapeDtypeStruct((N, D), F32),
        scratch_shapes=[pltpu.VMEM((2, TOP_K_INNER, T // SUBLANES, SUBLANES, C), F32),
                        pltpu.SemaphoreType.DMA((2,))],
        compiler_params=_cparams(("arbitrary",)),
        name="moe_gather_combine",
    )(dest3, dest3, ys.reshape(ys.shape[0] // SUBLANES, SUBLANES, C), route, x1, mod)


def _layer(x2, mod, pos_row, l, B, S, g_mix, w_in, q_norm_g, k_norm_g, lambda_q1, lambda_k1, lambda_q2,
           lambda_k2, subln_g, b_glu, w_dw, b_dw, conv_ln_g, conv_ln_b, w_out, g_ffn, w_group, b_group,
           w_router, b_router, w_gate, w_up, w_down):
    N, D = x2.shape
    dh = q_norm_g.shape[0]
    H = N_DIFF_HEADS
    aw = H * 2 * dh
    cw = w_dw.shape[1]
    rot = dh // 4
    n_grp = w_group.shape[1]
    n_exp = w_router.shape[1]
    lambda_init = 0.8 - 0.6 * math.exp(-0.3 * l)

    w_qk_t = w_in[:, :2 * aw].T.astype(BF16)
    w_vglu = w_in[:, 2 * aw:].astype(BF16)
    scale = dh ** -0.5 * math.log2(math.e)
    gq = jnp.tile(q_norm_g * scale, aw // dh)
    gk = jnp.tile(k_norm_g, aw // dh)
    gqk_tab = jnp.broadcast_to(jnp.concatenate([gq, gk])[:, None], (2 * aw, LANES))
    inv_freq = ROPE_THETA ** (-jnp.arange(0, rot, 2, dtype=F32) / rot)
    invf_tab = jnp.broadcast_to(inv_freq[:, None], (rot // 2, LANES))

    q, kt, v, u = _inproj(x2, mod, g_mix.reshape(1, D), pos_row, w_vglu, w_qk_t, b_glu.reshape(1, 2 * cw),
                          gqk_tab, invf_tab, B=B, S=S, aw=aw, cw=cw, dh=dh, rot=rot, T=min(1024, S))

    attn = _diff_attention(q, kt, v, lambda_q1.reshape(1, dh), lambda_k1.reshape(1, dh),
                           lambda_q2.reshape(1, dh), lambda_k2.reshape(1, dh), subln_g.reshape(1, 2 * dh),
                           B=B, S=S, H=H, dh=dh, lambda_init=lambda_init, tile=1024, chains=2)

    w_r = jnp.zeros((D, LANES), F32).at[:, :n_exp].set(w_router).at[:, n_exp:n_exp + n_grp].set(w_group)
    b_r = jnp.zeros((1, LANES), F32).at[0, :n_exp].set(b_router).at[0, n_exp:n_exp + n_grp].set(b_group)
    wr_hi = w_r.astype(BF16)
    wr_lo = (w_r - wr_hi.astype(F32)).astype(BF16)
    wo = w_out.astype(BF16)
    x1, h2, route, route_t, counts = _mixout(attn, u, x2, mod, w_dw, b_dw.reshape(1, cw), conv_ln_g.reshape(1, cw),
                                     conv_ln_b.reshape(1, cw), wo[:aw], wo[aw:], g_ffn.reshape(1, D),
                                     wr_hi, wr_lo, b_r, S=S, T=512, n_exp=n_exp, n_grp=n_grp)

    cnt = counts[0, :n_exp].astype(jnp.int32)
    off = jnp.cumsum(cnt) - cnt
    experts = jnp.arange(n_exp, dtype=jnp.int32)[:, None]
    dest = [jnp.sum(jnp.where(route_t[k].astype(jnp.int32)[None, :] == experts, off[:, None], 0), axis=0)
            + route_t[4 + k].astype(jnp.int32) for k in range(TOP_K_INNER)]
    tm = 256
    visits = _visit_tables(off, cnt, N * TOP_K_INNER, tm)

    xs = _scatter_rows(h2, dest, tokens=min(1024, N))
    ys = _experts(xs, w_gate, w_up, w_down, visits, tm=tm)
    return _combine(ys, dest, route, x1, mod, S=S, T=512)


def kernel(x, c, positions, w_ada, b_ada, g_mix, w_in, q_norm_g, k_norm_g, lambda_q1, lambda_k1, lambda_q2,
           lambda_k2, subln_g, b_glu, w_dw, b_dw, conv_ln_g, conv_ln_b, w_out, g_ffn, w_group, b_group,
           w_router, b_router, w_gate, w_up, w_down):
    B, S, D = x.shape
    depth = w_ada.shape[0]
    x2 = x.reshape(B * S, D)
    pos_row = positions.astype(F32).reshape(1, B * S)
    for l in range(depth):
        mod = _modulation(c, w_ada[l], b_ada[l])
        x2 = _layer(x2, mod, pos_row, l, B, S, g_mix[l], w_in[l], q_norm_g[l], k_norm_g[l], lambda_q1[l],
                    lambda_k1[l], lambda_q2[l], lambda_k2[l], subln_g[l], b_glu[l], w_dw[l], b_dw[l],
                    conv_ln_g[l], conv_ln_b[l], w_out[l], g_ffn[l], w_group[l], b_group[l], w_router[l],
                    b_router[l], w_gate[l], w_up[l], w_down[l])
    return x2.reshape(B, S, D)
```

```python
import collections
import functools
import math

import numpy as np
import jax
import jax.numpy as jnp
from jax import lax
from jax.experimental import pallas as pl
from jax.experimental.pallas import tpu as pltpu

F32 = jnp.float32
BF16 = jnp.bfloat16

EPS = 1e-6
ROPE_THETA = 500000.0
N_DIFF_HEADS = 4
TOP_K_INNER = 2

LANES = 128
SUBLANES = 8
CONV_HALO = 32
VMEM_LIMIT = 48 * 1024 * 1024


Tiles = collections.namedtuple("Tiles", "inproj attn attn_chains mixout scatter experts combine")


def _tiles(S):
    cap = lambda t: min(t, S)
    return Tiles(inproj=cap(1024), attn=cap(1024), attn_chains=2, mixout=cap(512), scatter=cap(1024),
                 experts=256, combine=cap(512))


def _cparams(sem):
    return pltpu.CompilerParams(dimension_semantics=sem, vmem_limit_bytes=VMEM_LIMIT)


def _silu(x):
    return x * jax.nn.sigmoid(x)


def _mod_kernel(c_ref, w_ref, b_ref, o_ref):
    c = c_ref[...]
    o_ref[...] = jnp.dot(_silu(c), w_ref[...], preferred_element_type=F32,
                         precision=lax.Precision.HIGHEST) + b_ref[...]


def _modulation(c, w_ada, b_ada):
    B, D = c.shape
    n_out = w_ada.shape[1]
    rows = 8
    c_pad = jnp.pad(c, ((0, rows - B), (0, 0)))
    bn = 512
    out = pl.pallas_call(
        _mod_kernel,
        grid=(n_out // bn,),
        in_specs=[pl.BlockSpec((rows, D), lambda j: (0, 0)),
                  pl.BlockSpec((D, bn), lambda j: (0, j)),
                  pl.BlockSpec((1, bn), lambda j: (0, j))],
        out_specs=pl.BlockSpec((rows, bn), lambda j: (0, j)),
        out_shape=jax.ShapeDtypeStruct((rows, n_out), F32),
        compiler_params=_cparams(("parallel",)),
        name="adaln_mod",
    )(c_pad, w_ada, b_ada.reshape(1, n_out))
    return out[:B].reshape(B, 6, D)


def _inproj_kernel(x_ref, mod_ref, g_ref, pos_ref, wv_ref, wqk_ref, bglu_ref, gqk_ref, invf_ref,
                   q_ref, kt_ref, v_ref, u_ref, *, aw, cw, dh, rot):
    T = x_ref.shape[0]
    half = rot // 2
    x = x_ref[...]
    ms = jnp.mean(x * x, axis=-1, keepdims=True)
    sh = mod_ref[0:1, :]
    sc = mod_ref[1:2, :]
    h = x * lax.rsqrt(ms + EPS) * g_ref[...] * (1.0 + sc) + sh
    hb = h.astype(BF16)

    pv = jnp.dot(hb, wv_ref[...], preferred_element_type=F32)
    v_ref[...] = pv[:, :aw].astype(BF16)
    a = pv[:, aw:aw + cw] + bglu_ref[:, :cw]
    gate = pv[:, aw + cw:] + bglu_ref[:, cw:]
    u_ref[...] = (a * jax.nn.sigmoid(gate)).astype(BF16)

    qkt = lax.dot_general(wqk_ref[...], hb, (((1,), (1,)), ((), ())), preferred_element_type=F32)
    nch = 2 * aw // dh
    for c in range(T // LANES):
        sl = slice(c * LANES, (c + 1) * LANES)
        s3 = qkt[:, sl].reshape(nch, dh, LANES)
        ssq = jnp.mean(s3 * s3, axis=1, keepdims=True)
        y = s3 * lax.rsqrt(ssq + EPS) * gqk_ref[...].reshape(nch, dh, LANES)
        ang = invf_ref[...] * pos_ref[:, sl]
        cs = jnp.cos(ang)
        sn = jnp.sin(ang)
        t1 = y[:, 0:half, :]
        t2 = y[:, half:rot, :]
        y = jnp.concatenate([t1 * cs - t2 * sn, t2 * cs + t1 * sn, y[:, rot:, :]], axis=1)
        y2 = y.reshape(2 * aw, LANES)
        kt_ref[:, sl] = y2[aw:].astype(BF16)
        q_ref[sl, :] = y2[:aw].T.astype(BF16)


def _inproj(x2, mod, g_mix, pos_row, w_vglu, w_qk_t, b_glu, gqk_tab, invf_tab, *, B, S, aw, cw, dh, rot, T):
    N, D = x2.shape
    nS = S // T
    kern = functools.partial(_inproj_kernel, aw=aw, cw=cw, dh=dh, rot=rot)
    return pl.pallas_call(
        kern,
        grid=(N // T,),
        in_specs=[pl.BlockSpec((T, D), lambda i: (i, 0)),
                  pl.BlockSpec((None, 6, D), lambda i: (i // nS, 0, 0)),
                  pl.BlockSpec((1, D), lambda i: (0, 0)),
                  pl.BlockSpec((1, T), lambda i: (0, i)),
                  pl.BlockSpec(w_vglu.shape, lambda i: (0, 0)),
                  pl.BlockSpec(w_qk_t.shape, lambda i: (0, 0)),
                  pl.BlockSpec((1, 2 * cw), lambda i: (0, 0)),
                  pl.BlockSpec(gqk_tab.shape, lambda i: (0, 0)),
                  pl.BlockSpec(invf_tab.shape, lambda i: (0, 0))],
        out_specs=[pl.BlockSpec((T, aw), lambda i: (i, 0)),
                   pl.BlockSpec((None, aw, T), lambda i: (i // nS, 0, i % nS)),
                   pl.BlockSpec((T, aw), lambda i: (i, 0)),
                   pl.BlockSpec((T, cw), lambda i: (i, 0))],
        out_shape=[jax.ShapeDtypeStruct((N, aw), BF16),
                   jax.ShapeDtypeStruct((B, aw, S), BF16),
                   jax.ShapeDtypeStruct((N, aw), BF16),
                   jax.ShapeDtypeStruct((N, cw), BF16)],
        compiler_params=_cparams(("parallel",)),
        name="inproj_qknorm_rope_glu",
    )(x2, mod, g_mix, pos_row, w_vglu, w_qk_t, b_glu, gqk_tab, invf_tab)


def _attn_kernel(q_ref, kt_ref, v_ref, lq1_ref, lk1_ref, lq2_ref, lk2_ref, sg_ref, o_ref, acc_ref, s_ref,
                 *, tile, chains, dh, lambda_init):
    i = pl.program_id(2)
    hd = 2 * dh
    rows = tile // chains

    def stacked(q):
        lane = lax.broadcasted_iota(jnp.int32, q.shape, 1)
        zero = jnp.zeros_like(q)
        return jnp.concatenate([jnp.where(lane < dh, q, zero), jnp.where(lane >= dh, q, zero)], axis=0)

    qs = [stacked(q_ref[c * rows:(c + 1) * rows, :]) for c in range(chains)]

    acc_ref[...] = jnp.zeros_like(acc_ref)
    ones = {w: jnp.ones((w, hd), BF16) for w in {tile} | {(c + 1) * rows for c in range(chains)}}

    def scores(t, c):
        start = pl.multiple_of(t * tile, tile)
        s_ref[c] = jnp.dot(qs[c], kt_ref[:, pl.ds(start, tile)], preferred_element_type=F32)

    def softmax_pv(t, c, m, width=tile, diagonal=False):
        start = pl.multiple_of(t * tile, tile)
        vt = jnp.concatenate([v_ref[pl.ds(start, width), :], ones[width]], axis=1)
        s = s_ref[c, :, :width]
        if diagonal:
            row = lax.broadcasted_iota(jnp.int32, (2 * rows, rows), 0)
            col = lax.broadcasted_iota(jnp.int32, (2 * rows, rows), 1)
            qrow = jnp.where(row >= rows, row - rows, row)
            tail = jnp.where(col <= qrow, s[:, width - rows:], -jnp.inf)
            s = tail if width == rows else jnp.concatenate([s[:, :width - rows], tail], axis=1)
        m_new = jnp.maximum(m, jnp.max(s, axis=1, keepdims=True))
        alpha = jnp.exp2(m - m_new)
        p = jnp.exp2(s - m_new).astype(BF16)
        acc_ref[c] = alpha * acc_ref[c] + jnp.dot(p, vt, preferred_element_type=F32)
        return m_new

    m0 = jnp.full((2 * rows, 1), -jnp.inf, F32)
    scores(0, 0)

    def full_tile(t, ms):
        ms = list(ms)
        for c in range(chains):
            if c + 1 < chains:
                scores(t, c + 1)
            else:
                scores(t + 1, 0)
            ms[c] = softmax_pv(t, c, ms[c])
        return tuple(ms)

    ms = lax.fori_loop(0, i, full_tile, (m0,) * chains)

    for c in range(chains):
        if c + 1 < chains:
            scores(i, c + 1)
        softmax_pv(i, c, ms[c], width=(c + 1) * rows, diagonal=True)

    lam = (jnp.exp(jnp.sum(lq1_ref[...] * lk1_ref[...], axis=1, keepdims=True))
           - jnp.exp(jnp.sum(lq2_ref[...] * lk2_ref[...], axis=1, keepdims=True)) + lambda_init)
    for c in range(chains):
        o = (acc_ref[c, 0:rows, 0:hd] / acc_ref[c, 0:rows, hd:]
             - lam * (acc_ref[c, rows:, 0:hd] / acc_ref[c, rows:, hd:]))
        ms_o = jnp.mean(o * o, axis=1, keepdims=True)
        o = o * lax.rsqrt(ms_o + EPS) * sg_ref[...] * (1.0 - lambda_init)
        o_ref[c * rows:(c + 1) * rows, :] = o.astype(o_ref.dtype)


def _diff_attention(q, kt, v, lq1, lk1, lq2, lk2, subln_g, *, B, S, H, dh, lambda_init, tile, chains):
    aw = H * 2 * dh
    q3 = q.reshape(B, S, aw)
    v3 = v.reshape(B, S, aw)
    hd = 2 * dh
    tile = min(tile, S)
    rows = tile // chains
    kern = functools.partial(_attn_kernel, tile=tile, chains=chains, dh=dh, lambda_init=lambda_init)
    vec = pl.BlockSpec((1, dh), lambda b, h, i: (0, 0))
    out = pl.pallas_call(
        kern,
        grid=(B, H, S // tile),
        in_specs=[pl.BlockSpec((None, tile, hd), lambda b, h, i: (b, i, h)),
                  pl.BlockSpec((None, hd, S), lambda b, h, i: (b, h, 0)),
                  pl.BlockSpec((None, S, hd), lambda b, h, i: (b, 0, h)),
                  vec, vec, vec, vec,
                  pl.BlockSpec((1, hd), lambda b, h, i: (0, 0))],
        out_specs=pl.BlockSpec((None, tile, hd), lambda b, h, i: (b, i, h)),
        out_shape=jax.ShapeDtypeStruct((B, S, aw), BF16),
        scratch_shapes=[pltpu.VMEM((chains, 2 * rows, 2 * hd), F32),
                        pltpu.VMEM((chains, 2 * rows, tile), F32)],
        compiler_params=_cparams(("parallel", "parallel", "parallel")),
        name="diff_flash_attention",
    )(q3, kt, v3, lq1, lk1, lq2, lk2, subln_g)
    return out.reshape(B * S, aw)


def _mixout_kernel(attn_ref, ucur_ref, uhalo_ref, x_ref, mod_ref, wdw_ref, bdw_ref, lng_ref, lnb_ref,
                   wo1_ref, wo2_ref, gffn_ref, wrh_ref, wrl_ref, br_ref,
                   x1_ref, h2_ref, route_ref, route_t_ref, cnt_ref, ubuf_ref, conv_ref,
                   *, nS, conv_k, n_exp, n_grp):
    T = x_ref.shape[0]
    cw = ucur_ref.shape[1]
    i = pl.program_id(0)

    @pl.when(i == 0)
    def _():
        cnt_ref[...] = jnp.zeros_like(cnt_ref)

    first = (i % nS) == 0
    halo = uhalo_ref[...].astype(F32)
    ubuf_ref[0, 0:CONV_HALO, :] = jnp.where(first, jnp.zeros_like(halo), halo)
    ubuf_ref[0, CONV_HALO:, :] = ucur_ref[...].astype(F32)
    span = T + CONV_HALO - SUBLANES
    for b in range(1, SUBLANES):
        ubuf_ref[b, 0:span, :] = ubuf_ref[0, b:b + span, :]

    off = CONV_HALO - (conv_k - 1)
    rows = 32
    for r0 in range(0, T, rows):
        acc = jnp.zeros((rows // SUBLANES, SUBLANES, cw), F32)
        for j in range(conv_k):
            a, b = divmod(j + off, SUBLANES)
            lo_r = r0 + a * SUBLANES
            acc = acc + (wdw_ref[j * SUBLANES:(j + 1) * SUBLANES, :]
                         * ubuf_ref[b, lo_r:lo_r + rows, :].reshape(rows // SUBLANES, SUBLANES, cw))
        y = acc.reshape(rows, cw) + bdw_ref[...]
        mu = jnp.mean(y, axis=1, keepdims=True)
        d = y - mu
        var = jnp.mean(d * d, axis=1, keepdims=True)
        z = d * lax.rsqrt(var + EPS) * lng_ref[...] + lnb_ref[...]
        conv_ref[r0:r0 + rows, :] = _silu(z).astype(BF16)

    yo = (jnp.dot(attn_ref[...], wo1_ref[...], preferred_element_type=F32)
          + jnp.dot(conv_ref[...], wo2_ref[...], preferred_element_type=F32))
    x1 = x_ref[...] + mod_ref[2:3, :] * yo
    x1_ref[...] = x1
    ms = jnp.mean(x1 * x1, axis=1, keepdims=True)
    h2 = x1 * lax.rsqrt(ms + EPS) * gffn_ref[...] * (1.0 + mod_ref[4:5, :]) + mod_ref[3:4, :]
    h2_ref[...] = h2
    hi = h2.astype(BF16)
    lo = (h2 - hi.astype(F32)).astype(BF16)

    logits = (jnp.dot(hi, wrh_ref[...], preferred_element_type=F32)
              + jnp.dot(lo, wrh_ref[...], preferred_element_type=F32)
              + jnp.dot(hi, wrl_ref[...], preferred_element_type=F32)) + br_ref[...]
    lane_i = lax.broadcasted_iota(jnp.int32, logits.shape, 1)
    lane = lane_i.astype(F32)
    big = jnp.float32(1e9)
    ninf = jnp.float32(-jnp.inf)
    is_g = (lane_i >= n_exp) & (lane_i < n_exp + n_grp)
    gl = jnp.where(is_g, logits, ninf)
    gmax = jnp.max(gl, axis=1, keepdims=True)
    gsum = jnp.sum(jnp.where(is_g, jnp.exp(gl - gmax), 0.0), axis=1, keepdims=True)
    g_p = 1.0 / gsum
    gidx = jnp.min(jnp.where(gl == gmax, lane, big), axis=1, keepdims=True) - n_exp
    epg = n_exp // n_grp
    lo_l = gidx * epg
    in_grp = (lane >= lo_l) & (lane < lo_l + epg)
    el = jnp.where(in_grp, logits, ninf)
    m1 = jnp.max(el, axis=1, keepdims=True)
    i1 = jnp.min(jnp.where(el == m1, lane, big), axis=1, keepdims=True)
    el2 = jnp.where(lane == i1, ninf, el)
    m2 = jnp.max(el2, axis=1, keepdims=True)
    i2 = jnp.min(jnp.where(el2 == m2, lane, big), axis=1, keepdims=True)
    e2 = jnp.exp(m2 - m1)
    p1 = 1.0 / (1.0 + e2)
    p2 = e2 / (1.0 + e2)

    sel = (lane == i1) | (lane == i2)
    rr = lax.broadcasted_iota(jnp.int32, (T, T), 0)
    cc = lax.broadcasted_iota(jnp.int32, (T, T), 1)
    before = jnp.dot((rr > cc).astype(BF16), sel.astype(BF16), preferred_element_type=F32) + cnt_ref[...]
    r1 = jnp.sum(jnp.where(lane == i1, before, 0.0), axis=1, keepdims=True)
    r2 = jnp.sum(jnp.where(lane == i2, before, 0.0), axis=1, keepdims=True)
    cnt_ref[...] += jnp.sum(sel.astype(F32), axis=0, keepdims=True)

    route = jnp.zeros_like(logits)
    for k, val in enumerate((i1, i2, g_p * p1, g_p * p2, r1, r2)):
        route = jnp.where(lane_i == k, val, route)
    route_ref[...] = route
    route_t_ref[...] = route.T[:SUBLANES]


def _mixout(attn, u, x2, mod, w_dw, b_dw, ln_g, ln_b, wo1, wo2, g_ffn, wr_hi, wr_lo, b_r,
            *, S, T, n_exp, n_grp):
    N, D = x2.shape
    aw = attn.shape[1]
    cw = u.shape[1]
    nS = S // T
    conv_k = w_dw.shape[0]
    w_dw = jnp.repeat(w_dw, SUBLANES, axis=0)
    hb = T // CONV_HALO
    kern = functools.partial(_mixout_kernel, nS=nS, conv_k=conv_k, n_exp=n_exp, n_grp=n_grp)
    full = lambda a: pl.BlockSpec(a.shape, lambda i: (0, 0))
    return pl.pallas_call(
        kern,
        grid=(N // T,),
        in_specs=[pl.BlockSpec((T, aw), lambda i: (i, 0)),
                  pl.BlockSpec((T, cw), lambda i: (i, 0)),
                  pl.BlockSpec((CONV_HALO, cw), lambda i: (jnp.maximum(i * hb - 1, 0), 0)),
                  pl.BlockSpec((T, D), lambda i: (i, 0)),
                  pl.BlockSpec((None, 6, D), lambda i: (i // nS, 0, 0)),
                  full(w_dw), full(b_dw), full(ln_g), full(ln_b), full(wo1), full(wo2), full(g_ffn),
                  full(wr_hi), full(wr_lo), full(b_r)],
        out_specs=[pl.BlockSpec((T, D), lambda i: (i, 0)),
                   pl.BlockSpec((T, D), lambda i: (i, 0)),
                   pl.BlockSpec((T, LANES), lambda i: (i, 0)),
                   pl.BlockSpec((SUBLANES, T), lambda i: (0, i)),
                   pl.BlockSpec((1, LANES), lambda i: (0, 0))],
        out_shape=[jax.ShapeDtypeStruct((N, D), F32),
                   jax.ShapeDtypeStruct((N, D), F32),
                   jax.ShapeDtypeStruct((N, LANES), F32),
                   jax.ShapeDtypeStruct((SUBLANES, N), F32),
                   jax.ShapeDtypeStruct((1, LANES), F32)],
        scratch_shapes=[pltpu.VMEM((SUBLANES, CONV_HALO + T, cw), F32),
                        pltpu.VMEM((T, cw), BF16)],
        compiler_params=_cparams(("arbitrary",)),
        name="conv_outproj_router",
    )(attn, u, u, x2, mod, w_dw, b_dw, ln_g, ln_b, wo1, wo2, g_ffn, wr_hi, wr_lo, b_r)


def _row_of(ref, r):
    if isinstance(r, int):
        return ref.at[r // SUBLANES, pl.ds(r % SUBLANES, 1)]
    return ref.at[lax.shift_right_logical(r, 3), pl.ds(r & (SUBLANES - 1), 1)]


def _scatter_kernel(dest_ref, h2_ref, xs_hbm, sem):
    groups = h2_ref.shape[0]
    tokens = groups * SUBLANES

    def issue(g, carry):
        for j in range(SUBLANES):
            for k in range(TOP_K_INNER):
                d = dest_ref[0, k * tokens + g * SUBLANES + j]
                pltpu.make_async_copy(h2_ref.at[g, pl.ds(j, 1)], _row_of(xs_hbm, d), sem).start(priority=k % 2)
        return carry

    lax.fori_loop(0, groups, issue, 0)
    for k in range(TOP_K_INNER):
        pltpu.make_async_copy(h2_ref, xs_hbm.at[pl.ds(0, groups)], sem).wait()


def _dest_table(dest, tokens):
    tab = jnp.concatenate([d.reshape(-1, tokens) for d in dest], axis=1)
    return tab.reshape(tab.shape[0], 1, tab.shape[1])


def _scatter_rows(h2, dest, *, tokens):
    N, C = h2.shape
    steps = N // tokens
    dest3 = _dest_table(dest, tokens)
    rows = N * TOP_K_INNER
    xs = pl.pallas_call(
        _scatter_kernel,
        grid=(steps,),
        in_specs=[pl.BlockSpec((None, 1, TOP_K_INNER * tokens), lambda s: (s, 0, 0), memory_space=pltpu.SMEM),
                  pl.BlockSpec((tokens // SUBLANES, SUBLANES, C), lambda s: (s, 0, 0))],
        out_specs=pl.BlockSpec(memory_space=pl.ANY),
        out_shape=jax.ShapeDtypeStruct((rows // SUBLANES, SUBLANES, C), h2.dtype),
        scratch_shapes=[pltpu.SemaphoreType.DMA],
        compiler_params=_cparams(("arbitrary",)),
        name="moe_scatter_rows",
    )(dest3, h2.reshape(N // SUBLANES, SUBLANES, C))
    return xs.reshape(rows, C)


XS_RING = 3


def _experts_kernel(vt_ref, ve_ref, vlo_ref, vhi_ref, vnew_ref, vnext_ref, vslot_ref,
                    xs_hbm, wg_hbm, wu_hbm, wd_hbm, ys_ref, xbuf, wg_buf, wu_buf, wd_buf, sem, wsem,
                    *, n_tiles):
    v = pl.program_id(0)
    lo = vlo_ref[v]
    hi = vhi_ref[v]
    t = vt_ref[v]
    tm = xbuf.shape[1]
    wslot = vslot_ref[v]

    def weight_copies(expert, slot):
        return [pltpu.make_async_copy(hbm.at[expert], buf.at[slot], wsem.at[slot, j])
                for j, (hbm, buf) in enumerate(((wg_hbm, wg_buf), (wu_hbm, wu_buf), (wd_hbm, wd_buf)))]

    @pl.when(v == 0)
    def _():
        for c in weight_copies(ve_ref[0], 0):
            c.start()

    @pl.when(vnew_ref[v] == 1)
    def _():
        @pl.when(vnext_ref[v] >= 0)
        def _():
            for c in weight_copies(vnext_ref[v], 1 - wslot):
                c.start()

        for c in weight_copies(ve_ref[v], wslot):
            c.wait()

    def tile_copy(tile):
        slot = lax.rem(tile, XS_RING)
        return pltpu.make_async_copy(xs_hbm.at[pl.ds(pl.multiple_of(tile * tm, tm), tm)], xbuf.at[slot],
                                     sem.at[slot])

    @pl.when(v == 0)
    def _():
        for tile in range(min(XS_RING - 1, n_tiles)):
            tile_copy(tile).start()

    @pl.when((lo == 0) & (hi > lo))
    def _():
        @pl.when(t + (XS_RING - 1) < n_tiles)
        def _():
            tile_copy(t + (XS_RING - 1)).start()

        tile_copy(t).wait()

    @pl.when(hi > lo)
    def _():
        x = xbuf[lax.rem(t, XS_RING)].astype(BF16)
        g = jnp.dot(x, wg_buf[wslot].astype(BF16), preferred_element_type=F32)
        u = jnp.dot(x, wu_buf[wslot].astype(BF16), preferred_element_type=F32)
        hid = (_silu(g) * u).astype(BF16)
        y = jnp.dot(hid, wd_buf[wslot].astype(BF16), preferred_element_type=F32)

        @pl.when(lo == 0)
        def _():
            ys_ref[...] = y

        @pl.when(lo > 0)
        def _():
            row = lax.broadcasted_iota(jnp.int32, y.shape, 0)
            ys_ref[...] = jnp.where((row >= lo) & (row < hi), y, ys_ref[...])


def _experts(xs, w_gate, w_up, w_down, visits, *, tm):
    R, C = xs.shape
    E, D, ff = w_gate.shape
    vt = visits[0]
    any_spec = pl.BlockSpec(memory_space=pl.ANY)
    grid_spec = pltpu.PrefetchScalarGridSpec(
        num_scalar_prefetch=len(visits),
        grid=(vt.shape[0],),
        in_specs=[any_spec, any_spec, any_spec, any_spec],
        out_specs=pl.BlockSpec((tm, C), lambda v, vt, *_: (vt[v], 0)),
        scratch_shapes=[pltpu.VMEM((XS_RING, tm, C), F32),
                        pltpu.VMEM((2, D, ff), w_gate.dtype),
                        pltpu.VMEM((2, D, ff), w_up.dtype),
                        pltpu.VMEM((2, ff, D), w_down.dtype),
                        pltpu.SemaphoreType.DMA((XS_RING,)),
                        pltpu.SemaphoreType.DMA((2, 3))],
    )
    return pl.pallas_call(
        functools.partial(_experts_kernel, n_tiles=R // tm),
        grid_spec=grid_spec,
        out_shape=jax.ShapeDtypeStruct((R, C), F32),
        compiler_params=_cparams(("arbitrary",)),
        name="moe_grouped_experts",
    )(*visits, xs, w_gate, w_up, w_down)


def _visit_tables(off, cnt, n_rows, tm):
    n_tiles = n_rows // tm
    n_exp = off.shape[0]
    n_visits = n_tiles + n_exp - 1
    tile_starts = jnp.arange(n_tiles, dtype=jnp.int32) * tm
    seg_starts = jnp.where((cnt > 0) & (off % tm != 0), off, n_rows)
    starts = jnp.sort(jnp.concatenate([tile_starts, seg_starts]))
    lo_abs = starts[:n_visits]
    hi_abs = starts[1:n_visits + 1]
    valid = lo_abs < n_rows
    tile = jnp.where(valid, lo_abs // tm, n_tiles - 1)
    ends = off + cnt
    probe = jnp.where(valid, lo_abs, n_rows - 1)
    expert = jnp.sum(ends[None, :] <= probe[:, None], axis=1).astype(jnp.int32)
    row_lo = jnp.where(valid, lo_abs - tile * tm, 0)
    row_hi = jnp.where(valid, hi_abs - tile * tm, 0)
    new = jnp.concatenate([jnp.ones((1,), jnp.int32), (expert[1:] != expert[:-1]).astype(jnp.int32)])
    slot = (jnp.cumsum(new) - 1) % 2
    nxt = jnp.sum(expert[None, :] <= expert[:, None], axis=1)
    nxt_expert = jnp.where(nxt < n_visits, expert[jnp.minimum(nxt, n_visits - 1)], -1)
    return tile, expert, row_lo, row_hi, new, nxt_expert.astype(jnp.int32), slot.astype(jnp.int32)


def _combine_kernel(dcur_ref, dnxt_ref, ys_hbm, route_ref, x1_ref, mod_ref, o_ref, ybuf, sem):
    T = x1_ref.shape[0]
    i = pl.program_id(0)
    slot = lax.rem(i, 2)

    def gather(d_ref, sl):
        def body(g, carry):
            for j in range(SUBLANES):
                for k in range(TOP_K_INNER):
                    d = d_ref[0, k * T + g * SUBLANES + j]
                    pltpu.make_async_copy(_row_of(ys_hbm, d), ybuf.at[sl, k, g, pl.ds(j, 1)],
                                          sem.at[sl]).start(priority=k % 2)
            return carry
        lax.fori_loop(0, T // SUBLANES, body, 0)

    @pl.when(i == 0)
    def _():
        gather(dcur_ref, 0)

    @pl.when(i + 1 < pl.num_programs(0))
    def _():
        gather(dnxt_ref, 1 - slot)

    for k in range(TOP_K_INNER):
        pltpu.make_async_copy(ys_hbm.at[pl.ds(0, T // SUBLANES)], ybuf.at[slot, k], sem.at[slot]).wait()

    route = route_ref[...]
    moe = jnp.zeros(x1_ref.shape, F32)
    for k in range(TOP_K_INNER):
        moe = moe + route[:, TOP_K_INNER + k:TOP_K_INNER + k + 1] * ybuf[slot, k].reshape(x1_ref.shape)
    o_ref[...] = x1_ref[...] + mod_ref[5:6, :] * moe


def _combine(ys, dest, route, x1, mod, *, S, T):
    N, D = x1.shape
    C = ys.shape[1]
    steps = N // T
    nS = S // T
    dest3 = _dest_table(dest, T)
    dspec = lambda f: pl.BlockSpec((None, 1, TOP_K_INNER * T), f, memory_space=pltpu.SMEM)
    return pl.pallas_call(
        _combine_kernel,
        grid=(steps,),
        in_specs=[dspec(lambda i: (i, 0, 0)),
                  dspec(lambda i: (jnp.minimum(i + 1, steps - 1), 0, 0)),
                  pl.BlockSpec(memory_space=pl.ANY),
                  pl.BlockSpec((T, LANES), lambda i: (i, 0)),
                  pl.BlockSpec((T, D), lambda i: (i, 0)),
                  pl.BlockSpec((None, 6, D), lambda i: (i // nS, 0, 0))],
        out_specs=pl.BlockSpec((T, D), lambda i: (i, 0)),
        out_shape=jax.ShapeDtypeStruct((N, D), F32),
        scratch_shapes=[pltpu.VMEM((2, TOP_K_INNER, T // SUBLANES, SUBLANES, C), F32),
                        pltpu.SemaphoreType.DMA((2,))],
        compiler_params=_cparams(("arbitrary",)),
        name="moe_gather_combine",
    )(dest3, dest3, ys.reshape(ys.shape[0] // SUBLANES, SUBLANES, C), route, x1, mod)


def _layer(x2, mod, pos_row, l, B, S, g_mix, w_in, q_norm_g, k_norm_g, lambda_q1, lambda_k1, lambda_q2,
           lambda_k2, subln_g, b_glu, w_dw, b_dw, conv_ln_g, conv_ln_b, w_out, g_ffn, w_group, b_group,
           w_router, b_router, w_gate, w_up, w_down):
    N, D = x2.shape
    dh = q_norm_g.shape[0]
    H = N_DIFF_HEADS
    aw = H * 2 * dh
    cw = w_dw.shape[1]
    rot = dh // 4
    n_grp = w_group.shape[1]
    n_exp = w_router.shape[1]
    lambda_init = 0.8 - 0.6 * math.exp(-0.3 * l)
    tiles = _tiles(S)

    w_qk_t = w_in[:, :2 * aw].T.astype(BF16)
    w_vglu = w_in[:, 2 * aw:].astype(BF16)
    scale = dh ** -0.5 * math.log2(math.e)
    gq = jnp.tile(q_norm_g * scale, aw // dh)
    gk = jnp.tile(k_norm_g, aw // dh)
    gqk_tab = jnp.broadcast_to(jnp.concatenate([gq, gk])[:, None], (2 * aw, LANES))
    inv_freq = ROPE_THETA ** (-jnp.arange(0, rot, 2, dtype=F32) / rot)
    invf_tab = jnp.broadcast_to(inv_freq[:, None], (rot // 2, LANES))

    q, kt, v, u = _inproj(x2, mod, g_mix.reshape(1, D), pos_row, w_vglu, w_qk_t, b_glu.reshape(1, 2 * cw),
                          gqk_tab, invf_tab, B=B, S=S, aw=aw, cw=cw, dh=dh, rot=rot, T=tiles.inproj)

    attn = _diff_attention(q, kt, v, lambda_q1.reshape(1, dh), lambda_k1.reshape(1, dh),
                           lambda_q2.reshape(1, dh), lambda_k2.reshape(1, dh), subln_g.reshape(1, 2 * dh),
                           B=B, S=S, H=H, dh=dh, lambda_init=lambda_init, tile=tiles.attn,
                           chains=tiles.attn_chains)

    pad = LANES - n_exp - n_grp
    w_r = jnp.concatenate([w_router, w_group, jnp.zeros((D, pad), F32)], axis=1)
    b_r = jnp.concatenate([b_router, b_group, jnp.zeros((pad,), F32)]).reshape(1, LANES)
    wr_hi = w_r.astype(BF16)
    wr_lo = (w_r - wr_hi.astype(F32)).astype(BF16)
    wo = w_out.astype(BF16)
    x1, h2, route, route_t, counts = _mixout(attn, u, x2, mod, w_dw, b_dw.reshape(1, cw), conv_ln_g.reshape(1, cw),
                                     conv_ln_b.reshape(1, cw), wo[:aw], wo[aw:], g_ffn.reshape(1, D),
                                     wr_hi, wr_lo, b_r, S=S, T=tiles.mixout, n_exp=n_exp, n_grp=n_grp)

    cnt = counts[0, :n_exp].astype(jnp.int32)
    off = jnp.cumsum(cnt) - cnt
    experts = jnp.arange(n_exp, dtype=jnp.int32)[:, None]
    dest = [jnp.sum(jnp.where(route_t[k].astype(jnp.int32)[None, :] == experts, off[:, None], 0), axis=0)
            + route_t[4 + k].astype(jnp.int32) for k in range(TOP_K_INNER)]
    visits = _visit_tables(off, cnt, N * TOP_K_INNER, tiles.experts)

    xs = _scatter_rows(h2, dest, tokens=tiles.scatter)
    ys = _experts(xs, w_gate, w_up, w_down, visits, tm=tiles.experts)
    return _combine(ys, dest, route, x1, mod, S=S, T=tiles.combine)


def kernel(x, c, positions, w_ada, b_ada, g_mix, w_in, q_norm_g, k_norm_g, lambda_q1, lambda_k1, lambda_q2,
           lambda_k2, subln_g, b_glu, w_dw, b_dw, conv_ln_g, conv_ln_b, w_out, g_ffn, w_group, b_group,
           w_router, b_router, w_gate, w_up, w_down):
    B, S, D = x.shape
    depth = w_ada.shape[0]
    x2 = x.reshape(B * S, D)
    pos_row = positions.astype(F32).reshape(1, B * S)
    for l in range(depth):
        mod = _modulation(c, w_ada[l], b_ada[l])
        x2 = _layer(x2, mod, pos_row, l, B, S, g_mix[l], w_in[l], q_norm_g[l], k_norm_g[l], lambda_q1[l],
                    lambda_k1[l], lambda_q2[l], lambda_k2[l], subln_g[l], b_glu[l], w_dw[l], b_dw[l],
                    conv_ln_g[l], conv_ln_b[l], w_out[l], g_ffn[l], w_group[l], b_group[l], w_router[l],
                    b_router[l], w_gate[l], w_up[l], w_down[l])
    return x2.reshape(B, S, D)
```

```python
import collections
import functools
import math

import numpy as np
import jax
import jax.numpy as jnp
from jax import lax
from jax.experimental import pallas as pl
from jax.experimental.pallas import tpu as pltpu

F32 = jnp.float32
BF16 = jnp.bfloat16

EPS = 1e-6
ROPE_THETA = 500000.0
N_DIFF_HEADS = 4
TOP_K_INNER = 2

LANES = 128
SUBLANES = 8
CONV_HALO = 32
VMEM_LIMIT = 48 * 1024 * 1024


Tiles = collections.namedtuple("Tiles", "inproj attn attn_chains mixout scatter experts combine")


def _tiles(S):
    cap = lambda t: min(t, S)
    return Tiles(inproj=cap(1024), attn=cap(1024), attn_chains=2, mixout=cap(512), scatter=cap(1024),
                 experts=256, combine=cap(512))


def _cparams(sem):
    return pltpu.CompilerParams(dimension_semantics=sem, vmem_limit_bytes=VMEM_LIMIT)


def _silu(x):
    return x * jax.nn.sigmoid(x)


def _mod_kernel(c_ref, w_ref, b_ref, o_ref):
    c = c_ref[...]
    o_ref[...] = jnp.dot(_silu(c), w_ref[...], preferred_element_type=F32,
                         precision=lax.Precision.HIGHEST) + b_ref[...]


def _modulation(c, w_ada, b_ada):
    B, D = c.shape
    n_out = w_ada.shape[1]
    rows = 8
    c_pad = jnp.pad(c, ((0, rows - B), (0, 0)))
    bn = 2048
    out = pl.pallas_call(
        _mod_kernel,
        grid=(n_out // bn,),
        in_specs=[pl.BlockSpec((rows, D), lambda j: (0, 0)),
                  pl.BlockSpec((D, bn), lambda j: (0, j)),
                  pl.BlockSpec((1, bn), lambda j: (0, j))],
        out_specs=pl.BlockSpec((rows, bn), lambda j: (0, j)),
        out_shape=jax.ShapeDtypeStruct((rows, n_out), F32),
        compiler_params=_cparams(("parallel",)),
        name="adaln_mod",
    )(c_pad, w_ada, b_ada.reshape(1, n_out))
    return out[:B].reshape(B, 6, D)


W_CHUNK = 256


def _inproj_kernel(x_ref, mod_ref, g_ref, pos_ref, w_hbm, bglu_ref, gqk_ref, invf_ref,
                   q_ref, kt_ref, v_ref, u_ref, wv_ref, wqk_ref, stage, sem, *, aw, cw, dh, rot):
    T = x_ref.shape[0]
    half = rot // 2

    @pl.when(pl.program_id(0) == 0)
    def _():
        n_chunks = w_hbm.shape[1] // W_CHUNK
        copies = [pltpu.make_async_copy(w_hbm.at[:, pl.ds(j * W_CHUNK, W_CHUNK)], stage.at[j % 2], sem.at[j % 2])
                  for j in range(n_chunks)]
        copies[0].start()
        for j in range(n_chunks):
            if j + 1 < n_chunks:
                copies[j + 1].start()
            copies[j].wait()
            chunk = stage[j % 2]
            c0 = j * W_CHUNK
            if c0 < 2 * aw:
                wqk_ref[c0:c0 + W_CHUNK, :] = chunk.T.astype(BF16)
            else:
                wv_ref[:, c0 - 2 * aw:c0 - 2 * aw + W_CHUNK] = chunk.astype(BF16)

    x = x_ref[...]
    ms = jnp.mean(x * x, axis=-1, keepdims=True)
    sh = mod_ref[0:1, :]
    sc = mod_ref[1:2, :]
    h = x * lax.rsqrt(ms + EPS) * g_ref[...] * (1.0 + sc) + sh
    hb = h.astype(BF16)

    pv = jnp.dot(hb, wv_ref[...], preferred_element_type=F32)
    v_ref[...] = pv[:, :aw].astype(BF16)
    a = pv[:, aw:aw + cw] + bglu_ref[:, :cw]
    gate = pv[:, aw + cw:] + bglu_ref[:, cw:]
    u_ref[...] = (a * jax.nn.sigmoid(gate)).astype(BF16)

    qkt = lax.dot_general(wqk_ref[...], hb, (((1,), (1,)), ((), ())), preferred_element_type=F32)
    nch = 2 * aw // dh
    for c in range(T // LANES):
        sl = slice(c * LANES, (c + 1) * LANES)
        s3 = qkt[:, sl].reshape(nch, dh, LANES)
        ssq = jnp.mean(s3 * s3, axis=1, keepdims=True)
        y = s3 * lax.rsqrt(ssq + EPS) * gqk_ref[...].reshape(nch, dh, LANES)
        ang = invf_ref[...] * pos_ref[:, sl]
        cs = jnp.cos(ang)
        sn = jnp.sin(ang)
        t1 = y[:, 0:half, :]
        t2 = y[:, half:rot, :]
        y = jnp.concatenate([t1 * cs - t2 * sn, t2 * cs + t1 * sn, y[:, rot:, :]], axis=1)
        y2 = y.reshape(2 * aw, LANES)
        kt_ref[:, sl] = y2[aw:].astype(BF16)
        q_ref[sl, :] = y2[:aw].T.astype(BF16)


def _inproj(x2, mod, g_mix, pos_row, w_in, b_glu, gqk_tab, invf_tab, *, B, S, aw, cw, dh, rot, T):
    N, D = x2.shape
    nS = S // T
    n_vglu = w_in.shape[1] - 2 * aw
    kern = functools.partial(_inproj_kernel, aw=aw, cw=cw, dh=dh, rot=rot)
    return pl.pallas_call(
        kern,
        grid=(N // T,),
        in_specs=[pl.BlockSpec((T, D), lambda i: (i, 0)),
                  pl.BlockSpec((None, 6, D), lambda i: (i // nS, 0, 0)),
                  pl.BlockSpec((1, D), lambda i: (0, 0)),
                  pl.BlockSpec((1, T), lambda i: (0, i)),
                  pl.BlockSpec(memory_space=pl.ANY),
                  pl.BlockSpec((1, 2 * cw), lambda i: (0, 0)),
                  pl.BlockSpec(gqk_tab.shape, lambda i: (0, 0)),
                  pl.BlockSpec(invf_tab.shape, lambda i: (0, 0))],
        out_specs=[pl.BlockSpec((T, aw), lambda i: (i, 0)),
                   pl.BlockSpec((None, aw, T), lambda i: (i // nS, 0, i % nS)),
                   pl.BlockSpec((T, aw), lambda i: (i, 0)),
                   pl.BlockSpec((T, cw), lambda i: (i, 0))],
        out_shape=[jax.ShapeDtypeStruct((N, aw), BF16),
                   jax.ShapeDtypeStruct((B, aw, S), BF16),
                   jax.ShapeDtypeStruct((N, aw), BF16),
                   jax.ShapeDtypeStruct((N, cw), BF16)],
        scratch_shapes=[pltpu.VMEM((D, n_vglu), BF16),
                        pltpu.VMEM((2 * aw, D), BF16),
                        pltpu.VMEM((2, D, W_CHUNK), F32),
                        pltpu.SemaphoreType.DMA((2,))],
        compiler_params=_cparams(("arbitrary",)),
        name="inproj_qknorm_rope_glu",
    )(x2, mod, g_mix, pos_row, w_in, b_glu, gqk_tab, invf_tab)


def _attn_kernel(q_ref, kt_ref, v_ref, lq1_ref, lk1_ref, lq2_ref, lk2_ref, sg_ref, o_ref, acc_ref, s_ref,
                 *, tile, chains, dh, lambda_init):
    i = pl.program_id(2)
    hd = 2 * dh
    rows = tile // chains

    def stacked(q):
        lane = lax.broadcasted_iota(jnp.int32, q.shape, 1)
        zero = jnp.zeros_like(q)
        return jnp.concatenate([jnp.where(lane < dh, q, zero), jnp.where(lane >= dh, q, zero)], axis=0)

    qs = [stacked(q_ref[c * rows:(c + 1) * rows, :]) for c in range(chains)]

    acc_ref[...] = jnp.zeros_like(acc_ref)
    ones = {w: jnp.ones((w, hd), BF16) for w in {tile} | {(c + 1) * rows for c in range(chains)}}

    def scores(t, c):
        start = pl.multiple_of(t * tile, tile)
        s_ref[c] = jnp.dot(qs[c], kt_ref[:, pl.ds(start, tile)], preferred_element_type=F32)

    def softmax_pv(t, c, m, width=tile, diagonal=False):
        start = pl.multiple_of(t * tile, tile)
        vt = jnp.concatenate([v_ref[pl.ds(start, width), :], ones[width]], axis=1)
        s = s_ref[c, :, :width]
        if diagonal:
            row = lax.broadcasted_iota(jnp.int32, (2 * rows, rows), 0)
            col = lax.broadcasted_iota(jnp.int32, (2 * rows, rows), 1)
            qrow = jnp.where(row >= rows, row - rows, row)
            tail = jnp.where(col <= qrow, s[:, width - rows:], -jnp.inf)
            s = tail if width == rows else jnp.concatenate([s[:, :width - rows], tail], axis=1)
        m_new = jnp.maximum(m, jnp.max(s, axis=1, keepdims=True))
        alpha = jnp.exp2(m - m_new)
        p = jnp.exp2(s - m_new).astype(BF16)
        acc_ref[c] = alpha * acc_ref[c] + jnp.dot(p, vt, preferred_element_type=F32)
        return m_new

    m0 = jnp.full((2 * rows, 1), -jnp.inf, F32)
    scores(0, 0)

    def full_tile(t, ms):
        ms = list(ms)
        for c in range(chains):
            if c + 1 < chains:
                scores(t, c + 1)
            else:
                scores(t + 1, 0)
            ms[c] = softmax_pv(t, c, ms[c])
        return tuple(ms)

    ms = lax.fori_loop(0, i, full_tile, (m0,) * chains)

    for c in range(chains):
        if c + 1 < chains:
            scores(i, c + 1)
        softmax_pv(i, c, ms[c], width=(c + 1) * rows, diagonal=True)

    lam = (jnp.exp(jnp.sum(lq1_ref[...] * lk1_ref[...], axis=1, keepdims=True))
           - jnp.exp(jnp.sum(lq2_ref[...] * lk2_ref[...], axis=1, keepdims=True)) + lambda_init)
    for c in range(chains):
        o = (acc_ref[c, 0:rows, 0:hd] / acc_ref[c, 0:rows, hd:]
             - lam * (acc_ref[c, rows:, 0:hd] / acc_ref[c, rows:, hd:]))
        ms_o = jnp.mean(o * o, axis=1, keepdims=True)
        o = o * lax.rsqrt(ms_o + EPS) * sg_ref[...] * (1.0 - lambda_init)
        o_ref[c * rows:(c + 1) * rows, :] = o.astype(o_ref.dtype)


def _diff_attention(q, kt, v, lq1, lk1, lq2, lk2, subln_g, *, B, S, H, dh, lambda_init, tile, chains):
    aw = H * 2 * dh
    q3 = q.reshape(B, S, aw)
    v3 = v.reshape(B, S, aw)
    hd = 2 * dh
    tile = min(tile, S)
    rows = tile // chains
    kern = functools.partial(_attn_kernel, tile=tile, chains=chains, dh=dh, lambda_init=lambda_init)
    vec = pl.BlockSpec((1, dh), lambda b, h, i: (0, 0))
    out = pl.pallas_call(
        kern,
        grid=(B, H, S // tile),
        in_specs=[pl.BlockSpec((None, tile, hd), lambda b, h, i: (b, i, h)),
                  pl.BlockSpec((None, hd, S), lambda b, h, i: (b, h, 0)),
                  pl.BlockSpec((None, S, hd), lambda b, h, i: (b, 0, h)),
                  vec, vec, vec, vec,
                  pl.BlockSpec((1, hd), lambda b, h, i: (0, 0))],
        out_specs=pl.BlockSpec((None, tile, hd), lambda b, h, i: (b, i, h)),
        out_shape=jax.ShapeDtypeStruct((B, S, aw), BF16),
        scratch_shapes=[pltpu.VMEM((chains, 2 * rows, 2 * hd), F32),
                        pltpu.VMEM((chains, 2 * rows, tile), F32)],
        compiler_params=_cparams(("parallel", "parallel", "parallel")),
        name="diff_flash_attention",
    )(q3, kt, v3, lq1, lk1, lq2, lk2, subln_g)
    return out.reshape(B * S, aw)


def _mixout_kernel(attn_ref, ucur_ref, uhalo_ref, x_ref, mod_ref, wdw_ref, bdw_ref, lng_ref, lnb_ref,
                   wo1_ref, wo2_ref, gffn_ref, wrh_ref, wrl_ref, br_ref,
                   x1_ref, h2_ref, route_ref, route_t_ref, cnt_ref, ubuf_ref, conv_ref,
                   *, nS, conv_k, n_exp, n_grp):
    T = x_ref.shape[0]
    cw = ucur_ref.shape[1]
    i = pl.program_id(0)

    @pl.when(i == 0)
    def _():
        cnt_ref[...] = jnp.zeros_like(cnt_ref)

    first = (i % nS) == 0
    halo = uhalo_ref[...].astype(F32)
    ubuf_ref[0, 0:CONV_HALO, :] = jnp.where(first, jnp.zeros_like(halo), halo)
    ubuf_ref[0, CONV_HALO:, :] = ucur_ref[...].astype(F32)
    span = T + CONV_HALO - SUBLANES
    for b in range(1, SUBLANES):
        ubuf_ref[b, 0:span, :] = ubuf_ref[0, b:b + span, :]

    off = CONV_HALO - (conv_k - 1)
    rows = 32
    for r0 in range(0, T, rows):
        acc = jnp.zeros((rows // SUBLANES, SUBLANES, cw), F32)
        for j in range(conv_k):
            a, b = divmod(j + off, SUBLANES)
            lo_r = r0 + a * SUBLANES
            acc = acc + (wdw_ref[j * SUBLANES:(j + 1) * SUBLANES, :]
                         * ubuf_ref[b, lo_r:lo_r + rows, :].reshape(rows // SUBLANES, SUBLANES, cw))
        y = acc.reshape(rows, cw) + bdw_ref[...]
        mu = jnp.mean(y, axis=1, keepdims=True)
        d = y - mu
        var = jnp.mean(d * d, axis=1, keepdims=True)
        z = d * lax.rsqrt(var + EPS) * lng_ref[...] + lnb_ref[...]
        conv_ref[r0:r0 + rows, :] = _silu(z).astype(BF16)

    yo = (jnp.dot(attn_ref[...], wo1_ref[...], preferred_element_type=F32)
          + jnp.dot(conv_ref[...], wo2_ref[...], preferred_element_type=F32))
    x1 = x_ref[...] + mod_ref[2:3, :] * yo
    x1_ref[...] = x1
    ms = jnp.mean(x1 * x1, axis=1, keepdims=True)
    h2 = x1 * lax.rsqrt(ms + EPS) * gffn_ref[...] * (1.0 + mod_ref[4:5, :]) + mod_ref[3:4, :]
    h2_ref[...] = h2
    hi = h2.astype(BF16)
    lo = (h2 - hi.astype(F32)).astype(BF16)

    logits = (jnp.dot(hi, wrh_ref[...], preferred_element_type=F32)
              + jnp.dot(lo, wrh_ref[...], preferred_element_type=F32)
              + jnp.dot(hi, wrl_ref[...], preferred_element_type=F32)) + br_ref[...]
    lane_i = lax.broadcasted_iota(jnp.int32, logits.shape, 1)
    lane = lane_i.astype(F32)
    big = jnp.float32(1e9)
    ninf = jnp.float32(-jnp.inf)
    is_g = (lane_i >= n_exp) & (lane_i < n_exp + n_grp)
    gl = jnp.where(is_g, logits, ninf)
    gmax = jnp.max(gl, axis=1, keepdims=True)
    gsum = jnp.sum(jnp.where(is_g, jnp.exp(gl - gmax), 0.0), axis=1, keepdims=True)
    g_p = 1.0 / gsum
    gidx = jnp.min(jnp.where(gl == gmax, lane, big), axis=1, keepdims=True) - n_exp
    epg = n_exp // n_grp
    lo_l = gidx * epg
    in_grp = (lane >= lo_l) & (lane < lo_l + epg)
    el = jnp.where(in_grp, logits, ninf)
    m1 = jnp.max(el, axis=1, keepdims=True)
    i1 = jnp.min(jnp.where(el == m1, lane, big), axis=1, keepdims=True)
    el2 = jnp.where(lane == i1, ninf, el)
    m2 = jnp.max(el2, axis=1, keepdims=True)
    i2 = jnp.min(jnp.where(el2 == m2, lane, big), axis=1, keepdims=True)
    e2 = jnp.exp(m2 - m1)
    p1 = 1.0 / (1.0 + e2)
    p2 = e2 / (1.0 + e2)

    sel = (lane == i1) | (lane == i2)
    rr = lax.broadcasted_iota(jnp.int32, (T, T), 0)
    cc = lax.broadcasted_iota(jnp.int32, (T, T), 1)
    before = jnp.dot((rr > cc).astype(BF16), sel.astype(BF16), preferred_element_type=F32) + cnt_ref[...]
    r1 = jnp.sum(jnp.where(lane == i1, before, 0.0), axis=1, keepdims=True)
    r2 = jnp.sum(jnp.where(lane == i2, before, 0.0), axis=1, keepdims=True)
    cnt_ref[...] += jnp.sum(sel.astype(F32), axis=0, keepdims=True)

    route = jnp.zeros_like(logits)
    for k, val in enumerate((i1, i2, g_p * p1, g_p * p2, r1, r2)):
        route = jnp.where(lane_i == k, val, route)
    route_ref[...] = route
    route_t_ref[...] = route.T[:SUBLANES]


def _mixout(attn, u, x2, mod, w_dw, b_dw, ln_g, ln_b, wo1, wo2, g_ffn, wr_hi, wr_lo, b_r,
            *, S, T, n_exp, n_grp):
    N, D = x2.shape
    aw = attn.shape[1]
    cw = u.shape[1]
    nS = S // T
    conv_k = w_dw.shape[0]
    w_dw = jnp.repeat(w_dw, SUBLANES, axis=0)
    hb = T // CONV_HALO
    kern = functools.partial(_mixout_kernel, nS=nS, conv_k=conv_k, n_exp=n_exp, n_grp=n_grp)
    full = lambda a: pl.BlockSpec(a.shape, lambda i: (0, 0))
    return pl.pallas_call(
        kern,
        grid=(N // T,),
        in_specs=[pl.BlockSpec((T, aw), lambda i: (i, 0)),
                  pl.BlockSpec((T, cw), lambda i: (i, 0)),
                  pl.BlockSpec((CONV_HALO, cw), lambda i: (jnp.maximum(i * hb - 1, 0), 0)),
                  pl.BlockSpec((T, D), lambda i: (i, 0)),
                  pl.BlockSpec((None, 6, D), lambda i: (i // nS, 0, 0)),
                  full(w_dw), full(b_dw), full(ln_g), full(ln_b), full(wo1), full(wo2), full(g_ffn),
                  full(wr_hi), full(wr_lo), full(b_r)],
        out_specs=[pl.BlockSpec((T, D), lambda i: (i, 0)),
                   pl.BlockSpec((T, D), lambda i: (i, 0)),
                   pl.BlockSpec((T, LANES), lambda i: (i, 0)),
                   pl.BlockSpec((SUBLANES, T), lambda i: (0, i)),
                   pl.BlockSpec((1, LANES), lambda i: (0, 0))],
        out_shape=[jax.ShapeDtypeStruct((N, D), F32),
                   jax.ShapeDtypeStruct((N, D), F32),
                   jax.ShapeDtypeStruct((N, LANES), F32),
                   jax.ShapeDtypeStruct((SUBLANES, N), F32),
                   jax.ShapeDtypeStruct((1, LANES), F32)],
        scratch_shapes=[pltpu.VMEM((SUBLANES, CONV_HALO + T, cw), F32),
                        pltpu.VMEM((T, cw), BF16)],
        compiler_params=_cparams(("arbitrary",)),
        name="conv_outproj_router",
    )(attn, u, u, x2, mod, w_dw, b_dw, ln_g, ln_b, wo1, wo2, g_ffn, wr_hi, wr_lo, b_r)


def _row_of(ref, r):
    if isinstance(r, int):
        return ref.at[r // SUBLANES, pl.ds(r % SUBLANES, 1)]
    return ref.at[lax.shift_right_logical(r, 3), pl.ds(r & (SUBLANES - 1), 1)]


def _scatter_kernel(dest_ref, h2_ref, xs_hbm, sem):
    groups = h2_ref.shape[0]
    tokens = groups * SUBLANES

    def issue(g, carry):
        for j in range(SUBLANES):
            for k in range(TOP_K_INNER):
                d = dest_ref[0, k * tokens + g * SUBLANES + j]
                pltpu.make_async_copy(h2_ref.at[g, pl.ds(j, 1)], _row_of(xs_hbm, d), sem).start(priority=k % 2)
        return carry

    lax.fori_loop(0, groups, issue, 0)
    for k in range(TOP_K_INNER):
        pltpu.make_async_copy(h2_ref, xs_hbm.at[pl.ds(0, groups)], sem).wait()


def _dest_table(dest, tokens):
    tab = jnp.concatenate([d.reshape(-1, tokens) for d in dest], axis=1)
    return tab.reshape(tab.shape[0], 1, tab.shape[1])


def _scatter_rows(h2, dest, *, tokens):
    N, C = h2.shape
    steps = N // tokens
    dest3 = _dest_table(dest, tokens)
    rows = N * TOP_K_INNER
    xs = pl.pallas_call(
        _scatter_kernel,
        grid=(steps,),
        in_specs=[pl.BlockSpec((None, 1, TOP_K_INNER * tokens), lambda s: (s, 0, 0), memory_space=pltpu.SMEM),
                  pl.BlockSpec((tokens // SUBLANES, SUBLANES, C), lambda s: (s, 0, 0))],
        out_specs=pl.BlockSpec(memory_space=pl.ANY),
        out_shape=jax.ShapeDtypeStruct((rows // SUBLANES, SUBLANES, C), h2.dtype),
        scratch_shapes=[pltpu.SemaphoreType.DMA],
        compiler_params=_cparams(("arbitrary",)),
        name="moe_scatter_rows",
    )(dest3, h2.reshape(N // SUBLANES, SUBLANES, C))
    return xs.reshape(rows, C)


XS_RING = 3


def _experts_kernel(vt_ref, ve_ref, vlo_ref, vhi_ref, vnew_ref, vnext_ref, vslot_ref,
                    xs_hbm, wg_hbm, wu_hbm, wd_hbm, ys_ref, xbuf, wg_buf, wu_buf, wd_buf, sem, wsem,
                    *, n_tiles):
    v = pl.program_id(0)
    lo = vlo_ref[v]
    hi = vhi_ref[v]
    t = vt_ref[v]
    tm = xbuf.shape[1]
    wslot = vslot_ref[v]

    def weight_copies(expert, slot):
        return [pltpu.make_async_copy(hbm.at[expert], buf.at[slot], wsem.at[slot, j])
                for j, (hbm, buf) in enumerate(((wg_hbm, wg_buf), (wu_hbm, wu_buf), (wd_hbm, wd_buf)))]

    @pl.when(v == 0)
    def _():
        for c in weight_copies(ve_ref[0], 0):
            c.start()

    @pl.when(vnew_ref[v] == 1)
    def _():
        @pl.when(vnext_ref[v] >= 0)
        def _():
            for c in weight_copies(vnext_ref[v], 1 - wslot):
                c.start()

        for c in weight_copies(ve_ref[v], wslot):
            c.wait()

    def tile_copy(tile):
        slot = lax.rem(tile, XS_RING)
        return pltpu.make_async_copy(xs_hbm.at[pl.ds(pl.multiple_of(tile * tm, tm), tm)], xbuf.at[slot],
                                     sem.at[slot])

    @pl.when(v == 0)
    def _():
        for tile in range(min(XS_RING - 1, n_tiles)):
            tile_copy(tile).start()

    @pl.when((lo == 0) & (hi > lo))
    def _():
        @pl.when(t + (XS_RING - 1) < n_tiles)
        def _():
            tile_copy(t + (XS_RING - 1)).start()

        tile_copy(t).wait()

    @pl.when(hi > lo)
    def _():
        x = xbuf[lax.rem(t, XS_RING)].astype(BF16)
        g = jnp.dot(x, wg_buf[wslot].astype(BF16), preferred_element_type=F32)
        u = jnp.dot(x, wu_buf[wslot].astype(BF16), preferred_element_type=F32)
        hid = (_silu(g) * u).astype(BF16)
        y = jnp.dot(hid, wd_buf[wslot].astype(BF16), preferred_element_type=F32)

        @pl.when(lo == 0)
        def _():
            ys_ref[...] = y

        @pl.when(lo > 0)
        def _():
            row = lax.broadcasted_iota(jnp.int32, y.shape, 0)
            ys_ref[...] = jnp.where((row >= lo) & (row < hi), y, ys_ref[...])


def _experts(xs, w_gate, w_up, w_down, visits, *, tm):
    R, C = xs.shape
    E, D, ff = w_gate.shape
    vt = visits[0]
    any_spec = pl.BlockSpec(memory_space=pl.ANY)
    grid_spec = pltpu.PrefetchScalarGridSpec(
        num_scalar_prefetch=len(visits),
        grid=(vt.shape[0],),
        in_specs=[any_spec, any_spec, any_spec, any_spec],
        out_specs=pl.BlockSpec((tm, C), lambda v, vt, *_: (vt[v], 0)),
        scratch_shapes=[pltpu.VMEM((XS_RING, tm, C), F32),
                        pltpu.VMEM((2, D, ff), w_gate.dtype),
                        pltpu.VMEM((2, D, ff), w_up.dtype),
                        pltpu.VMEM((2, ff, D), w_down.dtype),
                        pltpu.SemaphoreType.DMA((XS_RING,)),
                        pltpu.SemaphoreType.DMA((2, 3))],
    )
    return pl.pallas_call(
        functools.partial(_experts_kernel, n_tiles=R // tm),
        grid_spec=grid_spec,
        out_shape=jax.ShapeDtypeStruct((R, C), F32),
        compiler_params=_cparams(("arbitrary",)),
        name="moe_grouped_experts",
    )(*visits, xs, w_gate, w_up, w_down)


def _visit_tables(off, cnt, n_rows, tm):
    n_tiles = n_rows // tm
    n_exp = off.shape[0]
    n_visits = n_tiles + n_exp - 1
    tile_starts = jnp.arange(n_tiles, dtype=jnp.int32) * tm
    seg_starts = jnp.where((cnt > 0) & (off % tm != 0), off, n_rows)
    starts = jnp.sort(jnp.concatenate([tile_starts, seg_starts]))
    lo_abs = starts[:n_visits]
    hi_abs = starts[1:n_visits + 1]
    valid = lo_abs < n_rows
    tile = jnp.where(valid, lo_abs // tm, n_tiles - 1)
    ends = off + cnt
    probe = jnp.where(valid, lo_abs, n_rows - 1)
    expert = jnp.sum(ends[None, :] <= probe[:, None], axis=1).astype(jnp.int32)
    row_lo = jnp.where(valid, lo_abs - tile * tm, 0)
    row_hi = jnp.where(valid, hi_abs - tile * tm, 0)
    new = jnp.concatenate([jnp.ones((1,), jnp.int32), (expert[1:] != expert[:-1]).astype(jnp.int32)])
    slot = (jnp.cumsum(new) - 1) % 2
    nxt = jnp.sum(expert[None, :] <= expert[:, None], axis=1)
    nxt_expert = jnp.where(nxt < n_visits, expert[jnp.minimum(nxt, n_visits - 1)], -1)
    return tile, expert, row_lo, row_hi, new, nxt_expert.astype(jnp.int32), slot.astype(jnp.int32)


def _combine_kernel(dcur_ref, dnxt_ref, ys_hbm, route_ref, x1_ref, mod_ref, o_ref, ybuf, sem):
    T = x1_ref.shape[0]
    i = pl.program_id(0)
    slot = lax.rem(i, 2)

    def gather(d_ref, sl):
        def body(g, carry):
            for j in range(SUBLANES):
                for k in range(TOP_K_INNER):
                    d = d_ref[0, k * T + g * SUBLANES + j]
                    pltpu.make_async_copy(_row_of(ys_hbm, d), ybuf.at[sl, k, g, pl.ds(j, 1)],
                                          sem.at[sl]).start(priority=k % 2)
            return carry
        lax.fori_loop(0, T // SUBLANES, body, 0)

    @pl.when(i == 0)
    def _():
        gather(dcur_ref, 0)

    @pl.when(i + 1 < pl.num_programs(0))
    def _():
        gather(dnxt_ref, 1 - slot)

    for k in range(TOP_K_INNER):
        pltpu.make_async_copy(ys_hbm.at[pl.ds(0, T // SUBLANES)], ybuf.at[slot, k], sem.at[slot]).wait()

    route = route_ref[...]
    moe = jnp.zeros(x1_ref.shape, F32)
    for k in range(TOP_K_INNER):
        moe = moe + route[:, TOP_K_INNER + k:TOP_K_INNER + k + 1] * ybuf[slot, k].reshape(x1_ref.shape)
    o_ref[...] = x1_ref[...] + mod_ref[5:6, :] * moe


def _combine(ys, dest, route, x1, mod, *, S, T):
    N, D = x1.shape
    C = ys.shape[1]
    steps = N // T
    nS = S // T
    dest3 = _dest_table(dest, T)
    dspec = lambda f: pl.BlockSpec((None, 1, TOP_K_INNER * T), f, memory_space=pltpu.SMEM)
    return pl.pallas_call(
        _combine_kernel,
        grid=(steps,),
        in_specs=[dspec(lambda i: (i, 0, 0)),
                  dspec(lambda i: (jnp.minimum(i + 1, steps - 1), 0, 0)),
                  pl.BlockSpec(memory_space=pl.ANY),
                  pl.BlockSpec((T, LANES), lambda i: (i, 0)),
                  pl.BlockSpec((T, D), lambda i: (i, 0)),
                  pl.BlockSpec((None, 6, D), lambda i: (i // nS, 0, 0))],
        out_specs=pl.BlockSpec((T, D), lambda i: (i, 0)),
        out_shape=jax.ShapeDtypeStruct((N, D), F32),
        scratch_shapes=[pltpu.VMEM((2, TOP_K_INNER, T // SUBLANES, SUBLANES, C), F32),
                        pltpu.SemaphoreType.DMA((2,))],
        compiler_params=_cparams(("arbitrary",)),
        name="moe_gather_combine",
    )(dest3, dest3, ys.reshape(ys.shape[0] // SUBLANES, SUBLANES, C), route, x1, mod)


def _layer(x2, mod, pos_row, l, B, S, g_mix, w_in, q_norm_g, k_norm_g, lambda_q1, lambda_k1, lambda_q2,
           lambda_k2, subln_g, b_glu, w_dw, b_dw, conv_ln_g, conv_ln_b, w_out, g_ffn, w_group, b_group,
           w_router, b_router, w_gate, w_up, w_down):
    N, D = x2.shape
    dh = q_norm_g.shape[0]
    H = N_DIFF_HEADS
    aw = H * 2 * dh
    cw = w_dw.shape[1]
    rot = dh // 4
    n_grp = w_group.shape[1]
    n_exp = w_router.shape[1]
    lambda_init = 0.8 - 0.6 * math.exp(-0.3 * l)
    tiles = _tiles(S)

    scale = dh ** -0.5 * math.log2(math.e)
    gq = jnp.tile(q_norm_g * scale, aw // dh)
    gk = jnp.tile(k_norm_g, aw // dh)
    gqk_tab = jnp.broadcast_to(jnp.concatenate([gq, gk])[:, None], (2 * aw, LANES))
    inv_freq = ROPE_THETA ** (-jnp.arange(0, rot, 2, dtype=F32) / rot)
    invf_tab = jnp.broadcast_to(inv_freq[:, None], (rot // 2, LANES))

    q, kt, v, u = _inproj(x2, mod, g_mix.reshape(1, D), pos_row, w_in, b_glu.reshape(1, 2 * cw),
                          gqk_tab, invf_tab, B=B, S=S, aw=aw, cw=cw, dh=dh, rot=rot, T=tiles.inproj)

    attn = _diff_attention(q, kt, v, lambda_q1.reshape(1, dh), lambda_k1.reshape(1, dh),
                           lambda_q2.reshape(1, dh), lambda_k2.reshape(1, dh), subln_g.reshape(1, 2 * dh),
                           B=B, S=S, H=H, dh=dh, lambda_init=lambda_init, tile=tiles.attn,
                           chains=tiles.attn_chains)

    pad = LANES - n_exp - n_grp
    w_r = jnp.concatenate([w_router, w_group, jnp.zeros((D, pad), F32)], axis=1)
    b_r = jnp.concatenate([b_router, b_group, jnp.zeros((pad,), F32)]).reshape(1, LANES)
    wr_hi = w_r.astype(BF16)
    wr_lo = (w_r - wr_hi.astype(F32)).astype(BF16)
    wo = w_out.astype(BF16)
    x1, h2, route, route_t, counts = _mixout(attn, u, x2, mod, w_dw, b_dw.reshape(1, cw), conv_ln_g.reshape(1, cw),
                                     conv_ln_b.reshape(1, cw), wo[:aw], wo[aw:], g_ffn.reshape(1, D),
                                     wr_hi, wr_lo, b_r, S=S, T=tiles.mixout, n_exp=n_exp, n_grp=n_grp)

    cnt = counts[0, :n_exp].astype(jnp.int32)
    off = jnp.cumsum(cnt) - cnt
    experts = jnp.arange(n_exp, dtype=jnp.int32)[:, None]
    dest = [jnp.sum(jnp.where(route_t[k].astype(jnp.int32)[None, :] == experts, off[:, None], 0), axis=0)
            + route_t[4 + k].astype(jnp.int32) for k in range(TOP_K_INNER)]
    visits = _visit_tables(off, cnt, N * TOP_K_INNER, tiles.experts)

    xs = _scatter_rows(h2, dest, tokens=tiles.scatter)
    ys = _experts(xs, w_gate, w_up, w_down, visits, tm=tiles.experts)
    return _combine(ys, dest, route, x1, mod, S=S, T=tiles.combine)


def kernel(x, c, positions, w_ada, b_ada, g_mix, w_in, q_norm_g, k_norm_g, lambda_q1, lambda_k1, lambda_q2,
           lambda_k2, subln_g, b_glu, w_dw, b_dw, conv_ln_g, conv_ln_b, w_out, g_ffn, w_group, b_group,
           w_router, b_router, w_gate, w_up, w_down):
    B, S, D = x.shape
    depth = w_ada.shape[0]
    x2 = x.reshape(B * S, D)
    pos_row = positions.astype(F32).reshape(1, B * S)
    for l in range(depth):
        mod = _modulation(c, w_ada[l], b_ada[l])
        x2 = _layer(x2, mod, pos_row, l, B, S, g_mix[l], w_in[l], q_norm_g[l], k_norm_g[l], lambda_q1[l],
                    lambda_k1[l], lambda_q2[l], lambda_k2[l], subln_g[l], b_glu[l], w_dw[l], b_dw[l],
                    conv_ln_g[l], conv_ln_b[l], w_out[l], g_ffn[l], w_group[l], b_group[l], w_router[l],
                    b_router[l], w_gate[l], w_up[l], w_down[l])
    return x2.reshape(B, S, D)
```

```python
import collections
import functools
import math

import numpy as np
import jax
import jax.numpy as jnp
from jax import lax
from jax.experimental import pallas as pl
from jax.experimental.pallas import tpu as pltpu

F32 = jnp.float32
BF16 = jnp.bfloat16

EPS = 1e-6
ROPE_THETA = 500000.0
N_DIFF_HEADS = 4
TOP_K_INNER = 2

LANES = 128
SUBLANES = 8
CONV_HALO = 32
VMEM_LIMIT = 48 * 1024 * 1024


Tiles = collections.namedtuple("Tiles", "inproj attn attn_chains mixout scatter experts combine")


def _tiles(S):
    cap = lambda t: min(t, S)
    return Tiles(inproj=cap(1024), attn=cap(1024), attn_chains=2, mixout=cap(512), scatter=cap(1024),
                 experts=512, combine=cap(512))


def _cparams(sem):
    return pltpu.CompilerParams(dimension_semantics=sem, vmem_limit_bytes=VMEM_LIMIT)


def _silu(x):
    return x * jax.nn.sigmoid(x)


def _mod_kernel(c_ref, w_ref, b_ref, o_ref):
    c = c_ref[...]
    o_ref[...] = jnp.dot(_silu(c), w_ref[...], preferred_element_type=F32,
                         precision=lax.Precision.HIGHEST) + b_ref[...]


def _modulation(c, w_ada, b_ada):
    B, D = c.shape
    n_out = w_ada.shape[1]
    rows = 8
    c_pad = jnp.pad(c, ((0, rows - B), (0, 0)))
    bn = 2048
    out = pl.pallas_call(
        _mod_kernel,
        grid=(n_out // bn,),
        in_specs=[pl.BlockSpec((rows, D), lambda j: (0, 0)),
                  pl.BlockSpec((D, bn), lambda j: (0, j)),
                  pl.BlockSpec((1, bn), lambda j: (0, j))],
        out_specs=pl.BlockSpec((rows, bn), lambda j: (0, j)),
        out_shape=jax.ShapeDtypeStruct((rows, n_out), F32),
        compiler_params=_cparams(("parallel",)),
        name="adaln_mod",
    )(c_pad, w_ada, b_ada.reshape(1, n_out))
    return out[:B].reshape(B, 6, D)


W_CHUNK = 256


def _inproj_kernel(x_ref, mod_ref, g_ref, pos_ref, w_hbm, bglu_ref, gqk_ref, invf_ref,
                   q_ref, kt_ref, v_ref, u_ref, wv_ref, wqk_ref, stage, sem, *, aw, cw, dh, rot):
    T = x_ref.shape[0]
    half = rot // 2

    @pl.when(pl.program_id(0) == 0)
    def _():
        n_chunks = w_hbm.shape[1] // W_CHUNK
        copies = [pltpu.make_async_copy(w_hbm.at[:, pl.ds(j * W_CHUNK, W_CHUNK)], stage.at[j % 2], sem.at[j % 2])
                  for j in range(n_chunks)]
        copies[0].start()
        for j in range(n_chunks):
            if j + 1 < n_chunks:
                copies[j + 1].start()
            copies[j].wait()
            chunk = stage[j % 2]
            c0 = j * W_CHUNK
            if c0 < 2 * aw:
                wqk_ref[c0:c0 + W_CHUNK, :] = chunk.T.astype(BF16)
            else:
                wv_ref[:, c0 - 2 * aw:c0 - 2 * aw + W_CHUNK] = chunk.astype(BF16)

    x = x_ref[...]
    ms = jnp.mean(x * x, axis=-1, keepdims=True)
    sh = mod_ref[0:1, :]
    sc = mod_ref[1:2, :]
    h = x * lax.rsqrt(ms + EPS) * g_ref[...] * (1.0 + sc) + sh
    hb = h.astype(BF16)

    pv = jnp.dot(hb, wv_ref[...], preferred_element_type=F32)
    v_ref[...] = pv[:, :aw].astype(BF16)
    a = pv[:, aw:aw + cw] + bglu_ref[:, :cw]
    gate = pv[:, aw + cw:] + bglu_ref[:, cw:]
    u_ref[...] = (a * jax.nn.sigmoid(gate)).astype(BF16)

    qkt = lax.dot_general(wqk_ref[...], hb, (((1,), (1,)), ((), ())), preferred_element_type=F32)
    nch = 2 * aw // dh
    for c in range(T // LANES):
        sl = slice(c * LANES, (c + 1) * LANES)
        s3 = qkt[:, sl].reshape(nch, dh, LANES)
        ssq = jnp.mean(s3 * s3, axis=1, keepdims=True)
        y = s3 * lax.rsqrt(ssq + EPS) * gqk_ref[...].reshape(nch, dh, LANES)
        ang = invf_ref[...] * pos_ref[:, sl]
        cs = jnp.cos(ang)
        sn = jnp.sin(ang)
        t1 = y[:, 0:half, :]
        t2 = y[:, half:rot, :]
        y = jnp.concatenate([t1 * cs - t2 * sn, t2 * cs + t1 * sn, y[:, rot:, :]], axis=1)
        y2 = y.reshape(2 * aw, LANES)
        kt_ref[:, sl] = y2[aw:].astype(BF16)
        q_ref[sl, :] = y2[:aw].T.astype(BF16)


def _inproj(x2, mod, g_mix, pos_row, w_in, b_glu, gqk_tab, invf_tab, *, B, S, aw, cw, dh, rot, T):
    N, D = x2.shape
    nS = S // T
    n_vglu = w_in.shape[1] - 2 * aw
    kern = functools.partial(_inproj_kernel, aw=aw, cw=cw, dh=dh, rot=rot)
    return pl.pallas_call(
        kern,
        grid=(N // T,),
        in_specs=[pl.BlockSpec((T, D), lambda i: (i, 0)),
                  pl.BlockSpec((None, 6, D), lambda i: (i // nS, 0, 0)),
                  pl.BlockSpec((1, D), lambda i: (0, 0)),
                  pl.BlockSpec((1, T), lambda i: (0, i)),
                  pl.BlockSpec(memory_space=pl.ANY),
                  pl.BlockSpec((1, 2 * cw), lambda i: (0, 0)),
                  pl.BlockSpec(gqk_tab.shape, lambda i: (0, 0)),
                  pl.BlockSpec(invf_tab.shape, lambda i: (0, 0))],
        out_specs=[pl.BlockSpec((T, aw), lambda i: (i, 0)),
                   pl.BlockSpec((None, aw, T), lambda i: (i // nS, 0, i % nS)),
                   pl.BlockSpec((T, aw), lambda i: (i, 0)),
                   pl.BlockSpec((T, cw), lambda i: (i, 0))],
        out_shape=[jax.ShapeDtypeStruct((N, aw), BF16),
                   jax.ShapeDtypeStruct((B, aw, S), BF16),
                   jax.ShapeDtypeStruct((N, aw), BF16),
                   jax.ShapeDtypeStruct((N, cw), BF16)],
        scratch_shapes=[pltpu.VMEM((D, n_vglu), BF16),
                        pltpu.VMEM((2 * aw, D), BF16),
                        pltpu.VMEM((2, D, W_CHUNK), F32),
                        pltpu.SemaphoreType.DMA((2,))],
        compiler_params=_cparams(("arbitrary",)),
        name="inproj_qknorm_rope_glu",
    )(x2, mod, g_mix, pos_row, w_in, b_glu, gqk_tab, invf_tab)


def _attn_kernel(q_ref, kt_ref, v_ref, lq1_ref, lk1_ref, lq2_ref, lk2_ref, sg_ref, o_ref, acc_ref, s_ref,
                 *, tile, chains, dh, lambda_init):
    i = pl.program_id(2)
    hd = 2 * dh
    rows = tile // chains

    def stacked(q):
        lane = lax.broadcasted_iota(jnp.int32, q.shape, 1)
        zero = jnp.zeros_like(q)
        return jnp.concatenate([jnp.where(lane < dh, q, zero), jnp.where(lane >= dh, q, zero)], axis=0)

    qs = [stacked(q_ref[c * rows:(c + 1) * rows, :]) for c in range(chains)]

    acc_ref[...] = jnp.zeros_like(acc_ref)
    ones = {w: jnp.ones((w, hd), BF16) for w in {tile} | {(c + 1) * rows for c in range(chains)}}

    def scores(t, c):
        start = pl.multiple_of(t * tile, tile)
        s_ref[c] = jnp.dot(qs[c], kt_ref[:, pl.ds(start, tile)], preferred_element_type=F32)

    def softmax_pv(t, c, m, width=tile, diagonal=False):
        start = pl.multiple_of(t * tile, tile)
        vt = jnp.concatenate([v_ref[pl.ds(start, width), :], ones[width]], axis=1)
        s = s_ref[c, :, :width]
        if diagonal:
            row = lax.broadcasted_iota(jnp.int32, (2 * rows, rows), 0)
            col = lax.broadcasted_iota(jnp.int32, (2 * rows, rows), 1)
            qrow = jnp.where(row >= rows, row - rows, row)
            tail = jnp.where(col <= qrow, s[:, width - rows:], -jnp.inf)
            s = tail if width == rows else jnp.concatenate([s[:, :width - rows], tail], axis=1)
        m_new = jnp.maximum(m, jnp.max(s, axis=1, keepdims=True))
        alpha = jnp.exp2(m - m_new)
        p = jnp.exp2(s - m_new).astype(BF16)
        acc_ref[c] = alpha * acc_ref[c] + jnp.dot(p, vt, preferred_element_type=F32)
        return m_new

    m0 = jnp.full((2 * rows, 1), -jnp.inf, F32)
    scores(0, 0)

    def full_tile(t, ms):
        ms = list(ms)
        for c in range(chains):
            if c + 1 < chains:
                scores(t, c + 1)
            else:
                scores(t + 1, 0)
            ms[c] = softmax_pv(t, c, ms[c])
        return tuple(ms)

    ms = lax.fori_loop(0, i, full_tile, (m0,) * chains)

    for c in range(chains):
        if c + 1 < chains:
            scores(i, c + 1)
        softmax_pv(i, c, ms[c], width=(c + 1) * rows, diagonal=True)

    lam = (jnp.exp(jnp.sum(lq1_ref[...] * lk1_ref[...], axis=1, keepdims=True))
           - jnp.exp(jnp.sum(lq2_ref[...] * lk2_ref[...], axis=1, keepdims=True)) + lambda_init)
    for c in range(chains):
        o = (acc_ref[c, 0:rows, 0:hd] / acc_ref[c, 0:rows, hd:]
             - lam * (acc_ref[c, rows:, 0:hd] / acc_ref[c, rows:, hd:]))
        ms_o = jnp.mean(o * o, axis=1, keepdims=True)
        o = o * lax.rsqrt(ms_o + EPS) * sg_ref[...] * (1.0 - lambda_init)
        o_ref[c * rows:(c + 1) * rows, :] = o.astype(o_ref.dtype)


def _diff_attention(q, kt, v, lq1, lk1, lq2, lk2, subln_g, *, B, S, H, dh, lambda_init, tile, chains):
    aw = H * 2 * dh
    q3 = q.reshape(B, S, aw)
    v3 = v.reshape(B, S, aw)
    hd = 2 * dh
    tile = min(tile, S)
    rows = tile // chains
    kern = functools.partial(_attn_kernel, tile=tile, chains=chains, dh=dh, lambda_init=lambda_init)
    vec = pl.BlockSpec((1, dh), lambda b, h, i: (0, 0))
    out = pl.pallas_call(
        kern,
        grid=(B, H, S // tile),
        in_specs=[pl.BlockSpec((None, tile, hd), lambda b, h, i: (b, i, h)),
                  pl.BlockSpec((None, hd, S), lambda b, h, i: (b, h, 0)),
                  pl.BlockSpec((None, S, hd), lambda b, h, i: (b, 0, h)),
                  vec, vec, vec, vec,
                  pl.BlockSpec((1, hd), lambda b, h, i: (0, 0))],
        out_specs=pl.BlockSpec((None, tile, hd), lambda b, h, i: (b, i, h)),
        out_shape=jax.ShapeDtypeStruct((B, S, aw), BF16),
        scratch_shapes=[pltpu.VMEM((chains, 2 * rows, 2 * hd), F32),
                        pltpu.VMEM((chains, 2 * rows, tile), F32)],
        compiler_params=_cparams(("parallel", "parallel", "parallel")),
        name="diff_flash_attention",
    )(q3, kt, v3, lq1, lk1, lq2, lk2, subln_g)
    return out.reshape(B * S, aw)


def _mixout_kernel(attn_ref, ucur_ref, uhalo_ref, x_ref, mod_ref, wdw_ref, bdw_ref, lng_ref, lnb_ref,
                   wo1_ref, wo2_ref, gffn_ref, wrh_ref, wrl_ref, br_ref,
                   x1_ref, h2_ref, route_ref, route_t_ref, cnt_ref, ubuf_ref, conv_ref,
                   *, nS, conv_k, n_exp, n_grp):
    T = x_ref.shape[0]
    cw = ucur_ref.shape[1]
    i = pl.program_id(0)

    @pl.when(i == 0)
    def _():
        cnt_ref[...] = jnp.zeros_like(cnt_ref)

    first = (i % nS) == 0
    halo = uhalo_ref[...].astype(F32)
    ubuf_ref[0, 0:CONV_HALO, :] = jnp.where(first, jnp.zeros_like(halo), halo)
    ubuf_ref[0, CONV_HALO:, :] = ucur_ref[...].astype(F32)
    span = T + CONV_HALO - SUBLANES
    for b in range(1, SUBLANES):
        ubuf_ref[b, 0:span, :] = ubuf_ref[0, b:b + span, :]

    off = CONV_HALO - (conv_k - 1)
    rows = 32
    for r0 in range(0, T, rows):
        acc = jnp.zeros((rows // SUBLANES, SUBLANES, cw), F32)
        for j in range(conv_k):
            a, b = divmod(j + off, SUBLANES)
            lo_r = r0 + a * SUBLANES
            acc = acc + (wdw_ref[j * SUBLANES:(j + 1) * SUBLANES, :]
                         * ubuf_ref[b, lo_r:lo_r + rows, :].reshape(rows // SUBLANES, SUBLANES, cw))
        y = acc.reshape(rows, cw) + bdw_ref[...]
        mu = jnp.mean(y, axis=1, keepdims=True)
        d = y - mu
        var = jnp.mean(d * d, axis=1, keepdims=True)
        z = d * lax.rsqrt(var + EPS) * lng_ref[...] + lnb_ref[...]
        conv_ref[r0:r0 + rows, :] = _silu(z).astype(BF16)

    yo = (jnp.dot(attn_ref[...], wo1_ref[...], preferred_element_type=F32)
          + jnp.dot(conv_ref[...], wo2_ref[...], preferred_element_type=F32))
    x1 = x_ref[...] + mod_ref[2:3, :] * yo
    x1_ref[...] = x1
    ms = jnp.mean(x1 * x1, axis=1, keepdims=True)
    h2 = x1 * lax.rsqrt(ms + EPS) * gffn_ref[...] * (1.0 + mod_ref[4:5, :]) + mod_ref[3:4, :]
    h2_ref[...] = h2
    hi = h2.astype(BF16)
    lo = (h2 - hi.astype(F32)).astype(BF16)

    logits = (jnp.dot(hi, wrh_ref[...], preferred_element_type=F32)
              + jnp.dot(lo, wrh_ref[...], preferred_element_type=F32)
              + jnp.dot(hi, wrl_ref[...], preferred_element_type=F32)) + br_ref[...]
    lane_i = lax.broadcasted_iota(jnp.int32, logits.shape, 1)
    lane = lane_i.astype(F32)
    big = jnp.float32(1e9)
    ninf = jnp.float32(-jnp.inf)
    is_g = (lane_i >= n_exp) & (lane_i < n_exp + n_grp)
    gl = jnp.where(is_g, logits, ninf)
    gmax = jnp.max(gl, axis=1, keepdims=True)
    gsum = jnp.sum(jnp.where(is_g, jnp.exp(gl - gmax), 0.0), axis=1, keepdims=True)
    g_p = 1.0 / gsum
    gidx = jnp.min(jnp.where(gl == gmax, lane, big), axis=1, keepdims=True) - n_exp
    epg = n_exp // n_grp
    lo_l = gidx * epg
    in_grp = (lane >= lo_l) & (lane < lo_l + epg)
    el = jnp.where(in_grp, logits, ninf)
    m1 = jnp.max(el, axis=1, keepdims=True)
    i1 = jnp.min(jnp.where(el == m1, lane, big), axis=1, keepdims=True)
    el2 = jnp.where(lane == i1, ninf, el)
    m2 = jnp.max(el2, axis=1, keepdims=True)
    i2 = jnp.min(jnp.where(el2 == m2, lane, big), axis=1, keepdims=True)
    e2 = jnp.exp(m2 - m1)
    p1 = 1.0 / (1.0 + e2)
    p2 = e2 / (1.0 + e2)

    sel = (lane == i1) | (lane == i2)
    rr = lax.broadcasted_iota(jnp.int32, (T, T), 0)
    cc = lax.broadcasted_iota(jnp.int32, (T, T), 1)
    before = jnp.dot((rr > cc).astype(BF16), sel.astype(BF16), preferred_element_type=F32) + cnt_ref[...]
    r1 = jnp.sum(jnp.where(lane == i1, before, 0.0), axis=1, keepdims=True)
    r2 = jnp.sum(jnp.where(lane == i2, before, 0.0), axis=1, keepdims=True)
    cnt_ref[...] += jnp.sum(sel.astype(F32), axis=0, keepdims=True)

    route = jnp.zeros_like(logits)
    for k, val in enumerate((i1, i2, g_p * p1, g_p * p2, r1, r2)):
        route = jnp.where(lane_i == k, val, route)
    route_ref[...] = route
    route_t_ref[...] = route.T[:SUBLANES]


def _mixout(attn, u, x2, mod, w_dw, b_dw, ln_g, ln_b, wo1, wo2, g_ffn, wr_hi, wr_lo, b_r,
            *, S, T, n_exp, n_grp):
    N, D = x2.shape
    aw = attn.shape[1]
    cw = u.shape[1]
    nS = S // T
    conv_k = w_dw.shape[0]
    w_dw = jnp.repeat(w_dw, SUBLANES, axis=0)
    hb = T // CONV_HALO
    kern = functools.partial(_mixout_kernel, nS=nS, conv_k=conv_k, n_exp=n_exp, n_grp=n_grp)
    full = lambda a: pl.BlockSpec(a.shape, lambda i: (0, 0))
    return pl.pallas_call(
        kern,
        grid=(N // T,),
        in_specs=[pl.BlockSpec((T, aw), lambda i: (i, 0)),
                  pl.BlockSpec((T, cw), lambda i: (i, 0)),
                  pl.BlockSpec((CONV_HALO, cw), lambda i: (jnp.maximum(i * hb - 1, 0), 0)),
                  pl.BlockSpec((T, D), lambda i: (i, 0)),
                  pl.BlockSpec((None, 6, D), lambda i: (i // nS, 0, 0)),
                  full(w_dw), full(b_dw), full(ln_g), full(ln_b), full(wo1), full(wo2), full(g_ffn),
                  full(wr_hi), full(wr_lo), full(b_r)],
        out_specs=[pl.BlockSpec((T, D), lambda i: (i, 0)),
                   pl.BlockSpec((T, D), lambda i: (i, 0)),
                   pl.BlockSpec((T, LANES), lambda i: (i, 0)),
                   pl.BlockSpec((SUBLANES, T), lambda i: (0, i)),
                   pl.BlockSpec((1, LANES), lambda i: (0, 0))],
        out_shape=[jax.ShapeDtypeStruct((N, D), F32),
                   jax.ShapeDtypeStruct((N, D), F32),
                   jax.ShapeDtypeStruct((N, LANES), F32),
                   jax.ShapeDtypeStruct((SUBLANES, N), F32),
                   jax.ShapeDtypeStruct((1, LANES), F32)],
        scratch_shapes=[pltpu.VMEM((SUBLANES, CONV_HALO + T, cw), F32),
                        pltpu.VMEM((T, cw), BF16)],
        compiler_params=_cparams(("arbitrary",)),
        name="conv_outproj_router",
    )(attn, u, u, x2, mod, w_dw, b_dw, ln_g, ln_b, wo1, wo2, g_ffn, wr_hi, wr_lo, b_r)


def _row_of(ref, r):
    if isinstance(r, int):
        return ref.at[r // SUBLANES, pl.ds(r % SUBLANES, 1)]
    return ref.at[lax.shift_right_logical(r, 3), pl.ds(r & (SUBLANES - 1), 1)]


def _scatter_kernel(dest_ref, h2_ref, xs_hbm, sem):
    groups = h2_ref.shape[0]
    tokens = groups * SUBLANES

    def issue(g, carry):
        for j in range(SUBLANES):
            for k in range(TOP_K_INNER):
                d = dest_ref[0, k * tokens + g * SUBLANES + j]
                pltpu.make_async_copy(h2_ref.at[g, pl.ds(j, 1)], _row_of(xs_hbm, d), sem).start(priority=k % 2)
        return carry

    lax.fori_loop(0, groups, issue, 0)
    for k in range(TOP_K_INNER):
        pltpu.make_async_copy(h2_ref, xs_hbm.at[pl.ds(0, groups)], sem).wait()


def _dest_table(dest, tokens):
    tab = jnp.concatenate([d.reshape(-1, tokens) for d in dest], axis=1)
    return tab.reshape(tab.shape[0], 1, tab.shape[1])


def _scatter_rows(h2, dest, *, tokens):
    N, C = h2.shape
    steps = N // tokens
    dest3 = _dest_table(dest, tokens)
    rows = N * TOP_K_INNER
    xs = pl.pallas_call(
        _scatter_kernel,
        grid=(steps,),
        in_specs=[pl.BlockSpec((None, 1, TOP_K_INNER * tokens), lambda s: (s, 0, 0), memory_space=pltpu.SMEM),
                  pl.BlockSpec((tokens // SUBLANES, SUBLANES, C), lambda s: (s, 0, 0))],
        out_specs=pl.BlockSpec(memory_space=pl.ANY),
        out_shape=jax.ShapeDtypeStruct((rows // SUBLANES, SUBLANES, C), h2.dtype),
        scratch_shapes=[pltpu.SemaphoreType.DMA],
        compiler_params=_cparams(("arbitrary",)),
        name="moe_scatter_rows",
    )(dest3, h2.reshape(N // SUBLANES, SUBLANES, C))
    return xs.reshape(rows, C)


XS_RING = 3


def _experts_kernel(vt_ref, ve_ref, vlo_ref, vhi_ref, vnew_ref, vnext_ref, vslot_ref,
                    xs_hbm, wg_hbm, wu_hbm, wd_hbm, ys_ref, xbuf, wg_buf, wu_buf, wd_buf, sem, wsem,
                    *, n_tiles):
    v = pl.program_id(0)
    lo = vlo_ref[v]
    hi = vhi_ref[v]
    t = vt_ref[v]
    tm = xbuf.shape[1]
    wslot = vslot_ref[v]

    def weight_copies(expert, slot):
        return [pltpu.make_async_copy(hbm.at[expert], buf.at[slot], wsem.at[slot, j])
                for j, (hbm, buf) in enumerate(((wg_hbm, wg_buf), (wu_hbm, wu_buf), (wd_hbm, wd_buf)))]

    @pl.when(v == 0)
    def _():
        for c in weight_copies(ve_ref[0], 0):
            c.start()

    @pl.when(vnew_ref[v] == 1)
    def _():
        @pl.when(vnext_ref[v] >= 0)
        def _():
            for c in weight_copies(vnext_ref[v], 1 - wslot):
                c.start()

        for c in weight_copies(ve_ref[v], wslot):
            c.wait()

    def tile_copy(tile):
        slot = lax.rem(tile, XS_RING)
        return pltpu.make_async_copy(xs_hbm.at[pl.ds(pl.multiple_of(tile * tm, tm), tm)], xbuf.at[slot],
                                     sem.at[slot])

    @pl.when(v == 0)
    def _():
        for tile in range(min(XS_RING - 1, n_tiles)):
            tile_copy(tile).start()

    @pl.when((lo == 0) & (hi > lo))
    def _():
        @pl.when(t + (XS_RING - 1) < n_tiles)
        def _():
            tile_copy(t + (XS_RING - 1)).start()

        tile_copy(t).wait()

    @pl.when(hi > lo)
    def _():
        x = xbuf[lax.rem(t, XS_RING)].astype(BF16)
        g = jnp.dot(x, wg_buf[wslot].astype(BF16), preferred_element_type=F32)
        u = jnp.dot(x, wu_buf[wslot].astype(BF16), preferred_element_type=F32)
        hid = (_silu(g) * u).astype(BF16)
        y = jnp.dot(hid, wd_buf[wslot].astype(BF16), preferred_element_type=F32)

        @pl.when(lo == 0)
        def _():
            ys_ref[...] = y

        @pl.when(lo > 0)
        def _():
            row = lax.broadcasted_iota(jnp.int32, y.shape, 0)
            ys_ref[...] = jnp.where((row >= lo) & (row < hi), y, ys_ref[...])


def _experts(xs, w_gate, w_up, w_down, visits, *, tm):
    R, C = xs.shape
    E, D, ff = w_gate.shape
    vt = visits[0]
    any_spec = pl.BlockSpec(memory_space=pl.ANY)
    grid_spec = pltpu.PrefetchScalarGridSpec(
        num_scalar_prefetch=len(visits),
        grid=(vt.shape[0],),
        in_specs=[any_spec, any_spec, any_spec, any_spec],
        out_specs=pl.BlockSpec((tm, C), lambda v, vt, *_: (vt[v], 0)),
        scratch_shapes=[pltpu.VMEM((XS_RING, tm, C), F32),
                        pltpu.VMEM((2, D, ff), w_gate.dtype),
                        pltpu.VMEM((2, D, ff), w_up.dtype),
                        pltpu.VMEM((2, ff, D), w_down.dtype),
                        pltpu.SemaphoreType.DMA((XS_RING,)),
                        pltpu.SemaphoreType.DMA((2, 3))],
    )
    return pl.pallas_call(
        functools.partial(_experts_kernel, n_tiles=R // tm),
        grid_spec=grid_spec,
        out_shape=jax.ShapeDtypeStruct((R, C), F32),
        compiler_params=_cparams(("arbitrary",)),
        name="moe_grouped_experts",
    )(*visits, xs, w_gate, w_up, w_down)


def _visit_tables(off, cnt, n_rows, tm):
    n_tiles = n_rows // tm
    n_exp = off.shape[0]
    n_visits = n_tiles + n_exp - 1
    tile_starts = jnp.arange(n_tiles, dtype=jnp.int32) * tm
    seg_starts = jnp.where((cnt > 0) & (off % tm != 0), off, n_rows)
    starts = jnp.sort(jnp.concatenate([tile_starts, seg_starts]))
    lo_abs = starts[:n_visits]
    hi_abs = starts[1:n_visits + 1]
    valid = lo_abs < n_rows
    tile = jnp.where(valid, lo_abs // tm, n_tiles - 1)
    ends = off + cnt
    probe = jnp.where(valid, lo_abs, n_rows - 1)
    expert = jnp.sum(ends[None, :] <= probe[:, None], axis=1).astype(jnp.int32)
    row_lo = jnp.where(valid, lo_abs - tile * tm, 0)
    row_hi = jnp.where(valid, hi_abs - tile * tm, 0)
    new = jnp.concatenate([jnp.ones((1,), jnp.int32), (expert[1:] != expert[:-1]).astype(jnp.int32)])
    slot = (jnp.cumsum(new) - 1) % 2
    nxt = jnp.sum(expert[None, :] <= expert[:, None], axis=1)
    nxt_expert = jnp.where(nxt < n_visits, expert[jnp.minimum(nxt, n_visits - 1)], -1)
    return tile, expert, row_lo, row_hi, new, nxt_expert.astype(jnp.int32), slot.astype(jnp.int32)


def _combine_kernel(dcur_ref, dnxt_ref, ys_hbm, route_ref, x1_ref, mod_ref, o_ref, ybuf, sem):
    T = x1_ref.shape[0]
    i = pl.program_id(0)
    slot = lax.rem(i, 2)

    def gather(d_ref, sl):
        def body(g, carry):
            for j in range(SUBLANES):
                for k in range(TOP_K_INNER):
                    d = d_ref[0, k * T + g * SUBLANES + j]
                    pltpu.make_async_copy(_row_of(ys_hbm, d), ybuf.at[sl, k, g, pl.ds(j, 1)],
                                          sem.at[sl]).start(priority=k % 2)
            return carry
        lax.fori_loop(0, T // SUBLANES, body, 0)

    @pl.when(i == 0)
    def _():
        gather(dcur_ref, 0)

    @pl.when(i + 1 < pl.num_programs(0))
    def _():
        gather(dnxt_ref, 1 - slot)

    for k in range(TOP_K_INNER):
        pltpu.make_async_copy(ys_hbm.at[pl.ds(0, T // SUBLANES)], ybuf.at[slot, k], sem.at[slot]).wait()

    route = route_ref[...]
    moe = jnp.zeros(x1_ref.shape, F32)
    for k in range(TOP_K_INNER):
        moe = moe + route[:, TOP_K_INNER + k:TOP_K_INNER + k + 1] * ybuf[slot, k].reshape(x1_ref.shape)
    o_ref[...] = x1_ref[...] + mod_ref[5:6, :] * moe


def _combine(ys, dest, route, x1, mod, *, S, T):
    N, D = x1.shape
    C = ys.shape[1]
    steps = N // T
    nS = S // T
    dest3 = _dest_table(dest, T)
    dspec = lambda f: pl.BlockSpec((None, 1, TOP_K_INNER * T), f, memory_space=pltpu.SMEM)
    return pl.pallas_call(
        _combine_kernel,
        grid=(steps,),
        in_specs=[dspec(lambda i: (i, 0, 0)),
                  dspec(lambda i: (jnp.minimum(i + 1, steps - 1), 0, 0)),
                  pl.BlockSpec(memory_space=pl.ANY),
                  pl.BlockSpec((T, LANES), lambda i: (i, 0)),
                  pl.BlockSpec((T, D), lambda i: (i, 0)),
                  pl.BlockSpec((None, 6, D), lambda i: (i // nS, 0, 0))],
        out_specs=pl.BlockSpec((T, D), lambda i: (i, 0)),
        out_shape=jax.ShapeDtypeStruct((N, D), F32),
        scratch_shapes=[pltpu.VMEM((2, TOP_K_INNER, T // SUBLANES, SUBLANES, C), F32),
                        pltpu.SemaphoreType.DMA((2,))],
        compiler_params=_cparams(("arbitrary",)),
        name="moe_gather_combine",
    )(dest3, dest3, ys.reshape(ys.shape[0] // SUBLANES, SUBLANES, C), route, x1, mod)


def _layer(x2, mod, pos_row, l, B, S, g_mix, w_in, q_norm_g, k_norm_g, lambda_q1, lambda_k1, lambda_q2,
           lambda_k2, subln_g, b_glu, w_dw, b_dw, conv_ln_g, conv_ln_b, w_out, g_ffn, w_group, b_group,
           w_router, b_router, w_gate, w_up, w_down):
    N, D = x2.shape
    dh = q_norm_g.shape[0]
    H = N_DIFF_HEADS
    aw = H * 2 * dh
    cw = w_dw.shape[1]
    rot = dh // 4
    n_grp = w_group.shape[1]
    n_exp = w_router.shape[1]
    lambda_init = 0.8 - 0.6 * math.exp(-0.3 * l)
    tiles = _tiles(S)

    scale = dh ** -0.5 * math.log2(math.e)
    gq = jnp.tile(q_norm_g * scale, aw // dh)
    gk = jnp.tile(k_norm_g, aw // dh)
    gqk_tab = jnp.broadcast_to(jnp.concatenate([gq, gk])[:, None], (2 * aw, LANES))
    inv_freq = ROPE_THETA ** (-jnp.arange(0, rot, 2, dtype=F32) / rot)
    invf_tab = jnp.broadcast_to(inv_freq[:, None], (rot // 2, LANES))

    q, kt, v, u = _inproj(x2, mod, g_mix.reshape(1, D), pos_row, w_in, b_glu.reshape(1, 2 * cw),
                          gqk_tab, invf_tab, B=B, S=S, aw=aw, cw=cw, dh=dh, rot=rot, T=tiles.inproj)

    attn = _diff_attention(q, kt, v, lambda_q1.reshape(1, dh), lambda_k1.reshape(1, dh),
                           lambda_q2.reshape(1, dh), lambda_k2.reshape(1, dh), subln_g.reshape(1, 2 * dh),
                           B=B, S=S, H=H, dh=dh, lambda_init=lambda_init, tile=tiles.attn,
                           chains=tiles.attn_chains)

    pad = LANES - n_exp - n_grp
    w_r = jnp.concatenate([w_router, w_group, jnp.zeros((D, pad), F32)], axis=1)
    b_r = jnp.concatenate([b_router, b_group, jnp.zeros((pad,), F32)]).reshape(1, LANES)
    wr_hi = w_r.astype(BF16)
    wr_lo = (w_r - wr_hi.astype(F32)).astype(BF16)
    wo = w_out.astype(BF16)
    x1, h2, route, route_t, counts = _mixout(attn, u, x2, mod, w_dw, b_dw.reshape(1, cw), conv_ln_g.reshape(1, cw),
                                     conv_ln_b.reshape(1, cw), wo[:aw], wo[aw:], g_ffn.reshape(1, D),
                                     wr_hi, wr_lo, b_r, S=S, T=tiles.mixout, n_exp=n_exp, n_grp=n_grp)

    cnt = counts[0, :n_exp].astype(jnp.int32)
    off = jnp.cumsum(cnt) - cnt
    experts = jnp.arange(n_exp, dtype=jnp.int32)[:, None]
    dest = [jnp.sum(jnp.where(route_t[k].astype(jnp.int32)[None, :] == experts, off[:, None], 0), axis=0)
            + route_t[4 + k].astype(jnp.int32) for k in range(TOP_K_INNER)]
    visits = _visit_tables(off, cnt, N * TOP_K_INNER, tiles.experts)

    xs = _scatter_rows(h2, dest, tokens=tiles.scatter)
    ys = _experts(xs, w_gate, w_up, w_down, visits, tm=tiles.experts)
    return _combine(ys, dest, route, x1, mod, S=S, T=tiles.combine)


def kernel(x, c, positions, w_ada, b_ada, g_mix, w_in, q_norm_g, k_norm_g, lambda_q1, lambda_k1, lambda_q2,
           lambda_k2, subln_g, b_glu, w_dw, b_dw, conv_ln_g, conv_ln_b, w_out, g_ffn, w_group, b_group,
           w_router, b_router, w_gate, w_up, w_down):
    B, S, D = x.shape
    depth = w_ada.shape[0]
    x2 = x.reshape(B * S, D)
    pos_row = positions.astype(F32).reshape(1, B * S)
    for l in range(depth):
        mod = _modulation(c, w_ada[l], b_ada[l])
        x2 = _layer(x2, mod, pos_row, l, B, S, g_mix[l], w_in[l], q_norm_g[l], k_norm_g[l], lambda_q1[l],
                    lambda_k1[l], lambda_q2[l], lambda_k2[l], subln_g[l], b_glu[l], w_dw[l], b_dw[l],
                    conv_ln_g[l], conv_ln_b[l], w_out[l], g_ffn[l], w_group[l], b_group[l], w_router[l],
                    b_router[l], w_gate[l], w_up[l], w_down[l])
    return x2.reshape(B, S, D)
```

```python
import collections
import functools
import math

import numpy as np
import jax
import jax.numpy as jnp
from jax import lax
from jax.experimental import pallas as pl
from jax.experimental.pallas import tpu as pltpu

F32 = jnp.float32
BF16 = jnp.bfloat16

EPS = 1e-6
ROPE_THETA = 500000.0
N_DIFF_HEADS = 4
TOP_K_INNER = 2

LANES = 128
SUBLANES = 8
CONV_HALO = 32
VMEM_LIMIT = 48 * 1024 * 1024


Tiles = collections.namedtuple("Tiles", "inproj attn_rows attn_chains mixout scatter experts combine")


def _tiles(S):
    cap = lambda t: min(t, S)
    attn_rows = cap(1024) // 2
    return Tiles(inproj=cap(1024), attn_rows=attn_rows, attn_chains=min(4, S // attn_rows), mixout=cap(512),
                 scatter=cap(1024),
                 experts=512, combine=cap(512))


def _cparams(sem):
    return pltpu.CompilerParams(dimension_semantics=sem, vmem_limit_bytes=VMEM_LIMIT)


def _silu(x):
    return x * jax.nn.sigmoid(x)


def _mod_kernel(c_ref, w_ref, b_ref, o_ref):
    c = c_ref[...]
    o_ref[...] = jnp.dot(_silu(c), w_ref[...], preferred_element_type=F32,
                         precision=lax.Precision.HIGHEST) + b_ref[...]


def _modulation(c, w_ada, b_ada):
    B, D = c.shape
    n_out = w_ada.shape[1]
    rows = 8
    c_pad = jnp.pad(c, ((0, rows - B), (0, 0)))
    bn = 2048
    out = pl.pallas_call(
        _mod_kernel,
        grid=(n_out // bn,),
        in_specs=[pl.BlockSpec((rows, D), lambda j: (0, 0)),
                  pl.BlockSpec((D, bn), lambda j: (0, j)),
                  pl.BlockSpec((1, bn), lambda j: (0, j))],
        out_specs=pl.BlockSpec((rows, bn), lambda j: (0, j)),
        out_shape=jax.ShapeDtypeStruct((rows, n_out), F32),
        compiler_params=_cparams(("parallel",)),
        name="adaln_mod",
    )(c_pad, w_ada, b_ada.reshape(1, n_out))
    return out[:B].reshape(B, 6, D)


W_CHUNK = 256


def _inproj_kernel(x_ref, mod_ref, g_ref, pos_ref, w_hbm, bglu_ref, gqk_ref, invf_ref,
                   q_ref, kt_ref, v_ref, u_ref, wv_ref, wqk_ref, stage, sem, *, aw, cw, dh, rot):
    T = x_ref.shape[0]
    half = rot // 2

    @pl.when(pl.program_id(0) == 0)
    def _():
        n_chunks = w_hbm.shape[1] // W_CHUNK
        copies = [pltpu.make_async_copy(w_hbm.at[:, pl.ds(j * W_CHUNK, W_CHUNK)], stage.at[j % 2], sem.at[j % 2])
                  for j in range(n_chunks)]
        copies[0].start()
        for j in range(n_chunks):
            if j + 1 < n_chunks:
                copies[j + 1].start()
            copies[j].wait()
            chunk = stage[j % 2]
            c0 = j * W_CHUNK
            if c0 < 2 * aw:
                wqk_ref[c0:c0 + W_CHUNK, :] = chunk.T.astype(BF16)
            else:
                wv_ref[:, c0 - 2 * aw:c0 - 2 * aw + W_CHUNK] = chunk.astype(BF16)

    x = x_ref[...]
    ms = jnp.mean(x * x, axis=-1, keepdims=True)
    sh = mod_ref[0:1, :]
    sc = mod_ref[1:2, :]
    h = x * lax.rsqrt(ms + EPS) * g_ref[...] * (1.0 + sc) + sh
    hb = h.astype(BF16)

    pv = jnp.dot(hb, wv_ref[...], preferred_element_type=F32)
    v_ref[...] = pv[:, :aw].astype(BF16)
    a = pv[:, aw:aw + cw] + bglu_ref[:, :cw]
    gate = pv[:, aw + cw:] + bglu_ref[:, cw:]
    u_ref[...] = (a * jax.nn.sigmoid(gate)).astype(BF16)

    qkt = lax.dot_general(wqk_ref[...], hb, (((1,), (1,)), ((), ())), preferred_element_type=F32)
    nch = 2 * aw // dh
    for c in range(T // LANES):
        sl = slice(c * LANES, (c + 1) * LANES)
        s3 = qkt[:, sl].reshape(nch, dh, LANES)
        ssq = jnp.mean(s3 * s3, axis=1, keepdims=True)
        y = s3 * lax.rsqrt(ssq + EPS) * gqk_ref[...].reshape(nch, dh, LANES)
        ang = invf_ref[...] * pos_ref[:, sl]
        cs = jnp.cos(ang)
        sn = jnp.sin(ang)
        t1 = y[:, 0:half, :]
        t2 = y[:, half:rot, :]
        y = jnp.concatenate([t1 * cs - t2 * sn, t2 * cs + t1 * sn, y[:, rot:, :]], axis=1)
        y2 = y.reshape(2 * aw, LANES)
        kt_ref[:, sl] = y2[aw:].astype(BF16)
        q_ref[sl, :] = y2[:aw].T.astype(BF16)


def _inproj(x2, mod, g_mix, pos_row, w_in, b_glu, gqk_tab, invf_tab, *, B, S, aw, cw, dh, rot, T):
    N, D = x2.shape
    nS = S // T
    n_vglu = w_in.shape[1] - 2 * aw
    kern = functools.partial(_inproj_kernel, aw=aw, cw=cw, dh=dh, rot=rot)
    return pl.pallas_call(
        kern,
        grid=(N // T,),
        in_specs=[pl.BlockSpec((T, D), lambda i: (i, 0)),
                  pl.BlockSpec((None, 6, D), lambda i: (i // nS, 0, 0)),
                  pl.BlockSpec((1, D), lambda i: (0, 0)),
                  pl.BlockSpec((1, T), lambda i: (0, i)),
                  pl.BlockSpec(memory_space=pl.ANY),
                  pl.BlockSpec((1, 2 * cw), lambda i: (0, 0)),
                  pl.BlockSpec(gqk_tab.shape, lambda i: (0, 0)),
                  pl.BlockSpec(invf_tab.shape, lambda i: (0, 0))],
        out_specs=[pl.BlockSpec((T, aw), lambda i: (i, 0)),
                   pl.BlockSpec((None, aw, T), lambda i: (i // nS, 0, i % nS)),
                   pl.BlockSpec((T, aw), lambda i: (i, 0)),
                   pl.BlockSpec((T, cw), lambda i: (i, 0))],
        out_shape=[jax.ShapeDtypeStruct((N, aw), BF16),
                   jax.ShapeDtypeStruct((B, aw, S), BF16),
                   jax.ShapeDtypeStruct((N, aw), BF16),
                   jax.ShapeDtypeStruct((N, cw), BF16)],
        scratch_shapes=[pltpu.VMEM((D, n_vglu), BF16),
                        pltpu.VMEM((2 * aw, D), BF16),
                        pltpu.VMEM((2, D, W_CHUNK), F32),
                        pltpu.SemaphoreType.DMA((2,))],
        compiler_params=_cparams(("arbitrary",)),
        name="inproj_qknorm_rope_glu",
    )(x2, mod, g_mix, pos_row, w_in, b_glu, gqk_tab, invf_tab)


def _attn_kernel(q_ref, kt_ref, v_ref, lq1_ref, lk1_ref, lq2_ref, lk2_ref, sg_ref, o_ref, acc_ref, s_ref,
                 *, rows, chains, dh, lambda_init):
    i = pl.program_id(2)
    hd = 2 * dh
    tk = 2 * rows
    base = (chains // 2) * i

    def stacked(q):
        lane = lax.broadcasted_iota(jnp.int32, q.shape, 1)
        zero = jnp.zeros_like(q)
        return jnp.concatenate([jnp.where(lane < dh, q, zero), jnp.where(lane >= dh, q, zero)], axis=0)

    qs = [stacked(q_ref[c * rows:(c + 1) * rows, :]) for c in range(chains)]

    acc_ref[...] = jnp.zeros_like(acc_ref)
    ones = {w: jnp.ones((w, hd), BF16) for w in (rows, tk)}

    def scores(t, c):
        start = pl.multiple_of(t * tk, tk)
        s_ref[c] = jnp.dot(qs[c], kt_ref[:, pl.ds(start, tk)], preferred_element_type=F32)

    def softmax_pv(t, c, m, diagonal=False):
        width = rows * (c % 2 + 1) if diagonal else tk
        start = pl.multiple_of(t * tk, tk)
        vt = jnp.concatenate([v_ref[pl.ds(start, width), :], ones[width]], axis=1)
        s = s_ref[c, :, :width]
        if diagonal:
            row = lax.broadcasted_iota(jnp.int32, (2 * rows, rows), 0)
            col = lax.broadcasted_iota(jnp.int32, (2 * rows, rows), 1)
            qrow = jnp.where(row >= rows, row - rows, row)
            tail = jnp.where(col <= qrow, s[:, width - rows:], -jnp.inf)
            s = tail if width == rows else jnp.concatenate([s[:, :width - rows], tail], axis=1)
        m_new = jnp.maximum(m, jnp.max(s, axis=1, keepdims=True))
        alpha = jnp.exp2(m - m_new)
        p = jnp.exp2(s - m_new).astype(BF16)
        acc_ref[c] = alpha * acc_ref[c] + jnp.dot(p, vt, preferred_element_type=F32)
        return m_new

    m0 = jnp.full((2 * rows, 1), -jnp.inf, F32)
    scores(0, 0)

    def common_tile(t, ms):
        ms = list(ms)
        for c in range(chains):
            if c + 1 < chains:
                scores(t, c + 1)
            else:
                scores(t + 1, 0)
            ms[c] = softmax_pv(t, c, ms[c])
        return tuple(ms)

    ms = list(lax.fori_loop(0, base, common_tile, (m0,) * chains))

    items = [(k, c) for k in range(chains // 2) for c in range(2 * k, chains)]
    for n, (k, c) in enumerate(items):
        if n + 1 < len(items):
            k_next, c_next = items[n + 1]
            scores(base + k_next, c_next)
        ms[c] = softmax_pv(base + k, c, ms[c], diagonal=(c // 2 == k))

    lam = (jnp.exp(jnp.sum(lq1_ref[...] * lk1_ref[...], axis=1, keepdims=True))
           - jnp.exp(jnp.sum(lq2_ref[...] * lk2_ref[...], axis=1, keepdims=True)) + lambda_init)
    for c in range(chains):
        o = (acc_ref[c, 0:rows, 0:hd] / acc_ref[c, 0:rows, hd:]
             - lam * (acc_ref[c, rows:, 0:hd] / acc_ref[c, rows:, hd:]))
        ms_o = jnp.mean(o * o, axis=1, keepdims=True)
        o = o * lax.rsqrt(ms_o + EPS) * sg_ref[...] * (1.0 - lambda_init)
        o_ref[c * rows:(c + 1) * rows, :] = o.astype(o_ref.dtype)


def _diff_attention(q, kt, v, lq1, lk1, lq2, lk2, subln_g, *, B, S, H, dh, lambda_init, rows, chains):
    aw = H * 2 * dh
    q3 = q.reshape(B, S, aw)
    v3 = v.reshape(B, S, aw)
    hd = 2 * dh
    step = rows * chains
    kern = functools.partial(_attn_kernel, rows=rows, chains=chains, dh=dh, lambda_init=lambda_init)
    vec = pl.BlockSpec((1, dh), lambda b, h, i: (0, 0))
    out = pl.pallas_call(
        kern,
        grid=(B, H, S // step),
        in_specs=[pl.BlockSpec((None, step, hd), lambda b, h, i: (b, i, h)),
                  pl.BlockSpec((None, hd, S), lambda b, h, i: (b, h, 0)),
                  pl.BlockSpec((None, S, hd), lambda b, h, i: (b, 0, h)),
                  vec, vec, vec, vec,
                  pl.BlockSpec((1, hd), lambda b, h, i: (0, 0))],
        out_specs=pl.BlockSpec((None, step, hd), lambda b, h, i: (b, i, h)),
        out_shape=jax.ShapeDtypeStruct((B, S, aw), BF16),
        scratch_shapes=[pltpu.VMEM((chains, 2 * rows, 2 * hd), F32),
                        pltpu.VMEM((chains, 2 * rows, 2 * rows), F32)],
        compiler_params=_cparams(("parallel", "parallel", "parallel")),
        name="diff_flash_attention",
    )(q3, kt, v3, lq1, lk1, lq2, lk2, subln_g)
    return out.reshape(B * S, aw)


def _mixout_kernel(attn_ref, ucur_ref, uhalo_ref, x_ref, mod_ref, wdw_ref, bdw_ref, lng_ref, lnb_ref,
                   wo1_ref, wo2_ref, gffn_ref, wrh_ref, wrl_ref, br_ref,
                   x1_ref, h2_ref, route_ref, route_t_ref, cnt_ref, ubuf_ref, conv_ref,
                   *, nS, conv_k, n_exp, n_grp):
    T = x_ref.shape[0]
    cw = ucur_ref.shape[1]
    i = pl.program_id(0)

    @pl.when(i == 0)
    def _():
        cnt_ref[...] = jnp.zeros_like(cnt_ref)

    first = (i % nS) == 0
    halo = uhalo_ref[...].astype(F32)
    ubuf_ref[0, 0:CONV_HALO, :] = jnp.where(first, jnp.zeros_like(halo), halo)
    ubuf_ref[0, CONV_HALO:, :] = ucur_ref[...].astype(F32)
    span = T + CONV_HALO - SUBLANES
    for b in range(1, SUBLANES):
        ubuf_ref[b, 0:span, :] = ubuf_ref[0, b:b + span, :]

    off = CONV_HALO - (conv_k - 1)
    rows = 32
    for r0 in range(0, T, rows):
        acc = jnp.zeros((rows // SUBLANES, SUBLANES, cw), F32)
        for j in range(conv_k):
            a, b = divmod(j + off, SUBLANES)
            lo_r = r0 + a * SUBLANES
            acc = acc + (wdw_ref[j * SUBLANES:(j + 1) * SUBLANES, :]
                         * ubuf_ref[b, lo_r:lo_r + rows, :].reshape(rows // SUBLANES, SUBLANES, cw))
        y = acc.reshape(rows, cw) + bdw_ref[...]
        mu = jnp.mean(y, axis=1, keepdims=True)
        d = y - mu
        var = jnp.mean(d * d, axis=1, keepdims=True)
        z = d * lax.rsqrt(var + EPS) * lng_ref[...] + lnb_ref[...]
        conv_ref[r0:r0 + rows, :] = _silu(z).astype(BF16)

    yo = (jnp.dot(attn_ref[...], wo1_ref[...], preferred_element_type=F32)
          + jnp.dot(conv_ref[...], wo2_ref[...], preferred_element_type=F32))
    x1 = x_ref[...] + mod_ref[2:3, :] * yo
    x1_ref[...] = x1
    ms = jnp.mean(x1 * x1, axis=1, keepdims=True)
    h2 = x1 * lax.rsqrt(ms + EPS) * gffn_ref[...] * (1.0 + mod_ref[4:5, :]) + mod_ref[3:4, :]
    h2_ref[...] = h2
    hi = h2.astype(BF16)
    lo = (h2 - hi.astype(F32)).astype(BF16)

    logits = (jnp.dot(hi, wrh_ref[...], preferred_element_type=F32)
              + jnp.dot(lo, wrh_ref[...], preferred_element_type=F32)
              + jnp.dot(hi, wrl_ref[...], preferred_element_type=F32)) + br_ref[...]
    lane_i = lax.broadcasted_iota(jnp.int32, logits.shape, 1)
    lane = lane_i.astype(F32)
    big = jnp.float32(1e9)
    ninf = jnp.float32(-jnp.inf)
    is_g = (lane_i >= n_exp) & (lane_i < n_exp + n_grp)
    gl = jnp.where(is_g, logits, ninf)
    gmax = jnp.max(gl, axis=1, keepdims=True)
    gsum = jnp.sum(jnp.where(is_g, jnp.exp(gl - gmax), 0.0), axis=1, keepdims=True)
    g_p = 1.0 / gsum
    gidx = jnp.min(jnp.where(gl == gmax, lane, big), axis=1, keepdims=True) - n_exp
    epg = n_exp // n_grp
    lo_l = gidx * epg
    in_grp = (lane >= lo_l) & (lane < lo_l + epg)
    el = jnp.where(in_grp, logits, ninf)
    m1 = jnp.max(el, axis=1, keepdims=True)
    i1 = jnp.min(jnp.where(el == m1, lane, big), axis=1, keepdims=True)
    el2 = jnp.where(lane == i1, ninf, el)
    m2 = jnp.max(el2, axis=1, keepdims=True)
    i2 = jnp.min(jnp.where(el2 == m2, lane, big), axis=1, keepdims=True)
    e2 = jnp.exp(m2 - m1)
    p1 = 1.0 / (1.0 + e2)
    p2 = e2 / (1.0 + e2)

    sel = (lane == i1) | (lane == i2)
    rr = lax.broadcasted_iota(jnp.int32, (T, T), 0)
    cc = lax.broadcasted_iota(jnp.int32, (T, T), 1)
    before = jnp.dot((rr > cc).astype(BF16), sel.astype(BF16), preferred_element_type=F32) + cnt_ref[...]
    r1 = jnp.sum(jnp.where(lane == i1, before, 0.0), axis=1, keepdims=True)
    r2 = jnp.sum(jnp.where(lane == i2, before, 0.0), axis=1, keepdims=True)
    cnt_ref[...] += jnp.sum(sel.astype(F32), axis=0, keepdims=True)

    route = jnp.zeros_like(logits)
    for k, val in enumerate((i1, i2, g_p * p1, g_p * p2, r1, r2)):
        route = jnp.where(lane_i == k, val, route)
    route_ref[...] = route
    route_t_ref[...] = route.T[:SUBLANES]


def _mixout(attn, u, x2, mod, w_dw, b_dw, ln_g, ln_b, wo1, wo2, g_ffn, wr_hi, wr_lo, b_r,
            *, S, T, n_exp, n_grp):
    N, D = x2.shape
    aw = attn.shape[1]
    cw = u.shape[1]
    nS = S // T
    conv_k = w_dw.shape[0]
    w_dw = jnp.repeat(w_dw, SUBLANES, axis=0)
    hb = T // CONV_HALO
    kern = functools.partial(_mixout_kernel, nS=nS, conv_k=conv_k, n_exp=n_exp, n_grp=n_grp)
    full = lambda a: pl.BlockSpec(a.shape, lambda i: (0, 0))
    return pl.pallas_call(
        kern,
        grid=(N // T,),
        in_specs=[pl.BlockSpec((T, aw), lambda i: (i, 0)),
                  pl.BlockSpec((T, cw), lambda i: (i, 0)),
                  pl.BlockSpec((CONV_HALO, cw), lambda i: (jnp.maximum(i * hb - 1, 0), 0)),
                  pl.BlockSpec((T, D), lambda i: (i, 0)),
                  pl.BlockSpec((None, 6, D), lambda i: (i // nS, 0, 0)),
                  full(w_dw), full(b_dw), full(ln_g), full(ln_b), full(wo1), full(wo2), full(g_ffn),
                  full(wr_hi), full(wr_lo), full(b_r)],
        out_specs=[pl.BlockSpec((T, D), lambda i: (i, 0)),
                   pl.BlockSpec((T, D), lambda i: (i, 0)),
                   pl.BlockSpec((T, LANES), lambda i: (i, 0)),
                   pl.BlockSpec((SUBLANES, T), lambda i: (0, i)),
                   pl.BlockSpec((1, LANES), lambda i: (0, 0))],
        out_shape=[jax.ShapeDtypeStruct((N, D), F32),
                   jax.ShapeDtypeStruct((N, D), F32),
                   jax.ShapeDtypeStruct((N, LANES), F32),
                   jax.ShapeDtypeStruct((SUBLANES, N), F32),
                   jax.ShapeDtypeStruct((1, LANES), F32)],
        scratch_shapes=[pltpu.VMEM((SUBLANES, CONV_HALO + T, cw), F32),
                        pltpu.VMEM((T, cw), BF16)],
        compiler_params=_cparams(("arbitrary",)),
        name="conv_outproj_router",
    )(attn, u, u, x2, mod, w_dw, b_dw, ln_g, ln_b, wo1, wo2, g_ffn, wr_hi, wr_lo, b_r)


def _row_of(ref, r):
    if isinstance(r, int):
        return ref.at[r // SUBLANES, pl.ds(r % SUBLANES, 1)]
    return ref.at[lax.shift_right_logical(r, 3), pl.ds(r & (SUBLANES - 1), 1)]


def _scatter_kernel(dest_ref, h2_ref, xs_hbm, sem):
    groups = h2_ref.shape[0]
    tokens = groups * SUBLANES

    def issue(g, carry):
        for j in range(SUBLANES):
            for k in range(TOP_K_INNER):
                d = dest_ref[0, k * tokens + g * SUBLANES + j]
                pltpu.make_async_copy(h2_ref.at[g, pl.ds(j, 1)], _row_of(xs_hbm, d), sem).start(priority=k % 2)
        return carry

    lax.fori_loop(0, groups, issue, 0)
    for k in range(TOP_K_INNER):
        pltpu.make_async_copy(h2_ref, xs_hbm.at[pl.ds(0, groups)], sem).wait()


def _dest_table(dest, tokens):
    tab = jnp.concatenate([d.reshape(-1, tokens) for d in dest], axis=1)
    return tab.reshape(tab.shape[0], 1, tab.shape[1])


def _scatter_rows(h2, dest, *, tokens):
    N, C = h2.shape
    steps = N // tokens
    dest3 = _dest_table(dest, tokens)
    rows = N * TOP_K_INNER
    xs = pl.pallas_call(
        _scatter_kernel,
        grid=(steps,),
        in_specs=[pl.BlockSpec((None, 1, TOP_K_INNER * tokens), lambda s: (s, 0, 0), memory_space=pltpu.SMEM),
                  pl.BlockSpec((tokens // SUBLANES, SUBLANES, C), lambda s: (s, 0, 0))],
        out_specs=pl.BlockSpec(memory_space=pl.ANY),
        out_shape=jax.ShapeDtypeStruct((rows // SUBLANES, SUBLANES, C), h2.dtype),
        scratch_shapes=[pltpu.SemaphoreType.DMA],
        compiler_params=_cparams(("arbitrary",)),
        name="moe_scatter_rows",
    )(dest3, h2.reshape(N // SUBLANES, SUBLANES, C))
    return xs.reshape(rows, C)


XS_RING = 3


def _experts_kernel(vt_ref, ve_ref, vlo_ref, vhi_ref, vnew_ref, vnext_ref, vslot_ref,
                    xs_hbm, wg_hbm, wu_hbm, wd_hbm, ys_ref, xbuf, wg_buf, wu_buf, wd_buf, sem, wsem,
                    *, n_tiles):
    v = pl.program_id(0)
    lo = vlo_ref[v]
    hi = vhi_ref[v]
    t = vt_ref[v]
    tm = xbuf.shape[1]
    wslot = vslot_ref[v]

    def weight_copies(expert, slot):
        return [pltpu.make_async_copy(hbm.at[expert], buf.at[slot], wsem.at[slot, j])
                for j, (hbm, buf) in enumerate(((wg_hbm, wg_buf), (wu_hbm, wu_buf), (wd_hbm, wd_buf)))]

    @pl.when(v == 0)
    def _():
        for c in weight_copies(ve_ref[0], 0):
            c.start()

    @pl.when(vnew_ref[v] == 1)
    def _():
        @pl.when(vnext_ref[v] >= 0)
        def _():
            for c in weight_copies(vnext_ref[v], 1 - wslot):
                c.start()

        for c in weight_copies(ve_ref[v], wslot):
            c.wait()

    def tile_copy(tile):
        slot = lax.rem(tile, XS_RING)
        return pltpu.make_async_copy(xs_hbm.at[pl.ds(pl.multiple_of(tile * tm, tm), tm)], xbuf.at[slot],
                                     sem.at[slot])

    @pl.when(v == 0)
    def _():
        for tile in range(min(XS_RING - 1, n_tiles)):
            tile_copy(tile).start()

    @pl.when((lo == 0) & (hi > lo))
    def _():
        @pl.when(t + (XS_RING - 1) < n_tiles)
        def _():
            tile_copy(t + (XS_RING - 1)).start()

        tile_copy(t).wait()

    @pl.when(hi > lo)
    def _():
        x = xbuf[lax.rem(t, XS_RING)].astype(BF16)
        g = jnp.dot(x, wg_buf[wslot].astype(BF16), preferred_element_type=F32)
        u = jnp.dot(x, wu_buf[wslot].astype(BF16), preferred_element_type=F32)
        hid = (_silu(g) * u).astype(BF16)
        y = jnp.dot(hid, wd_buf[wslot].astype(BF16), preferred_element_type=F32)

        @pl.when(lo == 0)
        def _():
            ys_ref[...] = y

        @pl.when(lo > 0)
        def _():
            row = lax.broadcasted_iota(jnp.int32, y.shape, 0)
            ys_ref[...] = jnp.where((row >= lo) & (row < hi), y, ys_ref[...])


def _experts(xs, w_gate, w_up, w_down, visits, *, tm):
    R, C = xs.shape
    E, D, ff = w_gate.shape
    vt = visits[0]
    any_spec = pl.BlockSpec(memory_space=pl.ANY)
    grid_spec = pltpu.PrefetchScalarGridSpec(
        num_scalar_prefetch=len(visits),
        grid=(vt.shape[0],),
        in_specs=[any_spec, any_spec, any_spec, any_spec],
        out_specs=pl.BlockSpec((tm, C), lambda v, vt, *_: (vt[v], 0)),
        scratch_shapes=[pltpu.VMEM((XS_RING, tm, C), F32),
                        pltpu.VMEM((2, D, ff), w_gate.dtype),
                        pltpu.VMEM((2, D, ff), w_up.dtype),
                        pltpu.VMEM((2, ff, D), w_down.dtype),
                        pltpu.SemaphoreType.DMA((XS_RING,)),
                        pltpu.SemaphoreType.DMA((2, 3))],
    )
    return pl.pallas_call(
        functools.partial(_experts_kernel, n_tiles=R // tm),
        grid_spec=grid_spec,
        out_shape=jax.ShapeDtypeStruct((R, C), F32),
        compiler_params=_cparams(("arbitrary",)),
        name="moe_grouped_experts",
    )(*visits, xs, w_gate, w_up, w_down)


def _visit_tables(off, cnt, n_rows, tm):
    n_tiles = n_rows // tm
    n_exp = off.shape[0]
    n_visits = n_tiles + n_exp - 1
    tile_starts = jnp.arange(n_tiles, dtype=jnp.int32) * tm
    seg_starts = jnp.where((cnt > 0) & (off % tm != 0), off, n_rows)
    starts = jnp.sort(jnp.concatenate([tile_starts, seg_starts]))
    lo_abs = starts[:n_visits]
    hi_abs = starts[1:n_visits + 1]
    valid = lo_abs < n_rows
    tile = jnp.where(valid, lo_abs // tm, n_tiles - 1)
    ends = off + cnt
    probe = jnp.where(valid, lo_abs, n_rows - 1)
    expert = jnp.sum(ends[None, :] <= probe[:, None], axis=1).astype(jnp.int32)
    row_lo = jnp.where(valid, lo_abs - tile * tm, 0)
    row_hi = jnp.where(valid, hi_abs - tile * tm, 0)
    new = jnp.concatenate([jnp.ones((1,), jnp.int32), (expert[1:] != expert[:-1]).astype(jnp.int32)])
    slot = (jnp.cumsum(new) - 1) % 2
    nxt = jnp.sum(expert[None, :] <= expert[:, None], axis=1)
    nxt_expert = jnp.where(nxt < n_visits, expert[jnp.minimum(nxt, n_visits - 1)], -1)
    return tile, expert, row_lo, row_hi, new, nxt_expert.astype(jnp.int32), slot.astype(jnp.int32)


def _combine_kernel(dcur_ref, dnxt_ref, ys_hbm, route_ref, x1_ref, mod_ref, o_ref, ybuf, sem):
    T = x1_ref.shape[0]
    i = pl.program_id(0)
    slot = lax.rem(i, 2)

    def gather(d_ref, sl):
        def body(g, carry):
            for j in range(SUBLANES):
                for k in range(TOP_K_INNER):
                    d = d_ref[0, k * T + g * SUBLANES + j]
                    pltpu.make_async_copy(_row_of(ys_hbm, d), ybuf.at[sl, k, g, pl.ds(j, 1)],
                                          sem.at[sl]).start(priority=k % 2)
            return carry
        lax.fori_loop(0, T // SUBLANES, body, 0)

    @pl.when(i == 0)
    def _():
        gather(dcur_ref, 0)

    @pl.when(i + 1 < pl.num_programs(0))
    def _():
        gather(dnxt_ref, 1 - slot)

    for k in range(TOP_K_INNER):
        pltpu.make_async_copy(ys_hbm.at[pl.ds(0, T // SUBLANES)], ybuf.at[slot, k], sem.at[slot]).wait()

    route = route_ref[...]
    moe = jnp.zeros(x1_ref.shape, F32)
    for k in range(TOP_K_INNER):
        moe = moe + route[:, TOP_K_INNER + k:TOP_K_INNER + k + 1] * ybuf[slot, k].reshape(x1_ref.shape)
    o_ref[...] = x1_ref[...] + mod_ref[5:6, :] * moe


def _combine(ys, dest, route, x1, mod, *, S, T):
    N, D = x1.shape
    C = ys.shape[1]
    steps = N // T
    nS = S // T
    dest3 = _dest_table(dest, T)
    dspec = lambda f: pl.BlockSpec((None, 1, TOP_K_INNER * T), f, memory_space=pltpu.SMEM)
    return pl.pallas_call(
        _combine_kernel,
        grid=(steps,),
        in_specs=[dspec(lambda i: (i, 0, 0)),
                  dspec(lambda i: (jnp.minimum(i + 1, steps - 1), 0, 0)),
                  pl.BlockSpec(memory_space=pl.ANY),
                  pl.BlockSpec((T, LANES), lambda i: (i, 0)),
                  pl.BlockSpec((T, D), lambda i: (i, 0)),
                  pl.BlockSpec((None, 6, D), lambda i: (i // nS, 0, 0))],
        out_specs=pl.BlockSpec((T, D), lambda i: (i, 0)),
        out_shape=jax.ShapeDtypeStruct((N, D), F32),
        scratch_shapes=[pltpu.VMEM((2, TOP_K_INNER, T // SUBLANES, SUBLANES, C), F32),
                        pltpu.SemaphoreType.DMA((2,))],
        compiler_params=_cparams(("arbitrary",)),
        name="moe_gather_combine",
    )(dest3, dest3, ys.reshape(ys.shape[0] // SUBLANES, SUBLANES, C), route, x1, mod)


def _layer(x2, mod, pos_row, l, B, S, g_mix, w_in, q_norm_g, k_norm_g, lambda_q1, lambda_k1, lambda_q2,
           lambda_k2, subln_g, b_glu, w_dw, b_dw, conv_ln_g, conv_ln_b, w_out, g_ffn, w_group, b_group,
           w_router, b_router, w_gate, w_up, w_down):
    N, D = x2.shape
    dh = q_norm_g.shape[0]
    H = N_DIFF_HEADS
    aw = H * 2 * dh
    cw = w_dw.shape[1]
    rot = dh // 4
    n_grp = w_group.shape[1]
    n_exp = w_router.shape[1]
    lambda_init = 0.8 - 0.6 * math.exp(-0.3 * l)
    tiles = _tiles(S)

    scale = dh ** -0.5 * math.log2(math.e)
    gq = jnp.tile(q_norm_g * scale, aw // dh)
    gk = jnp.tile(k_norm_g, aw // dh)
    gqk_tab = jnp.broadcast_to(jnp.concatenate([gq, gk])[:, None], (2 * aw, LANES))
    inv_freq = ROPE_THETA ** (-jnp.arange(0, rot, 2, dtype=F32) / rot)
    invf_tab = jnp.broadcast_to(inv_freq[:, None], (rot // 2, LANES))

    q, kt, v, u = _inproj(x2, mod, g_mix.reshape(1, D), pos_row, w_in, b_glu.reshape(1, 2 * cw),
                          gqk_tab, invf_tab, B=B, S=S, aw=aw, cw=cw, dh=dh, rot=rot, T=tiles.inproj)

    attn = _diff_attention(q, kt, v, lambda_q1.reshape(1, dh), lambda_k1.reshape(1, dh),
                           lambda_q2.reshape(1, dh), lambda_k2.reshape(1, dh), subln_g.reshape(1, 2 * dh),
                           B=B, S=S, H=H, dh=dh, lambda_init=lambda_init, rows=tiles.attn_rows,
                           chains=tiles.attn_chains)

    pad = LANES - n_exp - n_grp
    w_r = jnp.concatenate([w_router, w_group, jnp.zeros((D, pad), F32)], axis=1)
    b_r = jnp.concatenate([b_router, b_group, jnp.zeros((pad,), F32)]).reshape(1, LANES)
    wr_hi = w_r.astype(BF16)
    wr_lo = (w_r - wr_hi.astype(F32)).astype(BF16)
    wo = w_out.astype(BF16)
    x1, h2, route, route_t, counts = _mixout(attn, u, x2, mod, w_dw, b_dw.reshape(1, cw), conv_ln_g.reshape(1, cw),
                                     conv_ln_b.reshape(1, cw), wo[:aw], wo[aw:], g_ffn.reshape(1, D),
                                     wr_hi, wr_lo, b_r, S=S, T=tiles.mixout, n_exp=n_exp, n_grp=n_grp)

    cnt = counts[0, :n_exp].astype(jnp.int32)
    off = jnp.cumsum(cnt) - cnt
    experts = jnp.arange(n_exp, dtype=jnp.int32)[:, None]
    dest = [jnp.sum(jnp.where(route_t[k].astype(jnp.int32)[None, :] == experts, off[:, None], 0), axis=0)
            + route_t[4 + k].astype(jnp.int32) for k in range(TOP_K_INNER)]
    visits = _visit_tables(off, cnt, N * TOP_K_INNER, tiles.experts)

    xs = _scatter_rows(h2, dest, tokens=tiles.scatter)
    ys = _experts(xs, w_gate, w_up, w_down, visits, tm=tiles.experts)
    return _combine(ys, dest, route, x1, mod, S=S, T=tiles.combine)


def kernel(x, c, positions, w_ada, b_ada, g_mix, w_in, q_norm_g, k_norm_g, lambda_q1, lambda_k1, lambda_q2,
           lambda_k2, subln_g, b_glu, w_dw, b_dw, conv_ln_g, conv_ln_b, w_out, g_ffn, w_group, b_group,
           w_router, b_router, w_gate, w_up, w_down):
    B, S, D = x.shape
    depth = w_ada.shape[0]
    x2 = x.reshape(B * S, D)
    pos_row = positions.astype(F32).reshape(1, B * S)
    for l in range(depth):
        mod = _modulation(c, w_ada[l], b_ada[l])
        x2 = _layer(x2, mod, pos_row, l, B, S, g_mix[l], w_in[l], q_norm_g[l], k_norm_g[l], lambda_q1[l],
                    lambda_k1[l], lambda_q2[l], lambda_k2[l], subln_g[l], b_glu[l], w_dw[l], b_dw[l],
                    conv_ln_g[l], conv_ln_b[l], w_out[l], g_ffn[l], w_group[l], b_group[l], w_router[l],
                    b_router[l], w_gate[l], w_up[l], w_down[l])
    return x2.reshape(B, S, D)
```

```python
import collections
import functools
import math

import numpy as np
import jax
import jax.numpy as jnp
from jax import lax
from jax.experimental import pallas as pl
from jax.experimental.pallas import tpu as pltpu

F32 = jnp.float32
BF16 = jnp.bfloat16

EPS = 1e-6
ROPE_THETA = 500000.0
N_DIFF_HEADS = 4
TOP_K_INNER = 2

LANES = 128
SUBLANES = 8
CONV_HALO = 32
VMEM_LIMIT = 48 * 1024 * 1024


Tiles = collections.namedtuple("Tiles", "inproj attn_rows attn_chains mixout scatter experts combine")


def _tiles(S):
    cap = lambda t: min(t, S)
    attn_rows = cap(1024) // 2
    return Tiles(inproj=cap(1024), attn_rows=attn_rows, attn_chains=min(4, S // attn_rows), mixout=cap(512),
                 scatter=cap(2048),
                 experts=512, combine=cap(1024))


def _cparams(sem):
    return pltpu.CompilerParams(dimension_semantics=sem, vmem_limit_bytes=VMEM_LIMIT)


def _silu(x):
    return x * jax.nn.sigmoid(x)


def _mod_kernel(c_ref, w_ref, b_ref, o_ref):
    c = c_ref[...]
    o_ref[...] = jnp.dot(_silu(c), w_ref[...], preferred_element_type=F32,
                         precision=lax.Precision.HIGHEST) + b_ref[...]


def _modulation(c, w_ada, b_ada):
    B, D = c.shape
    n_out = w_ada.shape[1]
    rows = 8
    c_pad = jnp.pad(c, ((0, rows - B), (0, 0)))
    bn = 2048
    out = pl.pallas_call(
        _mod_kernel,
        grid=(n_out // bn,),
        in_specs=[pl.BlockSpec((rows, D), lambda j: (0, 0)),
                  pl.BlockSpec((D, bn), lambda j: (0, j)),
                  pl.BlockSpec((1, bn), lambda j: (0, j))],
        out_specs=pl.BlockSpec((rows, bn), lambda j: (0, j)),
        out_shape=jax.ShapeDtypeStruct((rows, n_out), F32),
        compiler_params=_cparams(("parallel",)),
        name="adaln_mod",
    )(c_pad, w_ada, b_ada.reshape(1, n_out))
    return out[:B].reshape(B, 6, D)


W_CHUNK = 256


def _inproj_kernel(x_ref, mod_ref, g_ref, pos_ref, w_hbm, bglu_ref, gqk_ref, invf_ref,
                   q_ref, kt_ref, v_ref, u_ref, wv_ref, wqk_ref, stage, sem, *, aw, cw, dh, rot):
    T = x_ref.shape[0]
    half = rot // 2

    @pl.when(pl.program_id(0) == 0)
    def _():
        n_chunks = w_hbm.shape[1] // W_CHUNK
        copies = [pltpu.make_async_copy(w_hbm.at[:, pl.ds(j * W_CHUNK, W_CHUNK)], stage.at[j % 2], sem.at[j % 2])
                  for j in range(n_chunks)]
        copies[0].start()
        for j in range(n_chunks):
            if j + 1 < n_chunks:
                copies[j + 1].start()
            copies[j].wait()
            chunk = stage[j % 2]
            c0 = j * W_CHUNK
            if c0 < 2 * aw:
                wqk_ref[c0:c0 + W_CHUNK, :] = chunk.T.astype(BF16)
            else:
                wv_ref[:, c0 - 2 * aw:c0 - 2 * aw + W_CHUNK] = chunk.astype(BF16)

    x = x_ref[...]
    ms = jnp.mean(x * x, axis=-1, keepdims=True)
    sh = mod_ref[0:1, :]
    sc = mod_ref[1:2, :]
    h = x * lax.rsqrt(ms + EPS) * g_ref[...] * (1.0 + sc) + sh
    hb = h.astype(BF16)

    pv = jnp.dot(hb, wv_ref[...], preferred_element_type=F32)
    v_ref[...] = pv[:, :aw].astype(BF16)
    a = pv[:, aw:aw + cw] + bglu_ref[:, :cw]
    gate = pv[:, aw + cw:] + bglu_ref[:, cw:]
    u_ref[...] = (a * jax.nn.sigmoid(gate)).astype(BF16)

    qkt = lax.dot_general(wqk_ref[...], hb, (((1,), (1,)), ((), ())), preferred_element_type=F32)
    nch = 2 * aw // dh
    for c in range(T // LANES):
        sl = slice(c * LANES, (c + 1) * LANES)
        s3 = qkt[:, sl].reshape(nch, dh, LANES)
        ssq = jnp.mean(s3 * s3, axis=1, keepdims=True)
        y = s3 * lax.rsqrt(ssq + EPS) * gqk_ref[...].reshape(nch, dh, LANES)
        ang = invf_ref[...] * pos_ref[:, sl]
        cs = jnp.cos(ang)
        sn = jnp.sin(ang)
        t1 = y[:, 0:half, :]
        t2 = y[:, half:rot, :]
        y = jnp.concatenate([t1 * cs - t2 * sn, t2 * cs + t1 * sn, y[:, rot:, :]], axis=1)
        y2 = y.reshape(2 * aw, LANES)
        kt_ref[:, sl] = y2[aw:].astype(BF16)
        q_ref[sl, :] = y2[:aw].T.astype(BF16)


def _inproj(x2, mod, g_mix, pos_row, w_in, b_glu, gqk_tab, invf_tab, *, B, S, aw, cw, dh, rot, T):
    N, D = x2.shape
    nS = S // T
    n_vglu = w_in.shape[1] - 2 * aw
    kern = functools.partial(_inproj_kernel, aw=aw, cw=cw, dh=dh, rot=rot)
    return pl.pallas_call(
        kern,
        grid=(N // T,),
        in_specs=[pl.BlockSpec((T, D), lambda i: (i, 0)),
                  pl.BlockSpec((None, 6, D), lambda i: (i // nS, 0, 0)),
                  pl.BlockSpec((1, D), lambda i: (0, 0)),
                  pl.BlockSpec((1, T), lambda i: (0, i)),
                  pl.BlockSpec(memory_space=pl.ANY),
                  pl.BlockSpec((1, 2 * cw), lambda i: (0, 0)),
                  pl.BlockSpec(gqk_tab.shape, lambda i: (0, 0)),
                  pl.BlockSpec(invf_tab.shape, lambda i: (0, 0))],
        out_specs=[pl.BlockSpec((T, aw), lambda i: (i, 0)),
                   pl.BlockSpec((None, aw, T), lambda i: (i // nS, 0, i % nS)),
                   pl.BlockSpec((T, aw), lambda i: (i, 0)),
                   pl.BlockSpec((T, cw), lambda i: (i, 0))],
        out_shape=[jax.ShapeDtypeStruct((N, aw), BF16),
                   jax.ShapeDtypeStruct((B, aw, S), BF16),
                   jax.ShapeDtypeStruct((N, aw), BF16),
                   jax.ShapeDtypeStruct((N, cw), BF16)],
        scratch_shapes=[pltpu.VMEM((D, n_vglu), BF16),
                        pltpu.VMEM((2 * aw, D), BF16),
                        pltpu.VMEM((2, D, W_CHUNK), F32),
                        pltpu.SemaphoreType.DMA((2,))],
        compiler_params=_cparams(("arbitrary",)),
        name="inproj_qknorm_rope_glu",
    )(x2, mod, g_mix, pos_row, w_in, b_glu, gqk_tab, invf_tab)


def _attn_kernel(q_ref, kt_ref, v_ref, lq1_ref, lk1_ref, lq2_ref, lk2_ref, sg_ref, o_ref, acc_ref, s_ref,
                 *, rows, chains, dh, lambda_init):
    i = pl.program_id(2)
    hd = 2 * dh
    tk = 2 * rows
    base = (chains // 2) * i

    def stacked(q):
        lane = lax.broadcasted_iota(jnp.int32, q.shape, 1)
        zero = jnp.zeros_like(q)
        return jnp.concatenate([jnp.where(lane < dh, q, zero), jnp.where(lane >= dh, q, zero)], axis=0)

    qs = [stacked(q_ref[c * rows:(c + 1) * rows, :]) for c in range(chains)]

    acc_ref[...] = jnp.zeros_like(acc_ref)
    ones = {w: jnp.ones((w, hd), BF16) for w in (rows, tk)}

    def scores(t, c):
        start = pl.multiple_of(t * tk, tk)
        s_ref[c] = jnp.dot(qs[c], kt_ref[:, pl.ds(start, tk)], preferred_element_type=F32)

    def softmax_pv(t, c, m, diagonal=False):
        width = rows * (c % 2 + 1) if diagonal else tk
        start = pl.multiple_of(t * tk, tk)
        vt = jnp.concatenate([v_ref[pl.ds(start, width), :], ones[width]], axis=1)
        s = s_ref[c, :, :width]
        if diagonal:
            row = lax.broadcasted_iota(jnp.int32, (2 * rows, rows), 0)
            col = lax.broadcasted_iota(jnp.int32, (2 * rows, rows), 1)
            qrow = jnp.where(row >= rows, row - rows, row)
            tail = jnp.where(col <= qrow, s[:, width - rows:], -jnp.inf)
            s = tail if width == rows else jnp.concatenate([s[:, :width - rows], tail], axis=1)
        m_new = jnp.maximum(m, jnp.max(s, axis=1, keepdims=True))
        alpha = jnp.exp2(m - m_new)
        p = jnp.exp2(s - m_new).astype(BF16)
        acc_ref[c] = alpha * acc_ref[c] + jnp.dot(p, vt, preferred_element_type=F32)
        return m_new

    m0 = jnp.full((2 * rows, 1), -jnp.inf, F32)
    scores(0, 0)

    def common_tile(t, ms):
        ms = list(ms)
        for c in range(chains):
            if c + 1 < chains:
                scores(t, c + 1)
            else:
                scores(t + 1, 0)
            ms[c] = softmax_pv(t, c, ms[c])
        return tuple(ms)

    ms = list(lax.fori_loop(0, base, common_tile, (m0,) * chains))

    items = [(k, c) for k in range(chains // 2) for c in range(2 * k, chains)]
    for n, (k, c) in enumerate(items):
        if n + 1 < len(items):
            k_next, c_next = items[n + 1]
            scores(base + k_next, c_next)
        ms[c] = softmax_pv(base + k, c, ms[c], diagonal=(c // 2 == k))

    lam = (jnp.exp(jnp.sum(lq1_ref[...] * lk1_ref[...], axis=1, keepdims=True))
           - jnp.exp(jnp.sum(lq2_ref[...] * lk2_ref[...], axis=1, keepdims=True)) + lambda_init)
    for c in range(chains):
        o = (acc_ref[c, 0:rows, 0:hd] / acc_ref[c, 0:rows, hd:]
             - lam * (acc_ref[c, rows:, 0:hd] / acc_ref[c, rows:, hd:]))
        ms_o = jnp.mean(o * o, axis=1, keepdims=True)
        o = o * lax.rsqrt(ms_o + EPS) * sg_ref[...] * (1.0 - lambda_init)
        o_ref[c * rows:(c + 1) * rows, :] = o.astype(o_ref.dtype)


def _diff_attention(q, kt, v, lq1, lk1, lq2, lk2, subln_g, *, B, S, H, dh, lambda_init, rows, chains):
    aw = H * 2 * dh
    q3 = q.reshape(B, S, aw)
    v3 = v.reshape(B, S, aw)
    hd = 2 * dh
    step = rows * chains
    kern = functools.partial(_attn_kernel, rows=rows, chains=chains, dh=dh, lambda_init=lambda_init)
    vec = pl.BlockSpec((1, dh), lambda b, h, i: (0, 0))
    out = pl.pallas_call(
        kern,
        grid=(B, H, S // step),
        in_specs=[pl.BlockSpec((None, step, hd), lambda b, h, i: (b, i, h)),
                  pl.BlockSpec((None, hd, S), lambda b, h, i: (b, h, 0)),
                  pl.BlockSpec((None, S, hd), lambda b, h, i: (b, 0, h)),
                  vec, vec, vec, vec,
                  pl.BlockSpec((1, hd), lambda b, h, i: (0, 0))],
        out_specs=pl.BlockSpec((None, step, hd), lambda b, h, i: (b, i, h)),
        out_shape=jax.ShapeDtypeStruct((B, S, aw), BF16),
        scratch_shapes=[pltpu.VMEM((chains, 2 * rows, 2 * hd), F32),
                        pltpu.VMEM((chains, 2 * rows, 2 * rows), F32)],
        compiler_params=_cparams(("parallel", "parallel", "parallel")),
        name="diff_flash_attention",
    )(q3, kt, v3, lq1, lk1, lq2, lk2, subln_g)
    return out.reshape(B * S, aw)


def _mixout_kernel(attn_ref, ucur_ref, uhalo_ref, x_ref, mod_ref, wdw_ref, bdw_ref, lng_ref, lnb_ref,
                   wo1_ref, wo2_ref, gffn_ref, wrh_ref, wrl_ref, br_ref,
                   x1_ref, h2_ref, route_ref, route_t_ref, cnt_ref, ubuf_ref, conv_ref,
                   *, nS, conv_k, n_exp, n_grp):
    T = x_ref.shape[0]
    cw = ucur_ref.shape[1]
    i = pl.program_id(0)

    @pl.when(i == 0)
    def _():
        cnt_ref[...] = jnp.zeros_like(cnt_ref)

    first = (i % nS) == 0
    halo = uhalo_ref[...].astype(F32)
    ubuf_ref[0, 0:CONV_HALO, :] = jnp.where(first, jnp.zeros_like(halo), halo)
    ubuf_ref[0, CONV_HALO:, :] = ucur_ref[...].astype(F32)
    span = T + CONV_HALO - SUBLANES
    for b in range(1, SUBLANES):
        ubuf_ref[b, 0:span, :] = ubuf_ref[0, b:b + span, :]

    off = CONV_HALO - (conv_k - 1)
    rows = 32
    for r0 in range(0, T, rows):
        acc = jnp.zeros((rows // SUBLANES, SUBLANES, cw), F32)
        for j in range(conv_k):
            a, b = divmod(j + off, SUBLANES)
            lo_r = r0 + a * SUBLANES
            acc = acc + (wdw_ref[j * SUBLANES:(j + 1) * SUBLANES, :]
                         * ubuf_ref[b, lo_r:lo_r + rows, :].reshape(rows // SUBLANES, SUBLANES, cw))
        y = acc.reshape(rows, cw) + bdw_ref[...]
        mu = jnp.mean(y, axis=1, keepdims=True)
        d = y - mu
        var = jnp.mean(d * d, axis=1, keepdims=True)
        z = d * lax.rsqrt(var + EPS) * lng_ref[...] + lnb_ref[...]
        conv_ref[r0:r0 + rows, :] = _silu(z).astype(BF16)

    yo = (jnp.dot(attn_ref[...], wo1_ref[...], preferred_element_type=F32)
          + jnp.dot(conv_ref[...], wo2_ref[...], preferred_element_type=F32))
    x1 = x_ref[...] + mod_ref[2:3, :] * yo
    x1_ref[...] = x1
    ms = jnp.mean(x1 * x1, axis=1, keepdims=True)
    h2 = x1 * lax.rsqrt(ms + EPS) * gffn_ref[...] * (1.0 + mod_ref[4:5, :]) + mod_ref[3:4, :]
    h2_ref[...] = h2
    hi = h2.astype(BF16)
    lo = (h2 - hi.astype(F32)).astype(BF16)

    logits = (jnp.dot(hi, wrh_ref[...], preferred_element_type=F32)
              + jnp.dot(lo, wrh_ref[...], preferred_element_type=F32)
              + jnp.dot(hi, wrl_ref[...], preferred_element_type=F32)) + br_ref[...]
    lane_i = lax.broadcasted_iota(jnp.int32, logits.shape, 1)
    lane = lane_i.astype(F32)
    big = jnp.float32(1e9)
    ninf = jnp.float32(-jnp.inf)
    is_g = (lane_i >= n_exp) & (lane_i < n_exp + n_grp)
    gl = jnp.where(is_g, logits, ninf)
    gmax = jnp.max(gl, axis=1, keepdims=True)
    gsum = jnp.sum(jnp.where(is_g, jnp.exp(gl - gmax), 0.0), axis=1, keepdims=True)
    g_p = 1.0 / gsum
    gidx = jnp.min(jnp.where(gl == gmax, lane, big), axis=1, keepdims=True) - n_exp
    epg = n_exp // n_grp
    lo_l = gidx * epg
    in_grp = (lane >= lo_l) & (lane < lo_l + epg)
    el = jnp.where(in_grp, logits, ninf)
    m1 = jnp.max(el, axis=1, keepdims=True)
    i1 = jnp.min(jnp.where(el == m1, lane, big), axis=1, keepdims=True)
    el2 = jnp.where(lane == i1, ninf, el)
    m2 = jnp.max(el2, axis=1, keepdims=True)
    i2 = jnp.min(jnp.where(el2 == m2, lane, big), axis=1, keepdims=True)
    e2 = jnp.exp(m2 - m1)
    p1 = 1.0 / (1.0 + e2)
    p2 = e2 / (1.0 + e2)

    sel = (lane == i1) | (lane == i2)
    rr = lax.broadcasted_iota(jnp.int32, (T, T), 0)
    cc = lax.broadcasted_iota(jnp.int32, (T, T), 1)
    before = jnp.dot((rr > cc).astype(BF16), sel.astype(BF16), preferred_element_type=F32) + cnt_ref[...]
    r1 = jnp.sum(jnp.where(lane == i1, before, 0.0), axis=1, keepdims=True)
    r2 = jnp.sum(jnp.where(lane == i2, before, 0.0), axis=1, keepdims=True)
    cnt_ref[...] += jnp.sum(sel.astype(F32), axis=0, keepdims=True)

    route = jnp.zeros_like(logits)
    for k, val in enumerate((i1, i2, g_p * p1, g_p * p2, r1, r2)):
        route = jnp.where(lane_i == k, val, route)
    route_ref[...] = route
    route_t_ref[...] = route.T[:SUBLANES]


def _mixout(attn, u, x2, mod, w_dw, b_dw, ln_g, ln_b, wo1, wo2, g_ffn, wr_hi, wr_lo, b_r,
            *, S, T, n_exp, n_grp):
    N, D = x2.shape
    aw = attn.shape[1]
    cw = u.shape[1]
    nS = S // T
    conv_k = w_dw.shape[0]
    w_dw = jnp.repeat(w_dw, SUBLANES, axis=0)
    hb = T // CONV_HALO
    kern = functools.partial(_mixout_kernel, nS=nS, conv_k=conv_k, n_exp=n_exp, n_grp=n_grp)
    full = lambda a: pl.BlockSpec(a.shape, lambda i: (0, 0))
    return pl.pallas_call(
        kern,
        grid=(N // T,),
        in_specs=[pl.BlockSpec((T, aw), lambda i: (i, 0)),
                  pl.BlockSpec((T, cw), lambda i: (i, 0)),
                  pl.BlockSpec((CONV_HALO, cw), lambda i: (jnp.maximum(i * hb - 1, 0), 0)),
                  pl.BlockSpec((T, D), lambda i: (i, 0)),
                  pl.BlockSpec((None, 6, D), lambda i: (i // nS, 0, 0)),
                  full(w_dw), full(b_dw), full(ln_g), full(ln_b), full(wo1), full(wo2), full(g_ffn),
                  full(wr_hi), full(wr_lo), full(b_r)],
        out_specs=[pl.BlockSpec((T, D), lambda i: (i, 0)),
                   pl.BlockSpec((T, D), lambda i: (i, 0)),
                   pl.BlockSpec((T, LANES), lambda i: (i, 0)),
                   pl.BlockSpec((SUBLANES, T), lambda i: (0, i)),
                   pl.BlockSpec((1, LANES), lambda i: (0, 0))],
        out_shape=[jax.ShapeDtypeStruct((N, D), F32),
                   jax.ShapeDtypeStruct((N, D), F32),
                   jax.ShapeDtypeStruct((N, LANES), F32),
                   jax.ShapeDtypeStruct((SUBLANES, N), F32),
                   jax.ShapeDtypeStruct((1, LANES), F32)],
        scratch_shapes=[pltpu.VMEM((SUBLANES, CONV_HALO + T, cw), F32),
                        pltpu.VMEM((T, cw), BF16)],
        compiler_params=_cparams(("arbitrary",)),
        name="conv_outproj_router",
    )(attn, u, u, x2, mod, w_dw, b_dw, ln_g, ln_b, wo1, wo2, g_ffn, wr_hi, wr_lo, b_r)


def _row_of(ref, r):
    if isinstance(r, int):
        return ref.at[r // SUBLANES, pl.ds(r % SUBLANES, 1)]
    return ref.at[lax.shift_right_logical(r, 3), pl.ds(r & (SUBLANES - 1), 1)]


def _scatter_kernel(dest_ref, h2_ref, xs_hbm, sem):
    groups = h2_ref.shape[0]
    tokens = groups * SUBLANES

    def issue(g, carry):
        for j in range(SUBLANES):
            for k in range(TOP_K_INNER):
                d = dest_ref[0, k * tokens + g * SUBLANES + j]
                pltpu.make_async_copy(h2_ref.at[g, pl.ds(j, 1)], _row_of(xs_hbm, d), sem).start(priority=k % 2)
        return carry

    lax.fori_loop(0, groups, issue, 0)
    for k in range(TOP_K_INNER):
        pltpu.make_async_copy(h2_ref, xs_hbm.at[pl.ds(0, groups)], sem).wait()


def _dest_table(dest, tokens):
    tab = jnp.concatenate([d.reshape(-1, tokens) for d in dest], axis=1)
    return tab.reshape(tab.shape[0], 1, tab.shape[1])


def _scatter_rows(h2, dest, *, tokens):
    N, C = h2.shape
    steps = N // tokens
    dest3 = _dest_table(dest, tokens)
    rows = N * TOP_K_INNER
    xs = pl.pallas_call(
        _scatter_kernel,
        grid=(steps,),
        in_specs=[pl.BlockSpec((None, 1, TOP_K_INNER * tokens), lambda s: (s, 0, 0), memory_space=pltpu.SMEM),
                  pl.BlockSpec((tokens // SUBLANES, SUBLANES, C), lambda s: (s, 0, 0))],
        out_specs=pl.BlockSpec(memory_space=pl.ANY),
        out_shape=jax.ShapeDtypeStruct((rows // SUBLANES, SUBLANES, C), h2.dtype),
        scratch_shapes=[pltpu.SemaphoreType.DMA],
        compiler_params=_cparams(("arbitrary",)),
        name="moe_scatter_rows",
    )(dest3, h2.reshape(N // SUBLANES, SUBLANES, C))
    return xs.reshape(rows, C)


XS_RING = 3


def _experts_kernel(vt_ref, ve_ref, vlo_ref, vhi_ref, vnew_ref, vnext_ref, vslot_ref,
                    xs_hbm, wg_hbm, wu_hbm, wd_hbm, ys_ref, xbuf, wg_buf, wu_buf, wd_buf, sem, wsem,
                    *, n_tiles):
    v = pl.program_id(0)
    lo = vlo_ref[v]
    hi = vhi_ref[v]
    t = vt_ref[v]
    tm = xbuf.shape[1]
    wslot = vslot_ref[v]

    def weight_copies(expert, slot):
        return [pltpu.make_async_copy(hbm.at[expert], buf.at[slot], wsem.at[slot, j])
                for j, (hbm, buf) in enumerate(((wg_hbm, wg_buf), (wu_hbm, wu_buf), (wd_hbm, wd_buf)))]

    @pl.when(v == 0)
    def _():
        for c in weight_copies(ve_ref[0], 0):
            c.start()

    @pl.when(vnew_ref[v] == 1)
    def _():
        @pl.when(vnext_ref[v] >= 0)
        def _():
            for c in weight_copies(vnext_ref[v], 1 - wslot):
                c.start()

        for c in weight_copies(ve_ref[v], wslot):
            c.wait()

    def tile_copy(tile):
        slot = lax.rem(tile, XS_RING)
        return pltpu.make_async_copy(xs_hbm.at[pl.ds(pl.multiple_of(tile * tm, tm), tm)], xbuf.at[slot],
                                     sem.at[slot])

    @pl.when(v == 0)
    def _():
        for tile in range(min(XS_RING - 1, n_tiles)):
            tile_copy(tile).start()

    @pl.when((lo == 0) & (hi > lo))
    def _():
        @pl.when(t + (XS_RING - 1) < n_tiles)
        def _():
            tile_copy(t + (XS_RING - 1)).start()

        tile_copy(t).wait()

    @pl.when(hi > lo)
    def _():
        x = xbuf[lax.rem(t, XS_RING)].astype(BF16)
        g = jnp.dot(x, wg_buf[wslot].astype(BF16), preferred_element_type=F32)
        u = jnp.dot(x, wu_buf[wslot].astype(BF16), preferred_element_type=F32)
        hid = (_silu(g) * u).astype(BF16)
        y = jnp.dot(hid, wd_buf[wslot].astype(BF16), preferred_element_type=F32)

        @pl.when(lo == 0)
        def _():
            ys_ref[...] = y

        @pl.when(lo > 0)
        def _():
            row = lax.broadcasted_iota(jnp.int32, y.shape, 0)
            ys_ref[...] = jnp.where((row >= lo) & (row < hi), y, ys_ref[...])


def _experts(xs, w_gate, w_up, w_down, visits, *, tm):
    R, C = xs.shape
    E, D, ff = w_gate.shape
    vt = visits[0]
    any_spec = pl.BlockSpec(memory_space=pl.ANY)
    grid_spec = pltpu.PrefetchScalarGridSpec(
        num_scalar_prefetch=len(visits),
        grid=(vt.shape[0],),
        in_specs=[any_spec, any_spec, any_spec, any_spec],
        out_specs=pl.BlockSpec((tm, C), lambda v, vt, *_: (vt[v], 0)),
        scratch_shapes=[pltpu.VMEM((XS_RING, tm, C), F32),
                        pltpu.VMEM((2, D, ff), w_gate.dtype),
                        pltpu.VMEM((2, D, ff), w_up.dtype),
                        pltpu.VMEM((2, ff, D), w_down.dtype),
                        pltpu.SemaphoreType.DMA((XS_RING,)),
                        pltpu.SemaphoreType.DMA((2, 3))],
    )
    return pl.pallas_call(
        functools.partial(_experts_kernel, n_tiles=R // tm),
        grid_spec=grid_spec,
        out_shape=jax.ShapeDtypeStruct((R, C), F32),
        compiler_params=_cparams(("arbitrary",)),
        name="moe_grouped_experts",
    )(*visits, xs, w_gate, w_up, w_down)


def _visit_tables(off, cnt, n_rows, tm):
    n_tiles = n_rows // tm
    n_exp = off.shape[0]
    n_visits = n_tiles + n_exp - 1
    tile_starts = jnp.arange(n_tiles, dtype=jnp.int32) * tm
    seg_starts = jnp.where((cnt > 0) & (off % tm != 0), off, n_rows)
    starts = jnp.sort(jnp.concatenate([tile_starts, seg_starts]))
    lo_abs = starts[:n_visits]
    hi_abs = starts[1:n_visits + 1]
    valid = lo_abs < n_rows
    tile = jnp.where(valid, lo_abs // tm, n_tiles - 1)
    ends = off + cnt
    probe = jnp.where(valid, lo_abs, n_rows - 1)
    expert = jnp.sum(ends[None, :] <= probe[:, None], axis=1).astype(jnp.int32)
    row_lo = jnp.where(valid, lo_abs - tile * tm, 0)
    row_hi = jnp.where(valid, hi_abs - tile * tm, 0)
    new = jnp.concatenate([jnp.ones((1,), jnp.int32), (expert[1:] != expert[:-1]).astype(jnp.int32)])
    slot = (jnp.cumsum(new) - 1) % 2
    nxt = jnp.sum(expert[None, :] <= expert[:, None], axis=1)
    nxt_expert = jnp.where(nxt < n_visits, expert[jnp.minimum(nxt, n_visits - 1)], -1)
    return tile, expert, row_lo, row_hi, new, nxt_expert.astype(jnp.int32), slot.astype(jnp.int32)


def _combine_kernel(dcur_ref, dnxt_ref, ys_hbm, route_ref, x1_ref, mod_ref, o_ref, ybuf, sem):
    T = x1_ref.shape[0]
    i = pl.program_id(0)
    slot = lax.rem(i, 2)

    def gather(d_ref, sl):
        def body(g, carry):
            for j in range(SUBLANES):
                for k in range(TOP_K_INNER):
                    d = d_ref[0, k * T + g * SUBLANES + j]
                    pltpu.make_async_copy(_row_of(ys_hbm, d), ybuf.at[sl, k, g, pl.ds(j, 1)],
                                          sem.at[sl]).start(priority=k % 2)
            return carry
        lax.fori_loop(0, T // SUBLANES, body, 0)

    @pl.when(i == 0)
    def _():
        gather(dcur_ref, 0)

    @pl.when(i + 1 < pl.num_programs(0))
    def _():
        gather(dnxt_ref, 1 - slot)

    for k in range(TOP_K_INNER):
        pltpu.make_async_copy(ys_hbm.at[pl.ds(0, T // SUBLANES)], ybuf.at[slot, k], sem.at[slot]).wait()

    route = route_ref[...]
    moe = jnp.zeros(x1_ref.shape, F32)
    for k in range(TOP_K_INNER):
        moe = moe + route[:, TOP_K_INNER + k:TOP_K_INNER + k + 1] * ybuf[slot, k].reshape(x1_ref.shape)
    o_ref[...] = x1_ref[...] + mod_ref[5:6, :] * moe


def _combine(ys, dest, route, x1, mod, *, S, T):
    N, D = x1.shape
    C = ys.shape[1]
    steps = N // T
    nS = S // T
    dest3 = _dest_table(dest, T)
    dspec = lambda f: pl.BlockSpec((None, 1, TOP_K_INNER * T), f, memory_space=pltpu.SMEM)
    return pl.pallas_call(
        _combine_kernel,
        grid=(steps,),
        in_specs=[dspec(lambda i: (i, 0, 0)),
                  dspec(lambda i: (jnp.minimum(i + 1, steps - 1), 0, 0)),
                  pl.BlockSpec(memory_space=pl.ANY),
                  pl.BlockSpec((T, LANES), lambda i: (i, 0)),
                  pl.BlockSpec((T, D), lambda i: (i, 0)),
                  pl.BlockSpec((None, 6, D), lambda i: (i // nS, 0, 0))],
        out_specs=pl.BlockSpec((T, D), lambda i: (i, 0)),
        out_shape=jax.ShapeDtypeStruct((N, D), F32),
        scratch_shapes=[pltpu.VMEM((2, TOP_K_INNER, T // SUBLANES, SUBLANES, C), F32),
                        pltpu.SemaphoreType.DMA((2,))],
        compiler_params=_cparams(("arbitrary",)),
        name="moe_gather_combine",
    )(dest3, dest3, ys.reshape(ys.shape[0] // SUBLANES, SUBLANES, C), route, x1, mod)


def _layer(x2, mod, pos_row, l, B, S, g_mix, w_in, q_norm_g, k_norm_g, lambda_q1, lambda_k1, lambda_q2,
           lambda_k2, subln_g, b_glu, w_dw, b_dw, conv_ln_g, conv_ln_b, w_out, g_ffn, w_group, b_group,
           w_router, b_router, w_gate, w_up, w_down):
    N, D = x2.shape
    dh = q_norm_g.shape[0]
    H = N_DIFF_HEADS
    aw = H * 2 * dh
    cw = w_dw.shape[1]
    rot = dh // 4
    n_grp = w_group.shape[1]
    n_exp = w_router.shape[1]
    lambda_init = 0.8 - 0.6 * math.exp(-0.3 * l)
    tiles = _tiles(S)

    scale = dh ** -0.5 * math.log2(math.e)
    gq = jnp.tile(q_norm_g * scale, aw // dh)
    gk = jnp.tile(k_norm_g, aw // dh)
    gqk_tab = jnp.broadcast_to(jnp.concatenate([gq, gk])[:, None], (2 * aw, LANES))
    inv_freq = ROPE_THETA ** (-jnp.arange(0, rot, 2, dtype=F32) / rot)
    invf_tab = jnp.broadcast_to(inv_freq[:, None], (rot // 2, LANES))

    q, kt, v, u = _inproj(x2, mod, g_mix.reshape(1, D), pos_row, w_in, b_glu.reshape(1, 2 * cw),
                          gqk_tab, invf_tab, B=B, S=S, aw=aw, cw=cw, dh=dh, rot=rot, T=tiles.inproj)

    attn = _diff_attention(q, kt, v, lambda_q1.reshape(1, dh), lambda_k1.reshape(1, dh),
                           lambda_q2.reshape(1, dh), lambda_k2.reshape(1, dh), subln_g.reshape(1, 2 * dh),
                           B=B, S=S, H=H, dh=dh, lambda_init=lambda_init, rows=tiles.attn_rows,
                           chains=tiles.attn_chains)

    pad = LANES - n_exp - n_grp
    w_r = jnp.concatenate([w_router, w_group, jnp.zeros((D, pad), F32)], axis=1)
    b_r = jnp.concatenate([b_router, b_group, jnp.zeros((pad,), F32)]).reshape(1, LANES)
    wr_hi = w_r.astype(BF16)
    wr_lo = (w_r - wr_hi.astype(F32)).astype(BF16)
    wo = w_out.astype(BF16)
    x1, h2, route, route_t, counts = _mixout(attn, u, x2, mod, w_dw, b_dw.reshape(1, cw), conv_ln_g.reshape(1, cw),
                                     conv_ln_b.reshape(1, cw), wo[:aw], wo[aw:], g_ffn.reshape(1, D),
                                     wr_hi, wr_lo, b_r, S=S, T=tiles.mixout, n_exp=n_exp, n_grp=n_grp)

    cnt = counts[0, :n_exp].astype(jnp.int32)
    off = jnp.cumsum(cnt) - cnt
    experts = jnp.arange(n_exp, dtype=jnp.int32)[:, None]
    dest = [jnp.sum(jnp.where(route_t[k].astype(jnp.int32)[None, :] == experts, off[:, None], 0), axis=0)
            + route_t[4 + k].astype(jnp.int32) for k in range(TOP_K_INNER)]
    visits = _visit_tables(off, cnt, N * TOP_K_INNER, tiles.experts)

    xs = _scatter_rows(h2, dest, tokens=tiles.scatter)
    ys = _experts(xs, w_gate, w_up, w_down, visits, tm=tiles.experts)
    return _combine(ys, dest, route, x1, mod, S=S, T=tiles.combine)


def kernel(x, c, positions, w_ada, b_ada, g_mix, w_in, q_norm_g, k_norm_g, lambda_q1, lambda_k1, lambda_q2,
           lambda_k2, subln_g, b_glu, w_dw, b_dw, conv_ln_g, conv_ln_b, w_out, g_ffn, w_group, b_group,
           w_router, b_router, w_gate, w_up, w_down):
    B, S, D = x.shape
    depth = w_ada.shape[0]
    x2 = x.reshape(B * S, D)
    pos_row = positions.astype(F32).reshape(1, B * S)
    for l in range(depth):
        mod = _modulation(c, w_ada[l], b_ada[l])
        x2 = _layer(x2, mod, pos_row, l, B, S, g_mix[l], w_in[l], q_norm_g[l], k_norm_g[l], lambda_q1[l],
                    lambda_k1[l], lambda_q2[l], lambda_k2[l], subln_g[l], b_glu[l], w_dw[l], b_dw[l],
                    conv_ln_g[l], conv_ln_b[l], w_out[l], g_ffn[l], w_group[l], b_group[l], w_router[l],
                    b_router[l], w_gate[l], w_up[l], w_down[l])
    return x2.reshape(B, S, D)
```

```python
import collections
import functools
import math

import jax
import jax.numpy as jnp
from jax import lax
from jax.experimental import pallas as pl
from jax.experimental.pallas import tpu as pltpu

F32 = jnp.float32
BF16 = jnp.bfloat16

EPS = 1e-6
ROPE_THETA = 500000.0
N_DIFF_HEADS = 4
TOP_K_INNER = 2

LANES = 128
SUBLANES = 8
CONV_HALO = 32
VMEM_LIMIT = 48 * 1024 * 1024


Tiles = collections.namedtuple("Tiles", "inproj attn_rows attn_chains mixout scatter experts combine")


def _tiles(S):
    cap = lambda t: min(t, S)
    attn_rows = cap(1024) // 2
    return Tiles(inproj=cap(1024), attn_rows=attn_rows, attn_chains=min(4, S // attn_rows), mixout=cap(512),
                 scatter=cap(1024), experts=512, combine=cap(512))


def _cparams(sem):
    return pltpu.CompilerParams(dimension_semantics=sem, vmem_limit_bytes=VMEM_LIMIT)


def _silu(x):
    return x * jax.nn.sigmoid(x)


def _mod_kernel(c_ref, w_ref, b_ref, o_ref):
    c = c_ref[...]
    o_ref[...] = jnp.dot(_silu(c), w_ref[...], preferred_element_type=F32,
                         precision=lax.Precision.HIGHEST) + b_ref[...]


def _modulation(c, w_ada, b_ada):
    B, D = c.shape
    n_out = w_ada.shape[1]
    rows = 8
    c_pad = jnp.pad(c, ((0, rows - B), (0, 0)))
    bn = 2048
    out = pl.pallas_call(
        _mod_kernel,
        grid=(n_out // bn,),
        in_specs=[pl.BlockSpec((rows, D), lambda j: (0, 0)),
                  pl.BlockSpec((D, bn), lambda j: (0, j)),
                  pl.BlockSpec((1, bn), lambda j: (0, j))],
        out_specs=pl.BlockSpec((rows, bn), lambda j: (0, j)),
        out_shape=jax.ShapeDtypeStruct((rows, n_out), F32),
        compiler_params=_cparams(("parallel",)),
        name="adaln_mod",
    )(c_pad, w_ada, b_ada.reshape(1, n_out))
    return out[:B].reshape(B, 6, D)


W_CHUNK = 256


def _inproj_kernel(x_ref, mod_ref, g_ref, pos_ref, w_hbm, bglu_ref, gqk_ref, invf_ref,
                   q_ref, kt_ref, v_ref, u_ref, wv_ref, wqk_ref, stage, sem, *, aw, cw, dh, rot):
    T = x_ref.shape[0]
    half = rot // 2

    @pl.when(pl.program_id(0) == 0)
    def _():
        n_chunks = w_hbm.shape[1] // W_CHUNK
        copies = [pltpu.make_async_copy(w_hbm.at[:, pl.ds(j * W_CHUNK, W_CHUNK)], stage.at[j % 2], sem.at[j % 2])
                  for j in range(n_chunks)]
        copies[0].start()
        for j in range(n_chunks):
            if j + 1 < n_chunks:
                copies[j + 1].start()
            copies[j].wait()
            chunk = stage[j % 2]
            c0 = j * W_CHUNK
            if c0 < 2 * aw:
                wqk_ref[c0:c0 + W_CHUNK, :] = chunk.T.astype(BF16)
            else:
                wv_ref[:, c0 - 2 * aw:c0 - 2 * aw + W_CHUNK] = chunk.astype(BF16)

    x = x_ref[...]
    ms = jnp.mean(x * x, axis=-1, keepdims=True)
    sh = mod_ref[0:1, :]
    sc = mod_ref[1:2, :]
    h = x * lax.rsqrt(ms + EPS) * g_ref[...] * (1.0 + sc) + sh
    hb = h.astype(BF16)

    pv = jnp.dot(hb, wv_ref[...], preferred_element_type=F32)
    v_ref[...] = pv[:, :aw].astype(BF16)
    a = pv[:, aw:aw + cw] + bglu_ref[:, :cw]
    gate = pv[:, aw + cw:] + bglu_ref[:, cw:]
    u_ref[...] = (a * jax.nn.sigmoid(gate)).astype(BF16)

    qkt = lax.dot_general(wqk_ref[...], hb, (((1,), (1,)), ((), ())), preferred_element_type=F32)
    nch = 2 * aw // dh
    for c in range(T // LANES):
        sl = slice(c * LANES, (c + 1) * LANES)
        s3 = qkt[:, sl].reshape(nch, dh, LANES)
        ssq = jnp.mean(s3 * s3, axis=1, keepdims=True)
        y = s3 * lax.rsqrt(ssq + EPS) * gqk_ref[...].reshape(nch, dh, LANES)
        ang = invf_ref[...] * pos_ref[:, sl]
        cs = jnp.cos(ang)
        sn = jnp.sin(ang)
        t1 = y[:, 0:half, :]
        t2 = y[:, half:rot, :]
        y = jnp.concatenate([t1 * cs - t2 * sn, t2 * cs + t1 * sn, y[:, rot:, :]], axis=1)
        y2 = y.reshape(2 * aw, LANES)
        kt_ref[:, sl] = y2[aw:].astype(BF16)
        q_ref[sl, :] = y2[:aw].T.astype(BF16)


def _inproj(x2, mod, g_mix, pos_row, w_in, b_glu, gqk_tab, invf_tab, *, B, S, aw, cw, dh, rot, T):
    N, D = x2.shape
    nS = S // T
    n_vglu = w_in.shape[1] - 2 * aw
    kern = functools.partial(_inproj_kernel, aw=aw, cw=cw, dh=dh, rot=rot)
    return pl.pallas_call(
        kern,
        grid=(N // T,),
        in_specs=[pl.BlockSpec((T, D), lambda i: (i, 0)),
                  pl.BlockSpec((None, 6, D), lambda i: (i // nS, 0, 0)),
                  pl.BlockSpec((1, D), lambda i: (0, 0)),
                  pl.BlockSpec((1, T), lambda i: (0, i)),
                  pl.BlockSpec(memory_space=pl.ANY),
                  pl.BlockSpec((1, 2 * cw), lambda i: (0, 0)),
                  pl.BlockSpec(gqk_tab.shape, lambda i: (0, 0)),
                  pl.BlockSpec(invf_tab.shape, lambda i: (0, 0))],
        out_specs=[pl.BlockSpec((T, aw), lambda i: (i, 0)),
                   pl.BlockSpec((None, aw, T), lambda i: (i // nS, 0, i % nS)),
                   pl.BlockSpec((T, aw), lambda i: (i, 0)),
                   pl.BlockSpec((T, cw), lambda i: (i, 0))],
        out_shape=[jax.ShapeDtypeStruct((N, aw), BF16),
                   jax.ShapeDtypeStruct((B, aw, S), BF16),
                   jax.ShapeDtypeStruct((N, aw), BF16),
                   jax.ShapeDtypeStruct((N, cw), BF16)],
        scratch_shapes=[pltpu.VMEM((D, n_vglu), BF16),
                        pltpu.VMEM((2 * aw, D), BF16),
                        pltpu.VMEM((2, D, W_CHUNK), F32),
                        pltpu.SemaphoreType.DMA((2,))],
        compiler_params=_cparams(("arbitrary",)),
        name="inproj_qknorm_rope_glu",
    )(x2, mod, g_mix, pos_row, w_in, b_glu, gqk_tab, invf_tab)


def _attn_kernel(q_ref, kt_ref, v_ref, lq1_ref, lk1_ref, lq2_ref, lk2_ref, sg_ref, o_ref, acc_ref, s_ref,
                 *, rows, chains, dh, lambda_init):
    i = pl.program_id(2)
    hd = 2 * dh
    tk = 2 * rows
    base = (chains // 2) * i

    def stacked(q):
        lane = lax.broadcasted_iota(jnp.int32, q.shape, 1)
        zero = jnp.zeros_like(q)
        return jnp.concatenate([jnp.where(lane < dh, q, zero), jnp.where(lane >= dh, q, zero)], axis=0)

    qs = [stacked(q_ref[c * rows:(c + 1) * rows, :]) for c in range(chains)]

    acc_ref[...] = jnp.zeros_like(acc_ref)
    ones = {w: jnp.ones((w, hd), BF16) for w in (rows, tk)}

    def scores(t, c):
        start = pl.multiple_of(t * tk, tk)
        s_ref[c] = jnp.dot(qs[c], kt_ref[:, pl.ds(start, tk)], preferred_element_type=F32)

    def softmax_pv(t, c, m, diagonal=False):
        width = rows * (c % 2 + 1) if diagonal else tk
        start = pl.multiple_of(t * tk, tk)
        vt = jnp.concatenate([v_ref[pl.ds(start, width), :], ones[width]], axis=1)
        s = s_ref[c, :, :width]
        if diagonal:
            row = lax.broadcasted_iota(jnp.int32, (2 * rows, rows), 0)
            col = lax.broadcasted_iota(jnp.int32, (2 * rows, rows), 1)
            qrow = jnp.where(row >= rows, row - rows, row)
            tail = jnp.where(col <= qrow, s[:, width - rows:], -jnp.inf)
            s = tail if width == rows else jnp.concatenate([s[:, :width - rows], tail], axis=1)
        m_new = jnp.maximum(m, jnp.max(s, axis=1, keepdims=True))
        alpha = jnp.exp2(m - m_new)
        p = jnp.exp2(s - m_new).astype(BF16)
        acc_ref[c] = alpha * acc_ref[c] + jnp.dot(p, vt, preferred_element_type=F32)
        return m_new

    m0 = jnp.full((2 * rows, 1), -jnp.inf, F32)
    scores(0, 0)

    def common_tile(t, ms):
        ms = list(ms)
        for c in range(chains):
            if c + 1 < chains:
                scores(t, c + 1)
            else:
                scores(t + 1, 0)
            ms[c] = softmax_pv(t, c, ms[c])
        return tuple(ms)

    ms = list(lax.fori_loop(0, base, common_tile, (m0,) * chains))

    items = [(k, c) for k in range(chains // 2) for c in range(2 * k, chains)]
    for n, (k, c) in enumerate(items):
        if n + 1 < len(items):
            k_next, c_next = items[n + 1]
            scores(base + k_next, c_next)
        ms[c] = softmax_pv(base + k, c, ms[c], diagonal=(c // 2 == k))

    lam = (jnp.exp(jnp.sum(lq1_ref[...] * lk1_ref[...], axis=1, keepdims=True))
           - jnp.exp(jnp.sum(lq2_ref[...] * lk2_ref[...], axis=1, keepdims=True)) + lambda_init)
    for c in range(chains):
        o = (acc_ref[c, 0:rows, 0:hd] / acc_ref[c, 0:rows, hd:]
             - lam * (acc_ref[c, rows:, 0:hd] / acc_ref[c, rows:, hd:]))
        ms_o = jnp.mean(o * o, axis=1, keepdims=True)
        o = o * lax.rsqrt(ms_o + EPS) * sg_ref[...] * (1.0 - lambda_init)
        o_ref[c * rows:(c + 1) * rows, :] = o.astype(o_ref.dtype)


def _diff_attention(q, kt, v, lq1, lk1, lq2, lk2, subln_g, *, B, S, H, dh, lambda_init, rows, chains):
    aw = H * 2 * dh
    q3 = q.reshape(B, S, aw)
    v3 = v.reshape(B, S, aw)
    hd = 2 * dh
    step = rows * chains
    kern = functools.partial(_attn_kernel, rows=rows, chains=chains, dh=dh, lambda_init=lambda_init)
    vec = pl.BlockSpec((1, dh), lambda b, h, i: (0, 0))
    out = pl.pallas_call(
        kern,
        grid=(B, H, S // step),
        in_specs=[pl.BlockSpec((None, step, hd), lambda b, h, i: (b, i, h)),
                  pl.BlockSpec((None, hd, S), lambda b, h, i: (b, h, 0)),
                  pl.BlockSpec((None, S, hd), lambda b, h, i: (b, 0, h)),
                  vec, vec, vec, vec,
                  pl.BlockSpec((1, hd), lambda b, h, i: (0, 0))],
        out_specs=pl.BlockSpec((None, step, hd), lambda b, h, i: (b, i, h)),
        out_shape=jax.ShapeDtypeStruct((B, S, aw), BF16),
        scratch_shapes=[pltpu.VMEM((chains, 2 * rows, 2 * hd), F32),
                        pltpu.VMEM((chains, 2 * rows, 2 * rows), F32)],
        compiler_params=_cparams(("parallel", "parallel", "parallel")),
        name="diff_flash_attention",
    )(q3, kt, v3, lq1, lk1, lq2, lk2, subln_g)
    return out.reshape(B * S, aw)


def _mixout_kernel(attn_ref, ucur_ref, uhalo_ref, x_ref, mod_ref, wdw_ref, bdw_ref, lng_ref, lnb_ref,
                   wo1_ref, wo2_ref, gffn_ref, wrh_ref, wrl_ref, br_ref,
                   x1_ref, h2_ref, route_ref, route_t_ref, cnt_ref, ubuf_ref, conv_ref,
                   *, nS, conv_k, n_exp, n_grp):
    T = x_ref.shape[0]
    cw = ucur_ref.shape[1]
    i = pl.program_id(0)

    @pl.when(i == 0)
    def _():
        cnt_ref[...] = jnp.zeros_like(cnt_ref)

    first = (i % nS) == 0
    halo = uhalo_ref[...].astype(F32)
    ubuf_ref[0, 0:CONV_HALO, :] = jnp.where(first, jnp.zeros_like(halo), halo)
    ubuf_ref[0, CONV_HALO:, :] = ucur_ref[...].astype(F32)
    span = T + CONV_HALO - SUBLANES
    for b in range(1, SUBLANES):
        ubuf_ref[b, 0:span, :] = ubuf_ref[0, b:b + span, :]

    off = CONV_HALO - (conv_k - 1)
    rows = 32
    for r0 in range(0, T, rows):
        acc = jnp.zeros((rows // SUBLANES, SUBLANES, cw), F32)
        for j in range(conv_k):
            a, b = divmod(j + off, SUBLANES)
            lo_r = r0 + a * SUBLANES
            acc = acc + (wdw_ref[j * SUBLANES:(j + 1) * SUBLANES, :]
                         * ubuf_ref[b, lo_r:lo_r + rows, :].reshape(rows // SUBLANES, SUBLANES, cw))
        y = acc.reshape(rows, cw) + bdw_ref[...]
        mu = jnp.mean(y, axis=1, keepdims=True)
        d = y - mu
        var = jnp.mean(d * d, axis=1, keepdims=True)
        z = d * lax.rsqrt(var + EPS) * lng_ref[...] + lnb_ref[...]
        conv_ref[r0:r0 + rows, :] = _silu(z).astype(BF16)

    yo = (jnp.dot(attn_ref[...], wo1_ref[...], preferred_element_type=F32)
          + jnp.dot(conv_ref[...], wo2_ref[...], preferred_element_type=F32))
    x1 = x_ref[...] + mod_ref[2:3, :] * yo
    x1_ref[...] = x1
    ms = jnp.mean(x1 * x1, axis=1, keepdims=True)
    h2 = x1 * lax.rsqrt(ms + EPS) * gffn_ref[...] * (1.0 + mod_ref[4:5, :]) + mod_ref[3:4, :]
    h2_ref[...] = h2
    hi = h2.astype(BF16)
    lo = (h2 - hi.astype(F32)).astype(BF16)

    logits = (jnp.dot(hi, wrh_ref[...], preferred_element_type=F32)
              + jnp.dot(lo, wrh_ref[...], preferred_element_type=F32)
              + jnp.dot(hi, wrl_ref[...], preferred_element_type=F32)) + br_ref[...]
    lane_i = lax.broadcasted_iota(jnp.int32, logits.shape, 1)
    lane = lane_i.astype(F32)
    big = jnp.float32(1e9)
    ninf = jnp.float32(-jnp.inf)
    is_g = (lane_i >= n_exp) & (lane_i < n_exp + n_grp)
    gl = jnp.where(is_g, logits, ninf)
    gmax = jnp.max(gl, axis=1, keepdims=True)
    gsum = jnp.sum(jnp.where(is_g, jnp.exp(gl - gmax), 0.0), axis=1, keepdims=True)
    g_p = 1.0 / gsum
    gidx = jnp.min(jnp.where(gl == gmax, lane, big), axis=1, keepdims=True) - n_exp
    epg = n_exp // n_grp
    lo_l = gidx * epg
    in_grp = (lane >= lo_l) & (lane < lo_l + epg)
    el = jnp.where(in_grp, logits, ninf)
    m1 = jnp.max(el, axis=1, keepdims=True)
    i1 = jnp.min(jnp.where(el == m1, lane, big), axis=1, keepdims=True)
    el2 = jnp.where(lane == i1, ninf, el)
    m2 = jnp.max(el2, axis=1, keepdims=True)
    i2 = jnp.min(jnp.where(el2 == m2, lane, big), axis=1, keepdims=True)
    e2 = jnp.exp(m2 - m1)
    p1 = 1.0 / (1.0 + e2)
    p2 = e2 / (1.0 + e2)

    sel = (lane == i1) | (lane == i2)
    rr = lax.broadcasted_iota(jnp.int32, (T, T), 0)
    cc = lax.broadcasted_iota(jnp.int32, (T, T), 1)
    before = jnp.dot((rr > cc).astype(BF16), sel.astype(BF16), preferred_element_type=F32) + cnt_ref[...]
    r1 = jnp.sum(jnp.where(lane == i1, before, 0.0), axis=1, keepdims=True)
    r2 = jnp.sum(jnp.where(lane == i2, before, 0.0), axis=1, keepdims=True)
    cnt_ref[...] += jnp.sum(sel.astype(F32), axis=0, keepdims=True)

    route = jnp.zeros_like(logits)
    for k, val in enumerate((i1, i2, g_p * p1, g_p * p2, r1, r2)):
        route = jnp.where(lane_i == k, val, route)
    route_ref[...] = route
    route_t_ref[...] = route.T[:SUBLANES]


def _mixout(attn, u, x2, mod, w_dw, b_dw, ln_g, ln_b, wo1, wo2, g_ffn, wr_hi, wr_lo, b_r,
            *, S, T, n_exp, n_grp):
    N, D = x2.shape
    aw = attn.shape[1]
    cw = u.shape[1]
    nS = S // T
    conv_k = w_dw.shape[0]
    w_dw = jnp.repeat(w_dw, SUBLANES, axis=0)
    hb = T // CONV_HALO
    kern = functools.partial(_mixout_kernel, nS=nS, conv_k=conv_k, n_exp=n_exp, n_grp=n_grp)
    full = lambda a: pl.BlockSpec(a.shape, lambda i: (0, 0))
    return pl.pallas_call(
        kern,
        grid=(N // T,),
        in_specs=[pl.BlockSpec((T, aw), lambda i: (i, 0)),
                  pl.BlockSpec((T, cw), lambda i: (i, 0)),
                  pl.BlockSpec((CONV_HALO, cw), lambda i: (jnp.maximum(i * hb - 1, 0), 0)),
                  pl.BlockSpec((T, D), lambda i: (i, 0)),
                  pl.BlockSpec((None, 6, D), lambda i: (i // nS, 0, 0)),
                  full(w_dw), full(b_dw), full(ln_g), full(ln_b), full(wo1), full(wo2), full(g_ffn),
                  full(wr_hi), full(wr_lo), full(b_r)],
        out_specs=[pl.BlockSpec((T, D), lambda i: (i, 0)),
                   pl.BlockSpec((T, D), lambda i: (i, 0)),
                   pl.BlockSpec((T, LANES), lambda i: (i, 0)),
                   pl.BlockSpec((SUBLANES, T), lambda i: (0, i)),
                   pl.BlockSpec((1, LANES), lambda i: (0, 0))],
        out_shape=[jax.ShapeDtypeStruct((N, D), F32),
                   jax.ShapeDtypeStruct((N, D), F32),
                   jax.ShapeDtypeStruct((N, LANES), F32),
                   jax.ShapeDtypeStruct((SUBLANES, N), F32),
                   jax.ShapeDtypeStruct((1, LANES), F32)],
        scratch_shapes=[pltpu.VMEM((SUBLANES, CONV_HALO + T, cw), F32),
                        pltpu.VMEM((T, cw), BF16)],
        compiler_params=_cparams(("arbitrary",)),
        name="conv_outproj_router",
    )(attn, u, u, x2, mod, w_dw, b_dw, ln_g, ln_b, wo1, wo2, g_ffn, wr_hi, wr_lo, b_r)


def _row_of(ref, r):
    if isinstance(r, int):
        return ref.at[r // SUBLANES, pl.ds(r % SUBLANES, 1)]
    return ref.at[lax.shift_right_logical(r, 3), pl.ds(r & (SUBLANES - 1), 1)]


def _scatter_kernel(dest_ref, h2_ref, xs_hbm, sem):
    groups = h2_ref.shape[0]
    tokens = groups * SUBLANES

    def issue(g, carry):
        for j in range(SUBLANES):
            for k in range(TOP_K_INNER):
                d = dest_ref[0, k * tokens + g * SUBLANES + j]
                pltpu.make_async_copy(h2_ref.at[g, pl.ds(j, 1)], _row_of(xs_hbm, d), sem).start()
        return carry

    lax.fori_loop(0, groups, issue, 0)
    for k in range(TOP_K_INNER):
        pltpu.make_async_copy(h2_ref, xs_hbm.at[pl.ds(0, groups)], sem).wait()


def _dest_table(dest, tokens):
    tab = jnp.concatenate([d.reshape(-1, tokens) for d in dest], axis=1)
    return tab.reshape(tab.shape[0], 1, tab.shape[1])


def _scatter_rows(h2, dest, *, tokens):
    N, C = h2.shape
    steps = N // tokens
    dest3 = _dest_table(dest, tokens)
    rows = N * TOP_K_INNER
    xs = pl.pallas_call(
        _scatter_kernel,
        grid=(steps,),
        in_specs=[pl.BlockSpec((None, 1, TOP_K_INNER * tokens), lambda s: (s, 0, 0), memory_space=pltpu.SMEM),
                  pl.BlockSpec((tokens // SUBLANES, SUBLANES, C), lambda s: (s, 0, 0))],
        out_specs=pl.BlockSpec(memory_space=pl.ANY),
        out_shape=jax.ShapeDtypeStruct((rows // SUBLANES, SUBLANES, C), h2.dtype),
        scratch_shapes=[pltpu.SemaphoreType.DMA],
        compiler_params=_cparams(("arbitrary",)),
        name="moe_scatter_rows",
    )(dest3, h2.reshape(N // SUBLANES, SUBLANES, C))
    return xs.reshape(rows, C)


XS_RING = 3


def _experts_kernel(vt_ref, ve_ref, vlo_ref, vhi_ref, vnew_ref, vnext_ref, vslot_ref,
                    xs_hbm, wg_hbm, wu_hbm, wd_hbm, ys_ref, xbuf, wg_buf, wu_buf, wd_buf, sem, wsem,
                    *, n_tiles):
    v = pl.program_id(0)
    lo = vlo_ref[v]
    hi = vhi_ref[v]
    t = vt_ref[v]
    tm = xbuf.shape[1]
    wslot = vslot_ref[v]

    def weight_copies(expert, slot):
        return [pltpu.make_async_copy(hbm.at[expert], buf.at[slot], wsem.at[slot, j])
                for j, (hbm, buf) in enumerate(((wg_hbm, wg_buf), (wu_hbm, wu_buf), (wd_hbm, wd_buf)))]

    @pl.when(v == 0)
    def _():
        for c in weight_copies(ve_ref[0], 0):
            c.start()

    @pl.when(vnew_ref[v] == 1)
    def _():
        @pl.when(vnext_ref[v] >= 0)
        def _():
            for c in weight_copies(vnext_ref[v], 1 - wslot):
                c.start()

        for c in weight_copies(ve_ref[v], wslot):
            c.wait()

    def tile_copy(tile):
        slot = lax.rem(tile, XS_RING)
        return pltpu.make_async_copy(xs_hbm.at[pl.ds(pl.multiple_of(tile * tm, tm), tm)], xbuf.at[slot],
                                     sem.at[slot])

    @pl.when(v == 0)
    def _():
        for tile in range(min(XS_RING - 1, n_tiles)):
            tile_copy(tile).start()

    @pl.when((lo == 0) & (hi > lo))
    def _():
        @pl.when(t + (XS_RING - 1) < n_tiles)
        def _():
            tile_copy(t + (XS_RING - 1)).start()

        tile_copy(t).wait()

    @pl.when(hi > lo)
    def _():
        x = xbuf[lax.rem(t, XS_RING)].astype(BF16)
        g = jnp.dot(x, wg_buf[wslot].astype(BF16), preferred_element_type=F32)
        u = jnp.dot(x, wu_buf[wslot].astype(BF16), preferred_element_type=F32)
        hid = (_silu(g) * u).astype(BF16)
        y = jnp.dot(hid, wd_buf[wslot].astype(BF16), preferred_element_type=F32)

        @pl.when(lo == 0)
        def _():
            ys_ref[...] = y

        @pl.when(lo > 0)
        def _():
            row = lax.broadcasted_iota(jnp.int32, y.shape, 0)
            ys_ref[...] = jnp.where((row >= lo) & (row < hi), y, ys_ref[...])


def _experts(xs, w_gate, w_up, w_down, visits, *, tm):
    R, C = xs.shape
    E, D, ff = w_gate.shape
    vt = visits[0]
    any_spec = pl.BlockSpec(memory_space=pl.ANY)
    grid_spec = pltpu.PrefetchScalarGridSpec(
        num_scalar_prefetch=len(visits),
        grid=(vt.shape[0],),
        in_specs=[any_spec, any_spec, any_spec, any_spec],
        out_specs=pl.BlockSpec((tm, C), lambda v, vt, *_: (vt[v], 0)),
        scratch_shapes=[pltpu.VMEM((XS_RING, tm, C), F32),
                        pltpu.VMEM((2, D, ff), w_gate.dtype),
                        pltpu.VMEM((2, D, ff), w_up.dtype),
                        pltpu.VMEM((2, ff, D), w_down.dtype),
                        pltpu.SemaphoreType.DMA((XS_RING,)),
                        pltpu.SemaphoreType.DMA((2, 3))],
    )
    return pl.pallas_call(
        functools.partial(_experts_kernel, n_tiles=R // tm),
        grid_spec=grid_spec,
        out_shape=jax.ShapeDtypeStruct((R, C), F32),
        compiler_params=_cparams(("arbitrary",)),
        name="moe_grouped_experts",
    )(*visits, xs, w_gate, w_up, w_down)


def _visit_tables(off, cnt, n_rows, tm):
    n_tiles = n_rows // tm
    n_exp = off.shape[0]
    n_visits = n_tiles + n_exp - 1
    tile_starts = jnp.arange(n_tiles, dtype=jnp.int32) * tm
    seg_starts = jnp.where((cnt > 0) & (off % tm != 0), off, n_rows)
    starts = jnp.sort(jnp.concatenate([tile_starts, seg_starts]))
    lo_abs = starts[:n_visits]
    hi_abs = starts[1:n_visits + 1]
    valid = lo_abs < n_rows
    tile = jnp.where(valid, lo_abs // tm, n_tiles - 1)
    ends = off + cnt
    probe = jnp.where(valid, lo_abs, n_rows - 1)
    expert = jnp.sum(ends[None, :] <= probe[:, None], axis=1).astype(jnp.int32)
    row_lo = jnp.where(valid, lo_abs - tile * tm, 0)
    row_hi = jnp.where(valid, hi_abs - tile * tm, 0)
    new = jnp.concatenate([jnp.ones((1,), jnp.int32), (expert[1:] != expert[:-1]).astype(jnp.int32)])
    slot = (jnp.cumsum(new) - 1) % 2
    nxt = jnp.sum(expert[None, :] <= expert[:, None], axis=1)
    nxt_expert = jnp.where(nxt < n_visits, expert[jnp.minimum(nxt, n_visits - 1)], -1)
    return tile, expert, row_lo, row_hi, new, nxt_expert.astype(jnp.int32), slot.astype(jnp.int32)


def _combine_kernel(dcur_ref, dnxt_ref, ys_hbm, route_ref, x1_ref, mod_ref, o_ref, ybuf, sem):
    T = x1_ref.shape[0]
    i = pl.program_id(0)
    slot = lax.rem(i, 2)

    def gather(d_ref, sl):
        def body(g, carry):
            for j in range(SUBLANES):
                for k in range(TOP_K_INNER):
                    d = d_ref[0, k * T + g * SUBLANES + j]
                    pltpu.make_async_copy(_row_of(ys_hbm, d), ybuf.at[sl, k, g, pl.ds(j, 1)],
                                          sem.at[sl]).start()
            return carry
        lax.fori_loop(0, T // SUBLANES, body, 0)

    @pl.when(i == 0)
    def _():
        gather(dcur_ref, 0)

    @pl.when(i + 1 < pl.num_programs(0))
    def _():
        gather(dnxt_ref, 1 - slot)

    for k in range(TOP_K_INNER):
        pltpu.make_async_copy(ys_hbm.at[pl.ds(0, T // SUBLANES)], ybuf.at[slot, k], sem.at[slot]).wait()

    route = route_ref[...]
    moe = jnp.zeros(x1_ref.shape, F32)
    for k in range(TOP_K_INNER):
        moe = moe + route[:, TOP_K_INNER + k:TOP_K_INNER + k + 1] * ybuf[slot, k].reshape(x1_ref.shape)
    o_ref[...] = x1_ref[...] + mod_ref[5:6, :] * moe


def _combine(ys, dest, route, x1, mod, *, S, T):
    N, D = x1.shape
    C = ys.shape[1]
    steps = N // T
    nS = S // T
    dest3 = _dest_table(dest, T)
    dspec = lambda f: pl.BlockSpec((None, 1, TOP_K_INNER * T), f, memory_space=pltpu.SMEM)
    return pl.pallas_call(
        _combine_kernel,
        grid=(steps,),
        in_specs=[dspec(lambda i: (i, 0, 0)),
                  dspec(lambda i: (jnp.minimum(i + 1, steps - 1), 0, 0)),
                  pl.BlockSpec(memory_space=pl.ANY),
                  pl.BlockSpec((T, LANES), lambda i: (i, 0)),
                  pl.BlockSpec((T, D), lambda i: (i, 0)),
                  pl.BlockSpec((None, 6, D), lambda i: (i // nS, 0, 0))],
        out_specs=pl.BlockSpec((T, D), lambda i: (i, 0)),
        out_shape=jax.ShapeDtypeStruct((N, D), F32),
        scratch_shapes=[pltpu.VMEM((2, TOP_K_INNER, T // SUBLANES, SUBLANES, C), F32),
                        pltpu.SemaphoreType.DMA((2,))],
        compiler_params=_cparams(("arbitrary",)),
        name="moe_gather_combine",
    )(dest3, dest3, ys.reshape(ys.shape[0] // SUBLANES, SUBLANES, C), route, x1, mod)


def _layer(x2, mod, pos_row, l, B, S, g_mix, w_in, q_norm_g, k_norm_g, lambda_q1, lambda_k1, lambda_q2,
           lambda_k2, subln_g, b_glu, w_dw, b_dw, conv_ln_g, conv_ln_b, w_out, g_ffn, w_group, b_group,
           w_router, b_router, w_gate, w_up, w_down):
    N, D = x2.shape
    dh = q_norm_g.shape[0]
    H = N_DIFF_HEADS
    aw = H * 2 * dh
    cw = w_dw.shape[1]
    rot = dh // 4
    n_grp = w_group.shape[1]
    n_exp = w_router.shape[1]
    lambda_init = 0.8 - 0.6 * math.exp(-0.3 * l)
    tiles = _tiles(S)

    scale = dh ** -0.5 * math.log2(math.e)
    gq = jnp.tile(q_norm_g * scale, aw // dh)
    gk = jnp.tile(k_norm_g, aw // dh)
    gqk_tab = jnp.broadcast_to(jnp.concatenate([gq, gk])[:, None], (2 * aw, LANES))
    inv_freq = ROPE_THETA ** (-jnp.arange(0, rot, 2, dtype=F32) / rot)
    invf_tab = jnp.broadcast_to(inv_freq[:, None], (rot // 2, LANES))

    q, kt, v, u = _inproj(x2, mod, g_mix.reshape(1, D), pos_row, w_in, b_glu.reshape(1, 2 * cw),
                          gqk_tab, invf_tab, B=B, S=S, aw=aw, cw=cw, dh=dh, rot=rot, T=tiles.inproj)

    attn = _diff_attention(q, kt, v, lambda_q1.reshape(1, dh), lambda_k1.reshape(1, dh),
                           lambda_q2.reshape(1, dh), lambda_k2.reshape(1, dh), subln_g.reshape(1, 2 * dh),
                           B=B, S=S, H=H, dh=dh, lambda_init=lambda_init, rows=tiles.attn_rows,
                           chains=tiles.attn_chains)

    pad = LANES - n_exp - n_grp
    w_r = jnp.concatenate([w_router, w_group, jnp.zeros((D, pad), F32)], axis=1)
    b_r = jnp.concatenate([b_router, b_group, jnp.zeros((pad,), F32)]).reshape(1, LANES)
    wr_hi = w_r.astype(BF16)
    wr_lo = (w_r - wr_hi.astype(F32)).astype(BF16)
    wo = w_out.astype(BF16)
    x1, h2, route, route_t, counts = _mixout(attn, u, x2, mod, w_dw, b_dw.reshape(1, cw), conv_ln_g.reshape(1, cw),
                                     conv_ln_b.reshape(1, cw), wo[:aw], wo[aw:], g_ffn.reshape(1, D),
                                     wr_hi, wr_lo, b_r, S=S, T=tiles.mixout, n_exp=n_exp, n_grp=n_grp)

    cnt = counts[0, :n_exp].astype(jnp.int32)
    off = jnp.cumsum(cnt) - cnt
    experts = jnp.arange(n_exp, dtype=jnp.int32)[:, None]
    dest = [jnp.sum(jnp.where(route_t[k].astype(jnp.int32)[None, :] == experts, off[:, None], 0), axis=0)
            + route_t[4 + k].astype(jnp.int32) for k in range(TOP_K_INNER)]
    visits = _visit_tables(off, cnt, N * TOP_K_INNER, tiles.experts)

    xs = _scatter_rows(h2, dest, tokens=tiles.scatter)
    ys = _experts(xs, w_gate, w_up, w_down, visits, tm=tiles.experts)
    return _combine(ys, dest, route, x1, mod, S=S, T=tiles.combine)


def kernel(x, c, positions, w_ada, b_ada, g_mix, w_in, q_norm_g, k_norm_g, lambda_q1, lambda_k1, lambda_q2,
           lambda_k2, subln_g, b_glu, w_dw, b_dw, conv_ln_g, conv_ln_b, w_out, g_ffn, w_group, b_group,
           w_router, b_router, w_gate, w_up, w_down):
    B, S, D = x.shape
    depth = w_ada.shape[0]
    x2 = x.reshape(B * S, D)
    pos_row = positions.astype(F32).reshape(1, B * S)
    for l in range(depth):
        mod = _modulation(c, w_ada[l], b_ada[l])
        x2 = _layer(x2, mod, pos_row, l, B, S, g_mix[l], w_in[l], q_norm_g[l], k_norm_g[l], lambda_q1[l],
                    lambda_k1[l], lambda_q2[l], lambda_k2[l], subln_g[l], b_glu[l], w_dw[l], b_dw[l],
                    conv_ln_g[l], conv_ln_b[l], w_out[l], g_ffn[l], w_group[l], b_group[l], w_router[l],
                    b_router[l], w_gate[l], w_up[l], w_down[l])
    return x2.reshape(B, S, D)
```

```python
import collections
import functools
import math

import numpy as np
import jax
import jax.numpy as jnp
from jax import lax
from jax.experimental import pallas as pl
from jax.experimental.pallas import tpu as pltpu

F32 = jnp.float32
BF16 = jnp.bfloat16

EPS = 1e-6
ROPE_THETA = 500000.0
N_DIFF_HEADS = 4
TOP_K_INNER = 2

LANES = 128
SUBLANES = 8
CONV_HALO = 32
VMEM_LIMIT = 48 * 1024 * 1024


Tiles = collections.namedtuple("Tiles", "inproj attn_rows attn_chains mixout scatter experts combine")


def _tiles(S):
    cap = lambda t: min(t, S)
    attn_rows = cap(1024) // 2
    return Tiles(inproj=cap(1024), attn_rows=attn_rows, attn_chains=min(4, S // attn_rows), mixout=cap(512),
                 scatter=cap(2048),
                 experts=512, combine=cap(1024))


def _cparams(sem):
    return pltpu.CompilerParams(dimension_semantics=sem, vmem_limit_bytes=VMEM_LIMIT)


def _silu(x):
    return x * jax.nn.sigmoid(x)


def _mod_kernel(c_ref, w_ref, b_ref, o_ref):
    c = c_ref[...]
    o_ref[...] = jnp.dot(_silu(c), w_ref[...], preferred_element_type=F32,
                         precision=lax.Precision.HIGHEST) + b_ref[...]


def _modulation(c, w_ada, b_ada):
    B, D = c.shape
    n_out = w_ada.shape[1]
    rows = 8
    c_pad = jnp.pad(c, ((0, rows - B), (0, 0)))
    bn = 2048
    out = pl.pallas_call(
        _mod_kernel,
        grid=(n_out // bn,),
        in_specs=[pl.BlockSpec((rows, D), lambda j: (0, 0)),
                  pl.BlockSpec((D, bn), lambda j: (0, j)),
                  pl.BlockSpec((1, bn), lambda j: (0, j))],
        out_specs=pl.BlockSpec((rows, bn), lambda j: (0, j)),
        out_shape=jax.ShapeDtypeStruct((rows, n_out), F32),
        compiler_params=_cparams(("parallel",)),
        name="adaln_mod",
    )(c_pad, w_ada, b_ada.reshape(1, n_out))
    return out[:B].reshape(B, 6, D)


W_CHUNK = 256


def _inproj_kernel(x_ref, mod_ref, g_ref, pos_ref, w_hbm, bglu_ref, gqk_ref, invf_ref,
                   q_ref, kt_ref, v_ref, u_ref, wv_ref, wqk_ref, stage, sem, *, aw, cw, dh, rot):
    T = x_ref.shape[0]
    half = rot // 2

    @pl.when(pl.program_id(0) == 0)
    def _():
        n_chunks = w_hbm.shape[1] // W_CHUNK
        copies = [pltpu.make_async_copy(w_hbm.at[:, pl.ds(j * W_CHUNK, W_CHUNK)], stage.at[j % 2], sem.at[j % 2])
                  for j in range(n_chunks)]
        copies[0].start()
        for j in range(n_chunks):
            if j + 1 < n_chunks:
                copies[j + 1].start()
            copies[j].wait()
            chunk = stage[j % 2]
            c0 = j * W_CHUNK
            if c0 < 2 * aw:
                wqk_ref[c0:c0 + W_CHUNK, :] = chunk.T.astype(BF16)
            else:
                wv_ref[:, c0 - 2 * aw:c0 - 2 * aw + W_CHUNK] = chunk.astype(BF16)

    x = x_ref[...]
    ms = jnp.mean(x * x, axis=-1, keepdims=True)
    sh = mod_ref[0:1, :]
    sc = mod_ref[1:2, :]
    h = x * lax.rsqrt(ms + EPS) * g_ref[...] * (1.0 + sc) + sh
    hb = h.astype(BF16)

    pv = jnp.dot(hb, wv_ref[...], preferred_element_type=F32)
    v_ref[...] = pv[:, :aw].astype(BF16)
    a = pv[:, aw:aw + cw] + bglu_ref[:, :cw]
    gate = pv[:, aw + cw:] + bglu_ref[:, cw:]
    u_ref[...] = (a * jax.nn.sigmoid(gate)).astype(BF16)

    qkt = lax.dot_general(wqk_ref[...], hb, (((1,), (1,)), ((), ())), preferred_element_type=F32)
    nch = 2 * aw // dh
    for c in range(T // LANES):
        sl = slice(c * LANES, (c + 1) * LANES)
        s3 = qkt[:, sl].reshape(nch, dh, LANES)
        ssq = jnp.mean(s3 * s3, axis=1, keepdims=True)
        y = s3 * lax.rsqrt(ssq + EPS) * gqk_ref[...].reshape(nch, dh, LANES)
        ang = invf_ref[...] * pos_ref[:, sl]
        cs = jnp.cos(ang)
        sn = jnp.sin(ang)
        t1 = y[:, 0:half, :]
        t2 = y[:, half:rot, :]
        y = jnp.concatenate([t1 * cs - t2 * sn, t2 * cs + t1 * sn, y[:, rot:, :]], axis=1)
        y2 = y.reshape(2 * aw, LANES)
        kt_ref[:, sl] = y2[aw:].astype(BF16)
        q_ref[sl, :] = y2[:aw].T.astype(BF16)


def _inproj(x2, mod, g_mix, pos_row, w_in, b_glu, gqk_tab, invf_tab, *, B, S, aw, cw, dh, rot, T):
    N, D = x2.shape
    nS = S // T
    n_vglu = w_in.shape[1] - 2 * aw
    kern = functools.partial(_inproj_kernel, aw=aw, cw=cw, dh=dh, rot=rot)
    return pl.pallas_call(
        kern,
        grid=(N // T,),
        in_specs=[pl.BlockSpec((T, D), lambda i: (i, 0)),
                  pl.BlockSpec((None, 6, D), lambda i: (i // nS, 0, 0)),
                  pl.BlockSpec((1, D), lambda i: (0, 0)),
                  pl.BlockSpec((1, T), lambda i: (0, i)),
                  pl.BlockSpec(memory_space=pl.ANY),
                  pl.BlockSpec((1, 2 * cw), lambda i: (0, 0)),
                  pl.BlockSpec(gqk_tab.shape, lambda i: (0, 0)),
                  pl.BlockSpec(invf_tab.shape, lambda i: (0, 0))],
        out_specs=[pl.BlockSpec((T, aw), lambda i: (i, 0)),
                   pl.BlockSpec((None, aw, T), lambda i: (i // nS, 0, i % nS)),
                   pl.BlockSpec((T, aw), lambda i: (i, 0)),
                   pl.BlockSpec((T, cw), lambda i: (i, 0))],
        out_shape=[jax.ShapeDtypeStruct((N, aw), BF16),
                   jax.ShapeDtypeStruct((B, aw, S), BF16),
                   jax.ShapeDtypeStruct((N, aw), BF16),
                   jax.ShapeDtypeStruct((N, cw), BF16)],
        scratch_shapes=[pltpu.VMEM((D, n_vglu), BF16),
                        pltpu.VMEM((2 * aw, D), BF16),
                        pltpu.VMEM((2, D, W_CHUNK), F32),
                        pltpu.SemaphoreType.DMA((2,))],
        compiler_params=_cparams(("arbitrary",)),
        name="inproj_qknorm_rope_glu",
    )(x2, mod, g_mix, pos_row, w_in, b_glu, gqk_tab, invf_tab)


def _attn_kernel(q_ref, kt_ref, v_ref, lq1_ref, lk1_ref, lq2_ref, lk2_ref, sg_ref, o_ref, acc_ref, s_ref,
                 *, rows, chains, dh, lambda_init):
    i = pl.program_id(2)
    hd = 2 * dh
    tk = 2 * rows
    base = (chains // 2) * i

    def stacked(q):
        lane = lax.broadcasted_iota(jnp.int32, q.shape, 1)
        zero = jnp.zeros_like(q)
        return jnp.concatenate([jnp.where(lane < dh, q, zero), jnp.where(lane >= dh, q, zero)], axis=0)

    qs = [stacked(q_ref[c * rows:(c + 1) * rows, :]) for c in range(chains)]

    acc_ref[...] = jnp.zeros_like(acc_ref)
    ones = {w: jnp.ones((w, hd), BF16) for w in (rows, tk)}

    def scores(t, c):
        start = pl.multiple_of(t * tk, tk)
        s_ref[c] = jnp.dot(qs[c], kt_ref[:, pl.ds(start, tk)], preferred_element_type=F32)

    def softmax_pv(t, c, m, diagonal=False):
        width = rows * (c % 2 + 1) if diagonal else tk
        start = pl.multiple_of(t * tk, tk)
        vt = jnp.concatenate([v_ref[pl.ds(start, width), :], ones[width]], axis=1)
        s = s_ref[c, :, :width]
        if diagonal:
            row = lax.broadcasted_iota(jnp.int32, (2 * rows, rows), 0)
            col = lax.broadcasted_iota(jnp.int32, (2 * rows, rows), 1)
            qrow = jnp.where(row >= rows, row - rows, row)
            tail = jnp.where(col <= qrow, s[:, width - rows:], -jnp.inf)
            s = tail if width == rows else jnp.concatenate([s[:, :width - rows], tail], axis=1)
        m_new = jnp.maximum(m, jnp.max(s, axis=1, keepdims=True))
        alpha = jnp.exp2(m - m_new)
        p = jnp.exp2(s - m_new).astype(BF16)
        acc_ref[c] = alpha * acc_ref[c] + jnp.dot(p, vt, preferred_element_type=F32)
        return m_new

    m0 = jnp.full((2 * rows, 1), -jnp.inf, F32)
    scores(0, 0)

    def common_tile(t, ms):
        ms = list(ms)
        for c in range(chains):
            if c + 1 < chains:
                scores(t, c + 1)
            else:
                scores(t + 1, 0)
            ms[c] = softmax_pv(t, c, ms[c])
        return tuple(ms)

    ms = list(lax.fori_loop(0, base, common_tile, (m0,) * chains))

    items = [(k, c) for k in range(chains // 2) for c in range(2 * k, chains)]
    for n, (k, c) in enumerate(items):
        if n + 1 < len(items):
            k_next, c_next = items[n + 1]
            scores(base + k_next, c_next)
        ms[c] = softmax_pv(base + k, c, ms[c], diagonal=(c // 2 == k))

    lam = (jnp.exp(jnp.sum(lq1_ref[...] * lk1_ref[...], axis=1, keepdims=True))
           - jnp.exp(jnp.sum(lq2_ref[...] * lk2_ref[...], axis=1, keepdims=True)) + lambda_init)
    for c in range(chains):
        o = (acc_ref[c, 0:rows, 0:hd] / acc_ref[c, 0:rows, hd:]
             - lam * (acc_ref[c, rows:, 0:hd] / acc_ref[c, rows:, hd:]))
        ms_o = jnp.mean(o * o, axis=1, keepdims=True)
        o = o * lax.rsqrt(ms_o + EPS) * sg_ref[...] * (1.0 - lambda_init)
        o_ref[c * rows:(c + 1) * rows, :] = o.astype(o_ref.dtype)


def _diff_attention(q, kt, v, lq1, lk1, lq2, lk2, subln_g, *, B, S, H, dh, lambda_init, rows, chains):
    aw = H * 2 * dh
    q3 = q.reshape(B, S, aw)
    v3 = v.reshape(B, S, aw)
    hd = 2 * dh
    step = rows * chains
    kern = functools.partial(_attn_kernel, rows=rows, chains=chains, dh=dh, lambda_init=lambda_init)
    vec = pl.BlockSpec((1, dh), lambda b, h, i: (0, 0))
    out = pl.pallas_call(
        kern,
        grid=(B, H, S // step),
        in_specs=[pl.BlockSpec((None, step, hd), lambda b, h, i: (b, i, h)),
                  pl.BlockSpec((None, hd, S), lambda b, h, i: (b, h, 0)),
                  pl.BlockSpec((None, S, hd), lambda b, h, i: (b, 0, h)),
                  vec, vec, vec, vec,
                  pl.BlockSpec((1, hd), lambda b, h, i: (0, 0))],
        out_specs=pl.BlockSpec((None, step, hd), lambda b, h, i: (b, i, h)),
        out_shape=jax.ShapeDtypeStruct((B, S, aw), BF16),
        scratch_shapes=[pltpu.VMEM((chains, 2 * rows, 2 * hd), F32),
                        pltpu.VMEM((chains, 2 * rows, 2 * rows), F32)],
        compiler_params=_cparams(("parallel", "parallel", "parallel")),
        name="diff_flash_attention",
    )(q3, kt, v3, lq1, lk1, lq2, lk2, subln_g)
    return out.reshape(B * S, aw)


def _mixout_kernel(attn_ref, ucur_ref, uhalo_ref, x_ref, mod_ref, wdw_ref, bdw_ref, lng_ref, lnb_ref,
                   wo1_ref, wo2_ref, gffn_ref, wrh_ref, wrl_ref, br_ref,
                   x1_ref, h2_ref, route_ref, route_t_ref, cnt_ref, ubuf_ref, conv_ref,
                   *, nS, conv_k, n_exp, n_grp):
    T = x_ref.shape[0]
    cw = ucur_ref.shape[1]
    i = pl.program_id(0)

    @pl.when(i == 0)
    def _():
        cnt_ref[...] = jnp.zeros_like(cnt_ref)

    first = (i % nS) == 0
    halo = uhalo_ref[...].astype(F32)
    ubuf_ref[0, 0:CONV_HALO, :] = jnp.where(first, jnp.zeros_like(halo), halo)
    ubuf_ref[0, CONV_HALO:, :] = ucur_ref[...].astype(F32)
    span = T + CONV_HALO - SUBLANES
    for b in range(1, SUBLANES):
        ubuf_ref[b, 0:span, :] = ubuf_ref[0, b:b + span, :]

    off = CONV_HALO - (conv_k - 1)
    rows = 32
    for r0 in range(0, T, rows):
        acc = jnp.zeros((rows // SUBLANES, SUBLANES, cw), F32)
        for j in range(conv_k):
            a, b = divmod(j + off, SUBLANES)
            lo_r = r0 + a * SUBLANES
            acc = acc + (wdw_ref[j * SUBLANES:(j + 1) * SUBLANES, :]
                         * ubuf_ref[b, lo_r:lo_r + rows, :].reshape(rows // SUBLANES, SUBLANES, cw))
        y = acc.reshape(rows, cw) + bdw_ref[...]
        mu = jnp.mean(y, axis=1, keepdims=True)
        d = y - mu
        var = jnp.mean(d * d, axis=1, keepdims=True)
        z = d * lax.rsqrt(var + EPS) * lng_ref[...] + lnb_ref[...]
        conv_ref[r0:r0 + rows, :] = _silu(z).astype(BF16)

    yo = jnp.dot(jnp.concatenate([attn_ref[...], conv_ref[...]], axis=1),
                 jnp.concatenate([wo1_ref[...], wo2_ref[...]], axis=0), preferred_element_type=F32)
    x1 = x_ref[...] + mod_ref[2:3, :] * yo
    x1_ref[...] = x1
    ms = jnp.mean(x1 * x1, axis=1, keepdims=True)
    h2 = x1 * lax.rsqrt(ms + EPS) * gffn_ref[...] * (1.0 + mod_ref[4:5, :]) + mod_ref[3:4, :]
    h2_ref[...] = h2
    hi = h2.astype(BF16)
    lo = (h2 - hi.astype(F32)).astype(BF16)

    logits = (jnp.dot(hi, wrh_ref[...], preferred_element_type=F32)
              + jnp.dot(lo, wrh_ref[...], preferred_element_type=F32)
              + jnp.dot(hi, wrl_ref[...], preferred_element_type=F32)) + br_ref[...]
    lane_i = lax.broadcasted_iota(jnp.int32, logits.shape, 1)
    lane = lane_i.astype(F32)
    big = jnp.float32(1e9)
    ninf = jnp.float32(-jnp.inf)
    is_g = (lane_i >= n_exp) & (lane_i < n_exp + n_grp)
    gl = jnp.where(is_g, logits, ninf)
    gmax = jnp.max(gl, axis=1, keepdims=True)
    gsum = jnp.sum(jnp.where(is_g, jnp.exp(gl - gmax), 0.0), axis=1, keepdims=True)
    g_p = 1.0 / gsum
    gidx = jnp.min(jnp.where(gl == gmax, lane, big), axis=1, keepdims=True) - n_exp
    epg = n_exp // n_grp
    lo_l = gidx * epg
    in_grp = (lane >= lo_l) & (lane < lo_l + epg)
    el = jnp.where(in_grp, logits, ninf)
    m1 = jnp.max(el, axis=1, keepdims=True)
    i1 = jnp.min(jnp.where(el == m1, lane, big), axis=1, keepdims=True)
    el2 = jnp.where(lane == i1, ninf, el)
    m2 = jnp.max(el2, axis=1, keepdims=True)
    i2 = jnp.min(jnp.where(el2 == m2, lane, big), axis=1, keepdims=True)
    e2 = jnp.exp(m2 - m1)
    p1 = 1.0 / (1.0 + e2)
    p2 = e2 / (1.0 + e2)

    sel = (lane == i1) | (lane == i2)
    rr = lax.broadcasted_iota(jnp.int32, (T, T), 0)
    cc = lax.broadcasted_iota(jnp.int32, (T, T), 1)
    before = jnp.dot((rr > cc).astype(BF16), sel.astype(BF16), preferred_element_type=F32) + cnt_ref[...]
    r1 = jnp.sum(jnp.where(lane == i1, before, 0.0), axis=1, keepdims=True)
    r2 = jnp.sum(jnp.where(lane == i2, before, 0.0), axis=1, keepdims=True)
    cnt_ref[...] += jnp.sum(sel.astype(F32), axis=0, keepdims=True)

    route = jnp.zeros_like(logits)
    for k, val in enumerate((i1, i2, g_p * p1, g_p * p2, r1, r2)):
        route = jnp.where(lane_i == k, val, route)
    route_ref[...] = route
    route_t_ref[...] = route.T[:SUBLANES]


def _mixout(attn, u, x2, mod, w_dw, b_dw, ln_g, ln_b, wo1, wo2, g_ffn, wr_hi, wr_lo, b_r,
            *, S, T, n_exp, n_grp):
    N, D = x2.shape
    aw = attn.shape[1]
    cw = u.shape[1]
    nS = S // T
    conv_k = w_dw.shape[0]
    w_dw = jnp.repeat(w_dw, SUBLANES, axis=0)
    hb = T // CONV_HALO
    kern = functools.partial(_mixout_kernel, nS=nS, conv_k=conv_k, n_exp=n_exp, n_grp=n_grp)
    full = lambda a: pl.BlockSpec(a.shape, lambda i: (0, 0))
    return pl.pallas_call(
        kern,
        grid=(N // T,),
        in_specs=[pl.BlockSpec((T, aw), lambda i: (i, 0)),
                  pl.BlockSpec((T, cw), lambda i: (i, 0)),
                  pl.BlockSpec((CONV_HALO, cw), lambda i: (jnp.maximum(i * hb - 1, 0), 0)),
                  pl.BlockSpec((T, D), lambda i: (i, 0)),
                  pl.BlockSpec((None, 6, D), lambda i: (i // nS, 0, 0)),
                  full(w_dw), full(b_dw), full(ln_g), full(ln_b), full(wo1), full(wo2), full(g_ffn),
                  full(wr_hi), full(wr_lo), full(b_r)],
        out_specs=[pl.BlockSpec((T, D), lambda i: (i, 0)),
                   pl.BlockSpec((T, D), lambda i: (i, 0)),
                   pl.BlockSpec((T, LANES), lambda i: (i, 0)),
                   pl.BlockSpec((SUBLANES, T), lambda i: (0, i)),
                   pl.BlockSpec((1, LANES), lambda i: (0, 0))],
        out_shape=[jax.ShapeDtypeStruct((N, D), F32),
                   jax.ShapeDtypeStruct((N, D), F32),
                   jax.ShapeDtypeStruct((N, LANES), F32),
                   jax.ShapeDtypeStruct((SUBLANES, N), F32),
                   jax.ShapeDtypeStruct((1, LANES), F32)],
        scratch_shapes=[pltpu.VMEM((SUBLANES, CONV_HALO + T, cw), F32),
                        pltpu.VMEM((T, cw), BF16)],
        compiler_params=_cparams(("arbitrary",)),
        name="conv_outproj_router",
    )(attn, u, u, x2, mod, w_dw, b_dw, ln_g, ln_b, wo1, wo2, g_ffn, wr_hi, wr_lo, b_r)


def _row_of(ref, r):
    if isinstance(r, int):
        return ref.at[r // SUBLANES, pl.ds(r % SUBLANES, 1)]
    return ref.at[lax.shift_right_logical(r, 3), pl.ds(r & (SUBLANES - 1), 1)]


def _scatter_kernel(dest_ref, h2_ref, xs_hbm, sem):
    groups = h2_ref.shape[0]
    tokens = groups * SUBLANES

    def issue(g, carry):
        for j in range(SUBLANES):
            for k in range(TOP_K_INNER):
                d = dest_ref[0, k * tokens + g * SUBLANES + j]
                pltpu.make_async_copy(h2_ref.at[g, pl.ds(j, 1)], _row_of(xs_hbm, d), sem).start(priority=k % 2)
        return carry

    lax.fori_loop(0, groups, issue, 0)
    for k in range(TOP_K_INNER):
        pltpu.make_async_copy(h2_ref, xs_hbm.at[pl.ds(0, groups)], sem).wait()


def _dest_table(dest, tokens):
    tab = jnp.concatenate([d.reshape(-1, tokens) for d in dest], axis=1)
    return tab.reshape(tab.shape[0], 1, tab.shape[1])


def _scatter_rows(h2, dest, *, tokens):
    N, C = h2.shape
    steps = N // tokens
    dest3 = _dest_table(dest, tokens)
    rows = N * TOP_K_INNER
    xs = pl.pallas_call(
        _scatter_kernel,
        grid=(steps,),
        in_specs=[pl.BlockSpec((None, 1, TOP_K_INNER * tokens), lambda s: (s, 0, 0), memory_space=pltpu.SMEM),
                  pl.BlockSpec((tokens // SUBLANES, SUBLANES, C), lambda s: (s, 0, 0))],
        out_specs=pl.BlockSpec(memory_space=pl.ANY),
        out_shape=jax.ShapeDtypeStruct((rows // SUBLANES, SUBLANES, C), h2.dtype),
        scratch_shapes=[pltpu.SemaphoreType.DMA],
        compiler_params=_cparams(("arbitrary",)),
        name="moe_scatter_rows",
    )(dest3, h2.reshape(N // SUBLANES, SUBLANES, C))
    return xs.reshape(rows, C)


XS_RING = 3


def _experts_kernel(vt_ref, ve_ref, vlo_ref, vhi_ref, vnew_ref, vnext_ref, vslot_ref,
                    xs_hbm, wg_hbm, wu_hbm, wd_hbm, ys_ref, xbuf, wg_buf, wu_buf, wd_buf, sem, wsem,
                    *, n_tiles):
    v = pl.program_id(0)
    lo = vlo_ref[v]
    hi = vhi_ref[v]
    t = vt_ref[v]
    tm = xbuf.shape[1]
    wslot = vslot_ref[v]

    def weight_copies(expert, slot):
        return [pltpu.make_async_copy(hbm.at[expert], buf.at[slot], wsem.at[slot, j])
                for j, (hbm, buf) in enumerate(((wg_hbm, wg_buf), (wu_hbm, wu_buf), (wd_hbm, wd_buf)))]

    @pl.when(v == 0)
    def _():
        for c in weight_copies(ve_ref[0], 0):
            c.start()

    @pl.when(vnew_ref[v] == 1)
    def _():
        @pl.when(vnext_ref[v] >= 0)
        def _():
            for c in weight_copies(vnext_ref[v], 1 - wslot):
                c.start()

        for c in weight_copies(ve_ref[v], wslot):
            c.wait()

    def tile_copy(tile):
        slot = lax.rem(tile, XS_RING)
        return pltpu.make_async_copy(xs_hbm.at[pl.ds(pl.multiple_of(tile * tm, tm), tm)], xbuf.at[slot],
                                     sem.at[slot])

    @pl.when(v == 0)
    def _():
        for tile in range(min(XS_RING - 1, n_tiles)):
            tile_copy(tile).start()

    @pl.when((lo == 0) & (hi > lo))
    def _():
        @pl.when(t + (XS_RING - 1) < n_tiles)
        def _():
            tile_copy(t + (XS_RING - 1)).start()

        tile_copy(t).wait()

    @pl.when(hi > lo)
    def _():
        x = xbuf[lax.rem(t, XS_RING)].astype(BF16)
        g = jnp.dot(x, wg_buf[wslot].astype(BF16), preferred_element_type=F32)
        u = jnp.dot(x, wu_buf[wslot].astype(BF16), preferred_element_type=F32)
        hid = (_silu(g) * u).astype(BF16)
        y = jnp.dot(hid, wd_buf[wslot].astype(BF16), preferred_element_type=F32)

        @pl.when(lo == 0)
        def _():
            ys_ref[...] = y

        @pl.when(lo > 0)
        def _():
            row = lax.broadcasted_iota(jnp.int32, y.shape, 0)
            ys_ref[...] = jnp.where((row >= lo) & (row < hi), y, ys_ref[...])


def _experts(xs, w_gate, w_up, w_down, visits, *, tm):
    R, C = xs.shape
    E, D, ff = w_gate.shape
    vt = visits[0]
    any_spec = pl.BlockSpec(memory_space=pl.ANY)
    grid_spec = pltpu.PrefetchScalarGridSpec(
        num_scalar_prefetch=len(visits),
        grid=(vt.shape[0],),
        in_specs=[any_spec, any_spec, any_spec, any_spec],
        out_specs=pl.BlockSpec((tm, C), lambda v, vt, *_: (vt[v], 0)),
        scratch_shapes=[pltpu.VMEM((XS_RING, tm, C), F32),
                        pltpu.VMEM((2, D, ff), w_gate.dtype),
                        pltpu.VMEM((2, D, ff), w_up.dtype),
                        pltpu.VMEM((2, ff, D), w_down.dtype),
                        pltpu.SemaphoreType.DMA((XS_RING,)),
                        pltpu.SemaphoreType.DMA((2, 3))],
    )
    return pl.pallas_call(
        functools.partial(_experts_kernel, n_tiles=R // tm),
        grid_spec=grid_spec,
        out_shape=jax.ShapeDtypeStruct((R, C), F32),
        compiler_params=_cparams(("arbitrary",)),
        name="moe_grouped_experts",
    )(*visits, xs, w_gate, w_up, w_down)


def _visit_tables(off, cnt, n_rows, tm):
    n_tiles = n_rows // tm
    n_exp = off.shape[0]
    n_visits = n_tiles + n_exp - 1
    tile_starts = jnp.arange(n_tiles, dtype=jnp.int32) * tm
    seg_starts = jnp.where((cnt > 0) & (off % tm != 0), off, n_rows)
    starts = jnp.sort(jnp.concatenate([tile_starts, seg_starts]))
    lo_abs = starts[:n_visits]
    hi_abs = starts[1:n_visits + 1]
    valid = lo_abs < n_rows
    tile = jnp.where(valid, lo_abs // tm, n_tiles - 1)
    ends = off + cnt
    probe = jnp.where(valid, lo_abs, n_rows - 1)
    expert = jnp.sum(ends[None, :] <= probe[:, None], axis=1).astype(jnp.int32)
    row_lo = jnp.where(valid, lo_abs - tile * tm, 0)
    row_hi = jnp.where(valid, hi_abs - tile * tm, 0)
    new = jnp.concatenate([jnp.ones((1,), jnp.int32), (expert[1:] != expert[:-1]).astype(jnp.int32)])
    slot = (jnp.cumsum(new) - 1) % 2
    nxt = jnp.sum(expert[None, :] <= expert[:, None], axis=1)
    nxt_expert = jnp.where(nxt < n_visits, expert[jnp.minimum(nxt, n_visits - 1)], -1)
    return tile, expert, row_lo, row_hi, new, nxt_expert.astype(jnp.int32), slot.astype(jnp.int32)


def _combine_kernel(dcur_ref, dnxt_ref, ys_hbm, route_ref, x1_ref, mod_ref, o_ref, ybuf, sem):
    T = x1_ref.shape[0]
    i = pl.program_id(0)
    slot = lax.rem(i, 2)

    def gather(d_ref, sl):
        def body(g, carry):
            for j in range(SUBLANES):
                for k in range(TOP_K_INNER):
                    d = d_ref[0, k * T + g * SUBLANES + j]
                    pltpu.make_async_copy(_row_of(ys_hbm, d), ybuf.at[sl, k, g, pl.ds(j, 1)],
                                          sem.at[sl]).start(priority=k % 2)
            return carry
        lax.fori_loop(0, T // SUBLANES, body, 0)

    @pl.when(i == 0)
    def _():
        gather(dcur_ref, 0)

    @pl.when(i + 1 < pl.num_programs(0))
    def _():
        gather(dnxt_ref, 1 - slot)

    for k in range(TOP_K_INNER):
        pltpu.make_async_copy(ys_hbm.at[pl.ds(0, T // SUBLANES)], ybuf.at[slot, k], sem.at[slot]).wait()

    route = route_ref[...]
    moe = jnp.zeros(x1_ref.shape, F32)
    for k in range(TOP_K_INNER):
        moe = moe + route[:, TOP_K_INNER + k:TOP_K_INNER + k + 1] * ybuf[slot, k].reshape(x1_ref.shape)
    o_ref[...] = x1_ref[...] + mod_ref[5:6, :] * moe


def _combine(ys, dest, route, x1, mod, *, S, T):
    N, D = x1.shape
    C = ys.shape[1]
    steps = N // T
    nS = S // T
    dest3 = _dest_table(dest, T)
    dspec = lambda f: pl.BlockSpec((None, 1, TOP_K_INNER * T), f, memory_space=pltpu.SMEM)
    return pl.pallas_call(
        _combine_kernel,
        grid=(steps,),
        in_specs=[dspec(lambda i: (i, 0, 0)),
                  dspec(lambda i: (jnp.minimum(i + 1, steps - 1), 0, 0)),
                  pl.BlockSpec(memory_space=pl.ANY),
                  pl.BlockSpec((T, LANES), lambda i: (i, 0)),
                  pl.BlockSpec((T, D), lambda i: (i, 0)),
                  pl.BlockSpec((None, 6, D), lambda i: (i // nS, 0, 0))],
        out_specs=pl.BlockSpec((T, D), lambda i: (i, 0)),
        out_shape=jax.ShapeDtypeStruct((N, D), F32),
        scratch_shapes=[pltpu.VMEM((2, TOP_K_INNER, T // SUBLANES, SUBLANES, C), F32),
                        pltpu.SemaphoreType.DMA((2,))],
        compiler_params=_cparams(("arbitrary",)),
        name="moe_gather_combine",
    )(dest3, dest3, ys.reshape(ys.shape[0] // SUBLANES, SUBLANES, C), route, x1, mod)


def _layer(x2, mod, pos_row, l, B, S, g_mix, w_in, q_norm_g, k_norm_g, lambda_q1, lambda_k1, lambda_q2,
           lambda_k2, subln_g, b_glu, w_dw, b_dw, conv_ln_g, conv_ln_b, w_out, g_ffn, w_group, b_group,
           w_router, b_router, w_gate, w_up, w_down):
    N, D = x2.shape
    dh = q_norm_g.shape[0]
    H = N_DIFF_HEADS
    aw = H * 2 * dh
    cw = w_dw.shape[1]
    rot = dh // 4
    n_grp = w_group.shape[1]
    n_exp = w_router.shape[1]
    lambda_init = 0.8 - 0.6 * math.exp(-0.3 * l)
    tiles = _tiles(S)

    scale = dh ** -0.5 * math.log2(math.e)
    gq = jnp.tile(q_norm_g * scale, aw // dh)
    gk = jnp.tile(k_norm_g, aw // dh)
    gqk_tab = jnp.broadcast_to(jnp.concatenate([gq, gk])[:, None], (2 * aw, LANES))
    inv_freq = ROPE_THETA ** (-jnp.arange(0, rot, 2, dtype=F32) / rot)
    invf_tab = jnp.broadcast_to(inv_freq[:, None], (rot // 2, LANES))

    q, kt, v, u = _inproj(x2, mod, g_mix.reshape(1, D), pos_row, w_in, b_glu.reshape(1, 2 * cw),
                          gqk_tab, invf_tab, B=B, S=S, aw=aw, cw=cw, dh=dh, rot=rot, T=tiles.inproj)

    attn = _diff_attention(q, kt, v, lambda_q1.reshape(1, dh), lambda_k1.reshape(1, dh),
                           lambda_q2.reshape(1, dh), lambda_k2.reshape(1, dh), subln_g.reshape(1, 2 * dh),
                           B=B, S=S, H=H, dh=dh, lambda_init=lambda_init, rows=tiles.attn_rows,
                           chains=tiles.attn_chains)

    pad = LANES - n_exp - n_grp
    w_r = jnp.concatenate([w_router, w_group, jnp.zeros((D, pad), F32)], axis=1)
    b_r = jnp.concatenate([b_router, b_group, jnp.zeros((pad,), F32)]).reshape(1, LANES)
    wr_hi = w_r.astype(BF16)
    wr_lo = (w_r - wr_hi.astype(F32)).astype(BF16)
    wo = w_out.astype(BF16)
    x1, h2, route, route_t, counts = _mixout(attn, u, x2, mod, w_dw, b_dw.reshape(1, cw), conv_ln_g.reshape(1, cw),
                                     conv_ln_b.reshape(1, cw), wo[:aw], wo[aw:], g_ffn.reshape(1, D),
                                     wr_hi, wr_lo, b_r, S=S, T=tiles.mixout, n_exp=n_exp, n_grp=n_grp)

    cnt = counts[0, :n_exp].astype(jnp.int32)
    off = jnp.cumsum(cnt) - cnt
    experts = jnp.arange(n_exp, dtype=jnp.int32)[:, None]
    dest = [jnp.sum(jnp.where(route_t[k].astype(jnp.int32)[None, :] == experts, off[:, None], 0), axis=0)
            + route_t[4 + k].astype(jnp.int32) for k in range(TOP_K_INNER)]
    visits = _visit_tables(off, cnt, N * TOP_K_INNER, tiles.experts)

    xs = _scatter_rows(h2, dest, tokens=tiles.scatter)
    ys = _experts(xs, w_gate, w_up, w_down, visits, tm=tiles.experts)
    return _combine(ys, dest, route, x1, mod, S=S, T=tiles.combine)


def kernel(x, c, positions, w_ada, b_ada, g_mix, w_in, q_norm_g, k_norm_g, lambda_q1, lambda_k1, lambda_q2,
           lambda_k2, subln_g, b_glu, w_dw, b_dw, conv_ln_g, conv_ln_b, w_out, g_ffn, w_group, b_group,
           w_router, b_router, w_gate, w_up, w_down):
    B, S, D = x.shape
    depth = w_ada.shape[0]
    x2 = x.reshape(B * S, D)
    pos_row = positions.astype(F32).reshape(1, B * S)
    for l in range(depth):
        mod = _modulation(c, w_ada[l], b_ada[l])
        x2 = _layer(x2, mod, pos_row, l, B, S, g_mix[l], w_in[l], q_norm_g[l], k_norm_g[l], lambda_q1[l],
                    lambda_k1[l], lambda_q2[l], lambda_k2[l], subln_g[l], b_glu[l], w_dw[l], b_dw[l],
                    conv_ln_g[l], conv_ln_b[l], w_out[l], g_ffn[l], w_group[l], b_group[l], w_router[l],
                    b_router[l], w_gate[l], w_up[l], w_down[l])
    return x2.reshape(B, S, D)
```

```python
import collections
import functools
import math

import numpy as np
import jax
import jax.numpy as jnp
from jax import lax
from jax.experimental import pallas as pl
from jax.experimental.pallas import tpu as pltpu

F32 = jnp.float32
BF16 = jnp.bfloat16

EPS = 1e-6
ROPE_THETA = 500000.0
N_DIFF_HEADS = 4
TOP_K_INNER = 2

LANES = 128
SUBLANES = 8
CONV_HALO = 32
VMEM_LIMIT = 48 * 1024 * 1024


Tiles = collections.namedtuple("Tiles", "inproj attn_rows attn_chains mixout scatter experts combine")


def _tiles(S):
    cap = lambda t: min(t, S)
    attn_rows = cap(1024) // 2
    return Tiles(inproj=cap(1024), attn_rows=attn_rows, attn_chains=min(4, S // attn_rows), mixout=cap(512),
                 scatter=cap(2048),
                 experts=512, combine=cap(1024))


def _cparams(sem):
    return pltpu.CompilerParams(dimension_semantics=sem, vmem_limit_bytes=VMEM_LIMIT)


def _silu(x):
    return x * jax.nn.sigmoid(x)


def _mod_kernel(c_ref, w_ref, b_ref, o_ref):
    c = c_ref[...]
    o_ref[...] = jnp.dot(_silu(c), w_ref[...], preferred_element_type=F32,
                         precision=lax.Precision.HIGHEST) + b_ref[...]


def _modulation(c, w_ada, b_ada):
    B, D = c.shape
    n_out = w_ada.shape[1]
    rows = 8
    c_pad = jnp.pad(c, ((0, rows - B), (0, 0)))
    bn = 2048
    out = pl.pallas_call(
        _mod_kernel,
        grid=(n_out // bn,),
        in_specs=[pl.BlockSpec((rows, D), lambda j: (0, 0)),
                  pl.BlockSpec((D, bn), lambda j: (0, j)),
                  pl.BlockSpec((1, bn), lambda j: (0, j))],
        out_specs=pl.BlockSpec((rows, bn), lambda j: (0, j)),
        out_shape=jax.ShapeDtypeStruct((rows, n_out), F32),
        compiler_params=_cparams(("parallel",)),
        name="adaln_mod",
    )(c_pad, w_ada, b_ada.reshape(1, n_out))
    return out[:B].reshape(B, 6, D)


W_CHUNK = 256


def _inproj_kernel(x_ref, mod_ref, g_ref, pos_ref, w_hbm, bglu_ref, gqk_ref, invf_ref,
                   q_ref, kt_ref, v_ref, u_ref, wv_ref, wqk_ref, stage, sem, *, aw, cw, dh, rot):
    T = x_ref.shape[0]
    half = rot // 2

    @pl.when(pl.program_id(0) == 0)
    def _():
        n_chunks = w_hbm.shape[1] // W_CHUNK
        copies = [pltpu.make_async_copy(w_hbm.at[:, pl.ds(j * W_CHUNK, W_CHUNK)], stage.at[j % 2], sem.at[j % 2])
                  for j in range(n_chunks)]
        copies[0].start()
        for j in range(n_chunks):
            if j + 1 < n_chunks:
                copies[j + 1].start()
            copies[j].wait()
            chunk = stage[j % 2]
            c0 = j * W_CHUNK
            if c0 < 2 * aw:
                wqk_ref[c0:c0 + W_CHUNK, :] = chunk.T.astype(BF16)
            else:
                wv_ref[:, c0 - 2 * aw:c0 - 2 * aw + W_CHUNK] = chunk.astype(BF16)

    x = x_ref[...]
    ms = jnp.mean(x * x, axis=-1, keepdims=True)
    sh = mod_ref[0:1, :]
    sc = mod_ref[1:2, :]
    h = x * lax.rsqrt(ms + EPS) * g_ref[...] * (1.0 + sc) + sh
    hb = h.astype(BF16)

    pv = jnp.dot(hb, wv_ref[...], preferred_element_type=F32)
    v_ref[...] = pv[:, :aw].astype(BF16)
    a = pv[:, aw:aw + cw] + bglu_ref[:, :cw]
    gate = pv[:, aw + cw:] + bglu_ref[:, cw:]
    u_ref[...] = (a * jax.nn.sigmoid(gate)).astype(BF16)

    qkt = lax.dot_general(wqk_ref[...], hb, (((1,), (1,)), ((), ())), preferred_element_type=F32)
    nch = 2 * aw // dh
    for c in range(T // LANES):
        sl = slice(c * LANES, (c + 1) * LANES)
        s3 = qkt[:, sl].reshape(nch, dh, LANES)
        ssq = jnp.mean(s3 * s3, axis=1, keepdims=True)
        y = s3 * lax.rsqrt(ssq + EPS) * gqk_ref[...].reshape(nch, dh, LANES)
        ang = invf_ref[...] * pos_ref[:, sl]
        cs = jnp.cos(ang)
        sn = jnp.sin(ang)
        t1 = y[:, 0:half, :]
        t2 = y[:, half:rot, :]
        y = jnp.concatenate([t1 * cs - t2 * sn, t2 * cs + t1 * sn, y[:, rot:, :]], axis=1)
        y2 = y.reshape(2 * aw, LANES)
        kt_ref[:, sl] = y2[aw:].astype(BF16)
        q_ref[sl, :] = y2[:aw].T.astype(BF16)


def _inproj(x2, mod, g_mix, pos_row, w_in, b_glu, gqk_tab, invf_tab, *, B, S, aw, cw, dh, rot, T):
    N, D = x2.shape
    nS = S // T
    n_vglu = w_in.shape[1] - 2 * aw
    kern = functools.partial(_inproj_kernel, aw=aw, cw=cw, dh=dh, rot=rot)
    return pl.pallas_call(
        kern,
        grid=(N // T,),
        in_specs=[pl.BlockSpec((T, D), lambda i: (i, 0)),
                  pl.BlockSpec((None, 6, D), lambda i: (i // nS, 0, 0)),
                  pl.BlockSpec((1, D), lambda i: (0, 0)),
                  pl.BlockSpec((1, T), lambda i: (0, i)),
                  pl.BlockSpec(memory_space=pl.ANY),
                  pl.BlockSpec((1, 2 * cw), lambda i: (0, 0)),
                  pl.BlockSpec(gqk_tab.shape, lambda i: (0, 0)),
                  pl.BlockSpec(invf_tab.shape, lambda i: (0, 0))],
        out_specs=[pl.BlockSpec((T, aw), lambda i: (i, 0)),
                   pl.BlockSpec((None, aw, T), lambda i: (i // nS, 0, i % nS)),
                   pl.BlockSpec((T, aw), lambda i: (i, 0)),
                   pl.BlockSpec((T, cw), lambda i: (i, 0))],
        out_shape=[jax.ShapeDtypeStruct((N, aw), BF16),
                   jax.ShapeDtypeStruct((B, aw, S), BF16),
                   jax.ShapeDtypeStruct((N, aw), BF16),
                   jax.ShapeDtypeStruct((N, cw), BF16)],
        scratch_shapes=[pltpu.VMEM((D, n_vglu), BF16),
                        pltpu.VMEM((2 * aw, D), BF16),
                        pltpu.VMEM((2, D, W_CHUNK), F32),
                        pltpu.SemaphoreType.DMA((2,))],
        compiler_params=_cparams(("arbitrary",)),
        name="inproj_qknorm_rope_glu",
    )(x2, mod, g_mix, pos_row, w_in, b_glu, gqk_tab, invf_tab)


def _attn_kernel(q_ref, kt_ref, v_ref, lq1_ref, lk1_ref, lq2_ref, lk2_ref, sg_ref, o_ref, acc_ref, s_ref,
                 *, rows, chains, dh, lambda_init):
    i = pl.program_id(2)
    hd = 2 * dh
    tk = 2 * rows
    base = (chains // 2) * i

    def stacked(q):
        lane = lax.broadcasted_iota(jnp.int32, q.shape, 1)
        zero = jnp.zeros_like(q)
        return jnp.concatenate([jnp.where(lane < dh, q, zero), jnp.where(lane >= dh, q, zero)], axis=0)

    qs = [stacked(q_ref[c * rows:(c + 1) * rows, :]) for c in range(chains)]

    acc_ref[...] = jnp.zeros_like(acc_ref)
    ones = {w: jnp.ones((w, hd), BF16) for w in (rows, tk)}

    def scores(t, c):
        start = pl.multiple_of(t * tk, tk)
        s_ref[c] = jnp.dot(qs[c], kt_ref[:, pl.ds(start, tk)], preferred_element_type=F32)

    def softmax_pv(t, c, m, diagonal=False):
        width = rows * (c % 2 + 1) if diagonal else tk
        start = pl.multiple_of(t * tk, tk)
        vt = jnp.concatenate([v_ref[pl.ds(start, width), :], ones[width]], axis=1)
        s = s_ref[c, :, :width]
        if diagonal:
            row = lax.broadcasted_iota(jnp.int32, (2 * rows, rows), 0)
            col = lax.broadcasted_iota(jnp.int32, (2 * rows, rows), 1)
            qrow = jnp.where(row >= rows, row - rows, row)
            tail = jnp.where(col <= qrow, s[:, width - rows:], -jnp.inf)
            s = tail if width == rows else jnp.concatenate([s[:, :width - rows], tail], axis=1)
        m_new = jnp.maximum(m, jnp.max(s, axis=1, keepdims=True))
        alpha = jnp.exp2(m - m_new)
        p = jnp.exp2(s - m_new).astype(BF16)
        acc_ref[c] = alpha * acc_ref[c] + jnp.dot(p, vt, preferred_element_type=F32)
        return m_new

    m0 = jnp.full((2 * rows, 1), -jnp.inf, F32)
    scores(0, 0)

    def common_tile(t, ms):
        ms = list(ms)
        for c in range(chains):
            if c + 1 < chains:
                scores(t, c + 1)
            else:
                scores(t + 1, 0)
            ms[c] = softmax_pv(t, c, ms[c])
        return tuple(ms)

    ms = list(lax.fori_loop(0, base, common_tile, (m0,) * chains))

    items = [(k, c) for k in range(chains // 2) for c in range(2 * k, chains)]
    for n, (k, c) in enumerate(items):
        if n + 1 < len(items):
            k_next, c_next = items[n + 1]
            scores(base + k_next, c_next)
        ms[c] = softmax_pv(base + k, c, ms[c], diagonal=(c // 2 == k))

    lam = (jnp.exp(jnp.sum(lq1_ref[...] * lk1_ref[...], axis=1, keepdims=True))
           - jnp.exp(jnp.sum(lq2_ref[...] * lk2_ref[...], axis=1, keepdims=True)) + lambda_init)
    for c in range(chains):
        o = (acc_ref[c, 0:rows, 0:hd] / acc_ref[c, 0:rows, hd:]
             - lam * (acc_ref[c, rows:, 0:hd] / acc_ref[c, rows:, hd:]))
        ms_o = jnp.mean(o * o, axis=1, keepdims=True)
        o = o * lax.rsqrt(ms_o + EPS) * sg_ref[...] * (1.0 - lambda_init)
        o_ref[c * rows:(c + 1) * rows, :] = o.astype(o_ref.dtype)


def _diff_attention(q, kt, v, lq1, lk1, lq2, lk2, subln_g, *, B, S, H, dh, lambda_init, rows, chains):
    aw = H * 2 * dh
    q3 = q.reshape(B, S, aw)
    v3 = v.reshape(B, S, aw)
    hd = 2 * dh
    step = rows * chains
    kern = functools.partial(_attn_kernel, rows=rows, chains=chains, dh=dh, lambda_init=lambda_init)
    vec = pl.BlockSpec((1, dh), lambda b, h, i: (0, 0))
    out = pl.pallas_call(
        kern,
        grid=(B, H, S // step),
        in_specs=[pl.BlockSpec((None, step, hd), lambda b, h, i: (b, i, h)),
                  pl.BlockSpec((None, hd, S), lambda b, h, i: (b, h, 0)),
                  pl.BlockSpec((None, S, hd), lambda b, h, i: (b, 0, h)),
                  vec, vec, vec, vec,
                  pl.BlockSpec((1, hd), lambda b, h, i: (0, 0))],
        out_specs=pl.BlockSpec((None, step, hd), lambda b, h, i: (b, i, h)),
        out_shape=jax.ShapeDtypeStruct((B, S, aw), BF16),
        scratch_shapes=[pltpu.VMEM((chains, 2 * rows, 2 * hd), F32),
                        pltpu.VMEM((chains, 2 * rows, 2 * rows), F32)],
        compiler_params=_cparams(("parallel", "parallel", "parallel")),
        name="diff_flash_attention",
    )(q3, kt, v3, lq1, lk1, lq2, lk2, subln_g)
    return out.reshape(B * S, aw)


def _mixout_kernel(attn_ref, ucur_ref, uhalo_ref, x_ref, mod_ref, wdw_ref, bdw_ref, lng_ref, lnb_ref,
                   wo1_ref, wo2_ref, gffn_ref, wrh_ref, wrl_ref, br_ref,
                   x1_ref, h2_ref, route_ref, route_t_ref, cnt_ref, ubuf_ref, conv_ref,
                   *, nS, conv_k, n_exp, n_grp):
    T = x_ref.shape[0]
    cw = ucur_ref.shape[1]
    i = pl.program_id(0)

    @pl.when(i == 0)
    def _():
        cnt_ref[...] = jnp.zeros_like(cnt_ref)

    first = (i % nS) == 0
    halo = uhalo_ref[...].astype(F32)
    ubuf_ref[0, 0:CONV_HALO, :] = jnp.where(first, jnp.zeros_like(halo), halo)
    ubuf_ref[0, CONV_HALO:, :] = ucur_ref[...].astype(F32)
    span = T + CONV_HALO - SUBLANES
    for b in range(1, SUBLANES):
        ubuf_ref[b, 0:span, :] = ubuf_ref[0, b:b + span, :]

    off = CONV_HALO - (conv_k - 1)
    rows = 32
    for r0 in range(0, T, rows):
        acc = jnp.zeros((rows // SUBLANES, SUBLANES, cw), F32)
        for j in range(conv_k):
            a, b = divmod(j + off, SUBLANES)
            lo_r = r0 + a * SUBLANES
            acc = acc + (wdw_ref[j * SUBLANES:(j + 1) * SUBLANES, :]
                         * ubuf_ref[b, lo_r:lo_r + rows, :].reshape(rows // SUBLANES, SUBLANES, cw))
        y = acc.reshape(rows, cw) + bdw_ref[...]
        mu = jnp.mean(y, axis=1, keepdims=True)
        d = y - mu
        var = jnp.mean(d * d, axis=1, keepdims=True)
        z = d * lax.rsqrt(var + EPS) * lng_ref[...] + lnb_ref[...]
        conv_ref[r0:r0 + rows, :] = _silu(z).astype(BF16)

    yo = jnp.dot(jnp.concatenate([attn_ref[...], conv_ref[...]], axis=1),
                 jnp.concatenate([wo1_ref[...], wo2_ref[...]], axis=0), preferred_element_type=F32)
    chunk = 128
    parts = []
    for r in range(0, T, chunk):
        x1 = x_ref[r:r + chunk, :] + mod_ref[2:3, :] * yo[r:r + chunk]
        x1_ref[r:r + chunk, :] = x1
        ms = jnp.mean(x1 * x1, axis=1, keepdims=True)
        h2 = x1 * lax.rsqrt(ms + EPS) * gffn_ref[...] * (1.0 + mod_ref[4:5, :]) + mod_ref[3:4, :]
        h2_ref[r:r + chunk, :] = h2
        hi = h2.astype(BF16)
        lo = (h2 - hi.astype(F32)).astype(BF16)
        parts.append(jnp.dot(hi, wrh_ref[...], preferred_element_type=F32)
                     + jnp.dot(lo, wrh_ref[...], preferred_element_type=F32)
                     + jnp.dot(hi, wrl_ref[...], preferred_element_type=F32))
    logits = jnp.concatenate(parts, axis=0) + br_ref[...]
    lane_i = lax.broadcasted_iota(jnp.int32, logits.shape, 1)
    lane = lane_i.astype(F32)
    big = jnp.float32(1e9)
    ninf = jnp.float32(-jnp.inf)
    is_g = (lane_i >= n_exp) & (lane_i < n_exp + n_grp)
    gl = jnp.where(is_g, logits, ninf)
    gmax = jnp.max(gl, axis=1, keepdims=True)
    gsum = jnp.sum(jnp.where(is_g, jnp.exp(gl - gmax), 0.0), axis=1, keepdims=True)
    g_p = 1.0 / gsum
    gidx = jnp.min(jnp.where(gl == gmax, lane, big), axis=1, keepdims=True) - n_exp
    epg = n_exp // n_grp
    lo_l = gidx * epg
    in_grp = (lane >= lo_l) & (lane < lo_l + epg)
    el = jnp.where(in_grp, logits, ninf)
    m1 = jnp.max(el, axis=1, keepdims=True)
    i1 = jnp.min(jnp.where(el == m1, lane, big), axis=1, keepdims=True)
    el2 = jnp.where(lane == i1, ninf, el)
    m2 = jnp.max(el2, axis=1, keepdims=True)
    i2 = jnp.min(jnp.where(el2 == m2, lane, big), axis=1, keepdims=True)
    e2 = jnp.exp(m2 - m1)
    p1 = 1.0 / (1.0 + e2)
    p2 = e2 / (1.0 + e2)

    sel = (lane == i1) | (lane == i2)
    rr = lax.broadcasted_iota(jnp.int32, (T, T), 0)
    cc = lax.broadcasted_iota(jnp.int32, (T, T), 1)
    before = jnp.dot((rr > cc).astype(BF16), sel.astype(BF16), preferred_element_type=F32) + cnt_ref[...]
    r1 = jnp.sum(jnp.where(lane == i1, before, 0.0), axis=1, keepdims=True)
    r2 = jnp.sum(jnp.where(lane == i2, before, 0.0), axis=1, keepdims=True)
    cnt_ref[...] += jnp.sum(sel.astype(F32), axis=0, keepdims=True)

    route = jnp.zeros_like(logits)
    for k, val in enumerate((i1, i2, g_p * p1, g_p * p2, r1, r2)):
        route = jnp.where(lane_i == k, val, route)
    route_ref[...] = route
    route_t_ref[...] = route.T[:SUBLANES]


def _mixout(attn, u, x2, mod, w_dw, b_dw, ln_g, ln_b, wo1, wo2, g_ffn, wr_hi, wr_lo, b_r,
            *, S, T, n_exp, n_grp):
    N, D = x2.shape
    aw = attn.shape[1]
    cw = u.shape[1]
    nS = S // T
    conv_k = w_dw.shape[0]
    w_dw = jnp.repeat(w_dw, SUBLANES, axis=0)
    hb = T // CONV_HALO
    kern = functools.partial(_mixout_kernel, nS=nS, conv_k=conv_k, n_exp=n_exp, n_grp=n_grp)
    full = lambda a: pl.BlockSpec(a.shape, lambda i: (0, 0))
    return pl.pallas_call(
        kern,
        grid=(N // T,),
        in_specs=[pl.BlockSpec((T, aw), lambda i: (i, 0)),
                  pl.BlockSpec((T, cw), lambda i: (i, 0)),
                  pl.BlockSpec((CONV_HALO, cw), lambda i: (jnp.maximum(i * hb - 1, 0), 0)),
                  pl.BlockSpec((T, D), lambda i: (i, 0)),
                  pl.BlockSpec((None, 6, D), lambda i: (i // nS, 0, 0)),
                  full(w_dw), full(b_dw), full(ln_g), full(ln_b), full(wo1), full(wo2), full(g_ffn),
                  full(wr_hi), full(wr_lo), full(b_r)],
        out_specs=[pl.BlockSpec((T, D), lambda i: (i, 0)),
                   pl.BlockSpec((T, D), lambda i: (i, 0)),
                   pl.BlockSpec((T, LANES), lambda i: (i, 0)),
                   pl.BlockSpec((SUBLANES, T), lambda i: (0, i)),
                   pl.BlockSpec((1, LANES), lambda i: (0, 0))],
        out_shape=[jax.ShapeDtypeStruct((N, D), F32),
                   jax.ShapeDtypeStruct((N, D), F32),
                   jax.ShapeDtypeStruct((N, LANES), F32),
                   jax.ShapeDtypeStruct((SUBLANES, N), F32),
                   jax.ShapeDtypeStruct((1, LANES), F32)],
        scratch_shapes=[pltpu.VMEM((SUBLANES, CONV_HALO + T, cw), F32),
                        pltpu.VMEM((T, cw), BF16)],
        compiler_params=_cparams(("arbitrary",)),
        name="conv_outproj_router",
    )(attn, u, u, x2, mod, w_dw, b_dw, ln_g, ln_b, wo1, wo2, g_ffn, wr_hi, wr_lo, b_r)


def _row_of(ref, r):
    if isinstance(r, int):
        return ref.at[r // SUBLANES, pl.ds(r % SUBLANES, 1)]
    return ref.at[lax.shift_right_logical(r, 3), pl.ds(r & (SUBLANES - 1), 1)]


def _scatter_kernel(dest_ref, h2_ref, xs_hbm, sem):
    groups = h2_ref.shape[0]
    tokens = groups * SUBLANES

    def issue(g, carry):
        for j in range(SUBLANES):
            for k in range(TOP_K_INNER):
                d = dest_ref[0, k * tokens + g * SUBLANES + j]
                pltpu.make_async_copy(h2_ref.at[g, pl.ds(j, 1)], _row_of(xs_hbm, d), sem).start(priority=k % 2)
        return carry

    lax.fori_loop(0, groups, issue, 0)
    for k in range(TOP_K_INNER):
        pltpu.make_async_copy(h2_ref, xs_hbm.at[pl.ds(0, groups)], sem).wait()


def _dest_table(dest, tokens):
    tab = jnp.concatenate([d.reshape(-1, tokens) for d in dest], axis=1)
    return tab.reshape(tab.shape[0], 1, tab.shape[1])


def _scatter_rows(h2, dest, *, tokens):
    N, C = h2.shape
    steps = N // tokens
    dest3 = _dest_table(dest, tokens)
    rows = N * TOP_K_INNER
    xs = pl.pallas_call(
        _scatter_kernel,
        grid=(steps,),
        in_specs=[pl.BlockSpec((None, 1, TOP_K_INNER * tokens), lambda s: (s, 0, 0), memory_space=pltpu.SMEM),
                  pl.BlockSpec((tokens // SUBLANES, SUBLANES, C), lambda s: (s, 0, 0))],
        out_specs=pl.BlockSpec(memory_space=pl.ANY),
        out_shape=jax.ShapeDtypeStruct((rows // SUBLANES, SUBLANES, C), h2.dtype),
        scratch_shapes=[pltpu.SemaphoreType.DMA],
        compiler_params=_cparams(("arbitrary",)),
        name="moe_scatter_rows",
    )(dest3, h2.reshape(N // SUBLANES, SUBLANES, C))
    return xs.reshape(rows, C)


XS_RING = 3


def _experts_kernel(vt_ref, ve_ref, vlo_ref, vhi_ref, vnew_ref, vnext_ref, vslot_ref,
                    xs_hbm, wg_hbm, wu_hbm, wd_hbm, ys_ref, xbuf, wg_buf, wu_buf, wd_buf, sem, wsem,
                    *, n_tiles):
    v = pl.program_id(0)
    lo = vlo_ref[v]
    hi = vhi_ref[v]
    t = vt_ref[v]
    tm = xbuf.shape[1]
    wslot = vslot_ref[v]

    def weight_copies(expert, slot):
        return [pltpu.make_async_copy(hbm.at[expert], buf.at[slot], wsem.at[slot, j])
                for j, (hbm, buf) in enumerate(((wg_hbm, wg_buf), (wu_hbm, wu_buf), (wd_hbm, wd_buf)))]

    @pl.when(v == 0)
    def _():
        for c in weight_copies(ve_ref[0], 0):
            c.start()

    @pl.when(vnew_ref[v] == 1)
    def _():
        @pl.when(vnext_ref[v] >= 0)
        def _():
            for c in weight_copies(vnext_ref[v], 1 - wslot):
                c.start()

        for c in weight_copies(ve_ref[v], wslot):
            c.wait()

    def tile_copy(tile):
        slot = lax.rem(tile, XS_RING)
        return pltpu.make_async_copy(xs_hbm.at[pl.ds(pl.multiple_of(tile * tm, tm), tm)], xbuf.at[slot],
                                     sem.at[slot])

    @pl.when(v == 0)
    def _():
        for tile in range(min(XS_RING - 1, n_tiles)):
            tile_copy(tile).start()

    @pl.when((lo == 0) & (hi > lo))
    def _():
        @pl.when(t + (XS_RING - 1) < n_tiles)
        def _():
            tile_copy(t + (XS_RING - 1)).start()

        tile_copy(t).wait()

    @pl.when(hi > lo)
    def _():
        x = xbuf[lax.rem(t, XS_RING)].astype(BF16)
        g = jnp.dot(x, wg_buf[wslot].astype(BF16), preferred_element_type=F32)
        u = jnp.dot(x, wu_buf[wslot].astype(BF16), preferred_element_type=F32)
        hid = (_silu(g) * u).astype(BF16)
        y = jnp.dot(hid, wd_buf[wslot].astype(BF16), preferred_element_type=F32)

        @pl.when(lo == 0)
        def _():
            ys_ref[...] = y

        @pl.when(lo > 0)
        def _():
            row = lax.broadcasted_iota(jnp.int32, y.shape, 0)
            ys_ref[...] = jnp.where((row >= lo) & (row < hi), y, ys_ref[...])


def _experts(xs, w_gate, w_up, w_down, visits, *, tm):
    R, C = xs.shape
    E, D, ff = w_gate.shape
    vt = visits[0]
    any_spec = pl.BlockSpec(memory_space=pl.ANY)
    grid_spec = pltpu.PrefetchScalarGridSpec(
        num_scalar_prefetch=len(visits),
        grid=(vt.shape[0],),
        in_specs=[any_spec, any_spec, any_spec, any_spec],
        out_specs=pl.BlockSpec((tm, C), lambda v, vt, *_: (vt[v], 0)),
        scratch_shapes=[pltpu.VMEM((XS_RING, tm, C), F32),
                        pltpu.VMEM((2, D, ff), w_gate.dtype),
                        pltpu.VMEM((2, D, ff), w_up.dtype),
                        pltpu.VMEM((2, ff, D), w_down.dtype),
                        pltpu.SemaphoreType.DMA((XS_RING,)),
                        pltpu.SemaphoreType.DMA((2, 3))],
    )
    return pl.pallas_call(
        functools.partial(_experts_kernel, n_tiles=R // tm),
        grid_spec=grid_spec,
        out_shape=jax.ShapeDtypeStruct((R, C), F32),
        compiler_params=_cparams(("arbitrary",)),
        name="moe_grouped_experts",
    )(*visits, xs, w_gate, w_up, w_down)


def _visit_tables(off, cnt, n_rows, tm):
    n_tiles = n_rows // tm
    n_exp = off.shape[0]
    n_visits = n_tiles + n_exp - 1
    tile_starts = jnp.arange(n_tiles, dtype=jnp.int32) * tm
    seg_starts = jnp.where((cnt > 0) & (off % tm != 0), off, n_rows)
    starts = jnp.sort(jnp.concatenate([tile_starts, seg_starts]))
    lo_abs = starts[:n_visits]
    hi_abs = starts[1:n_visits + 1]
    valid = lo_abs < n_rows
    tile = jnp.where(valid, lo_abs // tm, n_tiles - 1)
    ends = off + cnt
    probe = jnp.where(valid, lo_abs, n_rows - 1)
    expert = jnp.sum(ends[None, :] <= probe[:, None], axis=1).astype(jnp.int32)
    row_lo = jnp.where(valid, lo_abs - tile * tm, 0)
    row_hi = jnp.where(valid, hi_abs - tile * tm, 0)
    new = jnp.concatenate([jnp.ones((1,), jnp.int32), (expert[1:] != expert[:-1]).astype(jnp.int32)])
    slot = (jnp.cumsum(new) - 1) % 2
    nxt = jnp.sum(expert[None, :] <= expert[:, None], axis=1)
    nxt_expert = jnp.where(nxt < n_visits, expert[jnp.minimum(nxt, n_visits - 1)], -1)
    return tile, expert, row_lo, row_hi, new, nxt_expert.astype(jnp.int32), slot.astype(jnp.int32)


def _combine_kernel(dcur_ref, dnxt_ref, ys_hbm, route_ref, x1_ref, mod_ref, o_ref, ybuf, sem):
    T = x1_ref.shape[0]
    i = pl.program_id(0)
    slot = lax.rem(i, 2)

    def gather(d_ref, sl):
        def body(g, carry):
            for j in range(SUBLANES):
                for k in range(TOP_K_INNER):
                    d = d_ref[0, k * T + g * SUBLANES + j]
                    pltpu.make_async_copy(_row_of(ys_hbm, d), ybuf.at[sl, k, g, pl.ds(j, 1)],
                                          sem.at[sl]).start(priority=k % 2)
            return carry
        lax.fori_loop(0, T // SUBLANES, body, 0)

    @pl.when(i == 0)
    def _():
        gather(dcur_ref, 0)

    @pl.when(i + 1 < pl.num_programs(0))
    def _():
        gather(dnxt_ref, 1 - slot)

    for k in range(TOP_K_INNER):
        pltpu.make_async_copy(ys_hbm.at[pl.ds(0, T // SUBLANES)], ybuf.at[slot, k], sem.at[slot]).wait()

    route = route_ref[...]
    moe = jnp.zeros(x1_ref.shape, F32)
    for k in range(TOP_K_INNER):
        moe = moe + route[:, TOP_K_INNER + k:TOP_K_INNER + k + 1] * ybuf[slot, k].reshape(x1_ref.shape)
    o_ref[...] = x1_ref[...] + mod_ref[5:6, :] * moe


def _combine(ys, dest, route, x1, mod, *, S, T):
    N, D = x1.shape
    C = ys.shape[1]
    steps = N // T
    nS = S // T
    dest3 = _dest_table(dest, T)
    dspec = lambda f: pl.BlockSpec((None, 1, TOP_K_INNER * T), f, memory_space=pltpu.SMEM)
    return pl.pallas_call(
        _combine_kernel,
        grid=(steps,),
        in_specs=[dspec(lambda i: (i, 0, 0)),
                  dspec(lambda i: (jnp.minimum(i + 1, steps - 1), 0, 0)),
                  pl.BlockSpec(memory_space=pl.ANY),
                  pl.BlockSpec((T, LANES), lambda i: (i, 0)),
                  pl.BlockSpec((T, D), lambda i: (i, 0)),
                  pl.BlockSpec((None, 6, D), lambda i: (i // nS, 0, 0))],
        out_specs=pl.BlockSpec((T, D), lambda i: (i, 0)),
        out_shape=jax.ShapeDtypeStruct((N, D), F32),
        scratch_shapes=[pltpu.VMEM((2, TOP_K_INNER, T // SUBLANES, SUBLANES, C), F32),
                        pltpu.SemaphoreType.DMA((2,))],
        compiler_params=_cparams(("arbitrary",)),
        name="moe_gather_combine",
    )(dest3, dest3, ys.reshape(ys.shape[0] // SUBLANES, SUBLANES, C), route, x1, mod)


def _layer(x2, mod, pos_row, l, B, S, g_mix, w_in, q_norm_g, k_norm_g, lambda_q1, lambda_k1, lambda_q2,
           lambda_k2, subln_g, b_glu, w_dw, b_dw, conv_ln_g, conv_ln_b, w_out, g_ffn, w_group, b_group,
           w_router, b_router, w_gate, w_up, w_down):
    N, D = x2.shape
    dh = q_norm_g.shape[0]
    H = N_DIFF_HEADS
    aw = H * 2 * dh
    cw = w_dw.shape[1]
    rot = dh // 4
    n_grp = w_group.shape[1]
    n_exp = w_router.shape[1]
    lambda_init = 0.8 - 0.6 * math.exp(-0.3 * l)
    tiles = _tiles(S)

    scale = dh ** -0.5 * math.log2(math.e)
    gq = jnp.tile(q_norm_g * scale, aw // dh)
    gk = jnp.tile(k_norm_g, aw // dh)
    gqk_tab = jnp.broadcast_to(jnp.concatenate([gq, gk])[:, None], (2 * aw, LANES))
    inv_freq = ROPE_THETA ** (-jnp.arange(0, rot, 2, dtype=F32) / rot)
    invf_tab = jnp.broadcast_to(inv_freq[:, None], (rot // 2, LANES))

    q, kt, v, u = _inproj(x2, mod, g_mix.reshape(1, D), pos_row, w_in, b_glu.reshape(1, 2 * cw),
                          gqk_tab, invf_tab, B=B, S=S, aw=aw, cw=cw, dh=dh, rot=rot, T=tiles.inproj)

    attn = _diff_attention(q, kt, v, lambda_q1.reshape(1, dh), lambda_k1.reshape(1, dh),
                           lambda_q2.reshape(1, dh), lambda_k2.reshape(1, dh), subln_g.reshape(1, 2 * dh),
                           B=B, S=S, H=H, dh=dh, lambda_init=lambda_init, rows=tiles.attn_rows,
                           chains=tiles.attn_chains)

    pad = LANES - n_exp - n_grp
    w_r = jnp.concatenate([w_router, w_group, jnp.zeros((D, pad), F32)], axis=1)
    b_r = jnp.concatenate([b_router, b_group, jnp.zeros((pad,), F32)]).reshape(1, LANES)
    wr_hi = w_r.astype(BF16)
    wr_lo = (w_r - wr_hi.astype(F32)).astype(BF16)
    wo = w_out.astype(BF16)
    x1, h2, route, route_t, counts = _mixout(attn, u, x2, mod, w_dw, b_dw.reshape(1, cw), conv_ln_g.reshape(1, cw),
                                     conv_ln_b.reshape(1, cw), wo[:aw], wo[aw:], g_ffn.reshape(1, D),
                                     wr_hi, wr_lo, b_r, S=S, T=tiles.mixout, n_exp=n_exp, n_grp=n_grp)

    cnt = counts[0, :n_exp].astype(jnp.int32)
    off = jnp.cumsum(cnt) - cnt
    experts = jnp.arange(n_exp, dtype=jnp.int32)[:, None]
    dest = [jnp.sum(jnp.where(route_t[k].astype(jnp.int32)[None, :] == experts, off[:, None], 0), axis=0)
            + route_t[4 + k].astype(jnp.int32) for k in range(TOP_K_INNER)]
    visits = _visit_tables(off, cnt, N * TOP_K_INNER, tiles.experts)

    xs = _scatter_rows(h2, dest, tokens=tiles.scatter)
    ys = _experts(xs, w_gate, w_up, w_down, visits, tm=tiles.experts)
    return _combine(ys, dest, route, x1, mod, S=S, T=tiles.combine)


def kernel(x, c, positions, w_ada, b_ada, g_mix, w_in, q_norm_g, k_norm_g, lambda_q1, lambda_k1, lambda_q2,
           lambda_k2, subln_g, b_glu, w_dw, b_dw, conv_ln_g, conv_ln_b, w_out, g_ffn, w_group, b_group,
           w_router, b_router, w_gate, w_up, w_down):
    B, S, D = x.shape
    depth = w_ada.shape[0]
    x2 = x.reshape(B * S, D)
    pos_row = positions.astype(F32).reshape(1, B * S)
    for l in range(depth):
        mod = _modulation(c, w_ada[l], b_ada[l])
        x2 = _layer(x2, mod, pos_row, l, B, S, g_mix[l], w_in[l], q_norm_g[l], k_norm_g[l], lambda_q1[l],
                    lambda_k1[l], lambda_q2[l], lambda_k2[l], subln_g[l], b_glu[l], w_dw[l], b_dw[l],
                    conv_ln_g[l], conv_ln_b[l], w_out[l], g_ffn[l], w_group[l], b_group[l], w_router[l],
                    b_router[l], w_gate[l], w_up[l], w_down[l])
    return x2.reshape(B, S, D)
```
